```python
import jax, jax.numpy as jnp
from jax import lax
import numpy as np

D_MODEL = 1024
BATCH = 8
SEQ = 4096
DEPTH = 2

HEAD_DIM = 64
ROPE_THETA = 10000.0
RMS_EPS = 1e-6

ATTN_HEADS = 8
ATTN_WIDTH = ATTN_HEADS * HEAD_DIM
MOBA_BLOCK = 256
MOBA_TOPK = 3
MOBA_QUERY_CHUNK = 16
NEG_INF = -1e30

CONV_WIDTH = 512
CONV_GROUPS = 8
CONV_KERNEL = 3

RET_HEADS = 4
RET_QK_WIDTH = RET_HEADS * HEAD_DIM
RET_V_DIM = 2 * HEAD_DIM
RET_V_WIDTH = RET_HEADS * RET_V_DIM
RET_CHUNK = 128

N_BRANCHES = 3
IN_PROJ_WIDTH = 3 * ATTN_WIDTH + 3 * CONV_WIDTH + 2 * RET_QK_WIDTH + 2 * RET_V_WIDTH + N_BRANCHES * D_MODEL

D_FF = 3584
N_EXPERTS = 8
TOP_K = 2
D_FF_EXPERT = 3584
N_DENSE = (DEPTH + 1) // 2
N_MOE = DEPTH // 2

kernel_name = "hybrid_moba_shortconv_retention_moe_block"


def rmsnorm(x, w):
    xf = x.astype(jnp.float32)
    y = xf * lax.rsqrt(jnp.mean(xf * xf, axis=-1, keepdims=True) + RMS_EPS)
    return y.astype(x.dtype) * w


def rope_tables(seq, dim):
    inv_freq = 1.0 / (ROPE_THETA ** (jnp.arange(0, dim, 2, dtype=jnp.float32) / dim))
    ang = jnp.arange(seq, dtype=jnp.float32)[:, None] * inv_freq[None, :]
    return jnp.cos(ang), jnp.sin(ang)


def apply_rope(x, cos, sin):
    x1, x2 = jnp.split(x, 2, axis=-1)
    c = cos[None, :, None, :]
    s = sin[None, :, None, :]
    return jnp.concatenate([x1 * c - x2 * s, x2 * c + x1 * s], axis=-1).astype(x.dtype)


def moba_attention(q, k, v):
    b, s, h, d = q.shape
    bh = b * h
    nb = -(-s // MOBA_BLOCK)
    pad = nb * MOBA_BLOCK - s
    k_p = jnp.pad(k, ((0, 0), (0, pad), (0, 0), (0, 0)))
    v_p = jnp.pad(v, ((0, 0), (0, pad), (0, 0), (0, 0)))
    kb = k_p.reshape(b, nb, MOBA_BLOCK, h, d).transpose(0, 3, 1, 2, 4).reshape(bh, nb, MOBA_BLOCK, d)
    vb = v_p.reshape(b, nb, MOBA_BLOCK, h, d).transpose(0, 3, 1, 2, 4).reshape(bh, nb, MOBA_BLOCK, d)
    qh = (q * HEAD_DIM ** -0.5).transpose(0, 2, 1, 3).reshape(bh, s, d)

    k_mean = jnp.mean(kb.astype(jnp.float32), axis=2)
    q_blk = jnp.arange(s) // MOBA_BLOCK
    gate = jnp.einsum('nsd,nkd->nsk', qh.astype(jnp.float32), k_mean)
    past = jnp.arange(nb)[None, :] < q_blk[:, None]
    gate = jnp.where(past[None], gate, -jnp.inf)
    n_sel = min(MOBA_TOPK, nb)
    _, sel = lax.top_k(gate, n_sel)
    valid = sel < q_blk[None, :, None]

    nq = s // MOBA_QUERY_CHUNK
    qc = qh.reshape(bh, nq, MOBA_QUERY_CHUNK, d).swapaxes(0, 1)
    selc = sel.reshape(bh, nq, MOBA_QUERY_CHUNK, n_sel).swapaxes(0, 1)
    valc = valid.reshape(bh, nq, MOBA_QUERY_CHUNK, n_sel).swapaxes(0, 1)
    bh_idx = jnp.arange(bh)[:, None, None]

    def one_chunk(args):
        c, q_c, sel_c, val_c = args
        j = (c * MOBA_QUERY_CHUNK) // MOBA_BLOCK
        k_sel = kb[bh_idx, sel_c]
        v_sel = vb[bh_idx, sel_c]
        s_sel = jnp.einsum('nqd,nqkld->nqkl', q_c, k_sel).astype(jnp.float32)
        s_sel = jnp.where(val_c[..., None], s_sel, NEG_INF)
        k_own = lax.dynamic_index_in_dim(kb, j, axis=1, keepdims=False)
        v_own = lax.dynamic_index_in_dim(vb, j, axis=1, keepdims=False)
        s_own = jnp.einsum('nqd,nld->nql', q_c, k_own).astype(jnp.float32)
        qpos = c * MOBA_QUERY_CHUNK + jnp.arange(MOBA_QUERY_CHUNK)
        kpos = j * MOBA_BLOCK + jnp.arange(MOBA_BLOCK)
        s_own = jnp.where((kpos[None, :] <= qpos[:, None])[None], s_own, NEG_INF)
        logits = jnp.concatenate([s_sel.reshape(bh, MOBA_QUERY_CHUNK, n_sel * MOBA_BLOCK), s_own], axis=-1)
        p = jax.nn.softmax(logits, axis=-1).astype(v.dtype)
        p_sel = p[..., :n_sel * MOBA_BLOCK].reshape(bh, MOBA_QUERY_CHUNK, n_sel, MOBA_BLOCK)
        p_own = p[..., n_sel * MOBA_BLOCK:]
        return (jnp.einsum('nqkl,nqkld->nqd', p_sel, v_sel)
                + jnp.einsum('nql,nld->nqd', p_own, v_own))

    out = lax.map(one_chunk, (jnp.arange(nq), qc, selc, valc))
    return out.swapaxes(0, 1).reshape(b, h, s, d).transpose(0, 2, 1, 3)


def retention(q, k, v):
    b, s, h, dk = q.shape
    dv = v.shape[-1]
    nc = s // RET_CHUNK
    log_g = jnp.log1p(-jnp.exp2(-5.0 - jnp.arange(h, dtype=jnp.float32)))
    n = jnp.arange(RET_CHUNK, dtype=jnp.float32)
    diff = n[:, None] - n[None, :]
    decay = jnp.where(diff[None] >= 0,
                      jnp.exp(jnp.maximum(diff, 0.0)[None] * log_g[:, None, None]), 0.0)
    xi = jnp.exp((n + 1.0)[None, :] * log_g[:, None]).T
    zeta = jnp.exp((RET_CHUNK - 1.0 - n)[None, :] * log_g[:, None]).T
    chunk_decay = jnp.exp(RET_CHUNK * log_g)

    def to_chunks(t):
        return t.astype(jnp.float32).reshape(b, nc, RET_CHUNK, h, t.shape[-1]).swapaxes(0, 1)

    def body(state, xs):
        q_i, k_i, v_i = xs
        scores = jnp.einsum('bnhd,bmhd->bhnm', q_i, k_i) * decay[None]
        o = jnp.einsum('bhnm,bmhe->bnhe', scores, v_i)
        o = o + jnp.einsum('bnhd,bhde->bnhe', q_i, state) * xi[None, :, :, None]
        state = (state * chunk_decay[None, :, None, None]
                 + jnp.einsum('bmhd,bmhe->bhde', k_i * zeta[None, :, :, None], v_i))
        return state, o

    state0 = jnp.zeros((b, h, dk, dv), jnp.float32)
    _, o = lax.scan(body, state0, (to_chunks(q), to_chunks(k), to_chunks(v)))
    return o.swapaxes(0, 1).reshape(b, s, h, dv).astype(v.dtype)


def hybrid_mixer(h, w_in, conv_w, conv_b, ret_norm_w, w_br_attn, w_br_conv, w_br_ret, w_out, cos, sin):
    b, s, _ = h.shape
    widths = (ATTN_WIDTH, ATTN_WIDTH, ATTN_WIDTH, CONV_WIDTH, CONV_WIDTH, CONV_WIDTH,
              RET_QK_WIDTH, RET_QK_WIDTH, RET_V_WIDTH, RET_V_WIDTH, N_BRANCHES * D_MODEL)
    points, acc = [], 0
    for wdt in widths[:-1]:
        acc += wdt
        points.append(acc)
    proj = h @ w_in
    aq, ak, av, cb, cc, ch, rq, rk, rv, rg, gates = jnp.split(proj, points, axis=-1)

    aq = apply_rope(aq.reshape(b, s, ATTN_HEADS, HEAD_DIM), cos, sin)
    ak = apply_rope(ak.reshape(b, s, ATTN_HEADS, HEAD_DIM), cos, sin)
    av = av.reshape(b, s, ATTN_HEADS, HEAD_DIM)
    y_attn = moba_attention(aq, ak, av).reshape(b, s, ATTN_WIDTH)

    u = cc * ch
    conv = lax.conv_general_dilated(u, conv_w[:, None, :], window_strides=(1,),
                                    padding=((CONV_KERNEL - 1, 0),),
                                    dimension_numbers=('NWC', 'WIO', 'NWC'),
                                    feature_group_count=CONV_WIDTH) + conv_b
    y_conv = cb * conv

    rq = apply_rope(rq.reshape(b, s, RET_HEADS, HEAD_DIM), cos, sin)
    rk = apply_rope(rk.reshape(b, s, RET_HEADS, HEAD_DIM), cos, sin) * HEAD_DIM ** -0.5
    rv = rv.reshape(b, s, RET_HEADS, RET_V_DIM)
    o = retention(rq, rk, rv)
    o = rmsnorm(o, ret_norm_w.reshape(RET_HEADS, RET_V_DIM)).reshape(b, s, RET_V_WIDTH)
    y_ret = jax.nn.silu(rg) * o

    g_attn, g_conv, g_ret = jnp.split(jax.nn.sigmoid(gates), N_BRANCHES, axis=-1)
    merged = (g_attn * (y_attn @ w_br_attn)
              + g_conv * (y_conv @ w_br_conv)
              + g_ret * (y_ret @ w_br_ret))
    return merged @ w_out


def swiglu(h, w_gate, w_up, w_down):
    return (jax.nn.silu(h @ w_gate) * (h @ w_up)) @ w_down


def moe_swiglu(h, w_router, w_gate, w_up, w_down):
    logits = (h @ w_router).astype(jnp.float32)
    top_v, top_i = lax.top_k(logits, TOP_K)
    top_p = jax.nn.softmax(top_v, axis=-1)
    combine = jnp.sum(jax.nn.one_hot(top_i, N_EXPERTS, dtype=jnp.float32) * top_p[..., None], axis=-2)
    combine = combine.astype(h.dtype)
    out = jnp.zeros_like(h)
    for e in range(N_EXPERTS):
        out = out + combine[..., e:e + 1] * swiglu(h, w_gate[e], w_up[e], w_down[e])
    return out


def setup_inputs(seed: int = 0) -> dict:
    key = jax.random.key(seed)
    ks = jax.random.split(key, 20)
    f32 = jnp.float32

    def nrm(k, shape, scale):
        return jax.random.normal(k, shape, f32) * scale

    def gain(k, shape):
        return 1.0 + 0.02 * jax.random.normal(k, shape, f32)

    return {
        "x": nrm(ks[0], (BATCH, SEQ, D_MODEL), 1.0),
        "mix_norm_w": gain(ks[1], (DEPTH, D_MODEL)),
        "w_in": nrm(ks[2], (DEPTH, D_MODEL, IN_PROJ_WIDTH), D_MODEL ** -0.5),
        "conv_w": nrm(ks[3], (DEPTH, CONV_KERNEL, CONV_WIDTH), CONV_KERNEL ** -0.5),
        "conv_b": nrm(ks[4], (DEPTH, CONV_WIDTH), 0.01),
        "ret_norm_w": gain(ks[5], (DEPTH, RET_V_WIDTH)),
        "w_br_attn": nrm(ks[6], (DEPTH, ATTN_WIDTH, D_MODEL), ATTN_WIDTH ** -0.5),
        "w_br_conv": nrm(ks[7], (DEPTH, CONV_WIDTH, D_MODEL), CONV_WIDTH ** -0.5),
        "w_br_ret": nrm(ks[8], (DEPTH, RET_V_WIDTH, D_MODEL), RET_V_WIDTH ** -0.5),
        "w_out": nrm(ks[9], (DEPTH, D_MODEL, D_MODEL), D_MODEL ** -0.5),
        "ffn_norm_w": gain(ks[10], (DEPTH, D_MODEL)),
        "dense_w_gate": nrm(ks[11], (N_DENSE, D_MODEL, D_FF), D_MODEL ** -0.5),
        "dense_w_up": nrm(ks[12], (N_DENSE, D_MODEL, D_FF), D_MODEL ** -0.5),
        "dense_w_down": nrm(ks[13], (N_DENSE, D_FF, D_MODEL), D_FF ** -0.5),
        "moe_router": nrm(ks[14], (N_MOE, D_MODEL, N_EXPERTS), D_MODEL ** -0.5),
        "moe_w_gate": nrm(ks[15], (N_MOE, N_EXPERTS, D_MODEL, D_FF_EXPERT), D_MODEL ** -0.5),
        "moe_w_up": nrm(ks[16], (N_MOE, N_EXPERTS, D_MODEL, D_FF_EXPERT), D_MODEL ** -0.5),
        "moe_w_down": nrm(ks[17], (N_MOE, N_EXPERTS, D_FF_EXPERT, D_MODEL), D_FF_EXPERT ** -0.5),
        "final_norm_w": gain(ks[18], (D_MODEL,)),
    }


def reference(x, mix_norm_w, w_in, conv_w, conv_b, ret_norm_w, w_br_attn, w_br_conv, w_br_ret,
              w_out, ffn_norm_w, dense_w_gate, dense_w_up, dense_w_down, moe_router,
              moe_w_gate, moe_w_up, moe_w_down, final_norm_w):
    cos, sin = rope_tables(x.shape[1], HEAD_DIM)
    for layer in range(DEPTH):
        h = rmsnorm(x, mix_norm_w[layer])
        x = x + hybrid_mixer(h, w_in[layer], conv_w[layer], conv_b[layer], ret_norm_w[layer],
                             w_br_attn[layer], w_br_conv[layer], w_br_ret[layer], w_out[layer],
                             cos, sin)
        h = rmsnorm(x, ffn_norm_w[layer])
        if layer % 2 == 0:
            i = layer // 2
            x = x + swiglu(h, dense_w_gate[i], dense_w_up[i], dense_w_down[i])
        else:
            i = layer // 2
            x = x + moe_swiglu(h, moe_router[i], moe_w_gate[i], moe_w_up[i], moe_w_down[i])
    return rmsnorm(x, final_norm_w)
```

```python
import functools
import math

import jax
import jax.numpy as jnp
import numpy as np
from jax import lax
from jax.experimental import pallas as pl
from jax.experimental.pallas import tpu as pltpu

F32 = jnp.float32
BF16 = jnp.bfloat16

D_MODEL = 1024
HEAD_DIM = 64
ROPE_THETA = 10000.0
RMS_EPS = 1e-6

ATTN_HEADS = 8
ATTN_WIDTH = ATTN_HEADS * HEAD_DIM
MOBA_BLOCK = 256
MOBA_TOPK = 3
NEG_INF = -1e30

CONV_WIDTH = 512
CONV_KERNEL = 3

RET_HEADS = 4
RET_QK_WIDTH = RET_HEADS * HEAD_DIM
RET_V_DIM = 2 * HEAD_DIM
RET_V_WIDTH = RET_HEADS * RET_V_DIM
RET_TILE = 256

D_FF = 3584
N_EXPERTS = 8
TOP_K = 2

LANES = 128
IN_PROJ_WIDTH = 7680
COL_AQ, COL_AK, COL_AV = 0, 512, 1024
COL_CB, COL_CC, COL_CH = 1536, 2048, 2560
COL_RV, COL_RG = 3072, 3584
COL_GATES = 4096
COL_RQ, COL_RK = 7168, 7424

PROJ_TM = 1024
PROJ_TN = 512
MERGE_TM = 512
FFN_TM = 1024
FFN_TF = 512
ROUTE_TM = 1024
GATHER_TM = 512
VMEM_LIMIT = 56 * 1024 * 1024


def _cparams(sem):
    return pltpu.CompilerParams(dimension_semantics=sem, vmem_limit_bytes=VMEM_LIMIT)


def _rope(acc, cos, sin_signed):
    lane = lax.broadcasted_iota(jnp.int32, (1, LANES), 1)
    first_half = (lane % HEAD_DIM) < (HEAD_DIM // 2)
    outs = []
    for g in range(acc.shape[1] // LANES):
        blk = acc[:, g * LANES:(g + 1) * LANES]
        partner = jnp.where(first_half,
                            pltpu.roll(blk, LANES - HEAD_DIM // 2, 1),
                            pltpu.roll(blk, HEAD_DIM // 2, 1))
        outs.append(blk * cos + partner * sin_signed)
    return jnp.concatenate(outs, axis=1)


def _inproj_kernel(x_ref, nw_ref, w_ref, cos_ref, sin_ref, proj_ref, vt_ref, h_scr):
    j = pl.program_id(1)

    @pl.when(j == 0)
    def _():
        x = x_ref[...]
        ms = jnp.mean(x * x, axis=-1, keepdims=True)
        h_scr[...] = (x * lax.rsqrt(ms + RMS_EPS) * nw_ref[...]).astype(BF16)

    acc = jnp.dot(h_scr[...], w_ref[...], preferred_element_type=F32)
    jq, jk, jv, jr = COL_AQ // PROJ_TN, COL_AK // PROJ_TN, COL_AV // PROJ_TN, COL_RQ // PROJ_TN
    scale = HEAD_DIM ** -0.5

    @pl.when(j == jq)
    def _():
        proj_ref[...] = (_rope(acc, cos_ref[...], sin_ref[...]) * scale).astype(BF16)

    @pl.when(j == jk)
    def _():
        proj_ref[...] = _rope(acc, cos_ref[...], sin_ref[...]).astype(BF16)

    @pl.when(j == jv)
    def _():
        proj_ref[...] = acc.astype(BF16)
        for c in range(PROJ_TM // MOBA_BLOCK):
            vt_ref[c] = acc[c * MOBA_BLOCK:(c + 1) * MOBA_BLOCK, :].T.astype(BF16)

    @pl.when(j == jr)
    def _():
        r = _rope(acc, cos_ref[...], sin_ref[...])
        col = lax.broadcasted_iota(jnp.int32, (1, PROJ_TN), 1)
        r = r * jnp.where(col >= RET_QK_WIDTH, scale, 1.0)
        proj_ref[...] = r.astype(BF16)

    @pl.when((j != jq) & (j != jk) & (j != jv) & (j != jr))
    def _():
        proj_ref[...] = acc.astype(BF16)


def _inproj(x2, norm_w, w_bf, cos_t, sin_t, batch, seq):
    n = x2.shape[0]
    nst = seq // PROJ_TM
    nblk = PROJ_TM // MOBA_BLOCK
    return pl.pallas_call(
        _inproj_kernel,
        grid=(n // PROJ_TM, IN_PROJ_WIDTH // PROJ_TN),
        in_specs=[
            pl.BlockSpec((PROJ_TM, D_MODEL), lambda i, j: (i, 0)),
            pl.BlockSpec((1, D_MODEL), lambda i, j: (0, 0)),
            pl.BlockSpec((D_MODEL, PROJ_TN), lambda i, j: (0, j)),
            pl.BlockSpec((PROJ_TM, LANES), lambda i, j: (i % nst, 0)),
            pl.BlockSpec((PROJ_TM, LANES), lambda i, j: (i % nst, 0)),
        ],
        out_specs=[
            pl.BlockSpec((PROJ_TM, PROJ_TN), lambda i, j: (i, j)),
            pl.BlockSpec((None, nblk, ATTN_WIDTH, MOBA_BLOCK), lambda i, j: (i // nst, i % nst, 0, 0)),
        ],
        out_shape=[
            jax.ShapeDtypeStruct((n, IN_PROJ_WIDTH), BF16),
            jax.ShapeDtypeStruct((batch, seq // MOBA_BLOCK, ATTN_WIDTH, MOBA_BLOCK), BF16),
        ],
        scratch_shapes=[pltpu.VMEM((PROJ_TM, D_MODEL), BF16)],
        compiler_params=_cparams(("arbitrary", "arbitrary")),
        name="inproj",
    )(x2, norm_w, w_bf, cos_t, sin_t)


_NT = (((1,), (1,)), ((), ()))


def _moba_kernel(q_ref, k_ref, vt_ref, o_ref, km_scr, sel_scr, *, nb):
    blk = MOBA_BLOCK
    lane = lax.broadcasted_iota(jnp.int32, (1, LANES), 1)

    def kmean_body(i, c):
        kb = k_ref[pl.ds(pl.multiple_of(i * blk, blk), blk), :].astype(F32)
        km_scr[pl.ds(i, 1), :] = jnp.sum(kb, axis=0, keepdims=True) * (1.0 / blk)
        return c

    lax.fori_loop(0, nb, kmean_body, 0)

    key_pos = lax.broadcasted_iota(jnp.int32, (blk, blk), 0)
    qry_pos = lax.broadcasted_iota(jnp.int32, (blk, blk), 1)
    causal = key_pos <= qry_pos
    blk_id = lax.broadcasted_iota(jnp.int32, (nb, blk), 0)

    def qblock(j, c):
        row0 = pl.multiple_of(j * blk, blk)
        q = q_ref[pl.ds(row0, blk), :]
        k_own = k_ref[pl.ds(row0, blk), :]
        outs = []
        for hh in range(2):
            hmask = (lane // HEAD_DIM) == hh
            qm = jnp.where(hmask, q, jnp.zeros_like(q))
            km = jnp.where(hmask, km_scr[...], 0.0)
            km_hi = km.astype(BF16)
            km_lo = (km - km_hi.astype(F32)).astype(BF16)
            gate = (lax.dot_general(km_hi, qm, _NT, preferred_element_type=F32)
                    + lax.dot_general(km_lo, qm, _NT, preferred_element_type=F32))
            past = blk_id < j
            gate = jnp.where(past, gate, -jnp.inf)
            rank = jnp.zeros((nb, blk), jnp.int32)
            for ip in range(nb):
                gi = gate[ip:ip + 1, :]
                beats = (gi > gate) | ((gi == gate) & (blk_id > ip))
                rank = rank + beats.astype(jnp.int32)
            sel = (rank < MOBA_TOPK) & past
            sel_scr[...] = sel.astype(F32)

            s = lax.dot_general(k_own, qm, _NT, preferred_element_type=F32)
            s = jnp.where(causal, s, NEG_INF)
            m0 = jnp.max(s, axis=0, keepdims=True)
            p = jnp.exp(s - m0)
            l0 = jnp.sum(p, axis=0, keepdims=True)
            vt = vt_ref[j, hh * HEAD_DIM:(hh + 1) * HEAD_DIM, :]
            acc0 = jnp.dot(vt, p.astype(BF16), preferred_element_type=F32)

            def kv_body(i, carry):
                m, l, acc = carry
                ki = k_ref[pl.ds(pl.multiple_of(i * blk, blk), blk), :]
                s = lax.dot_general(ki, qm, _NT, preferred_element_type=F32)
                s = jnp.where(sel_scr[pl.ds(i, 1), :] > 0.0, s, NEG_INF)
                m_new = jnp.maximum(m, jnp.max(s, axis=0, keepdims=True))
                alpha = jnp.exp(m - m_new)
                p = jnp.exp(s - m_new)
                l = alpha * l + jnp.sum(p, axis=0, keepdims=True)
                vti = vt_ref[i, hh * HEAD_DIM:(hh + 1) * HEAD_DIM, :]
                acc = alpha * acc + jnp.dot(vti, p.astype(BF16), preferred_element_type=F32)
                return m_new, l, acc

            _, l, acc = lax.fori_loop(0, j, kv_body, (m0, l0, acc0))
            outs.append(acc / l)
        o_ref[pl.ds(row0, blk), :] = jnp.concatenate(outs, axis=0).T.astype(BF16)
        return c

    lax.fori_loop(0, nb, qblock, 0)


def _moba(proj, vt, batch, seq):
    nb = seq // MOBA_BLOCK
    npair = ATTN_WIDTH // LANES
    return pl.pallas_call(
        functools.partial(_moba_kernel, nb=nb),
        grid=(batch, npair),
        in_specs=[
            pl.BlockSpec((seq, LANES), lambda b, p: (b, COL_AQ // LANES + p)),
            pl.BlockSpec((seq, LANES), lambda b, p: (b, COL_AK // LANES + p)),
            pl.BlockSpec((None, nb, LANES, MOBA_BLOCK), lambda b, p: (b, 0, p, 0)),
        ],
        out_specs=pl.BlockSpec((seq, LANES), lambda b, p: (b, p)),
        out_shape=jax.ShapeDtypeStruct((batch * seq, ATTN_WIDTH), BF16),
        scratch_shapes=[pltpu.VMEM((nb, LANES), F32), pltpu.VMEM((nb, MOBA_BLOCK), F32)],
        compiler_params=_cparams(("arbitrary", "arbitrary")),
        name="moba",
    )(proj, proj, vt)


def _ret_log_gamma():
    return [math.log1p(-(2.0 ** (-5.0 - h))) for h in range(RET_HEADS)]


def _retention_tables():
    c = RET_TILE
    lg = np.array(_ret_log_gamma(), np.float64)
    n = np.arange(c, dtype=np.float64)
    diff = n[:, None] - n[None, :]
    decay = np.where(diff[None] >= 0, np.exp(np.maximum(diff, 0.0)[None] * lg[:, None, None]), 0.0)
    head_of_lane = np.arange(RET_QK_WIDTH) // HEAD_DIM
    xi = np.exp((n + 1.0)[:, None] * lg[head_of_lane][None, :])
    zeta = np.exp((c - 1.0 - n)[:, None] * lg[head_of_lane][None, :])
    chunk_decay = np.exp(c * lg[head_of_lane])[:, None]
    return (jnp.asarray(decay, F32), jnp.asarray(xi, F32), jnp.asarray(zeta, F32),
            jnp.asarray(np.broadcast_to(chunk_decay, (RET_QK_WIDTH, LANES)), F32))


def _retention_kernel(q_ref, k_ref, v_ref, g_ref, nw_ref, decay_ref, xi_ref, zeta_ref, cd_ref,
                      o_ref, state_scr):
    @pl.when(pl.program_id(1) == 0)
    def _():
        state_scr[...] = jnp.zeros_like(state_scr)

    lane = lax.broadcasted_iota(jnp.int32, (1, LANES), 1)
    srow = lax.broadcasted_iota(jnp.int32, (LANES, 1), 0)
    for pr in range(RET_HEADS // 2):
        cols = slice(pr * LANES, (pr + 1) * LANES)
        q = q_ref[:, cols]
        k = k_ref[:, cols]
        state = state_scr[cols, :]
        state_bf = state.astype(BF16)
        q_xi = (q.astype(F32) * xi_ref[:, cols]).astype(BF16)
        kz_t = (k.astype(F32) * zeta_ref[:, cols]).T.astype(BF16)
        upd = []
        for hh in range(2):
            h = 2 * pr + hh
            hmask = (lane // HEAD_DIM) == hh
            vcols = slice(h * RET_V_DIM, (h + 1) * RET_V_DIM)
            v = v_ref[:, vcols]
            qm = jnp.where(hmask, q, jnp.zeros_like(q))
            scores = lax.dot_general(qm, k, _NT, preferred_element_type=F32) * decay_ref[h]
            o = jnp.dot(scores.astype(BF16), v, preferred_element_type=F32)
            o = o + jnp.dot(jnp.where(hmask, q_xi, jnp.zeros_like(q_xi)), state_bf,
                            preferred_element_type=F32)
            upd.append(jnp.dot(kz_t, v, preferred_element_type=F32))
            ms = jnp.mean(o * o, axis=-1, keepdims=True)
            y = o * lax.rsqrt(ms + RMS_EPS) * nw_ref[:, vcols]
            g = g_ref[:, vcols].astype(F32)
            o_ref[:, vcols] = (g * jax.nn.sigmoid(g) * y).astype(BF16)
        state_scr[cols, :] = state * cd_ref[cols, :] + jnp.where(srow < HEAD_DIM, upd[0], upd[1])


def _retention(proj, ret_norm_w, tables, batch, seq):
    nc = seq // RET_TILE
    decay, xi, zeta, cd = tables
    row = lambda b, c: b * nc + c
    const2 = lambda b, c: (0, 0)
    return pl.pallas_call(
        _retention_kernel,
        grid=(batch, nc),
        in_specs=[
            pl.BlockSpec((RET_TILE, RET_QK_WIDTH), lambda b, c: (row(b, c), COL_RQ // RET_QK_WIDTH)),
            pl.BlockSpec((RET_TILE, RET_QK_WIDTH), lambda b, c: (row(b, c), COL_RK // RET_QK_WIDTH)),
            pl.BlockSpec((RET_TILE, RET_V_WIDTH), lambda b, c: (row(b, c), COL_RV // RET_V_WIDTH)),
            pl.BlockSpec((RET_TILE, RET_V_WIDTH), lambda b, c: (row(b, c), COL_RG // RET_V_WIDTH)),
            pl.BlockSpec((1, RET_V_WIDTH), const2),
            pl.BlockSpec((RET_HEADS, RET_TILE, RET_TILE), lambda b, c: (0, 0, 0)),
            pl.BlockSpec((RET_TILE, RET_QK_WIDTH), const2),
            pl.BlockSpec((RET_TILE, RET_QK_WIDTH), const2),
            pl.BlockSpec((RET_QK_WIDTH, LANES), const2),
        ],
        out_specs=pl.BlockSpec((RET_TILE, RET_V_WIDTH), lambda b, c: (row(b, c), 0)),
        out_shape=jax.ShapeDtypeStruct((batch * seq, RET_V_WIDTH), BF16),
        scratch_shapes=[pltpu.VMEM((RET_QK_WIDTH, RET_V_DIM), F32)],
        compiler_params=_cparams(("arbitrary", "arbitrary")),
        name="retention",
    )(proj, proj, proj, proj, ret_norm_w, decay, xi, zeta, cd)


CONV_HALO = 8


def _merge_kernel(x_ref, ya_ref, cb_ref, cc_ref, ch_ref, yr_ref, ga_ref, gc_ref, gr_ref, cw_ref,
                  cbias_ref, wa_ref, wc_ref, wr_ref, wo_ref, o_ref, u_scr, *, tiles_per_seq):
    i = pl.program_id(0)
    tm = x_ref.shape[0]

    @pl.when(i % tiles_per_seq == 0)
    def _():
        u_scr[0:CONV_HALO, :] = jnp.zeros((CONV_HALO, CONV_WIDTH), F32)

    u_scr[CONV_HALO:CONV_HALO + tm, :] = cc_ref[...].astype(F32) * ch_ref[...].astype(F32)
    conv = (cw_ref[2:3, :] * u_scr[CONV_HALO:CONV_HALO + tm, :]
            + cw_ref[1:2, :] * u_scr[CONV_HALO - 1:CONV_HALO - 1 + tm, :]
            + cw_ref[0:1, :] * u_scr[CONV_HALO - 2:CONV_HALO - 2 + tm, :]
            + cbias_ref[...])
    y_conv = (cb_ref[...].astype(F32) * conv).astype(BF16)
    u_scr[0:CONV_HALO, :] = u_scr[tm:tm + CONV_HALO, :]

    def gate(g_ref):
        return jax.nn.sigmoid(g_ref[...].astype(F32))

    merged = gate(ga_ref) * jnp.dot(ya_ref[...], wa_ref[...], preferred_element_type=F32)
    merged = merged + gate(gc_ref) * jnp.dot(y_conv, wc_ref[...], preferred_element_type=F32)
    merged = merged + gate(gr_ref) * jnp.dot(yr_ref[...], wr_ref[...], preferred_element_type=F32)
    o_ref[...] = x_ref[...] + jnp.dot(merged.astype(BF16), wo_ref[...], preferred_element_type=F32)


def _merge(x2, y_attn, proj, y_ret, conv_w, conv_b, wa, wc, wr, wo, seq):
    n = x2.shape[0]
    tm = MERGE_TM
    const = lambda i: (0, 0)
    wide = lambda c: pl.BlockSpec((tm, 512), lambda i: (i, c // 512))
    gate_spec = lambda b: pl.BlockSpec((tm, D_MODEL), lambda i: (i, COL_GATES // D_MODEL + b))
    return pl.pallas_call(
        functools.partial(_merge_kernel, tiles_per_seq=seq // tm),
        grid=(n // tm,),
        in_specs=[
            pl.BlockSpec((tm, D_MODEL), lambda i: (i, 0)),
            pl.BlockSpec((tm, ATTN_WIDTH), lambda i: (i, 0)),
            wide(COL_CB), wide(COL_CC), wide(COL_CH),
            pl.BlockSpec((tm, RET_V_WIDTH), lambda i: (i, 0)),
            gate_spec(0), gate_spec(1), gate_spec(2),
            pl.BlockSpec((CONV_KERNEL, CONV_WIDTH), const),
            pl.BlockSpec((1, CONV_WIDTH), const),
            pl.BlockSpec((ATTN_WIDTH, D_MODEL), const),
            pl.BlockSpec((CONV_WIDTH, D_MODEL), const),
            pl.BlockSpec((RET_V_WIDTH, D_MODEL), const),
            pl.BlockSpec((D_MODEL, D_MODEL), const),
        ],
        out_specs=pl.BlockSpec((tm, D_MODEL), lambda i: (i, 0)),
        out_shape=jax.ShapeDtypeStruct((n, D_MODEL), F32),
        scratch_shapes=[pltpu.VMEM((tm + CONV_HALO, CONV_WIDTH), F32)],
        compiler_params=_cparams(("arbitrary",)),
        name="merge",
    )(x2, y_attn, proj, proj, proj, y_ret, proj, proj, proj, conv_w, conv_b, wa, wc, wr, wo)


def _swiglu_step(h, wg_ref, wu_ref, wd_ref, acc_scr, c):
    g = jnp.dot(h, wg_ref[...], preferred_element_type=F32)
    u = jnp.dot(h, wu_ref[...], preferred_element_type=F32)
    a = (g * jax.nn.sigmoid(g) * u).astype(BF16)
    part = jnp.dot(a, wd_ref[...], preferred_element_type=F32)

    @pl.when(c == 0)
    def _():
        acc_scr[...] = part

    @pl.when(c != 0)
    def _():
        acc_scr[...] += part


def _dense_ffn_kernel(x_ref, nw_ref, wg_ref, wu_ref, wd_ref, o_ref, h_scr, acc_scr):
    c = pl.program_id(1)

    @pl.when(c == 0)
    def _():
        x = x_ref[...]
        ms = jnp.mean(x * x, axis=-1, keepdims=True)
        h_scr[...] = (x * lax.rsqrt(ms + RMS_EPS) * nw_ref[...]).astype(BF16)

    _swiglu_step(h_scr[...], wg_ref, wu_ref, wd_ref, acc_scr, c)

    @pl.when(c == pl.num_programs(1) - 1)
    def _():
        o_ref[...] = x_ref[...] + acc_scr[...]


def _dense_ffn(x2, norm_w, wg, wu, wd):
    n = x2.shape[0]
    tm, tf = FFN_TM, FFN_TF
    return pl.pallas_call(
        _dense_ffn_kernel,
        grid=(n // tm, D_FF // tf),
        in_specs=[
            pl.BlockSpec((tm, D_MODEL), lambda i, c: (i, 0)),
            pl.BlockSpec((1, D_MODEL), lambda i, c: (0, 0)),
            pl.BlockSpec((D_MODEL, tf), lambda i, c: (0, c)),
            pl.BlockSpec((D_MODEL, tf), lambda i, c: (0, c)),
            pl.BlockSpec((tf, D_MODEL), lambda i, c: (c, 0)),
        ],
        out_specs=pl.BlockSpec((tm, D_MODEL), lambda i, c: (i, 0)),
        out_shape=jax.ShapeDtypeStruct((n, D_MODEL), F32),
        scratch_shapes=[pltpu.VMEM((tm, D_MODEL), BF16), pltpu.VMEM((tm, D_MODEL), F32)],
        compiler_params=_cparams(("arbitrary", "arbitrary")),
        name="dense_ffn",
    )(x2, norm_w, wg, wu, wd)


def _route_kernel(x_ref, nw_ref, wr_ref, h_ref, idx_ref, wgt_ref):
    x = x_ref[...]
    ms = jnp.mean(x * x, axis=-1, keepdims=True)
    h = x * lax.rsqrt(ms + RMS_EPS) * nw_ref[...]
    h_ref[...] = h.astype(BF16)
    logits = jnp.dot(h, wr_ref[...], preferred_element_type=F32, precision=lax.Precision.HIGHEST)
    eid = lax.broadcasted_iota(jnp.int32, logits.shape, 1).astype(F32)
    logits = jnp.where(eid < N_EXPERTS, logits, -jnp.inf)
    m1 = jnp.max(logits, axis=-1, keepdims=True)
    i1 = jnp.min(jnp.where(logits == m1, eid, float(LANES)), axis=-1, keepdims=True)
    rest = jnp.where(eid == i1, -jnp.inf, logits)
    m2 = jnp.max(rest, axis=-1, keepdims=True)
    i2 = jnp.min(jnp.where(rest == m2, eid, float(LANES)), axis=-1, keepdims=True)
    e2 = jnp.exp(m2 - m1)
    denom = 1.0 + e2
    idx_ref[...] = jnp.where(eid == 0.0, i1, jnp.where(eid == 1.0, i2, 0.0)).astype(jnp.int32)
    wgt_ref[...] = jnp.where(eid == 0.0, 1.0 / denom, jnp.where(eid == 1.0, e2 / denom, 0.0))


def _route(x2, norm_w, w_router):
    n = x2.shape[0]
    tm = ROUTE_TM
    return pl.pallas_call(
        _route_kernel,
        grid=(n // tm,),
        in_specs=[
            pl.BlockSpec((tm, D_MODEL), lambda i: (i, 0)),
            pl.BlockSpec((1, D_MODEL), lambda i: (0, 0)),
            pl.BlockSpec((D_MODEL, LANES), lambda i: (0, 0)),
        ],
        out_specs=[
            pl.BlockSpec((tm, D_MODEL), lambda i: (i, 0)),
            pl.BlockSpec((tm, LANES), lambda i: (i, 0)),
            pl.BlockSpec((tm, LANES), lambda i: (i, 0)),
        ],
        out_shape=[
            jax.ShapeDtypeStruct((n, D_MODEL), BF16),
            jax.ShapeDtypeStruct((n, LANES), jnp.int32),
            jax.ShapeDtypeStruct((n, LANES), F32),
        ],
        compiler_params=_cparams(("arbitrary",)),
        name="route",
    )(x2, norm_w, jnp.pad(w_router, ((0, 0), (0, LANES - N_EXPERTS))))


def _gather_kernel(ids_ref, src_ref, o_ref, sem):
    tm = o_ref.shape[0]

    def row_copy(r):
        return pltpu.make_async_copy(src_ref.at[pl.ds(ids_ref[0, 0, r], 1)], o_ref.at[pl.ds(r, 1)], sem)

    def start(r, c):
        row_copy(r).start()
        return c

    def wait(r, c):
        row_copy(r).wait()
        return c

    lax.fori_loop(0, tm, start, 0)
    lax.fori_loop(0, tm, wait, 0)


def _gather_rows(ids, src):
    p = ids.shape[0]
    tm = GATHER_TM
    return pl.pallas_call(
        _gather_kernel,
        grid=(p // tm,),
        in_specs=[
            pl.BlockSpec((1, 1, tm), lambda i: (i, 0, 0), memory_space=pltpu.SMEM),
            pl.BlockSpec(memory_space=pl.ANY),
        ],
        out_specs=pl.BlockSpec((tm, src.shape[1]), lambda i: (i, 0)),
        out_shape=jax.ShapeDtypeStruct((p, src.shape[1]), src.dtype),
        scratch_shapes=[pltpu.SemaphoreType.DMA(())],
        compiler_params=_cparams(("arbitrary",)),
        name="gather_rows",
    )(ids.reshape(p // tm, 1, tm), src)


def _moe_ffn_kernel(te_ref, nv_ref, h_ref, wg_ref, wu_ref, wd_ref, o_ref, acc_scr):
    t = pl.program_id(0)
    c = pl.program_id(1)

    valid = t < nv_ref[0]
    last = c == pl.num_programs(1) - 1

    @pl.when(valid)
    def _():
        _swiglu_step(h_ref[...], wg_ref, wu_ref, wd_ref, acc_scr, c)

    @pl.when(valid & last)
    def _():
        o_ref[...] = acc_scr[...].astype(BF16)

    @pl.when(jnp.logical_not(valid) & last)
    def _():
        o_ref[...] = jnp.zeros_like(o_ref)


def _moe_ffn(tile_expert, n_valid, hs, wg, wu, wd):
    p = hs.shape[0]
    tm, tf = FFN_TM, FFN_TF
    grid_spec = pltpu.PrefetchScalarGridSpec(
        num_scalar_prefetch=2,
        grid=(p // tm, D_FF // tf),
        in_specs=[
            pl.BlockSpec((tm, D_MODEL), lambda t, c, te, nv: (t, 0)),
            pl.BlockSpec((None, D_MODEL, tf), lambda t, c, te, nv: (te[t], 0, c)),
            pl.BlockSpec((None, D_MODEL, tf), lambda t, c, te, nv: (te[t], 0, c)),
            pl.BlockSpec((None, tf, D_MODEL), lambda t, c, te, nv: (te[t], c, 0)),
        ],
        out_specs=pl.BlockSpec((tm, D_MODEL), lambda t, c, te, nv: (t, 0)),
        scratch_shapes=[pltpu.VMEM((tm, D_MODEL), F32)],
    )
    return pl.pallas_call(
        _moe_ffn_kernel,
        grid_spec=grid_spec,
        out_shape=jax.ShapeDtypeStruct((p, D_MODEL), BF16),
        compiler_params=_cparams(("arbitrary", "arbitrary")),
        name="moe_ffn",
    )(tile_expert, n_valid, hs, wg, wu, wd)


def _combine_kernel(x_ref, y0_ref, y1_ref, w_ref, nw_ref, o_ref, *, final_norm):
    w = w_ref[...]
    x = (x_ref[...] + w[:, 0:1] * y0_ref[...].astype(F32) + w[:, 1:2] * y1_ref[...].astype(F32))
    if final_norm:
        ms = jnp.mean(x * x, axis=-1, keepdims=True)
        x = x * lax.rsqrt(ms + RMS_EPS) * nw_ref[...]
    o_ref[...] = x


def _combine(x2, y_pairs, wgt, norm_w, final_norm):
    n = x2.shape[0]
    tm = MERGE_TM
    nt = n // tm
    return pl.pallas_call(
        functools.partial(_combine_kernel, final_norm=final_norm),
        grid=(nt,),
        in_specs=[
            pl.BlockSpec((tm, D_MODEL), lambda i: (i, 0)),
            pl.BlockSpec((tm, D_MODEL), lambda i: (i, 0)),
            pl.BlockSpec((tm, D_MODEL), lambda i: (i + nt, 0)),
            pl.BlockSpec((tm, LANES), lambda i: (i, 0)),
            pl.BlockSpec((1, D_MODEL), lambda i: (0, 0)),
        ],
        out_specs=pl.BlockSpec((tm, D_MODEL), lambda i: (i, 0)),
        out_shape=jax.ShapeDtypeStruct((n, D_MODEL), F32),
        compiler_params=_cparams(("arbitrary",)),
        name="combine",
    )(x2, y_pairs, y_pairs, wgt, norm_w)


def _moe_layout(idx):
    n = idx.shape[0]
    tm = FFN_TM
    flat = idx.T.reshape(-1)
    onehot = (flat[:, None] == jnp.arange(N_EXPERTS, dtype=jnp.int32)[None, :]).astype(jnp.int32)
    cum = jnp.cumsum(onehot, axis=0)
    counts = cum[-1]
    rank = jnp.sum(cum * onehot, axis=1) - 1
    padded = ((counts + tm - 1) // tm) * tm
    ends = jnp.cumsum(padded)
    starts = ends - padded
    dest = starts[flat] + rank
    p = TOP_K * n + N_EXPERTS * tm
    token = jnp.arange(TOP_K * n, dtype=jnp.int32) % n
    row_token = jnp.zeros((p,), jnp.int32).at[dest].set(token)
    tile_start = jnp.arange(p // tm, dtype=jnp.int32) * tm
    tile_expert = jnp.minimum(jnp.searchsorted(ends, tile_start, side="right"),
                              N_EXPERTS - 1).astype(jnp.int32)
    n_valid = (ends[-1] // tm).astype(jnp.int32).reshape(1)
    return row_token, tile_expert, n_valid, dest.astype(jnp.int32)


def _moe_all_kernel(x_ref, h_ref, idx_ref, wgt_ref, wg_ref, wu_ref, wd_ref, nw_ref, o_ref, acc_scr,
                    *, final_norm):
    e = pl.program_id(1)
    c = pl.program_id(2)
    idx = idx_ref[...]
    wgt = wgt_ref[...]
    cw = (jnp.where(idx[:, 0:1] == e, wgt[:, 0:1], 0.0)
          + jnp.where(idx[:, 1:2] == e, wgt[:, 1:2], 0.0))
    h = h_ref[...]
    g = jnp.dot(h, wg_ref[...], preferred_element_type=F32)
    u = jnp.dot(h, wu_ref[...], preferred_element_type=F32)
    a = (g * jax.nn.sigmoid(g) * u * cw).astype(BF16)
    part = jnp.dot(a, wd_ref[...], preferred_element_type=F32)
    first = (e == 0) & (c == 0)

    @pl.when(first)
    def _():
        acc_scr[...] = part

    @pl.when(jnp.logical_not(first))
    def _():
        acc_scr[...] += part

    @pl.when((e == pl.num_programs(1) - 1) & (c == pl.num_programs(2) - 1))
    def _():
        x = x_ref[...] + acc_scr[...]
        if final_norm:
            ms = jnp.mean(x * x, axis=-1, keepdims=True)
            x = x * lax.rsqrt(ms + RMS_EPS) * nw_ref[...]
        o_ref[...] = x


def _moe_all(x2, h, idx, wgt, wg, wu, wd, norm_w, final_norm):
    n = x2.shape[0]
    tm, tf = FFN_TM, FFN_TF
    return pl.pallas_call(
        functools.partial(_moe_all_kernel, final_norm=final_norm),
        grid=(n // tm, N_EXPERTS, D_FF // tf),
        in_specs=[
            pl.BlockSpec((tm, D_MODEL), lambda i, e, c: (i, 0)),
            pl.BlockSpec((tm, D_MODEL), lambda i, e, c: (i, 0)),
            pl.BlockSpec((tm, LANES), lambda i, e, c: (i, 0)),
            pl.BlockSpec((tm, LANES), lambda i, e, c: (i, 0)),
            pl.BlockSpec((None, D_MODEL, tf), lambda i, e, c: (e, 0, c)),
            pl.BlockSpec((None, D_MODEL, tf), lambda i, e, c: (e, 0, c)),
            pl.BlockSpec((None, tf, D_MODEL), lambda i, e, c: (e, c, 0)),
            pl.BlockSpec((1, D_MODEL), lambda i, e, c: (0, 0)),
        ],
        out_specs=pl.BlockSpec((tm, D_MODEL), lambda i, e, c: (i, 0)),
        out_shape=jax.ShapeDtypeStruct((n, D_MODEL), F32),
        scratch_shapes=[pltpu.VMEM((tm, D_MODEL), F32)],
        compiler_params=_cparams(("arbitrary", "arbitrary", "arbitrary")),
        name="moe_all",
    )(x2, h, idx, wgt, wg, wu, wd, norm_w)


def _moe(x2, norm_w, w_router, wg, wu, wd, final_norm_w, final_norm):
    h, idx, wgt = _route(x2, norm_w, w_router)
    return _moe_all(x2, h, idx, wgt, wg, wu, wd, final_norm_w, final_norm)


def _rope_tables(seq):
    inv_freq = 1.0 / (ROPE_THETA ** (jnp.arange(0, HEAD_DIM, 2, dtype=F32) / HEAD_DIM))
    ang = jnp.arange(seq, dtype=F32)[:, None] * inv_freq[None, :]
    cos, sin = jnp.cos(ang), jnp.sin(ang)
    cos_t = jnp.tile(cos, (1, LANES // (HEAD_DIM // 2)))
    sin_t = jnp.tile(jnp.concatenate([-sin, sin], axis=1), (1, LANES // HEAD_DIM))
    return cos_t, sin_t


def _reorder_w_in(w):
    head, rq_rk, rv_rg_gates = w[:, :3072], w[:, 3072:3584], w[:, 3584:]
    return jnp.concatenate([head, rv_rg_gates, rq_rk], axis=1).astype(BF16)


def kernel(x, mix_norm_w, w_in, conv_w, conv_b, ret_norm_w, w_br_attn, w_br_conv, w_br_ret, w_out,
           ffn_norm_w, dense_w_gate, dense_w_up, dense_w_down, moe_router, moe_w_gate, moe_w_up,
           moe_w_down, final_norm_w):
    batch, seq, d = x.shape
    depth = w_in.shape[0]
    assert d == D_MODEL and seq % PROJ_TM == 0 and depth % 2 == 0
    n = batch * seq
    cos_t, sin_t = _rope_tables(seq)
    ret_tables = _retention_tables()
    x2 = x.reshape(n, d)
    for layer in range(depth):
        proj, vt = _inproj(x2, mix_norm_w[layer].reshape(1, d), _reorder_w_in(w_in[layer]),
                           cos_t, sin_t, batch, seq)
        y_attn = _moba(proj, vt, batch, seq)
        y_ret = _retention(proj, ret_norm_w[layer].reshape(1, RET_V_WIDTH), ret_tables, batch, seq)
        x2 = _merge(x2, y_attn, proj, y_ret, conv_w[layer], conv_b[layer].reshape(1, CONV_WIDTH),
                    w_br_attn[layer].astype(BF16), w_br_conv[layer].astype(BF16),
                    w_br_ret[layer].astype(BF16), w_out[layer].astype(BF16), seq)
        i = layer // 2
        nw = ffn_norm_w[layer].reshape(1, d)
        if layer % 2 == 0:
            x2 = _dense_ffn(x2, nw, dense_w_gate[i].astype(BF16), dense_w_up[i].astype(BF16),
                            dense_w_down[i].astype(BF16))
        else:
            last = layer == depth - 1
            x2 = _moe(x2, nw, moe_router[i], moe_w_gate[i].astype(BF16), moe_w_up[i].astype(BF16),
                      moe_w_down[i].astype(BF16), final_norm_w.reshape(1, d), last)
    return x2.reshape(batch, seq, d)
```

```python
import functools
import math

import jax
import jax.numpy as jnp
import numpy as np
from jax import lax
from jax.experimental import pallas as pl
from jax.experimental.pallas import tpu as pltpu

F32 = jnp.float32
BF16 = jnp.bfloat16

D_MODEL = 1024
HEAD_DIM = 64
ROPE_THETA = 10000.0
RMS_EPS = 1e-6

ATTN_HEADS = 8
ATTN_WIDTH = ATTN_HEADS * HEAD_DIM
MOBA_BLOCK = 256
MOBA_TOPK = 3
NEG_INF = -1e30

CONV_WIDTH = 512
CONV_KERNEL = 3

RET_HEADS = 4
RET_QK_WIDTH = RET_HEADS * HEAD_DIM
RET_V_DIM = 2 * HEAD_DIM
RET_V_WIDTH = RET_HEADS * RET_V_DIM
RET_TILE = 256

D_FF = 3584
N_EXPERTS = 8
TOP_K = 2

LANES = 128
IN_PROJ_WIDTH = 7680
COL_AQ, COL_AK, COL_AV = 0, 512, 1024
COL_CB, COL_CC, COL_CH = 1536, 2048, 2560
COL_RV, COL_RG = 3072, 3584
COL_GATES = 4096
COL_RQ, COL_RK = 7168, 7424

PROJ_TM = 1024
PROJ_TN = 512
MERGE_TM = 512
FFN_TM = 1024
FFN_TF = 512
ROUTE_TM = 1024
GATHER_TM = 512
VMEM_LIMIT = 56 * 1024 * 1024


def _cparams(sem):
    return pltpu.CompilerParams(dimension_semantics=sem, vmem_limit_bytes=VMEM_LIMIT)


def _rope(acc, cos, sin_signed):
    lane = lax.broadcasted_iota(jnp.int32, (1, LANES), 1)
    first_half = (lane % HEAD_DIM) < (HEAD_DIM // 2)
    outs = []
    for g in range(acc.shape[1] // LANES):
        blk = acc[:, g * LANES:(g + 1) * LANES]
        partner = jnp.where(first_half,
                            pltpu.roll(blk, LANES - HEAD_DIM // 2, 1),
                            pltpu.roll(blk, HEAD_DIM // 2, 1))
        outs.append(blk * cos + partner * sin_signed)
    return jnp.concatenate(outs, axis=1)


def _inproj_kernel(x_ref, nw_ref, w_ref, cos_ref, sin_ref, proj_ref, vt_ref, h_scr):
    j = pl.program_id(1)

    @pl.when(j == 0)
    def _():
        x = x_ref[...]
        ms = jnp.mean(x * x, axis=-1, keepdims=True)
        h_scr[...] = (x * lax.rsqrt(ms + RMS_EPS) * nw_ref[...]).astype(BF16)

    acc = jnp.dot(h_scr[...], w_ref[...], preferred_element_type=F32)
    jq, jk, jv, jr = COL_AQ // PROJ_TN, COL_AK // PROJ_TN, COL_AV // PROJ_TN, COL_RQ // PROJ_TN
    scale = HEAD_DIM ** -0.5

    @pl.when(j == jq)
    def _():
        proj_ref[...] = (_rope(acc, cos_ref[...], sin_ref[...]) * scale).astype(BF16)

    @pl.when(j == jk)
    def _():
        proj_ref[...] = _rope(acc, cos_ref[...], sin_ref[...]).astype(BF16)

    @pl.when(j == jv)
    def _():
        proj_ref[...] = acc.astype(BF16)
        for c in range(PROJ_TM // MOBA_BLOCK):
            vt_ref[c] = acc[c * MOBA_BLOCK:(c + 1) * MOBA_BLOCK, :].T.astype(BF16)

    @pl.when(j == jr)
    def _():
        r = _rope(acc, cos_ref[...], sin_ref[...])
        col = lax.broadcasted_iota(jnp.int32, (1, PROJ_TN), 1)
        r = r * jnp.where(col >= RET_QK_WIDTH, scale, 1.0)
        proj_ref[...] = r.astype(BF16)

    @pl.when((j != jq) & (j != jk) & (j != jv) & (j != jr))
    def _():
        proj_ref[...] = acc.astype(BF16)


def _inproj(x2, norm_w, w_bf, cos_t, sin_t, batch, seq):
    n = x2.shape[0]
    nst = seq // PROJ_TM
    nblk = PROJ_TM // MOBA_BLOCK
    return pl.pallas_call(
        _inproj_kernel,
        grid=(n // PROJ_TM, IN_PROJ_WIDTH // PROJ_TN),
        in_specs=[
            pl.BlockSpec((PROJ_TM, D_MODEL), lambda i, j: (i, 0)),
            pl.BlockSpec((1, D_MODEL), lambda i, j: (0, 0)),
            pl.BlockSpec((D_MODEL, PROJ_TN), lambda i, j: (0, j)),
            pl.BlockSpec((PROJ_TM, LANES), lambda i, j: (i % nst, 0)),
            pl.BlockSpec((PROJ_TM, LANES), lambda i, j: (i % nst, 0)),
        ],
        out_specs=[
            pl.BlockSpec((PROJ_TM, PROJ_TN), lambda i, j: (i, j)),
            pl.BlockSpec((None, nblk, ATTN_WIDTH, MOBA_BLOCK), lambda i, j: (i // nst, i % nst, 0, 0)),
        ],
        out_shape=[
            jax.ShapeDtypeStruct((n, IN_PROJ_WIDTH), BF16),
            jax.ShapeDtypeStruct((batch, seq // MOBA_BLOCK, ATTN_WIDTH, MOBA_BLOCK), BF16),
        ],
        scratch_shapes=[pltpu.VMEM((PROJ_TM, D_MODEL), BF16)],
        compiler_params=_cparams(("arbitrary", "arbitrary")),
        name="inproj",
    )(x2, norm_w, w_bf, cos_t, sin_t)


_NT = (((1,), (1,)), ((), ()))


def _moba_kernel(q_ref, k_ref, vt_ref, o_ref, km_scr, sel_scr, qm_scr, *, nb):
    blk = MOBA_BLOCK
    lane = lax.broadcasted_iota(jnp.int32, (1, LANES), 1)

    def kmean_body(i, c):
        kb = k_ref[pl.ds(pl.multiple_of(i * blk, blk), blk), :].astype(F32)
        km_scr[pl.ds(i, 1), :] = jnp.sum(kb, axis=0, keepdims=True) * (1.0 / blk)
        return c

    lax.fori_loop(0, nb, kmean_body, 0)

    tq = 2 * blk
    key_pos = lax.broadcasted_iota(jnp.int32, (tq, tq), 0)
    qry_pos = lax.broadcasted_iota(jnp.int32, (tq, tq), 1)
    own_causal = ((key_pos >= blk) == (qry_pos >= blk)) & (key_pos <= qry_pos)
    prev_blk = (key_pos < blk) & (qry_pos >= blk)
    blk_id = lax.broadcasted_iota(jnp.int32, (nb, tq), 0)
    q_half = (lax.broadcasted_iota(jnp.int32, (nb, tq), 1) >= blk).astype(jnp.int32)

    def pv(t, hh, p):
        rows = slice(hh * HEAD_DIM, (hh + 1) * HEAD_DIM)
        pb = p.astype(BF16)
        return (jnp.dot(vt_ref[2 * t, rows, :], pb[0:blk], preferred_element_type=F32)
                + jnp.dot(vt_ref[2 * t + 1, rows, :], pb[blk:tq], preferred_element_type=F32))

    def qtile(jt, c):
        row0 = pl.multiple_of(jt * tq, tq)
        q = q_ref[pl.ds(row0, tq), :]
        k_diag = k_ref[pl.ds(row0, tq), :]
        init = []
        for hh in range(2):
            hmask = (lane // HEAD_DIM) == hh
            qm = jnp.where(hmask, q, jnp.zeros_like(q))
            qm_scr[hh] = qm
            km = jnp.where(hmask, km_scr[...], 0.0)
            km_hi = km.astype(BF16)
            km_lo = (km - km_hi.astype(F32)).astype(BF16)
            gate = (lax.dot_general(km_hi, qm, _NT, preferred_element_type=F32)
                    + lax.dot_general(km_lo, qm, _NT, preferred_element_type=F32))
            past = blk_id < 2 * jt + q_half
            gate = jnp.where(past, gate, -jnp.inf)
            rank = jnp.zeros((nb, tq), jnp.int32)
            for ip in range(nb):
                gi = gate[ip:ip + 1, :]
                beats = (gi > gate) | ((gi == gate) & (blk_id > ip))
                rank = rank + beats.astype(jnp.int32)
            sel_scr[hh] = ((rank < MOBA_TOPK) & past).astype(F32)

            s = lax.dot_general(k_diag, qm, _NT, preferred_element_type=F32)
            allowed = own_causal | (prev_blk & (sel_scr[hh, pl.ds(2 * jt, 1), :] > 0.0))
            s = jnp.where(allowed, s, NEG_INF)
            m0 = jnp.max(s, axis=0, keepdims=True)
            p = jnp.exp(s - m0)
            init += [m0, jnp.sum(p, axis=0, keepdims=True), pv(jt, hh, p)]

        def kv_body(it, carry):
            k_t = k_ref[pl.ds(pl.multiple_of(it * tq, tq), tq), :]
            new = []
            for hh in range(2):
                m, l, acc = carry[3 * hh:3 * hh + 3]
                s = lax.dot_general(k_t, qm_scr[hh], _NT, preferred_element_type=F32)
                s = jnp.concatenate([
                    jnp.where(sel_scr[hh, pl.ds(2 * it, 1), :] > 0.0, s[0:blk], NEG_INF),
                    jnp.where(sel_scr[hh, pl.ds(2 * it + 1, 1), :] > 0.0, s[blk:tq], NEG_INF)], axis=0)
                m_new = jnp.maximum(m, jnp.max(s, axis=0, keepdims=True))
                alpha = jnp.exp(m - m_new)
                p = jnp.exp(s - m_new)
                l = alpha * l + jnp.sum(p, axis=0, keepdims=True)
                acc = alpha * acc + pv(it, hh, p)
                new += [m_new, l, acc]
            return tuple(new)

        fin = lax.fori_loop(0, jt, kv_body, tuple(init))
        out_t = jnp.concatenate([fin[2] / fin[1], fin[5] / fin[4]], axis=0)
        o_ref[pl.ds(row0, tq), :] = out_t.T.astype(BF16)
        return c

    lax.fori_loop(0, nb // 2, qtile, 0)


def _moba(proj, vt, batch, seq):
    nb = seq // MOBA_BLOCK
    npair = ATTN_WIDTH // LANES
    return pl.pallas_call(
        functools.partial(_moba_kernel, nb=nb),
        grid=(batch, npair),
        in_specs=[
            pl.BlockSpec((seq, LANES), lambda b, p: (b, COL_AQ // LANES + p)),
            pl.BlockSpec((seq, LANES), lambda b, p: (b, COL_AK // LANES + p)),
            pl.BlockSpec((None, nb, LANES, MOBA_BLOCK), lambda b, p: (b, 0, p, 0)),
        ],
        out_specs=pl.BlockSpec((seq, LANES), lambda b, p: (b, p)),
        out_shape=jax.ShapeDtypeStruct((batch * seq, ATTN_WIDTH), BF16),
        scratch_shapes=[pltpu.VMEM((nb, LANES), F32), pltpu.VMEM((2, nb, 2 * MOBA_BLOCK), F32),
                        pltpu.VMEM((2, 2 * MOBA_BLOCK, LANES), BF16)],
        compiler_params=_cparams(("arbitrary", "arbitrary")),
        name="moba",
    )(proj, proj, vt)


def _ret_log_gamma():
    return [math.log1p(-(2.0 ** (-5.0 - h))) for h in range(RET_HEADS)]


def _retention_tables():
    c = RET_TILE
    lg = np.array(_ret_log_gamma(), np.float64)
    n = np.arange(c, dtype=np.float64)
    diff = n[:, None] - n[None, :]
    decay = np.where(diff[None] >= 0, np.exp(np.maximum(diff, 0.0)[None] * lg[:, None, None]), 0.0)
    head_of_lane = np.arange(RET_QK_WIDTH) // HEAD_DIM
    xi = np.exp((n + 1.0)[:, None] * lg[head_of_lane][None, :])
    zeta = np.exp((c - 1.0 - n)[:, None] * lg[head_of_lane][None, :])
    chunk_decay = np.exp(c * lg[head_of_lane])[:, None]
    return (jnp.asarray(decay, F32), jnp.asarray(xi, F32), jnp.asarray(zeta, F32),
            jnp.asarray(np.broadcast_to(chunk_decay, (RET_QK_WIDTH, LANES)), F32))


def _retention_kernel(q_ref, k_ref, v_ref, g_ref, nw_ref, decay_ref, xi_ref, zeta_ref, cd_ref,
                      o_ref, state_scr):
    @pl.when(pl.program_id(1) == 0)
    def _():
        state_scr[...] = jnp.zeros_like(state_scr)

    lane = lax.broadcasted_iota(jnp.int32, (1, LANES), 1)
    srow = lax.broadcasted_iota(jnp.int32, (LANES, 1), 0)
    for pr in range(RET_HEADS // 2):
        cols = slice(pr * LANES, (pr + 1) * LANES)
        q = q_ref[:, cols]
        k = k_ref[:, cols]
        state = state_scr[cols, :]
        state_bf = state.astype(BF16)
        q_xi = (q.astype(F32) * xi_ref[:, cols]).astype(BF16)
        kz_t = (k.astype(F32) * zeta_ref[:, cols]).T.astype(BF16)
        upd = []
        for hh in range(2):
            h = 2 * pr + hh
            hmask = (lane // HEAD_DIM) == hh
            vcols = slice(h * RET_V_DIM, (h + 1) * RET_V_DIM)
            v = v_ref[:, vcols]
            qm = jnp.where(hmask, q, jnp.zeros_like(q))
            scores = lax.dot_general(qm, k, _NT, preferred_element_type=F32) * decay_ref[h]
            o = jnp.dot(scores.astype(BF16), v, preferred_element_type=F32)
            o = o + jnp.dot(jnp.where(hmask, q_xi, jnp.zeros_like(q_xi)), state_bf,
                            preferred_element_type=F32)
            upd.append(jnp.dot(kz_t, v, preferred_element_type=F32))
            ms = jnp.mean(o * o, axis=-1, keepdims=True)
            y = o * lax.rsqrt(ms + RMS_EPS) * nw_ref[:, vcols]
            g = g_ref[:, vcols].astype(F32)
            o_ref[:, vcols] = (g * jax.nn.sigmoid(g) * y).astype(BF16)
        state_scr[cols, :] = state * cd_ref[cols, :] + jnp.where(srow < HEAD_DIM, upd[0], upd[1])


def _retention(proj, ret_norm_w, tables, batch, seq):
    nc = seq // RET_TILE
    decay, xi, zeta, cd = tables
    row = lambda b, c: b * nc + c
    const2 = lambda b, c: (0, 0)
    return pl.pallas_call(
        _retention_kernel,
        grid=(batch, nc),
        in_specs=[
            pl.BlockSpec((RET_TILE, RET_QK_WIDTH), lambda b, c: (row(b, c), COL_RQ // RET_QK_WIDTH)),
            pl.BlockSpec((RET_TILE, RET_QK_WIDTH), lambda b, c: (row(b, c), COL_RK // RET_QK_WIDTH)),
            pl.BlockSpec((RET_TILE, RET_V_WIDTH), lambda b, c: (row(b, c), COL_RV // RET_V_WIDTH)),
            pl.BlockSpec((RET_TILE, RET_V_WIDTH), lambda b, c: (row(b, c), COL_RG // RET_V_WIDTH)),
            pl.BlockSpec((1, RET_V_WIDTH), const2),
            pl.BlockSpec((RET_HEADS, RET_TILE, RET_TILE), lambda b, c: (0, 0, 0)),
            pl.BlockSpec((RET_TILE, RET_QK_WIDTH), const2),
            pl.BlockSpec((RET_TILE, RET_QK_WIDTH), const2),
            pl.BlockSpec((RET_QK_WIDTH, LANES), const2),
        ],
        out_specs=pl.BlockSpec((RET_TILE, RET_V_WIDTH), lambda b, c: (row(b, c), 0)),
        out_shape=jax.ShapeDtypeStruct((batch * seq, RET_V_WIDTH), BF16),
        scratch_shapes=[pltpu.VMEM((RET_QK_WIDTH, RET_V_DIM), F32)],
        compiler_params=_cparams(("arbitrary", "arbitrary")),
        name="retention",
    )(proj, proj, proj, proj, ret_norm_w, decay, xi, zeta, cd)


CONV_HALO = 8


def _merge_kernel(x_ref, ya_ref, cb_ref, cc_ref, ch_ref, yr_ref, ga_ref, gc_ref, gr_ref, cw_ref,
                  cbias_ref, wa_ref, wc_ref, wr_ref, wo_ref, o_ref, u_scr, *, tiles_per_seq):
    i = pl.program_id(0)
    tm = x_ref.shape[0]

    @pl.when(i % tiles_per_seq == 0)
    def _():
        u_scr[0:CONV_HALO, :] = jnp.zeros((CONV_HALO, CONV_WIDTH), F32)

    u_scr[CONV_HALO:CONV_HALO + tm, :] = cc_ref[...].astype(F32) * ch_ref[...].astype(F32)
    conv = (cw_ref[2:3, :] * u_scr[CONV_HALO:CONV_HALO + tm, :]
            + cw_ref[1:2, :] * u_scr[CONV_HALO - 1:CONV_HALO - 1 + tm, :]
            + cw_ref[0:1, :] * u_scr[CONV_HALO - 2:CONV_HALO - 2 + tm, :]
            + cbias_ref[...])
    y_conv = (cb_ref[...].astype(F32) * conv).astype(BF16)
    u_scr[0:CONV_HALO, :] = u_scr[tm:tm + CONV_HALO, :]

    def gate(g_ref):
        return jax.nn.sigmoid(g_ref[...].astype(F32))

    merged = gate(ga_ref) * jnp.dot(ya_ref[...], wa_ref[...], preferred_element_type=F32)
    merged = merged + gate(gc_ref) * jnp.dot(y_conv, wc_ref[...], preferred_element_type=F32)
    merged = merged + gate(gr_ref) * jnp.dot(yr_ref[...], wr_ref[...], preferred_element_type=F32)
    o_ref[...] = x_ref[...] + jnp.dot(merged.astype(BF16), wo_ref[...], preferred_element_type=F32)


def _merge(x2, y_attn, proj, y_ret, conv_w, conv_b, wa, wc, wr, wo, seq):
    n = x2.shape[0]
    tm = MERGE_TM
    const = lambda i: (0, 0)
    wide = lambda c: pl.BlockSpec((tm, 512), lambda i: (i, c // 512))
    gate_spec = lambda b: pl.BlockSpec((tm, D_MODEL), lambda i: (i, COL_GATES // D_MODEL + b))
    return pl.pallas_call(
        functools.partial(_merge_kernel, tiles_per_seq=seq // tm),
        grid=(n // tm,),
        in_specs=[
            pl.BlockSpec((tm, D_MODEL), lambda i: (i, 0)),
            pl.BlockSpec((tm, ATTN_WIDTH), lambda i: (i, 0)),
            wide(COL_CB), wide(COL_CC), wide(COL_CH),
            pl.BlockSpec((tm, RET_V_WIDTH), lambda i: (i, 0)),
            gate_spec(0), gate_spec(1), gate_spec(2),
            pl.BlockSpec((CONV_KERNEL, CONV_WIDTH), const),
            pl.BlockSpec((1, CONV_WIDTH), const),
            pl.BlockSpec((ATTN_WIDTH, D_MODEL), const),
            pl.BlockSpec((CONV_WIDTH, D_MODEL), const),
            pl.BlockSpec((RET_V_WIDTH, D_MODEL), const),
            pl.BlockSpec((D_MODEL, D_MODEL), const),
        ],
        out_specs=pl.BlockSpec((tm, D_MODEL), lambda i: (i, 0)),
        out_shape=jax.ShapeDtypeStruct((n, D_MODEL), F32),
        scratch_shapes=[pltpu.VMEM((tm + CONV_HALO, CONV_WIDTH), F32)],
        compiler_params=_cparams(("arbitrary",)),
        name="merge",
    )(x2, y_attn, proj, proj, proj, y_ret, proj, proj, proj, conv_w, conv_b, wa, wc, wr, wo)


def _swiglu_step(h, wg_ref, wu_ref, wd_ref, acc_scr, c):
    g = jnp.dot(h, wg_ref[...], preferred_element_type=F32)
    u = jnp.dot(h, wu_ref[...], preferred_element_type=F32)
    a = (g * jax.nn.sigmoid(g) * u).astype(BF16)
    part = jnp.dot(a, wd_ref[...], preferred_element_type=F32)

    @pl.when(c == 0)
    def _():
        acc_scr[...] = part

    @pl.when(c != 0)
    def _():
        acc_scr[...] += part


def _dense_ffn_kernel(x_ref, nw_ref, wg_ref, wu_ref, wd_ref, o_ref, h_scr, acc_scr):
    c = pl.program_id(1)

    @pl.when(c == 0)
    def _():
        x = x_ref[...]
        ms = jnp.mean(x * x, axis=-1, keepdims=True)
        h_scr[...] = (x * lax.rsqrt(ms + RMS_EPS) * nw_ref[...]).astype(BF16)

    _swiglu_step(h_scr[...], wg_ref, wu_ref, wd_ref, acc_scr, c)

    @pl.when(c == pl.num_programs(1) - 1)
    def _():
        o_ref[...] = x_ref[...] + acc_scr[...]


def _dense_ffn(x2, norm_w, wg, wu, wd):
    n = x2.shape[0]
    tm, tf = FFN_TM, FFN_TF
    return pl.pallas_call(
        _dense_ffn_kernel,
        grid=(n // tm, D_FF // tf),
        in_specs=[
            pl.BlockSpec((tm, D_MODEL), lambda i, c: (i, 0)),
            pl.BlockSpec((1, D_MODEL), lambda i, c: (0, 0)),
            pl.BlockSpec((D_MODEL, tf), lambda i, c: (0, c)),
            pl.BlockSpec((D_MODEL, tf), lambda i, c: (0, c)),
            pl.BlockSpec((tf, D_MODEL), lambda i, c: (c, 0)),
        ],
        out_specs=pl.BlockSpec((tm, D_MODEL), lambda i, c: (i, 0)),
        out_shape=jax.ShapeDtypeStruct((n, D_MODEL), F32),
        scratch_shapes=[pltpu.VMEM((tm, D_MODEL), BF16), pltpu.VMEM((tm, D_MODEL), F32)],
        compiler_params=_cparams(("arbitrary", "arbitrary")),
        name="dense_ffn",
    )(x2, norm_w, wg, wu, wd)


def _route_kernel(x_ref, nw_ref, wr_ref, h_ref, idx_ref, wgt_ref):
    x = x_ref[...]
    ms = jnp.mean(x * x, axis=-1, keepdims=True)
    h = x * lax.rsqrt(ms + RMS_EPS) * nw_ref[...]
    h_ref[...] = h.astype(BF16)
    logits = jnp.dot(h, wr_ref[...], preferred_element_type=F32, precision=lax.Precision.HIGHEST)
    eid = lax.broadcasted_iota(jnp.int32, logits.shape, 1).astype(F32)
    logits = jnp.where(eid < N_EXPERTS, logits, -jnp.inf)
    m1 = jnp.max(logits, axis=-1, keepdims=True)
    i1 = jnp.min(jnp.where(logits == m1, eid, float(LANES)), axis=-1, keepdims=True)
    rest = jnp.where(eid == i1, -jnp.inf, logits)
    m2 = jnp.max(rest, axis=-1, keepdims=True)
    i2 = jnp.min(jnp.where(rest == m2, eid, float(LANES)), axis=-1, keepdims=True)
    e2 = jnp.exp(m2 - m1)
    denom = 1.0 + e2
    idx_ref[...] = jnp.where(eid == 0.0, i1, jnp.where(eid == 1.0, i2, 0.0)).astype(jnp.int32)
    wgt_ref[...] = jnp.where(eid == 0.0, 1.0 / denom, jnp.where(eid == 1.0, e2 / denom, 0.0))


def _route(x2, norm_w, w_router):
    n = x2.shape[0]
    tm = ROUTE_TM
    return pl.pallas_call(
        _route_kernel,
        grid=(n // tm,),
        in_specs=[
            pl.BlockSpec((tm, D_MODEL), lambda i: (i, 0)),
            pl.BlockSpec((1, D_MODEL), lambda i: (0, 0)),
            pl.BlockSpec((D_MODEL, LANES), lambda i: (0, 0)),
        ],
        out_specs=[
            pl.BlockSpec((tm, D_MODEL), lambda i: (i, 0)),
            pl.BlockSpec((tm, LANES), lambda i: (i, 0)),
            pl.BlockSpec((tm, LANES), lambda i: (i, 0)),
        ],
        out_shape=[
            jax.ShapeDtypeStruct((n, D_MODEL), BF16),
            jax.ShapeDtypeStruct((n, LANES), jnp.int32),
            jax.ShapeDtypeStruct((n, LANES), F32),
        ],
        compiler_params=_cparams(("arbitrary",)),
        name="route",
    )(x2, norm_w, jnp.pad(w_router, ((0, 0), (0, LANES - N_EXPERTS))))


def _gather_kernel(ids_ref, src_ref, o_ref, sem):
    tm = o_ref.shape[0]

    def row_copy(r):
        return pltpu.make_async_copy(src_ref.at[pl.ds(ids_ref[0, 0, r], 1)], o_ref.at[pl.ds(r, 1)], sem)

    def start(r, c):
        row_copy(r).start()
        return c

    def wait(r, c):
        row_copy(r).wait()
        return c

    lax.fori_loop(0, tm, start, 0)
    lax.fori_loop(0, tm, wait, 0)


def _gather_rows(ids, src):
    p = ids.shape[0]
    tm = GATHER_TM
    return pl.pallas_call(
        _gather_kernel,
        grid=(p // tm,),
        in_specs=[
            pl.BlockSpec((1, 1, tm), lambda i: (i, 0, 0), memory_space=pltpu.SMEM),
            pl.BlockSpec(memory_space=pl.ANY),
        ],
        out_specs=pl.BlockSpec((tm, src.shape[1]), lambda i: (i, 0)),
        out_shape=jax.ShapeDtypeStruct((p, src.shape[1]), src.dtype),
        scratch_shapes=[pltpu.SemaphoreType.DMA(())],
        compiler_params=_cparams(("arbitrary",)),
        name="gather_rows",
    )(ids.reshape(p // tm, 1, tm), src)


def _moe_ffn_kernel(te_ref, nv_ref, h_ref, wg_ref, wu_ref, wd_ref, o_ref, acc_scr):
    t = pl.program_id(0)
    c = pl.program_id(1)

    valid = t < nv_ref[0]
    last = c == pl.num_programs(1) - 1

    @pl.when(valid)
    def _():
        _swiglu_step(h_ref[...], wg_ref, wu_ref, wd_ref, acc_scr, c)

    @pl.when(valid & last)
    def _():
        o_ref[...] = acc_scr[...].astype(BF16)

    @pl.when(jnp.logical_not(valid) & last)
    def _():
        o_ref[...] = jnp.zeros_like(o_ref)


def _moe_ffn(tile_expert, n_valid, hs, wg, wu, wd):
    p = hs.shape[0]
    tm, tf = FFN_TM, FFN_TF
    grid_spec = pltpu.PrefetchScalarGridSpec(
        num_scalar_prefetch=2,
        grid=(p // tm, D_FF // tf),
        in_specs=[
            pl.BlockSpec((tm, D_MODEL), lambda t, c, te, nv: (t, 0)),
            pl.BlockSpec((None, D_MODEL, tf), lambda t, c, te, nv: (te[t], 0, c)),
            pl.BlockSpec((None, D_MODEL, tf), lambda t, c, te, nv: (te[t], 0, c)),
            pl.BlockSpec((None, tf, D_MODEL), lambda t, c, te, nv: (te[t], c, 0)),
        ],
        out_specs=pl.BlockSpec((tm, D_MODEL), lambda t, c, te, nv: (t, 0)),
        scratch_shapes=[pltpu.VMEM((tm, D_MODEL), F32)],
    )
    return pl.pallas_call(
        _moe_ffn_kernel,
        grid_spec=grid_spec,
        out_shape=jax.ShapeDtypeStruct((p, D_MODEL), BF16),
        compiler_params=_cparams(("arbitrary", "arbitrary")),
        name="moe_ffn",
    )(tile_expert, n_valid, hs, wg, wu, wd)


def _combine_kernel(x_ref, y0_ref, y1_ref, w_ref, nw_ref, o_ref, *, final_norm):
    w = w_ref[...]
    x = (x_ref[...] + w[:, 0:1] * y0_ref[...].astype(F32) + w[:, 1:2] * y1_ref[...].astype(F32))
    if final_norm:
        ms = jnp.mean(x * x, axis=-1, keepdims=True)
        x = x * lax.rsqrt(ms + RMS_EPS) * nw_ref[...]
    o_ref[...] = x


def _combine(x2, y_pairs, wgt, norm_w, final_norm):
    n = x2.shape[0]
    tm = MERGE_TM
    nt = n // tm
    return pl.pallas_call(
        functools.partial(_combine_kernel, final_norm=final_norm),
        grid=(nt,),
        in_specs=[
            pl.BlockSpec((tm, D_MODEL), lambda i: (i, 0)),
            pl.BlockSpec((tm, D_MODEL), lambda i: (i, 0)),
            pl.BlockSpec((tm, D_MODEL), lambda i: (i + nt, 0)),
            pl.BlockSpec((tm, LANES), lambda i: (i, 0)),
            pl.BlockSpec((1, D_MODEL), lambda i: (0, 0)),
        ],
        out_specs=pl.BlockSpec((tm, D_MODEL), lambda i: (i, 0)),
        out_shape=jax.ShapeDtypeStruct((n, D_MODEL), F32),
        compiler_params=_cparams(("arbitrary",)),
        name="combine",
    )(x2, y_pairs, y_pairs, wgt, norm_w)


def _moe_layout(idx):
    n = idx.shape[0]
    tm = FFN_TM
    flat = idx.T.reshape(-1)
    onehot = (flat[:, None] == jnp.arange(N_EXPERTS, dtype=jnp.int32)[None, :]).astype(jnp.int32)
    cum = jnp.cumsum(onehot, axis=0)
    counts = cum[-1]
    rank = jnp.sum(cum * onehot, axis=1) - 1
    padded = ((counts + tm - 1) // tm) * tm
    ends = jnp.cumsum(padded)
    starts = ends - padded
    dest = starts[flat] + rank
    p = TOP_K * n + N_EXPERTS * tm
    token = jnp.arange(TOP_K * n, dtype=jnp.int32) % n
    row_token = jnp.zeros((p,), jnp.int32).at[dest].set(token)
    tile_start = jnp.arange(p // tm, dtype=jnp.int32) * tm
    tile_expert = jnp.minimum(jnp.searchsorted(ends, tile_start, side="right"),
                              N_EXPERTS - 1).astype(jnp.int32)
    n_valid = (ends[-1] // tm).astype(jnp.int32).reshape(1)
    return row_token, tile_expert, n_valid, dest.astype(jnp.int32)


def _moe_all_kernel(x_ref, h_ref, idx_ref, wgt_ref, wg_ref, wu_ref, wd_ref, nw_ref, o_ref, acc_scr,
                    *, final_norm):
    e = pl.program_id(1)
    c = pl.program_id(2)
    idx = idx_ref[...]
    wgt = wgt_ref[...]
    cw = (jnp.where(idx[:, 0:1] == e, wgt[:, 0:1], 0.0)
          + jnp.where(idx[:, 1:2] == e, wgt[:, 1:2], 0.0))
    h = h_ref[...]
    g = jnp.dot(h, wg_ref[...], preferred_element_type=F32)
    u = jnp.dot(h, wu_ref[...], preferred_element_type=F32)
    a = (g * jax.nn.sigmoid(g) * u * cw).astype(BF16)
    part = jnp.dot(a, wd_ref[...], preferred_element_type=F32)
    first = (e == 0) & (c == 0)

    @pl.when(first)
    def _():
        acc_scr[...] = part

    @pl.when(jnp.logical_not(first))
    def _():
        acc_scr[...] += part

    @pl.when((e == pl.num_programs(1) - 1) & (c == pl.num_programs(2) - 1))
    def _():
        x = x_ref[...] + acc_scr[...]
        if final_norm:
            ms = jnp.mean(x * x, axis=-1, keepdims=True)
            x = x * lax.rsqrt(ms + RMS_EPS) * nw_ref[...]
        o_ref[...] = x


def _moe_all(x2, h, idx, wgt, wg, wu, wd, norm_w, final_norm):
    n = x2.shape[0]
    tm, tf = FFN_TM, FFN_TF
    return pl.pallas_call(
        functools.partial(_moe_all_kernel, final_norm=final_norm),
        grid=(n // tm, N_EXPERTS, D_FF // tf),
        in_specs=[
            pl.BlockSpec((tm, D_MODEL), lambda i, e, c: (i, 0)),
            pl.BlockSpec((tm, D_MODEL), lambda i, e, c: (i, 0)),
            pl.BlockSpec((tm, LANES), lambda i, e, c: (i, 0)),
            pl.BlockSpec((tm, LANES), lambda i, e, c: (i, 0)),
            pl.BlockSpec((None, D_MODEL, tf), lambda i, e, c: (e, 0, c)),
            pl.BlockSpec((None, D_MODEL, tf), lambda i, e, c: (e, 0, c)),
            pl.BlockSpec((None, tf, D_MODEL), lambda i, e, c: (e, c, 0)),
            pl.BlockSpec((1, D_MODEL), lambda i, e, c: (0, 0)),
        ],
        out_specs=pl.BlockSpec((tm, D_MODEL), lambda i, e, c: (i, 0)),
        out_shape=jax.ShapeDtypeStruct((n, D_MODEL), F32),
        scratch_shapes=[pltpu.VMEM((tm, D_MODEL), F32)],
        compiler_params=_cparams(("arbitrary", "arbitrary", "arbitrary")),
        name="moe_all",
    )(x2, h, idx, wgt, wg, wu, wd, norm_w)


def _moe(x2, norm_w, w_router, wg, wu, wd, final_norm_w, final_norm):
    h, idx, wgt = _route(x2, norm_w, w_router)
    return _moe_all(x2, h, idx, wgt, wg, wu, wd, final_norm_w, final_norm)


def _rope_tables(seq):
    inv_freq = 1.0 / (ROPE_THETA ** (jnp.arange(0, HEAD_DIM, 2, dtype=F32) / HEAD_DIM))
    ang = jnp.arange(seq, dtype=F32)[:, None] * inv_freq[None, :]
    cos, sin = jnp.cos(ang), jnp.sin(ang)
    cos_t = jnp.tile(cos, (1, LANES // (HEAD_DIM // 2)))
    sin_t = jnp.tile(jnp.concatenate([-sin, sin], axis=1), (1, LANES // HEAD_DIM))
    return cos_t, sin_t


def _reorder_w_in(w):
    head, rq_rk, rv_rg_gates = w[:, :3072], w[:, 3072:3584], w[:, 3584:]
    return jnp.concatenate([head, rv_rg_gates, rq_rk], axis=1).astype(BF16)


def kernel(x, mix_norm_w, w_in, conv_w, conv_b, ret_norm_w, w_br_attn, w_br_conv, w_br_ret, w_out,
           ffn_norm_w, dense_w_gate, dense_w_up, dense_w_down, moe_router, moe_w_gate, moe_w_up,
           moe_w_down, final_norm_w):
    batch, seq, d = x.shape
    depth = w_in.shape[0]
    assert d == D_MODEL and seq % PROJ_TM == 0 and depth % 2 == 0
    n = batch * seq
    cos_t, sin_t = _rope_tables(seq)
    ret_tables = _retention_tables()
    x2 = x.reshape(n, d)
    for layer in range(depth):
        proj, vt = _inproj(x2, mix_norm_w[layer].reshape(1, d), _reorder_w_in(w_in[layer]),
                           cos_t, sin_t, batch, seq)
        y_attn = _moba(proj, vt, batch, seq)
        y_ret = _retention(proj, ret_norm_w[layer].reshape(1, RET_V_WIDTH), ret_tables, batch, seq)
        x2 = _merge(x2, y_attn, proj, y_ret, conv_w[layer], conv_b[layer].reshape(1, CONV_WIDTH),
                    w_br_attn[layer].astype(BF16), w_br_conv[layer].astype(BF16),
                    w_br_ret[layer].astype(BF16), w_out[layer].astype(BF16), seq)
        i = layer // 2
        nw = ffn_norm_w[layer].reshape(1, d)
        if layer % 2 == 0:
            x2 = _dense_ffn(x2, nw, dense_w_gate[i].astype(BF16), dense_w_up[i].astype(BF16),
                            dense_w_down[i].astype(BF16))
        else:
            last = layer == depth - 1
            x2 = _moe(x2, nw, moe_router[i], moe_w_gate[i].astype(BF16), moe_w_up[i].astype(BF16),
                      moe_w_down[i].astype(BF16), final_norm_w.reshape(1, d), last)
    return x2.reshape(batch, seq, d)
```

```python
import functools
import math

import jax
import jax.numpy as jnp
import numpy as np
from jax import lax
from jax.experimental import pallas as pl
from jax.experimental.pallas import tpu as pltpu

F32 = jnp.float32
BF16 = jnp.bfloat16

D_MODEL = 1024
HEAD_DIM = 64
ROPE_THETA = 10000.0
RMS_EPS = 1e-6

ATTN_HEADS = 8
ATTN_WIDTH = ATTN_HEADS * HEAD_DIM
MOBA_BLOCK = 256
MOBA_TOPK = 3
NEG_INF = -1e30

CONV_WIDTH = 512
CONV_KERNEL = 3

RET_HEADS = 4
RET_QK_WIDTH = RET_HEADS * HEAD_DIM
RET_V_DIM = 2 * HEAD_DIM
RET_V_WIDTH = RET_HEADS * RET_V_DIM
RET_TILE = 256

D_FF = 3584
N_EXPERTS = 8
TOP_K = 2

LANES = 128
IN_PROJ_WIDTH = 7680
COL_AQ, COL_AK, COL_AV = 0, 512, 1024
COL_CB, COL_CC, COL_CH = 1536, 2048, 2560
COL_RV, COL_RG = 3072, 3584
COL_GATES = 4096
COL_RQ, COL_RK = 7168, 7424

PROJ_TM = 1024
PROJ_TN = 512
MERGE_TM = 512
FFN_TM = 1024
FFN_TF = 512
ROUTE_TM = 1024
GATHER_TM = 512
VMEM_LIMIT = 56 * 1024 * 1024


def _cparams(sem):
    return pltpu.CompilerParams(dimension_semantics=sem, vmem_limit_bytes=VMEM_LIMIT)


def _rope(acc, cos, sin_signed):
    lane = lax.broadcasted_iota(jnp.int32, (1, LANES), 1)
    first_half = (lane % HEAD_DIM) < (HEAD_DIM // 2)
    outs = []
    for g in range(acc.shape[1] // LANES):
        blk = acc[:, g * LANES:(g + 1) * LANES]
        partner = jnp.where(first_half,
                            pltpu.roll(blk, LANES - HEAD_DIM // 2, 1),
                            pltpu.roll(blk, HEAD_DIM // 2, 1))
        outs.append(blk * cos + partner * sin_signed)
    return jnp.concatenate(outs, axis=1)


def _inproj_kernel(x_ref, nw_ref, w_ref, cos_ref, sin_ref, proj_ref, vt_ref, h_scr):
    j = pl.program_id(1)

    @pl.when(j == 0)
    def _():
        x = x_ref[...]
        ms = jnp.mean(x * x, axis=-1, keepdims=True)
        h_scr[...] = (x * lax.rsqrt(ms + RMS_EPS) * nw_ref[...]).astype(BF16)

    acc = jnp.dot(h_scr[...], w_ref[...], preferred_element_type=F32)
    jq, jk, jv, jr = COL_AQ // PROJ_TN, COL_AK // PROJ_TN, COL_AV // PROJ_TN, COL_RQ // PROJ_TN
    scale = HEAD_DIM ** -0.5

    @pl.when(j == jq)
    def _():
        proj_ref[...] = (_rope(acc, cos_ref[...], sin_ref[...]) * scale).astype(BF16)

    @pl.when(j == jk)
    def _():
        proj_ref[...] = _rope(acc, cos_ref[...], sin_ref[...]).astype(BF16)

    @pl.when(j == jv)
    def _():
        proj_ref[...] = acc.astype(BF16)
        for c in range(PROJ_TM // MOBA_BLOCK):
            vt_ref[c] = acc[c * MOBA_BLOCK:(c + 1) * MOBA_BLOCK, :].T.astype(BF16)

    @pl.when(j == jr)
    def _():
        r = _rope(acc, cos_ref[...], sin_ref[...])
        col = lax.broadcasted_iota(jnp.int32, (1, PROJ_TN), 1)
        r = r * jnp.where(col >= RET_QK_WIDTH, scale, 1.0)
        proj_ref[...] = r.astype(BF16)

    @pl.when((j != jq) & (j != jk) & (j != jv) & (j != jr))
    def _():
        proj_ref[...] = acc.astype(BF16)


def _inproj(x2, norm_w, w_bf, cos_t, sin_t, batch, seq):
    n = x2.shape[0]
    nst = seq // PROJ_TM
    nblk = PROJ_TM // MOBA_BLOCK
    return pl.pallas_call(
        _inproj_kernel,
        grid=(n // PROJ_TM, IN_PROJ_WIDTH // PROJ_TN),
        in_specs=[
            pl.BlockSpec((PROJ_TM, D_MODEL), lambda i, j: (i, 0)),
            pl.BlockSpec((1, D_MODEL), lambda i, j: (0, 0)),
            pl.BlockSpec((D_MODEL, PROJ_TN), lambda i, j: (0, j)),
            pl.BlockSpec((PROJ_TM, LANES), lambda i, j: (i % nst, 0)),
            pl.BlockSpec((PROJ_TM, LANES), lambda i, j: (i % nst, 0)),
        ],
        out_specs=[
            pl.BlockSpec((PROJ_TM, PROJ_TN), lambda i, j: (i, j)),
            pl.BlockSpec((None, nblk, ATTN_WIDTH, MOBA_BLOCK), lambda i, j: (i // nst, i % nst, 0, 0)),
        ],
        out_shape=[
            jax.ShapeDtypeStruct((n, IN_PROJ_WIDTH), BF16),
            jax.ShapeDtypeStruct((batch, seq // MOBA_BLOCK, ATTN_WIDTH, MOBA_BLOCK), BF16),
        ],
        scratch_shapes=[pltpu.VMEM((PROJ_TM, D_MODEL), BF16)],
        compiler_params=_cparams(("arbitrary", "arbitrary")),
        name="inproj",
    )(x2, norm_w, w_bf, cos_t, sin_t)


_NT = (((1,), (1,)), ((), ()))


def _moba_kernel(q_ref, k_ref, vt_ref, o_ref, km_scr, sel_scr, qm_scr, *, nb):
    blk = MOBA_BLOCK
    lane = lax.broadcasted_iota(jnp.int32, (1, LANES), 1)

    def kmean_body(i, c):
        kb = k_ref[pl.ds(pl.multiple_of(i * blk, blk), blk), :].astype(F32)
        km_scr[pl.ds(i, 1), :] = jnp.sum(kb, axis=0, keepdims=True) * (1.0 / blk)
        return c

    lax.fori_loop(0, nb, kmean_body, 0)

    tq = 2 * blk
    key_pos = lax.broadcasted_iota(jnp.int32, (tq, tq), 0)
    qry_pos = lax.broadcasted_iota(jnp.int32, (tq, tq), 1)
    own_causal = ((key_pos >= blk) == (qry_pos >= blk)) & (key_pos <= qry_pos)
    prev_blk = (key_pos < blk) & (qry_pos >= blk)
    blk_id = lax.broadcasted_iota(jnp.int32, (nb, tq), 0)
    q_half = (lax.broadcasted_iota(jnp.int32, (nb, tq), 1) >= blk).astype(jnp.int32)

    def pv(t, hh, p):
        rows = slice(hh * HEAD_DIM, (hh + 1) * HEAD_DIM)
        pb = p.astype(BF16)
        return (jnp.dot(vt_ref[2 * t, rows, :], pb[0:blk], preferred_element_type=F32)
                + jnp.dot(vt_ref[2 * t + 1, rows, :], pb[blk:tq], preferred_element_type=F32))

    def qtile(jt, c):
        row0 = pl.multiple_of(jt * tq, tq)
        q = q_ref[pl.ds(row0, tq), :]
        k_diag = k_ref[pl.ds(row0, tq), :]
        init = []
        for hh in range(2):
            hmask = (lane // HEAD_DIM) == hh
            qm = jnp.where(hmask, q, jnp.zeros_like(q))
            qm_scr[hh] = qm
            km = jnp.where(hmask, km_scr[...], 0.0)
            km_hi = km.astype(BF16)
            km_lo = (km - km_hi.astype(F32)).astype(BF16)
            gate = (lax.dot_general(km_hi, qm, _NT, preferred_element_type=F32)
                    + lax.dot_general(km_lo, qm, _NT, preferred_element_type=F32))
            past = blk_id < 2 * jt + q_half
            gate = jnp.where(past, gate, -jnp.inf)
            rank = jnp.zeros((nb, tq), jnp.int32)
            for ip in range(nb):
                gi = gate[ip:ip + 1, :]
                beats = (gi > gate) | ((gi == gate) & (blk_id > ip))
                rank = rank + beats.astype(jnp.int32)
            sel_scr[hh] = ((rank < MOBA_TOPK) & past).astype(F32)

            s = lax.dot_general(k_diag, qm, _NT, preferred_element_type=F32)
            allowed = own_causal | (prev_blk & (sel_scr[hh, pl.ds(2 * jt, 1), :] > 0.0))
            s = jnp.where(allowed, s, NEG_INF)
            m0 = jnp.max(s, axis=0, keepdims=True)
            p = jnp.exp(s - m0)
            init += [m0, jnp.sum(p, axis=0, keepdims=True), pv(jt, hh, p)]

        def kv_body(it, carry):
            k_t = k_ref[pl.ds(pl.multiple_of(it * tq, tq), tq), :]
            new = []
            for hh in range(2):
                m, l, acc = carry[3 * hh:3 * hh + 3]
                s = lax.dot_general(k_t, qm_scr[hh], _NT, preferred_element_type=F32)
                s = jnp.concatenate([
                    jnp.where(sel_scr[hh, pl.ds(2 * it, 1), :] > 0.0, s[0:blk], NEG_INF),
                    jnp.where(sel_scr[hh, pl.ds(2 * it + 1, 1), :] > 0.0, s[blk:tq], NEG_INF)], axis=0)
                m_new = jnp.maximum(m, jnp.max(s, axis=0, keepdims=True))
                alpha = jnp.exp(m - m_new)
                p = jnp.exp(s - m_new)
                l = alpha * l + jnp.sum(p, axis=0, keepdims=True)
                acc = alpha * acc + pv(it, hh, p)
                new += [m_new, l, acc]
            return tuple(new)

        fin = lax.fori_loop(0, jt, kv_body, tuple(init))
        out_t = jnp.concatenate([fin[2] / fin[1], fin[5] / fin[4]], axis=0)
        o_ref[pl.ds(row0, tq), :] = out_t.T.astype(BF16)
        return c

    lax.fori_loop(0, nb // 2, qtile, 0)


def _moba(proj, vt, batch, seq):
    nb = seq // MOBA_BLOCK
    npair = ATTN_WIDTH // LANES
    return pl.pallas_call(
        functools.partial(_moba_kernel, nb=nb),
        grid=(batch, npair),
        in_specs=[
            pl.BlockSpec((seq, LANES), lambda b, p: (b, COL_AQ // LANES + p)),
            pl.BlockSpec((seq, LANES), lambda b, p: (b, COL_AK // LANES + p)),
            pl.BlockSpec((None, nb, LANES, MOBA_BLOCK), lambda b, p: (b, 0, p, 0)),
        ],
        out_specs=pl.BlockSpec((seq, LANES), lambda b, p: (b, p)),
        out_shape=jax.ShapeDtypeStruct((batch * seq, ATTN_WIDTH), BF16),
        scratch_shapes=[pltpu.VMEM((nb, LANES), F32), pltpu.VMEM((2, nb, 2 * MOBA_BLOCK), F32),
                        pltpu.VMEM((2, 2 * MOBA_BLOCK, LANES), BF16)],
        compiler_params=_cparams(("arbitrary", "arbitrary")),
        name="moba",
    )(proj, proj, vt)


def _ret_log_gamma():
    return [math.log1p(-(2.0 ** (-5.0 - h))) for h in range(RET_HEADS)]


def _retention_tables():
    c = RET_TILE
    lg = np.array(_ret_log_gamma(), np.float64)
    n = np.arange(c, dtype=np.float64)
    diff = n[:, None] - n[None, :]
    decay = np.where(diff[None] >= 0, np.exp(np.maximum(diff, 0.0)[None] * lg[:, None, None]), 0.0)
    head_of_lane = np.arange(RET_QK_WIDTH) // HEAD_DIM
    xi = np.exp((n + 1.0)[:, None] * lg[head_of_lane][None, :])
    zeta = np.exp((c - 1.0 - n)[:, None] * lg[head_of_lane][None, :])
    chunk_decay = np.exp(c * lg[head_of_lane])[:, None]
    return (jnp.asarray(decay, F32), jnp.asarray(xi, F32), jnp.asarray(zeta, F32),
            jnp.asarray(np.broadcast_to(chunk_decay, (RET_QK_WIDTH, LANES)), F32))


def _retention_kernel(q_ref, k_ref, v_ref, g_ref, nw_ref, decay_ref, xi_ref, zeta_ref, cd_ref,
                      o_ref, state_scr):
    @pl.when(pl.program_id(1) == 0)
    def _():
        state_scr[...] = jnp.zeros_like(state_scr)

    lane = lax.broadcasted_iota(jnp.int32, (1, LANES), 1)
    srow = lax.broadcasted_iota(jnp.int32, (LANES, 1), 0)
    for pr in range(RET_HEADS // 2):
        cols = slice(pr * LANES, (pr + 1) * LANES)
        q = q_ref[:, cols]
        k = k_ref[:, cols]
        state = state_scr[cols, :]
        state_bf = state.astype(BF16)
        q_xi = (q.astype(F32) * xi_ref[:, cols]).astype(BF16)
        kz_t = (k.astype(F32) * zeta_ref[:, cols]).T.astype(BF16)
        upd = []
        for hh in range(2):
            h = 2 * pr + hh
            hmask = (lane // HEAD_DIM) == hh
            vcols = slice(h * RET_V_DIM, (h + 1) * RET_V_DIM)
            v = v_ref[:, vcols]
            qm = jnp.where(hmask, q, jnp.zeros_like(q))
            scores = lax.dot_general(qm, k, _NT, preferred_element_type=F32) * decay_ref[h]
            o = jnp.dot(scores.astype(BF16), v, preferred_element_type=F32)
            o = o + jnp.dot(jnp.where(hmask, q_xi, jnp.zeros_like(q_xi)), state_bf,
                            preferred_element_type=F32)
            upd.append(jnp.dot(kz_t, v, preferred_element_type=F32))
            ms = jnp.mean(o * o, axis=-1, keepdims=True)
            y = o * lax.rsqrt(ms + RMS_EPS) * nw_ref[:, vcols]
            g = g_ref[:, vcols].astype(F32)
            o_ref[:, vcols] = (g * jax.nn.sigmoid(g) * y).astype(BF16)
        state_scr[cols, :] = state * cd_ref[cols, :] + jnp.where(srow < HEAD_DIM, upd[0], upd[1])


def _retention(proj, ret_norm_w, tables, batch, seq):
    nc = seq // RET_TILE
    decay, xi, zeta, cd = tables
    row = lambda b, c: b * nc + c
    const2 = lambda b, c: (0, 0)
    return pl.pallas_call(
        _retention_kernel,
        grid=(batch, nc),
        in_specs=[
            pl.BlockSpec((RET_TILE, RET_QK_WIDTH), lambda b, c: (row(b, c), COL_RQ // RET_QK_WIDTH)),
            pl.BlockSpec((RET_TILE, RET_QK_WIDTH), lambda b, c: (row(b, c), COL_RK // RET_QK_WIDTH)),
            pl.BlockSpec((RET_TILE, RET_V_WIDTH), lambda b, c: (row(b, c), COL_RV // RET_V_WIDTH)),
            pl.BlockSpec((RET_TILE, RET_V_WIDTH), lambda b, c: (row(b, c), COL_RG // RET_V_WIDTH)),
            pl.BlockSpec((1, RET_V_WIDTH), const2),
            pl.BlockSpec((RET_HEADS, RET_TILE, RET_TILE), lambda b, c: (0, 0, 0)),
            pl.BlockSpec((RET_TILE, RET_QK_WIDTH), const2),
            pl.BlockSpec((RET_TILE, RET_QK_WIDTH), const2),
            pl.BlockSpec((RET_QK_WIDTH, LANES), const2),
        ],
        out_specs=pl.BlockSpec((RET_TILE, RET_V_WIDTH), lambda b, c: (row(b, c), 0)),
        out_shape=jax.ShapeDtypeStruct((batch * seq, RET_V_WIDTH), BF16),
        scratch_shapes=[pltpu.VMEM((RET_QK_WIDTH, RET_V_DIM), F32)],
        compiler_params=_cparams(("arbitrary", "arbitrary")),
        name="retention",
    )(proj, proj, proj, proj, ret_norm_w, decay, xi, zeta, cd)


CONV_HALO = 8


def _merge_kernel(x_ref, ya_ref, cb_ref, cc_ref, ch_ref, yr_ref, ga_ref, gc_ref, gr_ref, cw_ref,
                  cbias_ref, wa_ref, wc_ref, wr_ref, wo_ref, o_ref, u_scr, *, tiles_per_seq):
    i = pl.program_id(0)
    tm = x_ref.shape[0]

    @pl.when(i % tiles_per_seq == 0)
    def _():
        u_scr[0:CONV_HALO, :] = jnp.zeros((CONV_HALO, CONV_WIDTH), F32)

    u_scr[CONV_HALO:CONV_HALO + tm, :] = cc_ref[...].astype(F32) * ch_ref[...].astype(F32)
    conv = (cw_ref[2:3, :] * u_scr[CONV_HALO:CONV_HALO + tm, :]
            + cw_ref[1:2, :] * u_scr[CONV_HALO - 1:CONV_HALO - 1 + tm, :]
            + cw_ref[0:1, :] * u_scr[CONV_HALO - 2:CONV_HALO - 2 + tm, :]
            + cbias_ref[...])
    y_conv = (cb_ref[...].astype(F32) * conv).astype(BF16)
    u_scr[0:CONV_HALO, :] = u_scr[tm:tm + CONV_HALO, :]

    def gate(g_ref):
        return jax.nn.sigmoid(g_ref[...].astype(F32))

    merged = gate(ga_ref) * jnp.dot(ya_ref[...], wa_ref[...], preferred_element_type=F32)
    merged = merged + gate(gc_ref) * jnp.dot(y_conv, wc_ref[...], preferred_element_type=F32)
    merged = merged + gate(gr_ref) * jnp.dot(yr_ref[...], wr_ref[...], preferred_element_type=F32)
    o_ref[...] = x_ref[...] + jnp.dot(merged.astype(BF16), wo_ref[...], preferred_element_type=F32)


def _merge(x2, y_attn, proj, y_ret, conv_w, conv_b, wa, wc, wr, wo, seq):
    n = x2.shape[0]
    tm = MERGE_TM
    const = lambda i: (0, 0)
    wide = lambda c: pl.BlockSpec((tm, 512), lambda i: (i, c // 512))
    gate_spec = lambda b: pl.BlockSpec((tm, D_MODEL), lambda i: (i, COL_GATES // D_MODEL + b))
    return pl.pallas_call(
        functools.partial(_merge_kernel, tiles_per_seq=seq // tm),
        grid=(n // tm,),
        in_specs=[
            pl.BlockSpec((tm, D_MODEL), lambda i: (i, 0)),
            pl.BlockSpec((tm, ATTN_WIDTH), lambda i: (i, 0)),
            wide(COL_CB), wide(COL_CC), wide(COL_CH),
            pl.BlockSpec((tm, RET_V_WIDTH), lambda i: (i, 0)),
            gate_spec(0), gate_spec(1), gate_spec(2),
            pl.BlockSpec((CONV_KERNEL, CONV_WIDTH), const),
            pl.BlockSpec((1, CONV_WIDTH), const),
            pl.BlockSpec((ATTN_WIDTH, D_MODEL), const),
            pl.BlockSpec((CONV_WIDTH, D_MODEL), const),
            pl.BlockSpec((RET_V_WIDTH, D_MODEL), const),
            pl.BlockSpec((D_MODEL, D_MODEL), const),
        ],
        out_specs=pl.BlockSpec((tm, D_MODEL), lambda i: (i, 0)),
        out_shape=jax.ShapeDtypeStruct((n, D_MODEL), F32),
        scratch_shapes=[pltpu.VMEM((tm + CONV_HALO, CONV_WIDTH), F32)],
        compiler_params=_cparams(("arbitrary",)),
        name="merge",
    )(x2, y_attn, proj, proj, proj, y_ret, proj, proj, proj, conv_w, conv_b, wa, wc, wr, wo)


def _swiglu_step(h, wg_ref, wu_ref, wd_ref, acc_scr, c):
    g = jnp.dot(h, wg_ref[...], preferred_element_type=F32)
    u = jnp.dot(h, wu_ref[...], preferred_element_type=F32)
    a = (g * jax.nn.sigmoid(g) * u).astype(BF16)
    part = jnp.dot(a, wd_ref[...], preferred_element_type=F32)

    @pl.when(c == 0)
    def _():
        acc_scr[...] = part

    @pl.when(c != 0)
    def _():
        acc_scr[...] += part


def _dense_ffn_kernel(x_ref, nw_ref, wg_ref, wu_ref, wd_ref, o_ref, h_scr, acc_scr):
    c = pl.program_id(1)

    @pl.when(c == 0)
    def _():
        x = x_ref[...]
        ms = jnp.mean(x * x, axis=-1, keepdims=True)
        h_scr[...] = (x * lax.rsqrt(ms + RMS_EPS) * nw_ref[...]).astype(BF16)

    _swiglu_step(h_scr[...], wg_ref, wu_ref, wd_ref, acc_scr, c)

    @pl.when(c == pl.num_programs(1) - 1)
    def _():
        o_ref[...] = x_ref[...] + acc_scr[...]


def _dense_ffn(x2, norm_w, wg, wu, wd):
    n = x2.shape[0]
    tm, tf = FFN_TM, FFN_TF
    return pl.pallas_call(
        _dense_ffn_kernel,
        grid=(n // tm, D_FF // tf),
        in_specs=[
            pl.BlockSpec((tm, D_MODEL), lambda i, c: (i, 0)),
            pl.BlockSpec((1, D_MODEL), lambda i, c: (0, 0)),
            pl.BlockSpec((D_MODEL, tf), lambda i, c: (0, c)),
            pl.BlockSpec((D_MODEL, tf), lambda i, c: (0, c)),
            pl.BlockSpec((tf, D_MODEL), lambda i, c: (c, 0)),
        ],
        out_specs=pl.BlockSpec((tm, D_MODEL), lambda i, c: (i, 0)),
        out_shape=jax.ShapeDtypeStruct((n, D_MODEL), F32),
        scratch_shapes=[pltpu.VMEM((tm, D_MODEL), BF16), pltpu.VMEM((tm, D_MODEL), F32)],
        compiler_params=_cparams(("arbitrary", "arbitrary")),
        name="dense_ffn",
    )(x2, norm_w, wg, wu, wd)


PACK_GROUPS = D_MODEL // (2 * LANES)
_HI_MASK = 0xFFFF0000


def _pack_rows(v, o_ref):
    half = D_MODEL // 2
    lo = pltpu.bitcast(v[:, :half].astype(BF16).astype(F32), jnp.uint32) >> 16
    hi = pltpu.bitcast(v[:, half:].astype(BF16).astype(F32), jnp.uint32) & jnp.uint32(_HI_MASK)
    w = lo | hi
    for s in range(PACK_GROUPS):
        o_ref[:, s, :] = w[:, s * LANES:(s + 1) * LANES]


def _unpack_rows(x_ref):
    words = [x_ref[:, s, :] for s in range(PACK_GROUPS)]
    lo = [pltpu.bitcast(w << 16, F32) for w in words]
    hi = [pltpu.bitcast(w & jnp.uint32(_HI_MASK), F32) for w in words]
    return jnp.concatenate(lo + hi, axis=1)


def _route_kernel(x_ref, nw_ref, wr_ref, h_ref, idx_ref, wgt_ref):
    x = x_ref[...]
    ms = jnp.mean(x * x, axis=-1, keepdims=True)
    h = x * lax.rsqrt(ms + RMS_EPS) * nw_ref[...]
    _pack_rows(h, h_ref)
    logits = jnp.dot(h, wr_ref[...], preferred_element_type=F32, precision=lax.Precision.HIGHEST)
    eid = lax.broadcasted_iota(jnp.int32, logits.shape, 1).astype(F32)
    logits = jnp.where(eid < N_EXPERTS, logits, -jnp.inf)
    m1 = jnp.max(logits, axis=-1, keepdims=True)
    i1 = jnp.min(jnp.where(logits == m1, eid, float(LANES)), axis=-1, keepdims=True)
    rest = jnp.where(eid == i1, -jnp.inf, logits)
    m2 = jnp.max(rest, axis=-1, keepdims=True)
    i2 = jnp.min(jnp.where(rest == m2, eid, float(LANES)), axis=-1, keepdims=True)
    e2 = jnp.exp(m2 - m1)
    denom = 1.0 + e2
    idx_ref[...] = jnp.where(eid == 0.0, i1, jnp.where(eid == 1.0, i2, 0.0)).astype(jnp.int32)
    wgt_ref[...] = jnp.where(eid == 0.0, 1.0 / denom, jnp.where(eid == 1.0, e2 / denom, 0.0))


def _route(x2, norm_w, w_router):
    n = x2.shape[0]
    tm = ROUTE_TM
    return pl.pallas_call(
        _route_kernel,
        grid=(n // tm,),
        in_specs=[
            pl.BlockSpec((tm, D_MODEL), lambda i: (i, 0)),
            pl.BlockSpec((1, D_MODEL), lambda i: (0, 0)),
            pl.BlockSpec((D_MODEL, LANES), lambda i: (0, 0)),
        ],
        out_specs=[
            pl.BlockSpec((tm, PACK_GROUPS, LANES), lambda i: (i, 0, 0)),
            pl.BlockSpec((tm, LANES), lambda i: (i, 0)),
            pl.BlockSpec((tm, LANES), lambda i: (i, 0)),
        ],
        out_shape=[
            jax.ShapeDtypeStruct((n, PACK_GROUPS, LANES), jnp.uint32),
            jax.ShapeDtypeStruct((n, LANES), jnp.int32),
            jax.ShapeDtypeStruct((n, LANES), F32),
        ],
        compiler_params=_cparams(("arbitrary",)),
        name="route",
    )(x2, norm_w, jnp.pad(w_router, ((0, 0), (0, LANES - N_EXPERTS))))


def _gather_kernel(ids_ref, src_ref, o_ref, sem):
    tm = o_ref.shape[0]

    def row_copy(r):
        return pltpu.make_async_copy(src_ref.at[ids_ref[0, 0, r]], o_ref.at[r], sem)

    def start(r, c):
        row_copy(r).start()
        return c

    def wait(r, c):
        row_copy(r).wait()
        return c

    lax.fori_loop(0, tm, start, 0, unroll=8)
    lax.fori_loop(0, tm, wait, 0, unroll=8)


def _gather_rows(ids, src):
    p = ids.shape[0]
    tm = GATHER_TM
    return pl.pallas_call(
        _gather_kernel,
        grid=(p // tm,),
        in_specs=[
            pl.BlockSpec((1, 1, tm), lambda i: (i, 0, 0), memory_space=pltpu.SMEM),
            pl.BlockSpec(memory_space=pl.ANY),
        ],
        out_specs=pl.BlockSpec((tm,) + src.shape[1:], lambda i: (i, 0, 0)),
        out_shape=jax.ShapeDtypeStruct((p,) + src.shape[1:], src.dtype),
        scratch_shapes=[pltpu.SemaphoreType.DMA(())],
        compiler_params=_cparams(("arbitrary",)),
        name="gather_rows",
    )(ids.reshape(p // tm, 1, tm), src)


def _moe_ffn_kernel(te_ref, nv_ref, h_ref, wg_ref, wu_ref, wd_ref, o_ref, h_scr, acc_scr):
    t = pl.program_id(0)
    c = pl.program_id(1)

    valid = t < nv_ref[0]
    last = c == pl.num_programs(1) - 1

    @pl.when(valid & (c == 0))
    def _():
        h_scr[...] = _unpack_rows(h_ref).astype(BF16)

    @pl.when(valid)
    def _():
        _swiglu_step(h_scr[...], wg_ref, wu_ref, wd_ref, acc_scr, c)

    @pl.when(valid & last)
    def _():
        _pack_rows(acc_scr[...], o_ref)

    @pl.when(jnp.logical_not(valid) & last)
    def _():
        o_ref[...] = jnp.zeros_like(o_ref)


def _moe_ffn(tile_expert, n_valid, hs, wg, wu, wd):
    p = hs.shape[0]
    tm, tf = FFN_TM, FFN_TF
    grid_spec = pltpu.PrefetchScalarGridSpec(
        num_scalar_prefetch=2,
        grid=(p // tm, D_FF // tf),
        in_specs=[
            pl.BlockSpec((tm, PACK_GROUPS, LANES), lambda t, c, te, nv: (t, 0, 0)),
            pl.BlockSpec((None, D_MODEL, tf), lambda t, c, te, nv: (te[t], 0, c)),
            pl.BlockSpec((None, D_MODEL, tf), lambda t, c, te, nv: (te[t], 0, c)),
            pl.BlockSpec((None, tf, D_MODEL), lambda t, c, te, nv: (te[t], c, 0)),
        ],
        out_specs=pl.BlockSpec((tm, PACK_GROUPS, LANES), lambda t, c, te, nv: (t, 0, 0)),
        scratch_shapes=[pltpu.VMEM((tm, D_MODEL), BF16), pltpu.VMEM((tm, D_MODEL), F32)],
    )
    return pl.pallas_call(
        _moe_ffn_kernel,
        grid_spec=grid_spec,
        out_shape=jax.ShapeDtypeStruct((p, PACK_GROUPS, LANES), jnp.uint32),
        compiler_params=_cparams(("arbitrary", "arbitrary")),
        name="moe_ffn",
    )(tile_expert, n_valid, hs, wg, wu, wd)


def _combine_kernel(x_ref, y0_ref, y1_ref, w_ref, nw_ref, o_ref, *, final_norm):
    w = w_ref[...]
    x = x_ref[...] + w[:, 0:1] * _unpack_rows(y0_ref) + w[:, 1:2] * _unpack_rows(y1_ref)
    if final_norm:
        ms = jnp.mean(x * x, axis=-1, keepdims=True)
        x = x * lax.rsqrt(ms + RMS_EPS) * nw_ref[...]
    o_ref[...] = x


def _combine(x2, y_pairs, wgt, norm_w, final_norm):
    n = x2.shape[0]
    tm = MERGE_TM
    nt = n // tm
    return pl.pallas_call(
        functools.partial(_combine_kernel, final_norm=final_norm),
        grid=(nt,),
        in_specs=[
            pl.BlockSpec((tm, D_MODEL), lambda i: (i, 0)),
            pl.BlockSpec((tm, PACK_GROUPS, LANES), lambda i: (i, 0, 0)),
            pl.BlockSpec((tm, PACK_GROUPS, LANES), lambda i: (i + nt, 0, 0)),
            pl.BlockSpec((tm, LANES), lambda i: (i, 0)),
            pl.BlockSpec((1, D_MODEL), lambda i: (0, 0)),
        ],
        out_specs=pl.BlockSpec((tm, D_MODEL), lambda i: (i, 0)),
        out_shape=jax.ShapeDtypeStruct((n, D_MODEL), F32),
        compiler_params=_cparams(("arbitrary",)),
        name="combine",
    )(x2, y_pairs, y_pairs, wgt, norm_w)


def _moe_layout(idx):
    n = idx.shape[0]
    tm = FFN_TM
    flat = idx.T.reshape(-1)
    onehot = (flat[:, None] == jnp.arange(N_EXPERTS, dtype=jnp.int32)[None, :]).astype(jnp.int32)
    cum = jnp.cumsum(onehot, axis=0)
    counts = cum[-1]
    rank = jnp.sum(cum * onehot, axis=1) - 1
    padded = ((counts + tm - 1) // tm) * tm
    ends = jnp.cumsum(padded)
    starts = ends - padded
    dest = starts[flat] + rank
    p = TOP_K * n + N_EXPERTS * tm
    token = jnp.arange(TOP_K * n, dtype=jnp.int32) % n
    row_token = jnp.zeros((p,), jnp.int32).at[dest].set(token)
    tile_start = jnp.arange(p // tm, dtype=jnp.int32) * tm
    tile_expert = jnp.minimum(jnp.sum((ends[None, :] <= tile_start[:, None]).astype(jnp.int32), axis=1),
                              N_EXPERTS - 1)
    n_valid = (ends[-1] // tm).astype(jnp.int32).reshape(1)
    return row_token, tile_expert, n_valid, dest.astype(jnp.int32)


def _moe(x2, norm_w, w_router, wg, wu, wd, final_norm_w, final_norm):
    h, idx, wgt = _route(x2, norm_w, w_router)
    row_token, tile_expert, n_valid, dest = _moe_layout(idx[:, :TOP_K])
    hs = _gather_rows(row_token, h)
    ys = _moe_ffn(tile_expert, n_valid, hs, wg, wu, wd)
    y_pairs = _gather_rows(dest, ys)
    return _combine(x2, y_pairs, wgt, final_norm_w, final_norm)


def _rope_tables(seq):
    inv_freq = 1.0 / (ROPE_THETA ** (jnp.arange(0, HEAD_DIM, 2, dtype=F32) / HEAD_DIM))
    ang = jnp.arange(seq, dtype=F32)[:, None] * inv_freq[None, :]
    cos, sin = jnp.cos(ang), jnp.sin(ang)
    cos_t = jnp.tile(cos, (1, LANES // (HEAD_DIM // 2)))
    sin_t = jnp.tile(jnp.concatenate([-sin, sin], axis=1), (1, LANES // HEAD_DIM))
    return cos_t, sin_t


def _reorder_w_in(w):
    head, rq_rk, rv_rg_gates = w[:, :3072], w[:, 3072:3584], w[:, 3584:]
    return jnp.concatenate([head, rv_rg_gates, rq_rk], axis=1).astype(BF16)


def kernel(x, mix_norm_w, w_in, conv_w, conv_b, ret_norm_w, w_br_attn, w_br_conv, w_br_ret, w_out,
           ffn_norm_w, dense_w_gate, dense_w_up, dense_w_down, moe_router, moe_w_gate, moe_w_up,
           moe_w_down, final_norm_w):
    batch, seq, d = x.shape
    depth = w_in.shape[0]
    assert d == D_MODEL and seq % PROJ_TM == 0 and depth % 2 == 0
    n = batch * seq
    cos_t, sin_t = _rope_tables(seq)
    ret_tables = _retention_tables()
    x2 = x.reshape(n, d)
    for layer in range(depth):
        proj, vt = _inproj(x2, mix_norm_w[layer].reshape(1, d), _reorder_w_in(w_in[layer]),
                           cos_t, sin_t, batch, seq)
        y_attn = _moba(proj, vt, batch, seq)
        y_ret = _retention(proj, ret_norm_w[layer].reshape(1, RET_V_WIDTH), ret_tables, batch, seq)
        x2 = _merge(x2, y_attn, proj, y_ret, conv_w[layer], conv_b[layer].reshape(1, CONV_WIDTH),
                    w_br_attn[layer].astype(BF16), w_br_conv[layer].astype(BF16),
                    w_br_ret[layer].astype(BF16), w_out[layer].astype(BF16), seq)
        i = layer // 2
        nw = ffn_norm_w[layer].reshape(1, d)
        if layer % 2 == 0:
            x2 = _dense_ffn(x2, nw, dense_w_gate[i].astype(BF16), dense_w_up[i].astype(BF16),
                            dense_w_down[i].astype(BF16))
        else:
            last = layer == depth - 1
            x2 = _moe(x2, nw, moe_router[i], moe_w_gate[i].astype(BF16), moe_w_up[i].astype(BF16),
                      moe_w_down[i].astype(BF16), final_norm_w.reshape(1, d), last)
    return x2.reshape(batch, seq, d)
```

```python
import functools
import math

import jax
import jax.numpy as jnp
import numpy as np
from jax import lax
from jax.experimental import pallas as pl
from jax.experimental.pallas import tpu as pltpu

F32 = jnp.float32
BF16 = jnp.bfloat16

D_MODEL = 1024
HEAD_DIM = 64
ROPE_THETA = 10000.0
RMS_EPS = 1e-6

ATTN_HEADS = 8
ATTN_WIDTH = ATTN_HEADS * HEAD_DIM
MOBA_BLOCK = 256
MOBA_TOPK = 3
NEG_INF = -1e30
VT_ONES = 16
VT_ROWS = HEAD_DIM + VT_ONES
QK_SCALE_LOG2 = HEAD_DIM ** -0.5 * math.log2(math.e)

CONV_WIDTH = 512
CONV_KERNEL = 3

RET_HEADS = 4
RET_QK_WIDTH = RET_HEADS * HEAD_DIM
RET_V_DIM = 2 * HEAD_DIM
RET_V_WIDTH = RET_HEADS * RET_V_DIM
RET_TILE = 256

D_FF = 3584
N_EXPERTS = 8
TOP_K = 2

LANES = 128
IN_PROJ_WIDTH = 7680
COL_AQ, COL_AK, COL_AV = 0, 512, 1024
COL_CB, COL_CC, COL_CH = 1536, 2048, 2560
COL_RV, COL_RG = 3072, 3584
COL_GATES = 4096
COL_RQ, COL_RK = 7168, 7424

PROJ_TM = 1024
PROJ_TN = 512
MERGE_TM = 512
FFN_TM = 1024
FFN_TF = 512
ROUTE_TM = 1024
GATHER_TM = 512
VMEM_LIMIT = 56 * 1024 * 1024


def _cparams(sem):
    return pltpu.CompilerParams(dimension_semantics=sem, vmem_limit_bytes=VMEM_LIMIT)


def _rope(acc, cos, sin_signed):
    lane = lax.broadcasted_iota(jnp.int32, (1, LANES), 1)
    first_half = (lane % HEAD_DIM) < (HEAD_DIM // 2)
    outs = []
    for g in range(acc.shape[1] // LANES):
        blk = acc[:, g * LANES:(g + 1) * LANES]
        partner = jnp.where(first_half,
                            pltpu.roll(blk, LANES - HEAD_DIM // 2, 1),
                            pltpu.roll(blk, HEAD_DIM // 2, 1))
        outs.append(blk * cos + partner * sin_signed)
    return jnp.concatenate(outs, axis=1)


def _inproj_kernel(x_ref, nw_ref, w_ref, cos_ref, sin_ref, proj_ref, vt_ref, h_scr):
    j = pl.program_id(1)

    @pl.when(j == 0)
    def _():
        x = x_ref[...]
        ms = jnp.mean(x * x, axis=-1, keepdims=True)
        h_scr[...] = (x * lax.rsqrt(ms + RMS_EPS) * nw_ref[...]).astype(BF16)

    acc = jnp.dot(h_scr[...], w_ref[...], preferred_element_type=F32)
    jq, jk, jv, jr = COL_AQ // PROJ_TN, COL_AK // PROJ_TN, COL_AV // PROJ_TN, COL_RQ // PROJ_TN
    scale = HEAD_DIM ** -0.5

    @pl.when(j == jq)
    def _():
        proj_ref[...] = (_rope(acc, cos_ref[...], sin_ref[...]) * QK_SCALE_LOG2).astype(BF16)

    @pl.when(j == jk)
    def _():
        proj_ref[...] = _rope(acc, cos_ref[...], sin_ref[...]).astype(BF16)

    @pl.when(j == jv)
    def _():
        proj_ref[...] = acc.astype(BF16)
        ones = jnp.ones((VT_ONES, MOBA_BLOCK), F32)
        for c in range(PROJ_TM // MOBA_BLOCK):
            v_t = acc[c * MOBA_BLOCK:(c + 1) * MOBA_BLOCK, :].T
            rows = []
            for h in range(ATTN_HEADS):
                rows += [v_t[h * HEAD_DIM:(h + 1) * HEAD_DIM], ones]
            vt_ref[c] = jnp.concatenate(rows, axis=0).astype(BF16)

    @pl.when(j == jr)
    def _():
        r = _rope(acc, cos_ref[...], sin_ref[...])
        col = lax.broadcasted_iota(jnp.int32, (1, PROJ_TN), 1)
        r = r * jnp.where(col >= RET_QK_WIDTH, scale, 1.0)
        proj_ref[...] = r.astype(BF16)

    @pl.when((j != jq) & (j != jk) & (j != jv) & (j != jr))
    def _():
        proj_ref[...] = acc.astype(BF16)


def _inproj(x2, norm_w, w_bf, cos_t, sin_t, batch, seq):
    n = x2.shape[0]
    nst = seq // PROJ_TM
    nblk = PROJ_TM // MOBA_BLOCK
    return pl.pallas_call(
        _inproj_kernel,
        grid=(n // PROJ_TM, IN_PROJ_WIDTH // PROJ_TN),
        in_specs=[
            pl.BlockSpec((PROJ_TM, D_MODEL), lambda i, j: (i, 0)),
            pl.BlockSpec((1, D_MODEL), lambda i, j: (0, 0)),
            pl.BlockSpec((D_MODEL, PROJ_TN), lambda i, j: (0, j)),
            pl.BlockSpec((PROJ_TM, LANES), lambda i, j: (i % nst, 0)),
            pl.BlockSpec((PROJ_TM, LANES), lambda i, j: (i % nst, 0)),
        ],
        out_specs=[
            pl.BlockSpec((PROJ_TM, PROJ_TN), lambda i, j: (i, j)),
            pl.BlockSpec((None, nblk, ATTN_HEADS * VT_ROWS, MOBA_BLOCK),
                         lambda i, j: (i // nst, i % nst, 0, 0)),
        ],
        out_shape=[
            jax.ShapeDtypeStruct((n, IN_PROJ_WIDTH), BF16),
            jax.ShapeDtypeStruct((batch, seq // MOBA_BLOCK, ATTN_HEADS * VT_ROWS, MOBA_BLOCK), BF16),
        ],
        scratch_shapes=[pltpu.VMEM((PROJ_TM, D_MODEL), BF16)],
        compiler_params=_cparams(("arbitrary", "arbitrary")),
        name="inproj",
    )(x2, norm_w, w_bf, cos_t, sin_t)


_NT = (((1,), (1,)), ((), ()))


def _moba_kernel(q_ref, k_ref, vt_ref, o_ref, km_scr, ka_scr, sel_scr, qt_scr, *, nb):
    blk = MOBA_BLOCK
    tq = 2 * blk
    lane = lax.broadcasted_iota(jnp.int32, (1, LANES), 1)
    crow = lax.broadcasted_iota(jnp.int32, (LANES, 1), 0)

    def prep(i, c):
        rows = pl.ds(pl.multiple_of(i * blk, blk), blk)
        kb = k_ref[rows, :]
        km_scr[pl.ds(i, 1), :] = jnp.sum(kb.astype(F32), axis=0, keepdims=True) * (1.0 / blk)
        for hh in range(2):
            onehot = jnp.where(lane == HEAD_DIM * (1 - hh) + i, 1.0, 0.0).astype(BF16)
            ka_scr[hh, rows, :] = jnp.where((lane // HEAD_DIM) == hh, kb, onehot)
        return c

    lax.fori_loop(0, nb, prep, 0)

    key_pos = lax.broadcasted_iota(jnp.int32, (tq, tq), 0)
    qry_pos = lax.broadcasted_iota(jnp.int32, (tq, tq), 1)
    own_causal = ((key_pos >= blk) == (qry_pos >= blk)) & (key_pos <= qry_pos)
    prev_blk = (key_pos < blk) & (qry_pos >= blk)
    blk_id = lax.broadcasted_iota(jnp.int32, (nb, tq), 0)
    q_half = (lax.broadcasted_iota(jnp.int32, (nb, tq), 1) >= blk).astype(jnp.int32)

    def pv(t, hh, p):
        rows = slice(hh * VT_ROWS, (hh + 1) * VT_ROWS)
        pb = p.astype(BF16)
        return (jnp.dot(vt_ref[2 * t, rows, :], pb[0:blk], preferred_element_type=F32)
                + jnp.dot(vt_ref[2 * t + 1, rows, :], pb[blk:tq], preferred_element_type=F32))

    def qtile(jt, c):
        row0 = pl.multiple_of(jt * tq, tq)
        q_t = q_ref[pl.ds(row0, tq), :].astype(F32).T
        k_diag = k_ref[pl.ds(row0, tq), :]
        init = []
        for hh in range(2):
            q_m = jnp.where((crow // HEAD_DIM) == hh, q_t, 0.0)
            q_mb = q_m.astype(BF16)
            km = jnp.where((lane // HEAD_DIM) == hh, km_scr[...], 0.0)
            km_hi = km.astype(BF16)
            km_lo = (km - km_hi.astype(F32)).astype(BF16)
            gate = (jnp.dot(km_hi, q_mb, preferred_element_type=F32)
                    + jnp.dot(km_lo, q_mb, preferred_element_type=F32))
            past = blk_id < 2 * jt + q_half
            gate = jnp.where(past, gate, -jnp.inf)
            rank = jnp.zeros((nb, tq), jnp.int32)
            for ip in range(nb):
                gi = gate[ip:ip + 1, :]
                beats = (gi > gate) | ((gi == gate) & (blk_id > ip))
                rank = rank + beats.astype(jnp.int32)
            sel = (rank < MOBA_TOPK) & past
            sel_scr[hh] = sel.astype(F32)
            bias = jnp.where(sel, 0.0, NEG_INF)
            spare = HEAD_DIM * (1 - hh)
            pieces = [jnp.zeros((spare, tq), F32)] if spare else []
            pieces += [bias, jnp.zeros((LANES - spare - nb, tq), F32)]
            qt_scr[hh] = (q_m + jnp.concatenate(pieces, axis=0)).astype(BF16)

            s = jnp.dot(k_diag, q_mb, preferred_element_type=F32)
            allowed = own_causal | (prev_blk & (sel_scr[hh, pl.ds(2 * jt, 1), :] > 0.0))
            s = jnp.where(allowed, s, NEG_INF)
            m0 = jnp.max(s, axis=0, keepdims=True)
            init += [m0, pv(jt, hh, jnp.exp2(s - m0))]

        def kv_body(it, carry):
            rows = pl.ds(pl.multiple_of(it * tq, tq), tq)
            s = [jnp.dot(ka_scr[hh, rows, :], qt_scr[hh], preferred_element_type=F32) for hh in range(2)]
            m_new = [jnp.maximum(carry[2 * hh], jnp.max(s[hh], axis=0, keepdims=True)) for hh in range(2)]
            new = []
            for hh in range(2):
                alpha = jnp.exp2(carry[2 * hh] - m_new[hh])
                new += [m_new[hh], alpha * carry[2 * hh + 1] + pv(it, hh, jnp.exp2(s[hh] - m_new[hh]))]
            return tuple(new)

        fin = lax.fori_loop(0, jt, kv_body, tuple(init))
        out_t = jnp.concatenate([fin[1][0:HEAD_DIM] / fin[1][HEAD_DIM:HEAD_DIM + 1],
                                 fin[3][0:HEAD_DIM] / fin[3][HEAD_DIM:HEAD_DIM + 1]], axis=0)
        o_ref[pl.ds(row0, tq), :] = out_t.T.astype(BF16)
        return c

    lax.fori_loop(0, nb // 2, qtile, 0)


def _moba(proj, vt, batch, seq):
    nb = seq // MOBA_BLOCK
    assert nb % 8 == 0 and nb <= HEAD_DIM
    npair = ATTN_WIDTH // LANES
    return pl.pallas_call(
        functools.partial(_moba_kernel, nb=nb),
        grid=(batch, npair),
        in_specs=[
            pl.BlockSpec((seq, LANES), lambda b, p: (b, COL_AQ // LANES + p)),
            pl.BlockSpec((seq, LANES), lambda b, p: (b, COL_AK // LANES + p)),
            pl.BlockSpec((None, nb, 2 * VT_ROWS, MOBA_BLOCK), lambda b, p: (b, 0, p, 0)),
        ],
        out_specs=pl.BlockSpec((seq, LANES), lambda b, p: (b, p)),
        out_shape=jax.ShapeDtypeStruct((batch * seq, ATTN_WIDTH), BF16),
        scratch_shapes=[pltpu.VMEM((nb, LANES), F32), pltpu.VMEM((2, seq, LANES), BF16),
                        pltpu.VMEM((2, nb, 2 * MOBA_BLOCK), F32),
                        pltpu.VMEM((2, LANES, 2 * MOBA_BLOCK), BF16)],
        compiler_params=_cparams(("arbitrary", "arbitrary")),
        name="moba",
    )(proj, proj, vt)


def _ret_log_gamma():
    return [math.log1p(-(2.0 ** (-5.0 - h))) for h in range(RET_HEADS)]


def _retention_tables():
    c = RET_TILE
    lg = np.array(_ret_log_gamma(), np.float64)
    n = np.arange(c, dtype=np.float64)
    diff = n[:, None] - n[None, :]
    decay = np.where(diff[None] >= 0, np.exp(np.maximum(diff, 0.0)[None] * lg[:, None, None]), 0.0)
    head_of_lane = np.arange(RET_QK_WIDTH) // HEAD_DIM
    xi = np.exp((n + 1.0)[:, None] * lg[head_of_lane][None, :])
    zeta = np.exp((c - 1.0 - n)[:, None] * lg[head_of_lane][None, :])
    chunk_decay = np.exp(c * lg[head_of_lane])[:, None]
    return (jnp.asarray(decay, F32), jnp.asarray(xi, F32), jnp.asarray(zeta, F32),
            jnp.asarray(np.broadcast_to(chunk_decay, (RET_QK_WIDTH, LANES)), F32))


def _retention_kernel(q_ref, k_ref, v_ref, g_ref, nw_ref, decay_ref, xi_ref, zeta_ref, cd_ref,
                      o_ref, state_scr):
    @pl.when(pl.program_id(1) == 0)
    def _():
        state_scr[...] = jnp.zeros_like(state_scr)

    lane = lax.broadcasted_iota(jnp.int32, (1, LANES), 1)
    srow = lax.broadcasted_iota(jnp.int32, (LANES, 1), 0)
    for pr in range(RET_HEADS // 2):
        cols = slice(pr * LANES, (pr + 1) * LANES)
        q = q_ref[:, cols]
        k = k_ref[:, cols]
        state = state_scr[cols, :]
        state_bf = state.astype(BF16)
        q_xi = (q.astype(F32) * xi_ref[:, cols]).astype(BF16)
        kz_t = (k.astype(F32) * zeta_ref[:, cols]).T.astype(BF16)
        upd = []
        for hh in range(2):
            h = 2 * pr + hh
            hmask = (lane // HEAD_DIM) == hh
            vcols = slice(h * RET_V_DIM, (h + 1) * RET_V_DIM)
            v = v_ref[:, vcols]
            qm = jnp.where(hmask, q, jnp.zeros_like(q))
            scores = lax.dot_general(qm, k, _NT, preferred_element_type=F32) * decay_ref[h]
            o = jnp.dot(scores.astype(BF16), v, preferred_element_type=F32)
            o = o + jnp.dot(jnp.where(hmask, q_xi, jnp.zeros_like(q_xi)), state_bf,
                            preferred_element_type=F32)
            upd.append(jnp.dot(kz_t, v, preferred_element_type=F32))
            ms = jnp.mean(o * o, axis=-1, keepdims=True)
            y = o * lax.rsqrt(ms + RMS_EPS) * nw_ref[:, vcols]
            g = g_ref[:, vcols].astype(F32)
            o_ref[:, vcols] = (g * jax.nn.sigmoid(g) * y).astype(BF16)
        state_scr[cols, :] = state * cd_ref[cols, :] + jnp.where(srow < HEAD_DIM, upd[0], upd[1])


def _retention(proj, ret_norm_w, tables, batch, seq):
    nc = seq // RET_TILE
    decay, xi, zeta, cd = tables
    row = lambda b, c: b * nc + c
    const2 = lambda b, c: (0, 0)
    return pl.pallas_call(
        _retention_kernel,
        grid=(batch, nc),
        in_specs=[
            pl.BlockSpec((RET_TILE, RET_QK_WIDTH), lambda b, c: (row(b, c), COL_RQ // RET_QK_WIDTH)),
            pl.BlockSpec((RET_TILE, RET_QK_WIDTH), lambda b, c: (row(b, c), COL_RK // RET_QK_WIDTH)),
            pl.BlockSpec((RET_TILE, RET_V_WIDTH), lambda b, c: (row(b, c), COL_RV // RET_V_WIDTH)),
            pl.BlockSpec((RET_TILE, RET_V_WIDTH), lambda b, c: (row(b, c), COL_RG // RET_V_WIDTH)),
            pl.BlockSpec((1, RET_V_WIDTH), const2),
            pl.BlockSpec((RET_HEADS, RET_TILE, RET_TILE), lambda b, c: (0, 0, 0)),
            pl.BlockSpec((RET_TILE, RET_QK_WIDTH), const2),
            pl.BlockSpec((RET_TILE, RET_QK_WIDTH), const2),
            pl.BlockSpec((RET_QK_WIDTH, LANES), const2),
        ],
        out_specs=pl.BlockSpec((RET_TILE, RET_V_WIDTH), lambda b, c: (row(b, c), 0)),
        out_shape=jax.ShapeDtypeStruct((batch * seq, RET_V_WIDTH), BF16),
        scratch_shapes=[pltpu.VMEM((RET_QK_WIDTH, RET_V_DIM), F32)],
        compiler_params=_cparams(("arbitrary", "arbitrary")),
        name="retention",
    )(proj, proj, proj, proj, ret_norm_w, decay, xi, zeta, cd)


CONV_HALO = 8


def _merge_kernel(x_ref, ya_ref, cb_ref, cc_ref, ch_ref, yr_ref, ga_ref, gc_ref, gr_ref, cw_ref,
                  cbias_ref, wa_ref, wc_ref, wr_ref, wo_ref, o_ref, u_scr, *, tiles_per_seq):
    i = pl.program_id(0)
    tm = x_ref.shape[0]

    @pl.when(i % tiles_per_seq == 0)
    def _():
        u_scr[0:CONV_HALO, :] = jnp.zeros((CONV_HALO, CONV_WIDTH), F32)

    u_scr[CONV_HALO:CONV_HALO + tm, :] = cc_ref[...].astype(F32) * ch_ref[...].astype(F32)
    conv = (cw_ref[2:3, :] * u_scr[CONV_HALO:CONV_HALO + tm, :]
            + cw_ref[1:2, :] * u_scr[CONV_HALO - 1:CONV_HALO - 1 + tm, :]
            + cw_ref[0:1, :] * u_scr[CONV_HALO - 2:CONV_HALO - 2 + tm, :]
            + cbias_ref[...])
    y_conv = (cb_ref[...].astype(F32) * conv).astype(BF16)
    u_scr[0:CONV_HALO, :] = u_scr[tm:tm + CONV_HALO, :]

    def gate(g_ref):
        return jax.nn.sigmoid(g_ref[...].astype(F32))

    merged = gate(ga_ref) * jnp.dot(ya_ref[...], wa_ref[...], preferred_element_type=F32)
    merged = merged + gate(gc_ref) * jnp.dot(y_conv, wc_ref[...], preferred_element_type=F32)
    merged = merged + gate(gr_ref) * jnp.dot(yr_ref[...], wr_ref[...], preferred_element_type=F32)
    o_ref[...] = x_ref[...] + jnp.dot(merged.astype(BF16), wo_ref[...], preferred_element_type=F32)


def _merge(x2, y_attn, proj, y_ret, conv_w, conv_b, wa, wc, wr, wo, seq):
    n = x2.shape[0]
    tm = MERGE_TM
    const = lambda i: (0, 0)
    wide = lambda c: pl.BlockSpec((tm, 512), lambda i: (i, c // 512))
    gate_spec = lambda b: pl.BlockSpec((tm, D_MODEL), lambda i: (i, COL_GATES // D_MODEL + b))
    return pl.pallas_call(
        functools.partial(_merge_kernel, tiles_per_seq=seq // tm),
        grid=(n // tm,),
        in_specs=[
            pl.BlockSpec((tm, D_MODEL), lambda i: (i, 0)),
            pl.BlockSpec((tm, ATTN_WIDTH), lambda i: (i, 0)),
            wide(COL_CB), wide(COL_CC), wide(COL_CH),
            pl.BlockSpec((tm, RET_V_WIDTH), lambda i: (i, 0)),
            gate_spec(0), gate_spec(1), gate_spec(2),
            pl.BlockSpec((CONV_KERNEL, CONV_WIDTH), const),
            pl.BlockSpec((1, CONV_WIDTH), const),
            pl.BlockSpec((ATTN_WIDTH, D_MODEL), const),
            pl.BlockSpec((CONV_WIDTH, D_MODEL), const),
            pl.BlockSpec((RET_V_WIDTH, D_MODEL), const),
            pl.BlockSpec((D_MODEL, D_MODEL), const),
        ],
        out_specs=pl.BlockSpec((tm, D_MODEL), lambda i: (i, 0)),
        out_shape=jax.ShapeDtypeStruct((n, D_MODEL), F32),
        scratch_shapes=[pltpu.VMEM((tm + CONV_HALO, CONV_WIDTH), F32)],
        compiler_params=_cparams(("arbitrary",)),
        name="merge",
    )(x2, y_attn, proj, proj, proj, y_ret, proj, proj, proj, conv_w, conv_b, wa, wc, wr, wo)


def _swiglu_step(h, wg_ref, wu_ref, wd_ref, acc_scr, c):
    g = jnp.dot(h, wg_ref[...], preferred_element_type=F32)
    u = jnp.dot(h, wu_ref[...], preferred_element_type=F32)
    a = (g * jax.nn.sigmoid(g) * u).astype(BF16)
    part = jnp.dot(a, wd_ref[...], preferred_element_type=F32)

    @pl.when(c == 0)
    def _():
        acc_scr[...] = part

    @pl.when(c != 0)
    def _():
        acc_scr[...] += part


def _dense_ffn_kernel(x_ref, nw_ref, wg_ref, wu_ref, wd_ref, o_ref, h_scr, acc_scr):
    c = pl.program_id(1)

    @pl.when(c == 0)
    def _():
        x = x_ref[...]
        ms = jnp.mean(x * x, axis=-1, keepdims=True)
        h_scr[...] = (x * lax.rsqrt(ms + RMS_EPS) * nw_ref[...]).astype(BF16)

    _swiglu_step(h_scr[...], wg_ref, wu_ref, wd_ref, acc_scr, c)

    @pl.when(c == pl.num_programs(1) - 1)
    def _():
        o_ref[...] = x_ref[...] + acc_scr[...]


def _dense_ffn(x2, norm_w, wg, wu, wd):
    n = x2.shape[0]
    tm, tf = FFN_TM, FFN_TF
    return pl.pallas_call(
        _dense_ffn_kernel,
        grid=(n // tm, D_FF // tf),
        in_specs=[
            pl.BlockSpec((tm, D_MODEL), lambda i, c: (i, 0)),
            pl.BlockSpec((1, D_MODEL), lambda i, c: (0, 0)),
            pl.BlockSpec((D_MODEL, tf), lambda i, c: (0, c)),
            pl.BlockSpec((D_MODEL, tf), lambda i, c: (0, c)),
            pl.BlockSpec((tf, D_MODEL), lambda i, c: (c, 0)),
        ],
        out_specs=pl.BlockSpec((tm, D_MODEL), lambda i, c: (i, 0)),
        out_shape=jax.ShapeDtypeStruct((n, D_MODEL), F32),
        scratch_shapes=[pltpu.VMEM((tm, D_MODEL), BF16), pltpu.VMEM((tm, D_MODEL), F32)],
        compiler_params=_cparams(("arbitrary", "arbitrary")),
        name="dense_ffn",
    )(x2, norm_w, wg, wu, wd)


PACK_GROUPS = D_MODEL // (2 * LANES)
_HI_MASK = 0xFFFF0000


def _pack_rows(v, o_ref):
    half = D_MODEL // 2
    lo = pltpu.bitcast(v[:, :half].astype(BF16).astype(F32), jnp.uint32) >> 16
    hi = pltpu.bitcast(v[:, half:].astype(BF16).astype(F32), jnp.uint32) & jnp.uint32(_HI_MASK)
    w = lo | hi
    for s in range(PACK_GROUPS):
        o_ref[:, s, :] = w[:, s * LANES:(s + 1) * LANES]


def _unpack_rows(x_ref):
    words = [x_ref[:, s, :] for s in range(PACK_GROUPS)]
    lo = [pltpu.bitcast(w << 16, F32) for w in words]
    hi = [pltpu.bitcast(w & jnp.uint32(_HI_MASK), F32) for w in words]
    return jnp.concatenate(lo + hi, axis=1)


def _route_kernel(x_ref, nw_ref, wr_ref, h_ref, idx_ref, wgt_ref):
    x = x_ref[...]
    ms = jnp.mean(x * x, axis=-1, keepdims=True)
    h = x * lax.rsqrt(ms + RMS_EPS) * nw_ref[...]
    _pack_rows(h, h_ref)
    logits = jnp.dot(h, wr_ref[...], preferred_element_type=F32, precision=lax.Precision.HIGHEST)
    eid = lax.broadcasted_iota(jnp.int32, logits.shape, 1).astype(F32)
    logits = jnp.where(eid < N_EXPERTS, logits, -jnp.inf)
    m1 = jnp.max(logits, axis=-1, keepdims=True)
    i1 = jnp.min(jnp.where(logits == m1, eid, float(LANES)), axis=-1, keepdims=True)
    rest = jnp.where(eid == i1, -jnp.inf, logits)
    m2 = jnp.max(rest, axis=-1, keepdims=True)
    i2 = jnp.min(jnp.where(rest == m2, eid, float(LANES)), axis=-1, keepdims=True)
    e2 = jnp.exp(m2 - m1)
    denom = 1.0 + e2
    idx_ref[...] = jnp.where(eid == 0.0, i1, jnp.where(eid == 1.0, i2, 0.0)).astype(jnp.int32)
    wgt_ref[...] = jnp.where(eid == 0.0, 1.0 / denom, jnp.where(eid == 1.0, e2 / denom, 0.0))


def _route(x2, norm_w, w_router):
    n = x2.shape[0]
    tm = ROUTE_TM
    return pl.pallas_call(
        _route_kernel,
        grid=(n // tm,),
        in_specs=[
            pl.BlockSpec((tm, D_MODEL), lambda i: (i, 0)),
            pl.BlockSpec((1, D_MODEL), lambda i: (0, 0)),
            pl.BlockSpec((D_MODEL, LANES), lambda i: (0, 0)),
        ],
        out_specs=[
            pl.BlockSpec((tm, PACK_GROUPS, LANES), lambda i: (i, 0, 0)),
            pl.BlockSpec((tm, LANES), lambda i: (i, 0)),
            pl.BlockSpec((tm, LANES), lambda i: (i, 0)),
        ],
        out_shape=[
            jax.ShapeDtypeStruct((n, PACK_GROUPS, LANES), jnp.uint32),
            jax.ShapeDtypeStruct((n, LANES), jnp.int32),
            jax.ShapeDtypeStruct((n, LANES), F32),
        ],
        compiler_params=_cparams(("arbitrary",)),
        name="route",
    )(x2, norm_w, jnp.pad(w_router, ((0, 0), (0, LANES - N_EXPERTS))))


def _gather_kernel(ids_ref, src_ref, o_ref, sem):
    tm = o_ref.shape[0]

    def row_copy(r):
        return pltpu.make_async_copy(src_ref.at[ids_ref[0, 0, r]], o_ref.at[r], sem)

    def start(r, c):
        row_copy(r).start()
        return c

    def wait(r, c):
        row_copy(r).wait()
        return c

    lax.fori_loop(0, tm, start, 0, unroll=8)
    lax.fori_loop(0, tm, wait, 0, unroll=8)


def _gather_rows(ids, src):
    p = ids.shape[0]
    tm = GATHER_TM
    return pl.pallas_call(
        _gather_kernel,
        grid=(p // tm,),
        in_specs=[
            pl.BlockSpec((1, 1, tm), lambda i: (i, 0, 0), memory_space=pltpu.SMEM),
            pl.BlockSpec(memory_space=pl.ANY),
        ],
        out_specs=pl.BlockSpec((tm,) + src.shape[1:], lambda i: (i, 0, 0)),
        out_shape=jax.ShapeDtypeStruct((p,) + src.shape[1:], src.dtype),
        scratch_shapes=[pltpu.SemaphoreType.DMA(())],
        compiler_params=_cparams(("arbitrary",)),
        name="gather_rows",
    )(ids.reshape(p // tm, 1, tm), src)


def _moe_ffn_kernel(te_ref, nv_ref, h_ref, wg_ref, wu_ref, wd_ref, o_ref, h_scr, acc_scr):
    t = pl.program_id(0)
    c = pl.program_id(1)

    valid = t < nv_ref[0]
    last = c == pl.num_programs(1) - 1

    @pl.when(valid & (c == 0))
    def _():
        h_scr[...] = _unpack_rows(h_ref).astype(BF16)

    @pl.when(valid)
    def _():
        _swiglu_step(h_scr[...], wg_ref, wu_ref, wd_ref, acc_scr, c)

    @pl.when(valid & last)
    def _():
        _pack_rows(acc_scr[...], o_ref)

    @pl.when(jnp.logical_not(valid) & last)
    def _():
        o_ref[...] = jnp.zeros_like(o_ref)


def _moe_ffn(tile_expert, n_valid, hs, wg, wu, wd):
    p = hs.shape[0]
    tm, tf = FFN_TM, FFN_TF
    grid_spec = pltpu.PrefetchScalarGridSpec(
        num_scalar_prefetch=2,
        grid=(p // tm, D_FF // tf),
        in_specs=[
            pl.BlockSpec((tm, PACK_GROUPS, LANES), lambda t, c, te, nv: (t, 0, 0)),
            pl.BlockSpec((None, D_MODEL, tf), lambda t, c, te, nv: (te[t], 0, c)),
            pl.BlockSpec((None, D_MODEL, tf), lambda t, c, te, nv: (te[t], 0, c)),
            pl.BlockSpec((None, tf, D_MODEL), lambda t, c, te, nv: (te[t], c, 0)),
        ],
        out_specs=pl.BlockSpec((tm, PACK_GROUPS, LANES), lambda t, c, te, nv: (t, 0, 0)),
        scratch_shapes=[pltpu.VMEM((tm, D_MODEL), BF16), pltpu.VMEM((tm, D_MODEL), F32)],
    )
    return pl.pallas_call(
        _moe_ffn_kernel,
        grid_spec=grid_spec,
        out_shape=jax.ShapeDtypeStruct((p, PACK_GROUPS, LANES), jnp.uint32),
        compiler_params=_cparams(("arbitrary", "arbitrary")),
        name="moe_ffn",
    )(tile_expert, n_valid, hs, wg, wu, wd)


def _combine_kernel(x_ref, y0_ref, y1_ref, w_ref, nw_ref, o_ref, *, final_norm):
    w = w_ref[...]
    x = x_ref[...] + w[:, 0:1] * _unpack_rows(y0_ref) + w[:, 1:2] * _unpack_rows(y1_ref)
    if final_norm:
        ms = jnp.mean(x * x, axis=-1, keepdims=True)
        x = x * lax.rsqrt(ms + RMS_EPS) * nw_ref[...]
    o_ref[...] = x


def _combine(x2, y_pairs, wgt, norm_w, final_norm):
    n = x2.shape[0]
    tm = MERGE_TM
    nt = n // tm
    return pl.pallas_call(
        functools.partial(_combine_kernel, final_norm=final_norm),
        grid=(nt,),
        in_specs=[
            pl.BlockSpec((tm, D_MODEL), lambda i: (i, 0)),
            pl.BlockSpec((tm, PACK_GROUPS, LANES), lambda i: (i, 0, 0)),
            pl.BlockSpec((tm, PACK_GROUPS, LANES), lambda i: (i + nt, 0, 0)),
            pl.BlockSpec((tm, LANES), lambda i: (i, 0)),
            pl.BlockSpec((1, D_MODEL), lambda i: (0, 0)),
        ],
        out_specs=pl.BlockSpec((tm, D_MODEL), lambda i: (i, 0)),
        out_shape=jax.ShapeDtypeStruct((n, D_MODEL), F32),
        compiler_params=_cparams(("arbitrary",)),
        name="combine",
    )(x2, y_pairs, y_pairs, wgt, norm_w)


def _moe_layout(idx):
    n = idx.shape[0]
    tm = FFN_TM
    flat = idx.T.reshape(-1)
    onehot = (flat[:, None] == jnp.arange(N_EXPERTS, dtype=jnp.int32)[None, :]).astype(jnp.int32)
    cum = jnp.cumsum(onehot, axis=0)
    counts = cum[-1]
    rank = jnp.sum(cum * onehot, axis=1) - 1
    padded = ((counts + tm - 1) // tm) * tm
    ends = jnp.cumsum(padded)
    starts = ends - padded
    dest = starts[flat] + rank
    p = TOP_K * n + N_EXPERTS * tm
    token = jnp.arange(TOP_K * n, dtype=jnp.int32) % n
    row_token = jnp.zeros((p,), jnp.int32).at[dest].set(token)
    tile_start = jnp.arange(p // tm, dtype=jnp.int32) * tm
    tile_expert = jnp.minimum(jnp.sum((ends[None, :] <= tile_start[:, None]).astype(jnp.int32), axis=1),
                              N_EXPERTS - 1)
    n_valid = (ends[-1] // tm).astype(jnp.int32).reshape(1)
    return row_token, tile_expert, n_valid, dest.astype(jnp.int32)


def _moe(x2, norm_w, w_router, wg, wu, wd, final_norm_w, final_norm):
    h, idx, wgt = _route(x2, norm_w, w_router)
    row_token, tile_expert, n_valid, dest = _moe_layout(idx[:, :TOP_K])
    hs = _gather_rows(row_token, h)
    ys = _moe_ffn(tile_expert, n_valid, hs, wg, wu, wd)
    y_pairs = _gather_rows(dest, ys)
    return _combine(x2, y_pairs, wgt, final_norm_w, final_norm)


def _rope_tables(seq):
    inv_freq = 1.0 / (ROPE_THETA ** (jnp.arange(0, HEAD_DIM, 2, dtype=F32) / HEAD_DIM))
    ang = jnp.arange(seq, dtype=F32)[:, None] * inv_freq[None, :]
    cos, sin = jnp.cos(ang), jnp.sin(ang)
    cos_t = jnp.tile(cos, (1, LANES // (HEAD_DIM // 2)))
    sin_t = jnp.tile(jnp.concatenate([-sin, sin], axis=1), (1, LANES // HEAD_DIM))
    return cos_t, sin_t


def _reorder_w_in(w):
    head, rq_rk, rv_rg_gates = w[:, :3072], w[:, 3072:3584], w[:, 3584:]
    return jnp.concatenate([head, rv_rg_gates, rq_rk], axis=1).astype(BF16)


def kernel(x, mix_norm_w, w_in, conv_w, conv_b, ret_norm_w, w_br_attn, w_br_conv, w_br_ret, w_out,
           ffn_norm_w, dense_w_gate, dense_w_up, dense_w_down, moe_router, moe_w_gate, moe_w_up,
           moe_w_down, final_norm_w):
    batch, seq, d = x.shape
    depth = w_in.shape[0]
    assert d == D_MODEL and seq % PROJ_TM == 0 and depth % 2 == 0
    n = batch * seq
    cos_t, sin_t = _rope_tables(seq)
    ret_tables = _retention_tables()
    x2 = x.reshape(n, d)
    for layer in range(depth):
        proj, vt = _inproj(x2, mix_norm_w[layer].reshape(1, d), _reorder_w_in(w_in[layer]),
                           cos_t, sin_t, batch, seq)
        y_attn = _moba(proj, vt, batch, seq)
        y_ret = _retention(proj, ret_norm_w[layer].reshape(1, RET_V_WIDTH), ret_tables, batch, seq)
        x2 = _merge(x2, y_attn, proj, y_ret, conv_w[layer], conv_b[layer].reshape(1, CONV_WIDTH),
                    w_br_attn[layer].astype(BF16), w_br_conv[layer].astype(BF16),
                    w_br_ret[layer].astype(BF16), w_out[layer].astype(BF16), seq)
        i = layer // 2
        nw = ffn_norm_w[layer].reshape(1, d)
        if layer % 2 == 0:
            x2 = _dense_ffn(x2, nw, dense_w_gate[i].astype(BF16), dense_w_up[i].astype(BF16),
                            dense_w_down[i].astype(BF16))
        else:
            last = layer == depth - 1
            x2 = _moe(x2, nw, moe_router[i], moe_w_gate[i].astype(BF16), moe_w_up[i].astype(BF16),
                      moe_w_down[i].astype(BF16), final_norm_w.reshape(1, d), last)
    return x2.reshape(batch, seq, d)
```

```python
import functools
import math

import jax
import jax.numpy as jnp
import numpy as np
from jax import lax
from jax.experimental import pallas as pl
from jax.experimental.pallas import tpu as pltpu

F32 = jnp.float32
BF16 = jnp.bfloat16

D_MODEL = 1024
HEAD_DIM = 64
ROPE_THETA = 10000.0
RMS_EPS = 1e-6

ATTN_HEADS = 8
ATTN_WIDTH = ATTN_HEADS * HEAD_DIM
MOBA_BLOCK = 256
MOBA_TOPK = 3
NEG_INF = -1e30
VT_ONES = 16
VT_ROWS = HEAD_DIM + VT_ONES
QK_SCALE_LOG2 = HEAD_DIM ** -0.5 * math.log2(math.e)

CONV_WIDTH = 512
CONV_KERNEL = 3

RET_HEADS = 4
RET_QK_WIDTH = RET_HEADS * HEAD_DIM
RET_V_DIM = 2 * HEAD_DIM
RET_V_WIDTH = RET_HEADS * RET_V_DIM
RET_TILE = 256

D_FF = 3584
N_EXPERTS = 8
TOP_K = 2

LANES = 128
IN_PROJ_WIDTH = 7680
COL_AQ, COL_AK, COL_AV = 0, 512, 1024
COL_CB, COL_CC, COL_CH = 1536, 2048, 2560
COL_RV, COL_RG = 3072, 3584
COL_GATES = 4096
COL_RQ, COL_RK = 7168, 7424

PROJ_TM = 1024
PROJ_TN = 512
MERGE_TM = 512
FFN_TM = 1024
FFN_TF = 512
ROUTE_TM = 1024
VMEM_LIMIT = 56 * 1024 * 1024


def _cparams(sem):
    return pltpu.CompilerParams(dimension_semantics=sem, vmem_limit_bytes=VMEM_LIMIT)


def _rope(acc, cos, sin_signed):
    lane = lax.broadcasted_iota(jnp.int32, (1, LANES), 1)
    first_half = (lane % HEAD_DIM) < (HEAD_DIM // 2)
    outs = []
    for g in range(acc.shape[1] // LANES):
        blk = acc[:, g * LANES:(g + 1) * LANES]
        partner = jnp.where(first_half,
                            pltpu.roll(blk, LANES - HEAD_DIM // 2, 1),
                            pltpu.roll(blk, HEAD_DIM // 2, 1))
        outs.append(blk * cos + partner * sin_signed)
    return jnp.concatenate(outs, axis=1)


def _inproj_kernel(x_ref, nw_ref, w_ref, cos_ref, sin_ref, proj_ref, vt_ref, h_scr):
    j = pl.program_id(1)

    @pl.when(j == 0)
    def _():
        x = x_ref[...]
        ms = jnp.mean(x * x, axis=-1, keepdims=True)
        h_scr[...] = (x * lax.rsqrt(ms + RMS_EPS) * nw_ref[...]).astype(BF16)

    acc = jnp.dot(h_scr[...], w_ref[...].astype(BF16), preferred_element_type=F32)
    jq, jk, jv, jr = COL_AQ // PROJ_TN, COL_AK // PROJ_TN, COL_AV // PROJ_TN, COL_RQ // PROJ_TN
    scale = HEAD_DIM ** -0.5

    @pl.when(j == jq)
    def _():
        proj_ref[...] = (_rope(acc, cos_ref[...], sin_ref[...]) * QK_SCALE_LOG2).astype(BF16)

    @pl.when(j == jk)
    def _():
        proj_ref[...] = _rope(acc, cos_ref[...], sin_ref[...]).astype(BF16)

    @pl.when(j == jv)
    def _():
        proj_ref[...] = acc.astype(BF16)
        ones = jnp.ones((VT_ONES, MOBA_BLOCK), F32)
        for c in range(PROJ_TM // MOBA_BLOCK):
            v_t = acc[c * MOBA_BLOCK:(c + 1) * MOBA_BLOCK, :].T
            rows = []
            for h in range(ATTN_HEADS):
                rows += [v_t[h * HEAD_DIM:(h + 1) * HEAD_DIM], ones]
            vt_ref[c] = jnp.concatenate(rows, axis=0).astype(BF16)

    @pl.when(j == jr)
    def _():
        r = _rope(acc, cos_ref[...], sin_ref[...])
        col = lax.broadcasted_iota(jnp.int32, (1, PROJ_TN), 1)
        r = r * jnp.where(col >= RET_QK_WIDTH, scale, 1.0)
        proj_ref[...] = r.astype(BF16)

    @pl.when((j != jq) & (j != jk) & (j != jv) & (j != jr))
    def _():
        proj_ref[...] = acc.astype(BF16)


def _inproj_src_block(j):
    first_moved = COL_RV // PROJ_TN
    last = IN_PROJ_WIDTH // PROJ_TN - 1
    return jnp.where(j < first_moved, j, jnp.where(j == last, first_moved, j + 1))


def _inproj(x2, norm_w, w_in, layer, cos_t, sin_t, batch, seq):
    n = x2.shape[0]
    nst = seq // PROJ_TM
    nblk = PROJ_TM // MOBA_BLOCK
    return pl.pallas_call(
        _inproj_kernel,
        grid=(n // PROJ_TM, IN_PROJ_WIDTH // PROJ_TN),
        in_specs=[
            pl.BlockSpec((PROJ_TM, D_MODEL), lambda i, j: (i, 0)),
            pl.BlockSpec((1, D_MODEL), lambda i, j: (0, 0)),
            pl.BlockSpec((None, D_MODEL, PROJ_TN), lambda i, j: (layer, 0, _inproj_src_block(j))),
            pl.BlockSpec((PROJ_TM, LANES), lambda i, j: (i % nst, 0)),
            pl.BlockSpec((PROJ_TM, LANES), lambda i, j: (i % nst, 0)),
        ],
        out_specs=[
            pl.BlockSpec((PROJ_TM, PROJ_TN), lambda i, j: (i, j)),
            pl.BlockSpec((None, nblk, ATTN_HEADS * VT_ROWS, MOBA_BLOCK),
                         lambda i, j: (i // nst, i % nst, 0, 0)),
        ],
        out_shape=[
            jax.ShapeDtypeStruct((n, IN_PROJ_WIDTH), BF16),
            jax.ShapeDtypeStruct((batch, seq // MOBA_BLOCK, ATTN_HEADS * VT_ROWS, MOBA_BLOCK), BF16),
        ],
        scratch_shapes=[pltpu.VMEM((PROJ_TM, D_MODEL), BF16)],
        compiler_params=_cparams(("arbitrary", "arbitrary")),
        name="inproj",
    )(x2, norm_w, w_in, cos_t, sin_t)


_NT = (((1,), (1,)), ((), ()))


def _moba_kernel(q_ref, k_ref, vt_ref, o_ref, km_scr, ka_scr, sel_scr, qt_scr, *, nb):
    blk = MOBA_BLOCK
    tq = 2 * blk
    lane = lax.broadcasted_iota(jnp.int32, (1, LANES), 1)
    crow = lax.broadcasted_iota(jnp.int32, (LANES, 1), 0)

    def prep(i, c):
        rows = pl.ds(pl.multiple_of(i * blk, blk), blk)
        kb = k_ref[rows, :]
        km_scr[pl.ds(i, 1), :] = jnp.sum(kb.astype(F32), axis=0, keepdims=True) * (1.0 / blk)
        for hh in range(2):
            onehot = jnp.where(lane == HEAD_DIM * (1 - hh) + i, 1.0, 0.0).astype(BF16)
            ka_scr[hh, rows, :] = jnp.where((lane // HEAD_DIM) == hh, kb, onehot)
        return c

    lax.fori_loop(0, nb, prep, 0)

    key_pos = lax.broadcasted_iota(jnp.int32, (tq, tq), 0)
    qry_pos = lax.broadcasted_iota(jnp.int32, (tq, tq), 1)
    own_causal = ((key_pos >= blk) == (qry_pos >= blk)) & (key_pos <= qry_pos)
    prev_blk = (key_pos < blk) & (qry_pos >= blk)
    blk_id = lax.broadcasted_iota(jnp.int32, (nb, tq), 0)
    q_half = (lax.broadcasted_iota(jnp.int32, (nb, tq), 1) >= blk).astype(jnp.int32)

    def pv(t, hh, p):
        rows = slice(hh * VT_ROWS, (hh + 1) * VT_ROWS)
        pb = p.astype(BF16)
        return (jnp.dot(vt_ref[2 * t, rows, :], pb[0:blk], preferred_element_type=F32)
                + jnp.dot(vt_ref[2 * t + 1, rows, :], pb[blk:tq], preferred_element_type=F32))

    def qtile(jt, c):
        row0 = pl.multiple_of(jt * tq, tq)
        q_t = q_ref[pl.ds(row0, tq), :].astype(F32).T
        k_diag = k_ref[pl.ds(row0, tq), :]
        init = []
        for hh in range(2):
            q_m = jnp.where((crow // HEAD_DIM) == hh, q_t, 0.0)
            q_mb = q_m.astype(BF16)
            km = jnp.where((lane // HEAD_DIM) == hh, km_scr[...], 0.0)
            km_hi = km.astype(BF16)
            km_lo = (km - km_hi.astype(F32)).astype(BF16)
            gate = (jnp.dot(km_hi, q_mb, preferred_element_type=F32)
                    + jnp.dot(km_lo, q_mb, preferred_element_type=F32))
            past = blk_id < 2 * jt + q_half
            gate = jnp.where(past, gate, -jnp.inf)
            rank = jnp.zeros((nb, tq), jnp.int32)
            for ip in range(nb):
                gi = gate[ip:ip + 1, :]
                beats = (gi > gate) | ((gi == gate) & (blk_id > ip))
                rank = rank + beats.astype(jnp.int32)
            sel = (rank < MOBA_TOPK) & past
            sel_scr[hh] = sel.astype(F32)
            bias = jnp.where(sel, 0.0, NEG_INF)
            spare = HEAD_DIM * (1 - hh)
            pieces = [jnp.zeros((spare, tq), F32)] if spare else []
            pieces += [bias, jnp.zeros((LANES - spare - nb, tq), F32)]
            qt_scr[hh] = (q_m + jnp.concatenate(pieces, axis=0)).astype(BF16)

            s = jnp.dot(k_diag, q_mb, preferred_element_type=F32)
            allowed = own_causal | (prev_blk & (sel_scr[hh, pl.ds(2 * jt, 1), :] > 0.0))
            s = jnp.where(allowed, s, NEG_INF)
            m0 = jnp.max(s, axis=0, keepdims=True)
            init += [m0, pv(jt, hh, jnp.exp2(s - m0))]

        def kv_body(it, carry):
            rows = pl.ds(pl.multiple_of(it * tq, tq), tq)
            s = [jnp.dot(ka_scr[hh, rows, :], qt_scr[hh], preferred_element_type=F32) for hh in range(2)]
            m_new = [jnp.maximum(carry[2 * hh], jnp.max(s[hh], axis=0, keepdims=True)) for hh in range(2)]
            new = []
            for hh in range(2):
                alpha = jnp.exp2(carry[2 * hh] - m_new[hh])
                new += [m_new[hh], alpha * carry[2 * hh + 1] + pv(it, hh, jnp.exp2(s[hh] - m_new[hh]))]
            return tuple(new)

        fin = lax.fori_loop(0, jt, kv_body, tuple(init))
        out_t = jnp.concatenate([fin[1][0:HEAD_DIM] / fin[1][HEAD_DIM:HEAD_DIM + 1],
                                 fin[3][0:HEAD_DIM] / fin[3][HEAD_DIM:HEAD_DIM + 1]], axis=0)
        o_ref[pl.ds(row0, tq), :] = out_t.T.astype(BF16)
        return c

    lax.fori_loop(0, nb // 2, qtile, 0)


def _moba(proj, vt, batch, seq):
    nb = seq // MOBA_BLOCK
    assert nb % 8 == 0 and nb <= HEAD_DIM
    npair = ATTN_WIDTH // LANES
    return pl.pallas_call(
        functools.partial(_moba_kernel, nb=nb),
        grid=(batch, npair),
        in_specs=[
            pl.BlockSpec((seq, LANES), lambda b, p: (b, COL_AQ // LANES + p)),
            pl.BlockSpec((seq, LANES), lambda b, p: (b, COL_AK // LANES + p)),
            pl.BlockSpec((None, nb, 2 * VT_ROWS, MOBA_BLOCK), lambda b, p: (b, 0, p, 0)),
        ],
        out_specs=pl.BlockSpec((seq, LANES), lambda b, p: (b, p)),
        out_shape=jax.ShapeDtypeStruct((batch * seq, ATTN_WIDTH), BF16),
        scratch_shapes=[pltpu.VMEM((nb, LANES), F32), pltpu.VMEM((2, seq, LANES), BF16),
                        pltpu.VMEM((2, nb, 2 * MOBA_BLOCK), F32),
                        pltpu.VMEM((2, LANES, 2 * MOBA_BLOCK), BF16)],
        compiler_params=_cparams(("arbitrary", "arbitrary")),
        name="moba",
    )(proj, proj, vt)


def _ret_log_gamma():
    return [math.log1p(-(2.0 ** (-5.0 - h))) for h in range(RET_HEADS)]


def _retention_tables():
    c = RET_TILE
    lg = np.array(_ret_log_gamma(), np.float64)
    n = np.arange(c, dtype=np.float64)
    diff = n[:, None] - n[None, :]
    decay = np.where(diff[None] >= 0, np.exp(np.maximum(diff, 0.0)[None] * lg[:, None, None]), 0.0)
    head_of_lane = np.arange(RET_QK_WIDTH) // HEAD_DIM
    xi = np.exp((n + 1.0)[:, None] * lg[head_of_lane][None, :])
    zeta = np.exp((c - 1.0 - n)[:, None] * lg[head_of_lane][None, :])
    chunk_decay = np.exp(c * lg[head_of_lane])[:, None]
    return (jnp.asarray(decay, F32), jnp.asarray(xi, F32), jnp.asarray(zeta, F32),
            jnp.asarray(np.broadcast_to(chunk_decay, (RET_QK_WIDTH, LANES)), F32))


def _retention_kernel(q_ref, k_ref, v_ref, g_ref, nw_ref, decay_ref, xi_ref, zeta_ref, cd_ref,
                      o_ref, state_scr):
    @pl.when(pl.program_id(1) == 0)
    def _():
        state_scr[...] = jnp.zeros_like(state_scr)

    lane = lax.broadcasted_iota(jnp.int32, (1, LANES), 1)
    srow = lax.broadcasted_iota(jnp.int32, (LANES, 1), 0)
    for pr in range(RET_HEADS // 2):
        cols = slice(pr * LANES, (pr + 1) * LANES)
        q = q_ref[:, cols]
        k = k_ref[:, cols]
        state = state_scr[cols, :]
        state_bf = state.astype(BF16)
        q_xi = (q.astype(F32) * xi_ref[:, cols]).astype(BF16)
        kz_t = (k.astype(F32) * zeta_ref[:, cols]).T.astype(BF16)
        upd = []
        for hh in range(2):
            h = 2 * pr + hh
            hmask = (lane // HEAD_DIM) == hh
            vcols = slice(h * RET_V_DIM, (h + 1) * RET_V_DIM)
            v = v_ref[:, vcols]
            qm = jnp.where(hmask, q, jnp.zeros_like(q))
            scores = lax.dot_general(qm, k, _NT, preferred_element_type=F32) * decay_ref[h]
            o = jnp.dot(scores.astype(BF16), v, preferred_element_type=F32)
            o = o + jnp.dot(jnp.where(hmask, q_xi, jnp.zeros_like(q_xi)), state_bf,
                            preferred_element_type=F32)
            upd.append(jnp.dot(kz_t, v, preferred_element_type=F32))
            ms = jnp.mean(o * o, axis=-1, keepdims=True)
            y = o * lax.rsqrt(ms + RMS_EPS) * nw_ref[:, vcols]
            g = g_ref[:, vcols].astype(F32)
            o_ref[:, vcols] = (g * jax.nn.sigmoid(g) * y).astype(BF16)
        state_scr[cols, :] = state * cd_ref[cols, :] + jnp.where(srow < HEAD_DIM, upd[0], upd[1])


def _retention(proj, ret_norm_w, tables, batch, seq):
    nc = seq // RET_TILE
    decay, xi, zeta, cd = tables
    row = lambda b, c: b * nc + c
    const2 = lambda b, c: (0, 0)
    return pl.pallas_call(
        _retention_kernel,
        grid=(batch, nc),
        in_specs=[
            pl.BlockSpec((RET_TILE, RET_QK_WIDTH), lambda b, c: (row(b, c), COL_RQ // RET_QK_WIDTH)),
            pl.BlockSpec((RET_TILE, RET_QK_WIDTH), lambda b, c: (row(b, c), COL_RK // RET_QK_WIDTH)),
            pl.BlockSpec((RET_TILE, RET_V_WIDTH), lambda b, c: (row(b, c), COL_RV // RET_V_WIDTH)),
            pl.BlockSpec((RET_TILE, RET_V_WIDTH), lambda b, c: (row(b, c), COL_RG // RET_V_WIDTH)),
            pl.BlockSpec((1, RET_V_WIDTH), const2),
            pl.BlockSpec((RET_HEADS, RET_TILE, RET_TILE), lambda b, c: (0, 0, 0)),
            pl.BlockSpec((RET_TILE, RET_QK_WIDTH), const2),
            pl.BlockSpec((RET_TILE, RET_QK_WIDTH), const2),
            pl.BlockSpec((RET_QK_WIDTH, LANES), const2),
        ],
        out_specs=pl.BlockSpec((RET_TILE, RET_V_WIDTH), lambda b, c: (row(b, c), 0)),
        out_shape=jax.ShapeDtypeStruct((batch * seq, RET_V_WIDTH), BF16),
        scratch_shapes=[pltpu.VMEM((RET_QK_WIDTH, RET_V_DIM), F32)],
        compiler_params=_cparams(("arbitrary", "arbitrary")),
        name="retention",
    )(proj, proj, proj, proj, ret_norm_w, decay, xi, zeta, cd)


CONV_HALO = 8


def _merge_kernel(x_ref, ya_ref, cb_ref, cc_ref, ch_ref, yr_ref, ga_ref, gc_ref, gr_ref, cw_ref,
                  cbias_ref, wa_ref, wc_ref, wr_ref, wo_ref, o_ref, u_scr, *, tiles_per_seq):
    i = pl.program_id(0)
    tm = x_ref.shape[0]

    @pl.when(i % tiles_per_seq == 0)
    def _():
        u_scr[0:CONV_HALO, :] = jnp.zeros((CONV_HALO, CONV_WIDTH), F32)

    u_scr[CONV_HALO:CONV_HALO + tm, :] = cc_ref[...].astype(F32) * ch_ref[...].astype(F32)
    conv = (cw_ref[2:3, :] * u_scr[CONV_HALO:CONV_HALO + tm, :]
            + cw_ref[1:2, :] * u_scr[CONV_HALO - 1:CONV_HALO - 1 + tm, :]
            + cw_ref[0:1, :] * u_scr[CONV_HALO - 2:CONV_HALO - 2 + tm, :]
            + cbias_ref[...])
    y_conv = (cb_ref[...].astype(F32) * conv).astype(BF16)
    u_scr[0:CONV_HALO, :] = u_scr[tm:tm + CONV_HALO, :]

    def gate(g_ref):
        return jax.nn.sigmoid(g_ref[...].astype(F32))

    merged = gate(ga_ref) * jnp.dot(ya_ref[...], wa_ref[...], preferred_element_type=F32)
    merged = merged + gate(gc_ref) * jnp.dot(y_conv, wc_ref[...], preferred_element_type=F32)
    merged = merged + gate(gr_ref) * jnp.dot(yr_ref[...], wr_ref[...], preferred_element_type=F32)
    o_ref[...] = x_ref[...] + jnp.dot(merged.astype(BF16), wo_ref[...], preferred_element_type=F32)


def _merge(x2, y_attn, proj, y_ret, conv_w, conv_b, wa, wc, wr, wo, seq):
    n = x2.shape[0]
    tm = MERGE_TM
    const = lambda i: (0, 0)
    wide = lambda c: pl.BlockSpec((tm, 512), lambda i: (i, c // 512))
    gate_spec = lambda b: pl.BlockSpec((tm, D_MODEL), lambda i: (i, COL_GATES // D_MODEL + b))
    return pl.pallas_call(
        functools.partial(_merge_kernel, tiles_per_seq=seq // tm),
        grid=(n // tm,),
        in_specs=[
            pl.BlockSpec((tm, D_MODEL), lambda i: (i, 0)),
            pl.BlockSpec((tm, ATTN_WIDTH), lambda i: (i, 0)),
            wide(COL_CB), wide(COL_CC), wide(COL_CH),
            pl.BlockSpec((tm, RET_V_WIDTH), lambda i: (i, 0)),
            gate_spec(0), gate_spec(1), gate_spec(2),
            pl.BlockSpec((CONV_KERNEL, CONV_WIDTH), const),
            pl.BlockSpec((1, CONV_WIDTH), const),
            pl.BlockSpec((ATTN_WIDTH, D_MODEL), const),
            pl.BlockSpec((CONV_WIDTH, D_MODEL), const),
            pl.BlockSpec((RET_V_WIDTH, D_MODEL), const),
            pl.BlockSpec((D_MODEL, D_MODEL), const),
        ],
        out_specs=pl.BlockSpec((tm, D_MODEL), lambda i: (i, 0)),
        out_shape=jax.ShapeDtypeStruct((n, D_MODEL), F32),
        scratch_shapes=[pltpu.VMEM((tm + CONV_HALO, CONV_WIDTH), F32)],
        compiler_params=_cparams(("arbitrary",)),
        name="merge",
    )(x2, y_attn, proj, proj, proj, y_ret, proj, proj, proj, conv_w, conv_b, wa, wc, wr, wo)


def _swiglu_step(h, wg_ref, wu_ref, wd_ref, acc_scr, c):
    g = jnp.dot(h, wg_ref[...].astype(BF16), preferred_element_type=F32)
    u = jnp.dot(h, wu_ref[...].astype(BF16), preferred_element_type=F32)
    a = (g * jax.nn.sigmoid(g) * u).astype(BF16)
    part = jnp.dot(a, wd_ref[...].astype(BF16), preferred_element_type=F32)

    @pl.when(c == 0)
    def _():
        acc_scr[...] = part

    @pl.when(c != 0)
    def _():
        acc_scr[...] += part


def _dense_ffn_kernel(x_ref, nw_ref, wg_ref, wu_ref, wd_ref, o_ref, h_scr, acc_scr):
    c = pl.program_id(1)

    @pl.when(c == 0)
    def _():
        x = x_ref[...]
        ms = jnp.mean(x * x, axis=-1, keepdims=True)
        h_scr[...] = (x * lax.rsqrt(ms + RMS_EPS) * nw_ref[...]).astype(BF16)

    _swiglu_step(h_scr[...], wg_ref, wu_ref, wd_ref, acc_scr, c)

    @pl.when(c == pl.num_programs(1) - 1)
    def _():
        o_ref[...] = x_ref[...] + acc_scr[...]


def _dense_ffn(x2, norm_w, wg, wu, wd, li):
    n = x2.shape[0]
    tm, tf = FFN_TM, FFN_TF
    return pl.pallas_call(
        _dense_ffn_kernel,
        grid=(n // tm, D_FF // tf),
        in_specs=[
            pl.BlockSpec((tm, D_MODEL), lambda i, c: (i, 0)),
            pl.BlockSpec((1, D_MODEL), lambda i, c: (0, 0)),
            pl.BlockSpec((None, D_MODEL, tf), lambda i, c: (li, 0, c)),
            pl.BlockSpec((None, D_MODEL, tf), lambda i, c: (li, 0, c)),
            pl.BlockSpec((None, tf, D_MODEL), lambda i, c: (li, c, 0)),
        ],
        out_specs=pl.BlockSpec((tm, D_MODEL), lambda i, c: (i, 0)),
        out_shape=jax.ShapeDtypeStruct((n, D_MODEL), F32),
        scratch_shapes=[pltpu.VMEM((tm, D_MODEL), BF16), pltpu.VMEM((tm, D_MODEL), F32)],
        compiler_params=_cparams(("arbitrary", "arbitrary")),
        name="dense_ffn",
    )(x2, norm_w, wg, wu, wd)


PACK_GROUPS = D_MODEL // (2 * LANES)
_HI_MASK = 0xFFFF0000


def _pack_rows(v, o_ref):
    half = D_MODEL // 2
    lo = pltpu.bitcast(v[:, :half].astype(BF16).astype(F32), jnp.uint32) >> 16
    hi = pltpu.bitcast(v[:, half:].astype(BF16).astype(F32), jnp.uint32) & jnp.uint32(_HI_MASK)
    w = lo | hi
    for s in range(PACK_GROUPS):
        o_ref[:, s, :] = w[:, s * LANES:(s + 1) * LANES]


def _unpack_rows(x_ref):
    words = [x_ref[:, s, :] for s in range(PACK_GROUPS)]
    lo = [pltpu.bitcast(w << 16, F32) for w in words]
    hi = [pltpu.bitcast(w & jnp.uint32(_HI_MASK), F32) for w in words]
    return jnp.concatenate(lo + hi, axis=1)


def _route_kernel(x_ref, nw_ref, wr_ref, h_ref, idx_ref, wgt_ref):
    x = x_ref[...]
    ms = jnp.mean(x * x, axis=-1, keepdims=True)
    h = x * lax.rsqrt(ms + RMS_EPS) * nw_ref[...]
    _pack_rows(h, h_ref)
    logits = jnp.dot(h, wr_ref[...], preferred_element_type=F32, precision=lax.Precision.HIGHEST)
    eid = lax.broadcasted_iota(jnp.int32, logits.shape, 1).astype(F32)
    logits = jnp.where(eid < N_EXPERTS, logits, -jnp.inf)
    m1 = jnp.max(logits, axis=-1, keepdims=True)
    i1 = jnp.min(jnp.where(logits == m1, eid, float(LANES)), axis=-1, keepdims=True)
    rest = jnp.where(eid == i1, -jnp.inf, logits)
    m2 = jnp.max(rest, axis=-1, keepdims=True)
    i2 = jnp.min(jnp.where(rest == m2, eid, float(LANES)), axis=-1, keepdims=True)
    e2 = jnp.exp(m2 - m1)
    denom = 1.0 + e2
    idx_ref[...] = jnp.where(eid == 0.0, i1, jnp.where(eid == 1.0, i2, 0.0)).astype(jnp.int32)
    wgt_ref[...] = jnp.where(eid == 0.0, 1.0 / denom, jnp.where(eid == 1.0, e2 / denom, 0.0))


def _route(x2, norm_w, w_router):
    n = x2.shape[0]
    tm = ROUTE_TM
    return pl.pallas_call(
        _route_kernel,
        grid=(n // tm,),
        in_specs=[
            pl.BlockSpec((tm, D_MODEL), lambda i: (i, 0)),
            pl.BlockSpec((1, D_MODEL), lambda i: (0, 0)),
            pl.BlockSpec((D_MODEL, LANES), lambda i: (0, 0)),
        ],
        out_specs=[
            pl.BlockSpec((tm, PACK_GROUPS, LANES), lambda i: (i, 0, 0)),
            pl.BlockSpec((tm, LANES), lambda i: (i, 0)),
            pl.BlockSpec((tm, LANES), lambda i: (i, 0)),
        ],
        out_shape=[
            jax.ShapeDtypeStruct((n, PACK_GROUPS, LANES), jnp.uint32),
            jax.ShapeDtypeStruct((n, LANES), jnp.int32),
            jax.ShapeDtypeStruct((n, LANES), F32),
        ],
        compiler_params=_cparams(("arbitrary",)),
        name="route",
    )(x2, norm_w, jnp.pad(w_router, ((0, 0), (0, LANES - N_EXPERTS))))


MOE_STEPS = D_FF // FFN_TF
MOE_ROWS_PER_STEP = -(-FFN_TM // MOE_STEPS)
MOE_DMA_ROWS = MOE_ROWS_PER_STEP * MOE_STEPS
MOE_BUF_ROWS = -(-MOE_DMA_ROWS // 8) * 8
MOE_SPARE_TILES = 2


def _moe_ffn_kernel(te_ref, nv_ref, gid0_ref, gidn_ref, sid_ref, h3_ref, wg_ref, wu_ref, wd_ref,
                    y_init_ref, y_ref, gbuf, obuf, h_scr, acc_scr, gsem, ssem):
    t = pl.program_id(0)
    c = pl.program_id(1)
    nv = nv_ref[0]
    slot = t % 2
    other = 1 - slot
    first = c == 0

    def gather(ids_ref, row, s):
        return pltpu.make_async_copy(h3_ref.at[ids_ref[0, 0, row]], gbuf.at[s, row], gsem.at[s])

    def scatter(row, s):
        return pltpu.make_async_copy(obuf.at[s, row], y_ref.at[sid_ref[0, 0, row]], ssem.at[s])

    def tile_gather(s):
        return pltpu.make_async_copy(h3_ref.at[pl.ds(0, MOE_DMA_ROWS)],
                                     gbuf.at[s, pl.ds(0, MOE_DMA_ROWS)], gsem.at[s])

    def tile_scatter(s):
        return pltpu.make_async_copy(obuf.at[s, pl.ds(0, MOE_DMA_ROWS)],
                                     y_ref.at[pl.ds(0, MOE_DMA_ROWS)], ssem.at[s])

    @pl.when(first & (t == 0))
    def _():
        obuf[...] = jnp.zeros_like(obuf)

        def body(r, carry):
            gather(gid0_ref, r, 0).start()
            return carry

        lax.fori_loop(0, MOE_DMA_ROWS, body, 0, unroll=MOE_STEPS)

    @pl.when(first & (t <= nv))
    def _():
        tile_gather(slot).wait()

    @pl.when(first & (t >= 1) & (t <= nv + 1))
    def _():
        tile_scatter(slot).wait()

    @pl.when(first & (t < nv))
    def _():
        h_scr[...] = _unpack_rows(gbuf.at[slot, pl.ds(0, FFN_TM)]).astype(BF16)

    @pl.when(t < nv)
    def _():
        for u in range(MOE_ROWS_PER_STEP):
            row = c * MOE_ROWS_PER_STEP + u
            gather(gidn_ref, row, other).start()
            scatter(row, other).start()
        _swiglu_step(h_scr[...], wg_ref, wu_ref, wd_ref, acc_scr, c)

    @pl.when(t == nv)
    def _():
        def body(u, carry):
            scatter(c * MOE_ROWS_PER_STEP + u, other).start()
            return carry

        lax.fori_loop(0, MOE_ROWS_PER_STEP, body, 0, unroll=MOE_STEPS)

    @pl.when((t < nv) & (c == pl.num_programs(1) - 1))
    def _():
        _pack_rows(acc_scr[...], obuf.at[slot, pl.ds(0, FFN_TM)])


def _moe_ffn(tile_expert, n_valid, gid, sid, h3, wg, wu, wd, li, y_rows):
    nt = gid.shape[0]
    tm, tf = FFN_TM, FFN_TF
    ids_block = (1, 1, MOE_BUF_ROWS)
    smem = pltpu.SMEM
    wspec = lambda shape, imap: pl.BlockSpec((None, None) + shape, imap)
    grid_spec = pltpu.PrefetchScalarGridSpec(
        num_scalar_prefetch=2,
        grid=(nt, MOE_STEPS),
        in_specs=[
            pl.BlockSpec(ids_block, lambda t, c, te, nv: (0, 0, 0), memory_space=smem),
            pl.BlockSpec(ids_block, lambda t, c, te, nv: (jnp.minimum(t + 1, nt - 1), 0, 0),
                         memory_space=smem),
            pl.BlockSpec(ids_block, lambda t, c, te, nv: (t, 0, 0), memory_space=smem),
            pl.BlockSpec(memory_space=pl.ANY),
            wspec((D_MODEL, tf), lambda t, c, te, nv: (li, te[t], 0, c)),
            wspec((D_MODEL, tf), lambda t, c, te, nv: (li, te[t], 0, c)),
            wspec((tf, D_MODEL), lambda t, c, te, nv: (li, te[t], c, 0)),
            pl.BlockSpec(memory_space=pl.ANY),
        ],
        out_specs=pl.BlockSpec(memory_space=pl.ANY),
        scratch_shapes=[
            pltpu.VMEM((2, MOE_BUF_ROWS, PACK_GROUPS, LANES), jnp.uint32),
            pltpu.VMEM((2, MOE_BUF_ROWS, PACK_GROUPS, LANES), jnp.uint32),
            pltpu.VMEM((tm, D_MODEL), BF16),
            pltpu.VMEM((tm, D_MODEL), F32),
            pltpu.SemaphoreType.DMA((2,)),
            pltpu.SemaphoreType.DMA((2,)),
        ],
    )
    return pl.pallas_call(
        _moe_ffn_kernel,
        grid_spec=grid_spec,
        out_shape=jax.ShapeDtypeStruct((y_rows, PACK_GROUPS, LANES), jnp.uint32),
        input_output_aliases={9: 0},
        compiler_params=_cparams(("arbitrary", "arbitrary")),
        name="moe_ffn",
    )(tile_expert, n_valid, gid, gid, sid, h3, wg, wu, wd,
      jnp.zeros((y_rows, PACK_GROUPS, LANES), jnp.uint32))


def _combine_kernel(x_ref, y0_ref, y1_ref, w_ref, nw_ref, o_ref, *, final_norm):
    w = w_ref[...]
    x = x_ref[...] + w[:, 0:1] * _unpack_rows(y0_ref) + w[:, 1:2] * _unpack_rows(y1_ref)
    if final_norm:
        ms = jnp.mean(x * x, axis=-1, keepdims=True)
        x = x * lax.rsqrt(ms + RMS_EPS) * nw_ref[...]
    o_ref[...] = x


def _combine(x2, y_pairs, wgt, norm_w, final_norm):
    n = x2.shape[0]
    tm = MERGE_TM
    nt = n // tm
    return pl.pallas_call(
        functools.partial(_combine_kernel, final_norm=final_norm),
        grid=(nt,),
        in_specs=[
            pl.BlockSpec((tm, D_MODEL), lambda i: (i, 0)),
            pl.BlockSpec((tm, PACK_GROUPS, LANES), lambda i: (i, 0, 0)),
            pl.BlockSpec((tm, PACK_GROUPS, LANES), lambda i: (i + nt, 0, 0)),
            pl.BlockSpec((tm, LANES), lambda i: (i, 0)),
            pl.BlockSpec((1, D_MODEL), lambda i: (0, 0)),
        ],
        out_specs=pl.BlockSpec((tm, D_MODEL), lambda i: (i, 0)),
        out_shape=jax.ShapeDtypeStruct((n, D_MODEL), F32),
        compiler_params=_cparams(("arbitrary",)),
        name="combine",
    )(x2, y_pairs, y_pairs, wgt, norm_w)


def _moe_layout(idx):
    n = idx.shape[0]
    tm = FFN_TM
    flat = idx.T.reshape(-1)
    onehot = (flat[:, None] == jnp.arange(N_EXPERTS, dtype=jnp.int32)[None, :]).astype(jnp.int32)
    cum = jnp.cumsum(onehot, axis=0)
    counts = cum[-1]
    rank = jnp.sum(cum * onehot, axis=1) - 1
    padded = ((counts + tm - 1) // tm) * tm
    ends = jnp.cumsum(padded)
    starts = ends - padded
    dest = starts[flat] + rank
    pairs = TOP_K * n
    nt = pairs // tm + N_EXPERTS + MOE_SPARE_TILES
    p = nt * tm
    tile_start = jnp.arange(nt, dtype=jnp.int32) * tm
    tile_expert = jnp.minimum(jnp.sum((ends[None, :] <= tile_start[:, None]).astype(jnp.int32), axis=1),
                              N_EXPERTS - 1)
    n_valid = (ends[-1] // tm).astype(jnp.int32).reshape(1)
    real_before = jnp.cumsum(counts)[tile_expert]
    pos = jnp.arange(p, dtype=jnp.int32).reshape(nt, tm)
    pad_row = (pairs + pos - real_before[:, None]).reshape(p)
    row_pair = pad_row.at[dest].set(jnp.arange(pairs, dtype=jnp.int32))
    row_token = jnp.where(row_pair < pairs, row_pair % n, 0)
    tail = MOE_BUF_ROWS - tm
    gid = jnp.pad(row_token.reshape(nt, tm), ((0, 0), (0, tail))).reshape(nt, 1, MOE_BUF_ROWS)
    first_spare = p + jnp.arange(MOE_BUF_ROWS, dtype=jnp.int32)[None, :]
    tails = p + MOE_BUF_ROWS + jnp.arange(nt * tail, dtype=jnp.int32).reshape(nt, tail)
    sid = jnp.concatenate([row_pair.reshape(nt, tm), tails], axis=1)
    sid = jnp.concatenate([first_spare, sid], axis=0).reshape(nt + 1, 1, MOE_BUF_ROWS)
    y_rows = p + MOE_BUF_ROWS + nt * tail
    return gid, sid, tile_expert, n_valid, y_rows


def _moe(x2, norm_w, w_router, wg, wu, wd, li, final_norm_w, final_norm):
    h3, idx, wgt = _route(x2, norm_w, w_router)
    gid, sid, tile_expert, n_valid, y_rows = _moe_layout(idx[:, :TOP_K])
    y = _moe_ffn(tile_expert, n_valid, gid, sid, h3, wg, wu, wd, li, y_rows)
    return _combine(x2, y, wgt, final_norm_w, final_norm)


def _rope_tables(seq):
    inv_freq = 1.0 / (ROPE_THETA ** (jnp.arange(0, HEAD_DIM, 2, dtype=F32) / HEAD_DIM))
    ang = jnp.arange(seq, dtype=F32)[:, None] * inv_freq[None, :]
    cos, sin = jnp.cos(ang), jnp.sin(ang)
    cos_t = jnp.tile(cos, (1, LANES // (HEAD_DIM // 2)))
    sin_t = jnp.tile(jnp.concatenate([-sin, sin], axis=1), (1, LANES // HEAD_DIM))
    return cos_t, sin_t


def kernel(x, mix_norm_w, w_in, conv_w, conv_b, ret_norm_w, w_br_attn, w_br_conv, w_br_ret, w_out,
           ffn_norm_w, dense_w_gate, dense_w_up, dense_w_down, moe_router, moe_w_gate, moe_w_up,
           moe_w_down, final_norm_w):
    batch, seq, d = x.shape
    depth = w_in.shape[0]
    assert d == D_MODEL and seq % PROJ_TM == 0 and depth % 2 == 0
    n = batch * seq
    cos_t, sin_t = _rope_tables(seq)
    ret_tables = _retention_tables()
    x2 = x.reshape(n, d)
    for layer in range(depth):
        proj, vt = _inproj(x2, mix_norm_w[layer].reshape(1, d), w_in, layer, cos_t, sin_t, batch, seq)
        y_attn = _moba(proj, vt, batch, seq)
        y_ret = _retention(proj, ret_norm_w[layer].reshape(1, RET_V_WIDTH), ret_tables, batch, seq)
        x2 = _merge(x2, y_attn, proj, y_ret, conv_w[layer], conv_b[layer].reshape(1, CONV_WIDTH),
                    w_br_attn[layer].astype(BF16), w_br_conv[layer].astype(BF16),
                    w_br_ret[layer].astype(BF16), w_out[layer].astype(BF16), seq)
        i = layer // 2
        nw = ffn_norm_w[layer].reshape(1, d)
        if layer % 2 == 0:
            x2 = _dense_ffn(x2, nw, dense_w_gate, dense_w_up, dense_w_down, i)
        else:
            last = layer == depth - 1
            x2 = _moe(x2, nw, moe_router[i], moe_w_gate, moe_w_up, moe_w_down, i,
                      final_norm_w.reshape(1, d), last)
    return x2.reshape(batch, seq, d)
```

```python
import functools
import math

import jax
import jax.numpy as jnp
import numpy as np
from jax import lax
from jax.experimental import pallas as pl
from jax.experimental.pallas import tpu as pltpu

F32 = jnp.float32
BF16 = jnp.bfloat16

D_MODEL = 1024
HEAD_DIM = 64
ROPE_THETA = 10000.0
RMS_EPS = 1e-6

ATTN_HEADS = 8
ATTN_WIDTH = ATTN_HEADS * HEAD_DIM
MOBA_BLOCK = 256
MOBA_TOPK = 3
NEG_INF = -1e30
VT_ONES = 16
VT_ROWS = HEAD_DIM + VT_ONES
QK_SCALE_LOG2 = HEAD_DIM ** -0.5 * math.log2(math.e)

CONV_WIDTH = 512
CONV_KERNEL = 3

RET_HEADS = 4
RET_QK_WIDTH = RET_HEADS * HEAD_DIM
RET_V_DIM = 2 * HEAD_DIM
RET_V_WIDTH = RET_HEADS * RET_V_DIM
RET_TILE = 256

D_FF = 3584
N_EXPERTS = 8
TOP_K = 2

LANES = 128
IN_PROJ_WIDTH = 7680
COL_AQ, COL_AK, COL_AV = 0, 512, 1024
COL_CB, COL_CC, COL_CH = 1536, 2048, 2560
COL_RV, COL_RG = 3072, 3584
COL_GATES = 4096
COL_RQ, COL_RK = 7168, 7424

PROJ_TM = 1024
PROJ_TN = 512
MERGE_TM = 512
FFN_TM = 1024
FFN_TF = 512
ROUTE_TM = 1024
VMEM_LIMIT = 56 * 1024 * 1024


def _cparams(sem):
    return pltpu.CompilerParams(dimension_semantics=sem, vmem_limit_bytes=VMEM_LIMIT)


def _rope(acc, cos, sin_signed):
    lane = lax.broadcasted_iota(jnp.int32, (1, LANES), 1)
    first_half = (lane % HEAD_DIM) < (HEAD_DIM // 2)
    outs = []
    for g in range(acc.shape[1] // LANES):
        blk = acc[:, g * LANES:(g + 1) * LANES]
        partner = jnp.where(first_half,
                            pltpu.roll(blk, LANES - HEAD_DIM // 2, 1),
                            pltpu.roll(blk, HEAD_DIM // 2, 1))
        outs.append(blk * cos + partner * sin_signed)
    return jnp.concatenate(outs, axis=1)


def _inproj_kernel(x_ref, nw_ref, w_ref, cos_ref, sin_ref, proj_ref, vt_ref, h_scr):
    j = pl.program_id(1)

    @pl.when(j == 0)
    def _():
        x = x_ref[...]
        ms = jnp.mean(x * x, axis=-1, keepdims=True)
        h_scr[...] = (x * lax.rsqrt(ms + RMS_EPS) * nw_ref[...]).astype(BF16)

    acc = jnp.dot(h_scr[...], w_ref[...].astype(BF16), preferred_element_type=F32)
    jq, jk, jv, jr = COL_AQ // PROJ_TN, COL_AK // PROJ_TN, COL_AV // PROJ_TN, COL_RQ // PROJ_TN
    scale = HEAD_DIM ** -0.5

    @pl.when(j == jq)
    def _():
        proj_ref[...] = (_rope(acc, cos_ref[...], sin_ref[...]) * QK_SCALE_LOG2).astype(BF16)

    @pl.when(j == jk)
    def _():
        proj_ref[...] = _rope(acc, cos_ref[...], sin_ref[...]).astype(BF16)

    @pl.when(j == jv)
    def _():
        proj_ref[...] = acc.astype(BF16)
        ones = jnp.ones((VT_ONES, MOBA_BLOCK), F32)
        for c in range(PROJ_TM // MOBA_BLOCK):
            v_t = acc[c * MOBA_BLOCK:(c + 1) * MOBA_BLOCK, :].T
            rows = []
            for h in range(ATTN_HEADS):
                rows += [v_t[h * HEAD_DIM:(h + 1) * HEAD_DIM], ones]
            vt_ref[c] = jnp.concatenate(rows, axis=0).astype(BF16)

    @pl.when(j == jr)
    def _():
        r = _rope(acc, cos_ref[...], sin_ref[...])
        col = lax.broadcasted_iota(jnp.int32, (1, PROJ_TN), 1)
        r = r * jnp.where(col >= RET_QK_WIDTH, scale, 1.0)
        proj_ref[...] = r.astype(BF16)

    @pl.when((j != jq) & (j != jk) & (j != jv) & (j != jr))
    def _():
        proj_ref[...] = acc.astype(BF16)


def _inproj_src_block(j):
    first_moved = COL_RV // PROJ_TN
    last = IN_PROJ_WIDTH // PROJ_TN - 1
    return jnp.where(j < first_moved, j, jnp.where(j == last, first_moved, j + 1))


def _inproj(x2, norm_w, w_in, layer, cos_t, sin_t, batch, seq):
    n = x2.shape[0]
    nst = seq // PROJ_TM
    nblk = PROJ_TM // MOBA_BLOCK
    return pl.pallas_call(
        _inproj_kernel,
        grid=(n // PROJ_TM, IN_PROJ_WIDTH // PROJ_TN),
        in_specs=[
            pl.BlockSpec((PROJ_TM, D_MODEL), lambda i, j: (i, 0)),
            pl.BlockSpec((1, D_MODEL), lambda i, j: (0, 0)),
            pl.BlockSpec((None, D_MODEL, PROJ_TN), lambda i, j: (layer, 0, _inproj_src_block(j))),
            pl.BlockSpec((PROJ_TM, LANES), lambda i, j: (i % nst, 0)),
            pl.BlockSpec((PROJ_TM, LANES), lambda i, j: (i % nst, 0)),
        ],
        out_specs=[
            pl.BlockSpec((PROJ_TM, PROJ_TN), lambda i, j: (i, j)),
            pl.BlockSpec((None, nblk, ATTN_HEADS * VT_ROWS, MOBA_BLOCK),
                         lambda i, j: (i // nst, i % nst, 0, 0)),
        ],
        out_shape=[
            jax.ShapeDtypeStruct((n, IN_PROJ_WIDTH), BF16),
            jax.ShapeDtypeStruct((batch, seq // MOBA_BLOCK, ATTN_HEADS * VT_ROWS, MOBA_BLOCK), BF16),
        ],
        scratch_shapes=[pltpu.VMEM((PROJ_TM, D_MODEL), BF16)],
        compiler_params=_cparams(("arbitrary", "arbitrary")),
        name="inproj",
    )(x2, norm_w, w_in, cos_t, sin_t)


_NT = (((1,), (1,)), ((), ()))


def _moba_kernel(q_ref, k_ref, vt_ref, o_ref, km_scr, ka_scr, sel_scr, qt_scr, *, nb):
    blk = MOBA_BLOCK
    tq = 2 * blk
    lane = lax.broadcasted_iota(jnp.int32, (1, LANES), 1)
    crow = lax.broadcasted_iota(jnp.int32, (LANES, 1), 0)

    def prep(i, c):
        rows = pl.ds(pl.multiple_of(i * blk, blk), blk)
        kb = k_ref[rows, :]
        km_scr[pl.ds(i, 1), :] = jnp.sum(kb.astype(F32), axis=0, keepdims=True) * (1.0 / blk)
        for hh in range(2):
            onehot = jnp.where(lane == HEAD_DIM * (1 - hh) + i, 1.0, 0.0).astype(BF16)
            ka_scr[hh, rows, :] = jnp.where((lane // HEAD_DIM) == hh, kb, onehot)
        return c

    lax.fori_loop(0, nb, prep, 0)

    causal = (lax.broadcasted_iota(jnp.int32, (blk, blk), 0)
              <= lax.broadcasted_iota(jnp.int32, (blk, blk), 1))
    blk_id = lax.broadcasted_iota(jnp.int32, (nb, tq), 0)
    q_half = (lax.broadcasted_iota(jnp.int32, (nb, tq), 1) >= blk).astype(jnp.int32)

    def pv(t, hh, p):
        rows = slice(hh * VT_ROWS, (hh + 1) * VT_ROWS)
        pb = p.astype(BF16)
        return (jnp.dot(vt_ref[2 * t, rows, :], pb[0:blk], preferred_element_type=F32)
                + jnp.dot(vt_ref[2 * t + 1, rows, :], pb[blk:tq], preferred_element_type=F32))

    def qtile(jt, c):
        row0 = pl.multiple_of(jt * tq, tq)
        q_t = q_ref[pl.ds(row0, tq), :].astype(F32).T
        k_diag = k_ref[pl.ds(row0, tq), :]
        init = []
        for hh in range(2):
            q_m = jnp.where((crow // HEAD_DIM) == hh, q_t, 0.0)
            q_mb = q_m.astype(BF16)
            km = jnp.where((lane // HEAD_DIM) == hh, km_scr[...], 0.0)
            km_hi = km.astype(BF16)
            km_lo = (km - km_hi.astype(F32)).astype(BF16)
            gate = (jnp.dot(km_hi, q_mb, preferred_element_type=F32)
                    + jnp.dot(km_lo, q_mb, preferred_element_type=F32))
            past = blk_id < 2 * jt + q_half
            gate = jnp.where(past, gate, -jnp.inf)
            rank = jnp.zeros((nb, tq), jnp.int32)
            for ip in range(nb):
                gi = gate[ip:ip + 1, :]
                beats = (gi > gate) | ((gi == gate) & (blk_id > ip))
                rank = rank + beats.astype(jnp.int32)
            sel = (rank < MOBA_TOPK) & past
            sel_scr[hh] = sel.astype(F32)
            bias = jnp.where(sel, 0.0, NEG_INF)
            spare = HEAD_DIM * (1 - hh)
            pieces = [jnp.zeros((spare, tq), F32)] if spare else []
            pieces += [bias, jnp.zeros((LANES - spare - nb, tq), F32)]
            qt_scr[hh] = (q_m + jnp.concatenate(pieces, axis=0)).astype(BF16)

            s_top = jnp.dot(k_diag[0:blk], q_mb, preferred_element_type=F32)
            s11 = jnp.where(causal, s_top[:, 0:blk], NEG_INF)
            s12 = jnp.where(sel_scr[hh, pl.ds(2 * jt, 1), blk:tq] > 0.0, s_top[:, blk:tq], NEG_INF)
            s22 = jnp.where(causal, jnp.dot(k_diag[blk:tq], q_mb[:, blk:tq], preferred_element_type=F32),
                            NEG_INF)
            m_a = jnp.max(s11, axis=0, keepdims=True)
            m_b = jnp.maximum(jnp.max(s12, axis=0, keepdims=True), jnp.max(s22, axis=0, keepdims=True))
            rows = slice(hh * VT_ROWS, (hh + 1) * VT_ROWS)
            v_lo, v_hi = vt_ref[2 * jt, rows, :], vt_ref[2 * jt + 1, rows, :]
            acc_a = jnp.dot(v_lo, jnp.exp2(s11 - m_a).astype(BF16), preferred_element_type=F32)
            acc_b = (jnp.dot(v_lo, jnp.exp2(s12 - m_b).astype(BF16), preferred_element_type=F32)
                     + jnp.dot(v_hi, jnp.exp2(s22 - m_b).astype(BF16), preferred_element_type=F32))
            init += [jnp.concatenate([m_a, m_b], axis=1), jnp.concatenate([acc_a, acc_b], axis=1)]

        def kv_body(it, carry):
            rows = pl.ds(pl.multiple_of(it * tq, tq), tq)
            s = [jnp.dot(ka_scr[hh, rows, :], qt_scr[hh], preferred_element_type=F32) for hh in range(2)]
            m_new = [jnp.maximum(carry[2 * hh], jnp.max(s[hh], axis=0, keepdims=True)) for hh in range(2)]
            new = []
            for hh in range(2):
                alpha = jnp.exp2(carry[2 * hh] - m_new[hh])
                new += [m_new[hh], alpha * carry[2 * hh + 1] + pv(it, hh, jnp.exp2(s[hh] - m_new[hh]))]
            return tuple(new)

        fin = lax.fori_loop(0, jt, kv_body, tuple(init))
        out_t = jnp.concatenate([fin[1][0:HEAD_DIM] / fin[1][HEAD_DIM:HEAD_DIM + 1],
                                 fin[3][0:HEAD_DIM] / fin[3][HEAD_DIM:HEAD_DIM + 1]], axis=0)
        o_ref[pl.ds(row0, tq), :] = out_t.T.astype(BF16)
        return c

    lax.fori_loop(0, nb // 2, qtile, 0)


def _moba(proj, vt, batch, seq):
    nb = seq // MOBA_BLOCK
    assert nb % 8 == 0 and nb <= HEAD_DIM
    npair = ATTN_WIDTH // LANES
    return pl.pallas_call(
        functools.partial(_moba_kernel, nb=nb),
        grid=(batch, npair),
        in_specs=[
            pl.BlockSpec((seq, LANES), lambda b, p: (b, COL_AQ // LANES + p)),
            pl.BlockSpec((seq, LANES), lambda b, p: (b, COL_AK // LANES + p)),
            pl.BlockSpec((None, nb, 2 * VT_ROWS, MOBA_BLOCK), lambda b, p: (b, 0, p, 0)),
        ],
        out_specs=pl.BlockSpec((seq, LANES), lambda b, p: (b, p)),
        out_shape=jax.ShapeDtypeStruct((batch * seq, ATTN_WIDTH), BF16),
        scratch_shapes=[pltpu.VMEM((nb, LANES), F32), pltpu.VMEM((2, seq, LANES), BF16),
                        pltpu.VMEM((2, nb, 2 * MOBA_BLOCK), F32),
                        pltpu.VMEM((2, LANES, 2 * MOBA_BLOCK), BF16)],
        compiler_params=_cparams(("arbitrary", "arbitrary")),
        name="moba",
    )(proj, proj, vt)


def _ret_log_gamma():
    return [math.log1p(-(2.0 ** (-5.0 - h))) for h in range(RET_HEADS)]


def _retention_tables():
    c = RET_TILE
    lg = np.array(_ret_log_gamma(), np.float64)
    n = np.arange(c, dtype=np.float64)
    diff = n[:, None] - n[None, :]
    decay = np.where(diff[None] >= 0, np.exp(np.maximum(diff, 0.0)[None] * lg[:, None, None]), 0.0)
    head_of_lane = np.arange(RET_QK_WIDTH) // HEAD_DIM
    xi = np.exp((n + 1.0)[:, None] * lg[head_of_lane][None, :])
    zeta = np.exp((c - 1.0 - n)[:, None] * lg[head_of_lane][None, :])
    chunk_decay = np.exp(c * lg[head_of_lane])[:, None]
    return (jnp.asarray(decay, F32), jnp.asarray(xi, F32), jnp.asarray(zeta, F32),
            jnp.asarray(np.broadcast_to(chunk_decay, (RET_QK_WIDTH, LANES)), F32))


def _retention_kernel(q_ref, k_ref, v_ref, g_ref, nw_ref, decay_ref, xi_ref, zeta_ref, cd_ref,
                      o_ref, state_scr):
    @pl.when(pl.program_id(1) == 0)
    def _():
        state_scr[...] = jnp.zeros_like(state_scr)

    lane = lax.broadcasted_iota(jnp.int32, (1, LANES), 1)
    srow = lax.broadcasted_iota(jnp.int32, (LANES, 1), 0)
    for pr in range(RET_HEADS // 2):
        cols = slice(pr * LANES, (pr + 1) * LANES)
        q = q_ref[:, cols]
        k = k_ref[:, cols]
        state = state_scr[cols, :]
        state_bf = state.astype(BF16)
        q_xi = (q.astype(F32) * xi_ref[:, cols]).astype(BF16)
        kz_t = (k.astype(F32) * zeta_ref[:, cols]).T.astype(BF16)
        upd = []
        for hh in range(2):
            h = 2 * pr + hh
            hmask = (lane // HEAD_DIM) == hh
            vcols = slice(h * RET_V_DIM, (h + 1) * RET_V_DIM)
            v = v_ref[:, vcols]
            qm = jnp.where(hmask, q, jnp.zeros_like(q))
            scores = lax.dot_general(qm, k, _NT, preferred_element_type=F32) * decay_ref[h]
            o = jnp.dot(scores.astype(BF16), v, preferred_element_type=F32)
            o = o + jnp.dot(jnp.where(hmask, q_xi, jnp.zeros_like(q_xi)), state_bf,
                            preferred_element_type=F32)
            upd.append(jnp.dot(kz_t, v, preferred_element_type=F32))
            ms = jnp.mean(o * o, axis=-1, keepdims=True)
            y = o * lax.rsqrt(ms + RMS_EPS) * nw_ref[:, vcols]
            g = g_ref[:, vcols].astype(F32)
            o_ref[:, vcols] = (g * jax.nn.sigmoid(g) * y).astype(BF16)
        state_scr[cols, :] = state * cd_ref[cols, :] + jnp.where(srow < HEAD_DIM, upd[0], upd[1])


def _retention(proj, ret_norm_w, tables, batch, seq):
    nc = seq // RET_TILE
    decay, xi, zeta, cd = tables
    row = lambda b, c: b * nc + c
    const2 = lambda b, c: (0, 0)
    return pl.pallas_call(
        _retention_kernel,
        grid=(batch, nc),
        in_specs=[
            pl.BlockSpec((RET_TILE, RET_QK_WIDTH), lambda b, c: (row(b, c), COL_RQ // RET_QK_WIDTH)),
            pl.BlockSpec((RET_TILE, RET_QK_WIDTH), lambda b, c: (row(b, c), COL_RK // RET_QK_WIDTH)),
            pl.BlockSpec((RET_TILE, RET_V_WIDTH), lambda b, c: (row(b, c), COL_RV // RET_V_WIDTH)),
            pl.BlockSpec((RET_TILE, RET_V_WIDTH), lambda b, c: (row(b, c), COL_RG // RET_V_WIDTH)),
            pl.BlockSpec((1, RET_V_WIDTH), const2),
            pl.BlockSpec((RET_HEADS, RET_TILE, RET_TILE), lambda b, c: (0, 0, 0)),
            pl.BlockSpec((RET_TILE, RET_QK_WIDTH), const2),
            pl.BlockSpec((RET_TILE, RET_QK_WIDTH), const2),
            pl.BlockSpec((RET_QK_WIDTH, LANES), const2),
        ],
        out_specs=pl.BlockSpec((RET_TILE, RET_V_WIDTH), lambda b, c: (row(b, c), 0)),
        out_shape=jax.ShapeDtypeStruct((batch * seq, RET_V_WIDTH), BF16),
        scratch_shapes=[pltpu.VMEM((RET_QK_WIDTH, RET_V_DIM), F32)],
        compiler_params=_cparams(("arbitrary", "arbitrary")),
        name="retention",
    )(proj, proj, proj, proj, ret_norm_w, decay, xi, zeta, cd)


CONV_HALO = 8


def _merge_kernel(x_ref, ya_ref, cb_ref, cc_ref, ch_ref, yr_ref, ga_ref, gc_ref, gr_ref, cw_ref,
                  cbias_ref, wa_ref, wc_ref, wr_ref, wo_ref, o_ref, u_scr, *, tiles_per_seq):
    i = pl.program_id(0)
    tm = x_ref.shape[0]

    @pl.when(i % tiles_per_seq == 0)
    def _():
        u_scr[0:CONV_HALO, :] = jnp.zeros((CONV_HALO, CONV_WIDTH), F32)

    u_scr[CONV_HALO:CONV_HALO + tm, :] = cc_ref[...].astype(F32) * ch_ref[...].astype(F32)
    conv = (cw_ref[2:3, :] * u_scr[CONV_HALO:CONV_HALO + tm, :]
            + cw_ref[1:2, :] * u_scr[CONV_HALO - 1:CONV_HALO - 1 + tm, :]
            + cw_ref[0:1, :] * u_scr[CONV_HALO - 2:CONV_HALO - 2 + tm, :]
            + cbias_ref[...])
    y_conv = (cb_ref[...].astype(F32) * conv).astype(BF16)
    u_scr[0:CONV_HALO, :] = u_scr[tm:tm + CONV_HALO, :]

    def gate(g_ref):
        return jax.nn.sigmoid(g_ref[...].astype(F32))

    merged = gate(ga_ref) * jnp.dot(ya_ref[...], wa_ref[...], preferred_element_type=F32)
    merged = merged + gate(gc_ref) * jnp.dot(y_conv, wc_ref[...], preferred_element_type=F32)
    merged = merged + gate(gr_ref) * jnp.dot(yr_ref[...], wr_ref[...], preferred_element_type=F32)
    o_ref[...] = x_ref[...] + jnp.dot(merged.astype(BF16), wo_ref[...], preferred_element_type=F32)


def _merge(x2, y_attn, proj, y_ret, conv_w, conv_b, wa, wc, wr, wo, seq):
    n = x2.shape[0]
    tm = MERGE_TM
    const = lambda i: (0, 0)
    wide = lambda c: pl.BlockSpec((tm, 512), lambda i: (i, c // 512))
    gate_spec = lambda b: pl.BlockSpec((tm, D_MODEL), lambda i: (i, COL_GATES // D_MODEL + b))
    return pl.pallas_call(
        functools.partial(_merge_kernel, tiles_per_seq=seq // tm),
        grid=(n // tm,),
        in_specs=[
            pl.BlockSpec((tm, D_MODEL), lambda i: (i, 0)),
            pl.BlockSpec((tm, ATTN_WIDTH), lambda i: (i, 0)),
            wide(COL_CB), wide(COL_CC), wide(COL_CH),
            pl.BlockSpec((tm, RET_V_WIDTH), lambda i: (i, 0)),
            gate_spec(0), gate_spec(1), gate_spec(2),
            pl.BlockSpec((CONV_KERNEL, CONV_WIDTH), const),
            pl.BlockSpec((1, CONV_WIDTH), const),
            pl.BlockSpec((ATTN_WIDTH, D_MODEL), const),
            pl.BlockSpec((CONV_WIDTH, D_MODEL), const),
            pl.BlockSpec((RET_V_WIDTH, D_MODEL), const),
            pl.BlockSpec((D_MODEL, D_MODEL), const),
        ],
        out_specs=pl.BlockSpec((tm, D_MODEL), lambda i: (i, 0)),
        out_shape=jax.ShapeDtypeStruct((n, D_MODEL), F32),
        scratch_shapes=[pltpu.VMEM((tm + CONV_HALO, CONV_WIDTH), F32)],
        compiler_params=_cparams(("arbitrary",)),
        name="merge",
    )(x2, y_attn, proj, proj, proj, y_ret, proj, proj, proj, conv_w, conv_b, wa, wc, wr, wo)


def _swiglu_step(h, wg_ref, wu_ref, wd_ref, acc_scr, c):
    g = jnp.dot(h, wg_ref[...].astype(BF16), preferred_element_type=F32)
    u = jnp.dot(h, wu_ref[...].astype(BF16), preferred_element_type=F32)
    a = (g * jax.nn.sigmoid(g) * u).astype(BF16)
    part = jnp.dot(a, wd_ref[...].astype(BF16), preferred_element_type=F32)

    @pl.when(c == 0)
    def _():
        acc_scr[...] = part

    @pl.when(c != 0)
    def _():
        acc_scr[...] += part


def _dense_ffn_kernel(x_ref, nw_ref, wg_ref, wu_ref, wd_ref, o_ref, h_scr, acc_scr):
    c = pl.program_id(1)

    @pl.when(c == 0)
    def _():
        x = x_ref[...]
        ms = jnp.mean(x * x, axis=-1, keepdims=True)
        h_scr[...] = (x * lax.rsqrt(ms + RMS_EPS) * nw_ref[...]).astype(BF16)

    _swiglu_step(h_scr[...], wg_ref, wu_ref, wd_ref, acc_scr, c)

    @pl.when(c == pl.num_programs(1) - 1)
    def _():
        o_ref[...] = x_ref[...] + acc_scr[...]


def _dense_ffn(x2, norm_w, wg, wu, wd, li):
    n = x2.shape[0]
    tm, tf = FFN_TM, FFN_TF
    return pl.pallas_call(
        _dense_ffn_kernel,
        grid=(n // tm, D_FF // tf),
        in_specs=[
            pl.BlockSpec((tm, D_MODEL), lambda i, c: (i, 0)),
            pl.BlockSpec((1, D_MODEL), lambda i, c: (0, 0)),
            pl.BlockSpec((None, D_MODEL, tf), lambda i, c: (li, 0, c)),
            pl.BlockSpec((None, D_MODEL, tf), lambda i, c: (li, 0, c)),
            pl.BlockSpec((None, tf, D_MODEL), lambda i, c: (li, c, 0)),
        ],
        out_specs=pl.BlockSpec((tm, D_MODEL), lambda i, c: (i, 0)),
        out_shape=jax.ShapeDtypeStruct((n, D_MODEL), F32),
        scratch_shapes=[pltpu.VMEM((tm, D_MODEL), BF16), pltpu.VMEM((tm, D_MODEL), F32)],
        compiler_params=_cparams(("arbitrary", "arbitrary")),
        name="dense_ffn",
    )(x2, norm_w, wg, wu, wd)


SLAB = D_MODEL // LANES


def _pack_rows(v, o_ref):
    rows = v.shape[0]
    for s in range(SLAB):
        o_ref[pl.ds(s, rows, stride=SLAB), :] = v[:, s * LANES:(s + 1) * LANES]


def _unpack_rows(x_ref, rows):
    return jnp.concatenate([x_ref[pl.ds(s, rows, stride=SLAB), :] for s in range(SLAB)], axis=1)


def _route_kernel(x_ref, nw_ref, wr_ref, h_ref, idx_ref, wgt_ref):
    x = x_ref[...]
    ms = jnp.mean(x * x, axis=-1, keepdims=True)
    h = x * lax.rsqrt(ms + RMS_EPS) * nw_ref[...]
    _pack_rows(h, h_ref)
    w = wr_ref[...]
    h_hi, w_hi = h.astype(BF16), w.astype(BF16)
    h_lo, w_lo = (h - h_hi.astype(F32)).astype(BF16), (w - w_hi.astype(F32)).astype(BF16)
    logits = (jnp.dot(h_hi, w_hi, preferred_element_type=F32)
              + jnp.dot(h_lo, w_hi, preferred_element_type=F32)
              + jnp.dot(h_hi, w_lo, preferred_element_type=F32))
    eid = lax.broadcasted_iota(jnp.int32, logits.shape, 1).astype(F32)
    logits = jnp.where(eid < N_EXPERTS, logits, -jnp.inf)
    m1 = jnp.max(logits, axis=-1, keepdims=True)
    i1 = jnp.min(jnp.where(logits == m1, eid, float(LANES)), axis=-1, keepdims=True)
    rest = jnp.where(eid == i1, -jnp.inf, logits)
    m2 = jnp.max(rest, axis=-1, keepdims=True)
    i2 = jnp.min(jnp.where(rest == m2, eid, float(LANES)), axis=-1, keepdims=True)
    e2 = jnp.exp(m2 - m1)
    denom = 1.0 + e2
    idx_ref[...] = jnp.where(eid == 0.0, i1, jnp.where(eid == 1.0, i2, 0.0)).astype(jnp.int32)
    wgt_ref[...] = jnp.where(eid == 0.0, 1.0 / denom, jnp.where(eid == 1.0, e2 / denom, 0.0))


def _route(x2, norm_w, w_router):
    n = x2.shape[0]
    tm = ROUTE_TM
    return pl.pallas_call(
        _route_kernel,
        grid=(n // tm,),
        in_specs=[
            pl.BlockSpec((tm, D_MODEL), lambda i: (i, 0)),
            pl.BlockSpec((1, D_MODEL), lambda i: (0, 0)),
            pl.BlockSpec((D_MODEL, LANES), lambda i: (0, 0)),
        ],
        out_specs=[
            pl.BlockSpec((tm * SLAB, LANES), lambda i: (i, 0)),
            pl.BlockSpec((tm, LANES), lambda i: (i, 0)),
            pl.BlockSpec((tm, LANES), lambda i: (i, 0)),
        ],
        out_shape=[
            jax.ShapeDtypeStruct((n * SLAB, LANES), F32),
            jax.ShapeDtypeStruct((n, LANES), jnp.int32),
            jax.ShapeDtypeStruct((n, LANES), F32),
        ],
        compiler_params=_cparams(("arbitrary",)),
        name="route",
    )(x2, norm_w, jnp.pad(w_router, ((0, 0), (0, LANES - N_EXPERTS))))


MOE_STEPS = D_FF // FFN_TF
MOE_ROWS_PER_STEP = -(-FFN_TM // MOE_STEPS)
MOE_DMA_ROWS = MOE_ROWS_PER_STEP * MOE_STEPS
MOE_BUF_ROWS = -(-MOE_DMA_ROWS // 8) * 8
MOE_SPARE_TILES = 2


def _moe_ffn_kernel(te_ref, nv_ref, gid0_ref, gidn_ref, sid_ref, h3_ref, wg_ref, wu_ref, wd_ref,
                    y_init_ref, y_ref, gbuf, obuf, h_scr, acc_scr, gsem, ssem):
    t = pl.program_id(0)
    c = pl.program_id(1)
    nv = nv_ref[0]
    slot = t % 2
    other = 1 - slot
    first = c == 0

    def slab(ref, start):
        return ref.at[pl.ds(pl.multiple_of(start, SLAB), SLAB)]

    def gather(ids_ref, row, s):
        return pltpu.make_async_copy(slab(h3_ref, ids_ref[0, 0, row]), slab(gbuf.at[s], row * SLAB),
                                     gsem.at[s])

    def scatter(row, s):
        return pltpu.make_async_copy(slab(obuf.at[s], row * SLAB), slab(y_ref, sid_ref[0, 0, row]),
                                     ssem.at[s])

    def tile_gather(s):
        return pltpu.make_async_copy(h3_ref.at[pl.ds(0, MOE_DMA_ROWS * SLAB)],
                                     gbuf.at[s, pl.ds(0, MOE_DMA_ROWS * SLAB)], gsem.at[s])

    def tile_scatter(s):
        return pltpu.make_async_copy(obuf.at[s, pl.ds(0, MOE_DMA_ROWS * SLAB)],
                                     y_ref.at[pl.ds(0, MOE_DMA_ROWS * SLAB)], ssem.at[s])

    @pl.when(first & (t == 0))
    def _():
        obuf[...] = jnp.zeros_like(obuf)

        def body(r, carry):
            gather(gid0_ref, r, 0).start()
            return carry

        lax.fori_loop(0, MOE_DMA_ROWS, body, 0, unroll=MOE_STEPS)

    @pl.when(first & (t <= nv))
    def _():
        tile_gather(slot).wait()

    @pl.when(first & (t >= 1) & (t <= nv + 1))
    def _():
        tile_scatter(slot).wait()

    @pl.when(first & (t < nv))
    def _():
        h_scr[...] = _unpack_rows(gbuf.at[slot], FFN_TM).astype(BF16)

    @pl.when(t < nv)
    def _():
        for u in range(MOE_ROWS_PER_STEP):
            row = c * MOE_ROWS_PER_STEP + u
            gather(gidn_ref, row, other).start()
            scatter(row, other).start()
        _swiglu_step(h_scr[...], wg_ref, wu_ref, wd_ref, acc_scr, c)

    @pl.when(t == nv)
    def _():
        def body(u, carry):
            scatter(c * MOE_ROWS_PER_STEP + u, other).start()
            return carry

        lax.fori_loop(0, MOE_ROWS_PER_STEP, body, 0, unroll=MOE_STEPS)

    @pl.when((t < nv) & (c == pl.num_programs(1) - 1))
    def _():
        _pack_rows(acc_scr[...], obuf.at[slot])


def _moe_ffn(tile_expert, n_valid, gid, sid, h3, wg, wu, wd, li, y_rows):
    nt = gid.shape[0]
    tm, tf = FFN_TM, FFN_TF
    ids_block = (1, 1, MOE_BUF_ROWS)
    smem = pltpu.SMEM
    wspec = lambda shape, imap: pl.BlockSpec((None, None) + shape, imap)
    grid_spec = pltpu.PrefetchScalarGridSpec(
        num_scalar_prefetch=2,
        grid=(nt, MOE_STEPS),
        in_specs=[
            pl.BlockSpec(ids_block, lambda t, c, te, nv: (0, 0, 0), memory_space=smem),
            pl.BlockSpec(ids_block, lambda t, c, te, nv: (jnp.minimum(t + 1, nt - 1), 0, 0),
                         memory_space=smem),
            pl.BlockSpec(ids_block, lambda t, c, te, nv: (t, 0, 0), memory_space=smem),
            pl.BlockSpec(memory_space=pl.ANY),
            wspec((D_MODEL, tf), lambda t, c, te, nv: (li, te[t], 0, c)),
            wspec((D_MODEL, tf), lambda t, c, te, nv: (li, te[t], 0, c)),
            wspec((tf, D_MODEL), lambda t, c, te, nv: (li, te[t], c, 0)),
            pl.BlockSpec(memory_space=pl.ANY),
        ],
        out_specs=pl.BlockSpec(memory_space=pl.ANY),
        scratch_shapes=[
            pltpu.VMEM((2, MOE_BUF_ROWS * SLAB, LANES), F32),
            pltpu.VMEM((2, MOE_BUF_ROWS * SLAB, LANES), F32),
            pltpu.VMEM((tm, D_MODEL), BF16),
            pltpu.VMEM((tm, D_MODEL), F32),
            pltpu.SemaphoreType.DMA((2,)),
            pltpu.SemaphoreType.DMA((2,)),
        ],
    )
    return pl.pallas_call(
        _moe_ffn_kernel,
        grid_spec=grid_spec,
        out_shape=jax.ShapeDtypeStruct((y_rows * SLAB, LANES), F32),
        input_output_aliases={9: 0},
        compiler_params=_cparams(("arbitrary", "arbitrary")),
        name="moe_ffn",
    )(tile_expert, n_valid, gid * SLAB, gid * SLAB, sid * SLAB, h3, wg, wu, wd,
      jnp.zeros((y_rows * SLAB, LANES), F32))


def _combine_kernel(x_ref, y0_ref, y1_ref, w_ref, nw_ref, o_ref, *, final_norm):
    w = w_ref[...]
    rows = x_ref.shape[0]
    x = x_ref[...] + w[:, 0:1] * _unpack_rows(y0_ref, rows) + w[:, 1:2] * _unpack_rows(y1_ref, rows)
    if final_norm:
        ms = jnp.mean(x * x, axis=-1, keepdims=True)
        x = x * lax.rsqrt(ms + RMS_EPS) * nw_ref[...]
    o_ref[...] = x


def _combine(x2, y_pairs, wgt, norm_w, final_norm):
    n = x2.shape[0]
    tm = MERGE_TM
    nt = n // tm
    return pl.pallas_call(
        functools.partial(_combine_kernel, final_norm=final_norm),
        grid=(nt,),
        in_specs=[
            pl.BlockSpec((tm, D_MODEL), lambda i: (i, 0)),
            pl.BlockSpec((tm * SLAB, LANES), lambda i: (i, 0)),
            pl.BlockSpec((tm * SLAB, LANES), lambda i: (i + nt, 0)),
            pl.BlockSpec((tm, LANES), lambda i: (i, 0)),
            pl.BlockSpec((1, D_MODEL), lambda i: (0, 0)),
        ],
        out_specs=pl.BlockSpec((tm, D_MODEL), lambda i: (i, 0)),
        out_shape=jax.ShapeDtypeStruct((n, D_MODEL), F32),
        compiler_params=_cparams(("arbitrary",)),
        name="combine",
    )(x2, y_pairs, y_pairs, wgt, norm_w)


def _moe_layout(idx):
    n = idx.shape[0]
    tm = FFN_TM
    flat = idx.T.reshape(-1)
    onehot = (flat[:, None] == jnp.arange(N_EXPERTS, dtype=jnp.int32)[None, :]).astype(jnp.int32)
    cum = jnp.cumsum(onehot, axis=0)
    counts = cum[-1]
    rank = jnp.sum(cum * onehot, axis=1) - 1
    padded = ((counts + tm - 1) // tm) * tm
    ends = jnp.cumsum(padded)
    starts = ends - padded
    dest = starts[flat] + rank
    pairs = TOP_K * n
    nt = pairs // tm + N_EXPERTS + MOE_SPARE_TILES
    p = nt * tm
    tile_start = jnp.arange(nt, dtype=jnp.int32) * tm
    tile_expert = jnp.minimum(jnp.sum((ends[None, :] <= tile_start[:, None]).astype(jnp.int32), axis=1),
                              N_EXPERTS - 1)
    n_valid = (ends[-1] // tm).astype(jnp.int32).reshape(1)
    real_before = jnp.cumsum(counts)[tile_expert]
    pos = jnp.arange(p, dtype=jnp.int32).reshape(nt, tm)
    pad_row = (pairs + pos - real_before[:, None]).reshape(p)
    row_pair = pad_row.at[dest].set(jnp.arange(pairs, dtype=jnp.int32))
    row_token = jnp.where(row_pair < pairs, row_pair % n, 0)
    tail = MOE_BUF_ROWS - tm
    gid = jnp.pad(row_token.reshape(nt, tm), ((0, 0), (0, tail))).reshape(nt, 1, MOE_BUF_ROWS)
    first_spare = p + jnp.arange(MOE_BUF_ROWS, dtype=jnp.int32)[None, :]
    tails = p + MOE_BUF_ROWS + jnp.arange(nt * tail, dtype=jnp.int32).reshape(nt, tail)
    sid = jnp.concatenate([row_pair.reshape(nt, tm), tails], axis=1)
    sid = jnp.concatenate([first_spare, sid], axis=0).reshape(nt + 1, 1, MOE_BUF_ROWS)
    y_rows = p + MOE_BUF_ROWS + nt * tail
    return gid, sid, tile_expert, n_valid, y_rows


def _moe(x2, norm_w, w_router, wg, wu, wd, li, final_norm_w, final_norm):
    h3, idx, wgt = _route(x2, norm_w, w_router)
    gid, sid, tile_expert, n_valid, y_rows = _moe_layout(idx[:, :TOP_K])
    y = _moe_ffn(tile_expert, n_valid, gid, sid, h3, wg, wu, wd, li, y_rows)
    return _combine(x2, y, wgt, final_norm_w, final_norm)


def _rope_tables(seq):
    inv_freq = 1.0 / (ROPE_THETA ** (jnp.arange(0, HEAD_DIM, 2, dtype=F32) / HEAD_DIM))
    ang = jnp.arange(seq, dtype=F32)[:, None] * inv_freq[None, :]
    cos, sin = jnp.cos(ang), jnp.sin(ang)
    cos_t = jnp.tile(cos, (1, LANES // (HEAD_DIM // 2)))
    sin_t = jnp.tile(jnp.concatenate([-sin, sin], axis=1), (1, LANES // HEAD_DIM))
    return cos_t, sin_t


def kernel(x, mix_norm_w, w_in, conv_w, conv_b, ret_norm_w, w_br_attn, w_br_conv, w_br_ret, w_out,
           ffn_norm_w, dense_w_gate, dense_w_up, dense_w_down, moe_router, moe_w_gate, moe_w_up,
           moe_w_down, final_norm_w):
    batch, seq, d = x.shape
    depth = w_in.shape[0]
    assert d == D_MODEL and seq % PROJ_TM == 0 and depth % 2 == 0
    n = batch * seq
    cos_t, sin_t = _rope_tables(seq)
    ret_tables = _retention_tables()
    x2 = x.reshape(n, d)
    for layer in range(depth):
        proj, vt = _inproj(x2, mix_norm_w[layer].reshape(1, d), w_in, layer, cos_t, sin_t, batch, seq)
        y_attn = _moba(proj, vt, batch, seq)
        y_ret = _retention(proj, ret_norm_w[layer].reshape(1, RET_V_WIDTH), ret_tables, batch, seq)
        x2 = _merge(x2, y_attn, proj, y_ret, conv_w[layer], conv_b[layer].reshape(1, CONV_WIDTH),
                    w_br_attn[layer].astype(BF16), w_br_conv[layer].astype(BF16),
                    w_br_ret[layer].astype(BF16), w_out[layer].astype(BF16), seq)
        i = layer // 2
        nw = ffn_norm_w[layer].reshape(1, d)
        if layer % 2 == 0:
            x2 = _dense_ffn(x2, nw, dense_w_gate, dense_w_up, dense_w_down, i)
        else:
            last = layer == depth - 1
            x2 = _moe(x2, nw, moe_router[i], moe_w_gate, moe_w_up, moe_w_down, i,
                      final_norm_w.reshape(1, d), last)
    return x2.reshape(batch, seq, d)
```

```python
import functools
import math

import jax
import jax.numpy as jnp
import numpy as np
from jax import lax
from jax.experimental import pallas as pl
from jax.experimental.pallas import tpu as pltpu

F32 = jnp.float32
BF16 = jnp.bfloat16

D_MODEL = 1024
HEAD_DIM = 64
ROPE_THETA = 10000.0
RMS_EPS = 1e-6

ATTN_HEADS = 8
ATTN_WIDTH = ATTN_HEADS * HEAD_DIM
MOBA_BLOCK = 256
MOBA_TOPK = 3
NEG_INF = -1e30
MOBA_HEADS_PER_STEP = 4
VT_ONES = 16
VT_ROWS = HEAD_DIM + VT_ONES
QK_SCALE_LOG2 = HEAD_DIM ** -0.5 * math.log2(math.e)

CONV_WIDTH = 512
CONV_KERNEL = 3

RET_HEADS = 4
RET_QK_WIDTH = RET_HEADS * HEAD_DIM
RET_V_DIM = 2 * HEAD_DIM
RET_V_WIDTH = RET_HEADS * RET_V_DIM
RET_TILE = 256

D_FF = 3584
N_EXPERTS = 8
TOP_K = 2

LANES = 128
IN_PROJ_WIDTH = 7680
COL_AQ, COL_AK, COL_AV = 0, 512, 1024
COL_CB, COL_CC, COL_CH = 1536, 2048, 2560
COL_RV, COL_RG = 3072, 3584
COL_GATES = 4096
COL_RQ, COL_RK = 7168, 7424

PROJ_TM = 2048
PROJ_TN = 512
MERGE_TM = 512
FFN_TM = 1024
FFN_TF = 512
ROUTE_TM = 1024
VMEM_LIMIT = 56 * 1024 * 1024


def _cparams(sem):
    return pltpu.CompilerParams(dimension_semantics=sem, vmem_limit_bytes=VMEM_LIMIT)


def _rope(acc, cos, sin_signed):
    lane = lax.broadcasted_iota(jnp.int32, (1, LANES), 1)
    first_half = (lane % HEAD_DIM) < (HEAD_DIM // 2)
    outs = []
    for g in range(acc.shape[1] // LANES):
        blk = acc[:, g * LANES:(g + 1) * LANES]
        partner = jnp.where(first_half,
                            pltpu.roll(blk, LANES - HEAD_DIM // 2, 1),
                            pltpu.roll(blk, HEAD_DIM // 2, 1))
        outs.append(blk * cos + partner * sin_signed)
    return jnp.concatenate(outs, axis=1)


def _inproj_kernel(x_ref, nw_ref, w_ref, cos_ref, sin_ref, proj_ref, vt_ref, h_scr):
    j = pl.program_id(1)

    @pl.when(j == 0)
    def _():
        x = x_ref[...]
        ms = jnp.mean(x * x, axis=-1, keepdims=True)
        h_scr[...] = (x * lax.rsqrt(ms + RMS_EPS) * nw_ref[...]).astype(BF16)

    acc = jnp.dot(h_scr[...], w_ref[...].astype(BF16), preferred_element_type=F32)
    jq, jk, jv, jr = COL_AQ // PROJ_TN, COL_AK // PROJ_TN, COL_AV // PROJ_TN, COL_RQ // PROJ_TN
    scale = HEAD_DIM ** -0.5

    @pl.when(j == jq)
    def _():
        proj_ref[...] = (_rope(acc, cos_ref[...], sin_ref[...]) * QK_SCALE_LOG2).astype(BF16)

    @pl.when(j == jk)
    def _():
        proj_ref[...] = _rope(acc, cos_ref[...], sin_ref[...]).astype(BF16)

    @pl.when(j == jv)
    def _():
        proj_ref[...] = acc.astype(BF16)
        ones = jnp.ones((VT_ONES, MOBA_BLOCK), F32)
        for c in range(PROJ_TM // MOBA_BLOCK):
            v_t = acc[c * MOBA_BLOCK:(c + 1) * MOBA_BLOCK, :].T
            rows = []
            for h in range(ATTN_HEADS):
                rows += [v_t[h * HEAD_DIM:(h + 1) * HEAD_DIM], ones]
            vt_ref[c] = jnp.concatenate(rows, axis=0).astype(BF16)

    @pl.when(j == jr)
    def _():
        r = _rope(acc, cos_ref[...], sin_ref[...])
        col = lax.broadcasted_iota(jnp.int32, (1, PROJ_TN), 1)
        r = r * jnp.where(col >= RET_QK_WIDTH, scale, 1.0)
        proj_ref[...] = r.astype(BF16)

    @pl.when((j != jq) & (j != jk) & (j != jv) & (j != jr))
    def _():
        proj_ref[...] = acc.astype(BF16)


def _inproj_src_block(j):
    first_moved = COL_RV // PROJ_TN
    last = IN_PROJ_WIDTH // PROJ_TN - 1
    return jnp.where(j < first_moved, j, jnp.where(j == last, first_moved, j + 1))


def _inproj(x2, norm_w, w_in, layer, cos_t, sin_t, batch, seq):
    n = x2.shape[0]
    nst = seq // PROJ_TM
    nblk = PROJ_TM // MOBA_BLOCK
    return pl.pallas_call(
        _inproj_kernel,
        grid=(n // PROJ_TM, IN_PROJ_WIDTH // PROJ_TN),
        in_specs=[
            pl.BlockSpec((PROJ_TM, D_MODEL), lambda i, j: (i, 0)),
            pl.BlockSpec((1, D_MODEL), lambda i, j: (0, 0)),
            pl.BlockSpec((None, D_MODEL, PROJ_TN), lambda i, j: (layer, 0, _inproj_src_block(j))),
            pl.BlockSpec((PROJ_TM, LANES), lambda i, j: (i % nst, 0)),
            pl.BlockSpec((PROJ_TM, LANES), lambda i, j: (i % nst, 0)),
        ],
        out_specs=[
            pl.BlockSpec((PROJ_TM, PROJ_TN), lambda i, j: (i, j)),
            pl.BlockSpec((None, nblk, ATTN_HEADS * VT_ROWS, MOBA_BLOCK),
                         lambda i, j: (i // nst, i % nst, 0, 0)),
        ],
        out_shape=[
            jax.ShapeDtypeStruct((n, IN_PROJ_WIDTH), BF16),
            jax.ShapeDtypeStruct((batch, seq // MOBA_BLOCK, ATTN_HEADS * VT_ROWS, MOBA_BLOCK), BF16),
        ],
        scratch_shapes=[pltpu.VMEM((PROJ_TM, D_MODEL), BF16)],
        compiler_params=_cparams(("arbitrary", "arbitrary")),
        name="inproj",
    )(x2, norm_w, w_in, cos_t, sin_t)


_NT = (((1,), (1,)), ((), ()))


def _moba_kernel(q_ref, k_ref, vt_ref, o_ref, km_scr, ka_scr, sel_scr, qt_scr, *, nb):
    nh = MOBA_HEADS_PER_STEP
    blk = MOBA_BLOCK
    tq = 2 * blk
    lane = lax.broadcasted_iota(jnp.int32, (1, LANES), 1)
    crow = lax.broadcasted_iota(jnp.int32, (LANES, 1), 0)

    def group(ref, rows, h):
        g = h // 2
        return ref[rows, g * LANES:(g + 1) * LANES]

    def prep(i, c):
        rows = pl.ds(pl.multiple_of(i * blk, blk), blk)
        for h in range(nh):
            kb, hh = group(k_ref, rows, h), h % 2
            if hh == 0:
                km_scr[h // 2, pl.ds(i, 1), :] = jnp.sum(kb.astype(F32), axis=0, keepdims=True) * (1.0 / blk)
            onehot = jnp.where(lane == HEAD_DIM * (1 - hh) + i, 1.0, 0.0).astype(BF16)
            ka_scr[h, rows, :] = jnp.where((lane // HEAD_DIM) == hh, kb, onehot)
        return c

    lax.fori_loop(0, nb, prep, 0)

    causal = (lax.broadcasted_iota(jnp.int32, (blk, blk), 0)
              <= lax.broadcasted_iota(jnp.int32, (blk, blk), 1))
    blk_id = lax.broadcasted_iota(jnp.int32, (nb, tq), 0)
    q_half = (lax.broadcasted_iota(jnp.int32, (nb, tq), 1) >= blk).astype(jnp.int32)

    def pv(t, h, p):
        rows = slice(h * VT_ROWS, (h + 1) * VT_ROWS)
        pb = p.astype(BF16)
        return (jnp.dot(vt_ref[2 * t, rows, :], pb[0:blk], preferred_element_type=F32)
                + jnp.dot(vt_ref[2 * t + 1, rows, :], pb[blk:tq], preferred_element_type=F32))

    def qtile(jt, c):
        row0 = pl.multiple_of(jt * tq, tq)
        init = []
        for h in range(nh):
            hh = h % 2
            q_t = group(q_ref, pl.ds(row0, tq), h).astype(F32).T
            k_diag = group(k_ref, pl.ds(row0, tq), h)
            q_m = jnp.where((crow // HEAD_DIM) == hh, q_t, 0.0)
            q_mb = q_m.astype(BF16)
            km = jnp.where((lane // HEAD_DIM) == hh, km_scr[h // 2], 0.0)
            km_hi = km.astype(BF16)
            km_lo = (km - km_hi.astype(F32)).astype(BF16)
            gate = (jnp.dot(km_hi, q_mb, preferred_element_type=F32)
                    + jnp.dot(km_lo, q_mb, preferred_element_type=F32))
            past = blk_id < 2 * jt + q_half
            gate = jnp.where(past, gate, -jnp.inf)
            rank = jnp.zeros((nb, tq), jnp.int32)
            for ip in range(nb):
                gi = gate[ip:ip + 1, :]
                beats = (gi > gate) | ((gi == gate) & (blk_id > ip))
                rank = rank + beats.astype(jnp.int32)
            sel = (rank < MOBA_TOPK) & past
            sel_scr[h] = sel.astype(F32)
            bias = jnp.where(sel, 0.0, NEG_INF)
            spare = HEAD_DIM * (1 - hh)
            pieces = [jnp.zeros((spare, tq), F32)] if spare else []
            pieces += [bias, jnp.zeros((LANES - spare - nb, tq), F32)]
            qt_scr[h] = (q_m + jnp.concatenate(pieces, axis=0)).astype(BF16)

            s_top = jnp.dot(k_diag[0:blk], q_mb, preferred_element_type=F32)
            s11 = jnp.where(causal, s_top[:, 0:blk], NEG_INF)
            s12 = jnp.where(sel_scr[h, pl.ds(2 * jt, 1), blk:tq] > 0.0, s_top[:, blk:tq], NEG_INF)
            s22 = jnp.where(causal, jnp.dot(k_diag[blk:tq], q_mb[:, blk:tq], preferred_element_type=F32),
                            NEG_INF)
            m_a = jnp.max(s11, axis=0, keepdims=True)
            m_b = jnp.maximum(jnp.max(s12, axis=0, keepdims=True), jnp.max(s22, axis=0, keepdims=True))
            rows = slice(h * VT_ROWS, (h + 1) * VT_ROWS)
            v_lo, v_hi = vt_ref[2 * jt, rows, :], vt_ref[2 * jt + 1, rows, :]
            acc_a = jnp.dot(v_lo, jnp.exp2(s11 - m_a).astype(BF16), preferred_element_type=F32)
            acc_b = (jnp.dot(v_lo, jnp.exp2(s12 - m_b).astype(BF16), preferred_element_type=F32)
                     + jnp.dot(v_hi, jnp.exp2(s22 - m_b).astype(BF16), preferred_element_type=F32))
            init += [jnp.concatenate([m_a, m_b], axis=1), jnp.concatenate([acc_a, acc_b], axis=1)]

        def kv_body(it, carry):
            rows = pl.ds(pl.multiple_of(it * tq, tq), tq)
            s = [jnp.dot(ka_scr[h, rows, :], qt_scr[h], preferred_element_type=F32) for h in range(nh)]
            m_new = [jnp.maximum(carry[2 * h], jnp.max(s[h], axis=0, keepdims=True)) for h in range(nh)]
            new = []
            for h in range(nh):
                alpha = jnp.exp2(carry[2 * h] - m_new[h])
                new += [m_new[h], alpha * carry[2 * h + 1] + pv(it, h, jnp.exp2(s[h] - m_new[h]))]
            return tuple(new)

        fin = lax.fori_loop(0, jt, kv_body, tuple(init))
        out_t = jnp.concatenate([fin[2 * h + 1][0:HEAD_DIM] / fin[2 * h + 1][HEAD_DIM:HEAD_DIM + 1]
                                 for h in range(nh)], axis=0)
        o_ref[pl.ds(row0, tq), :] = out_t.T.astype(BF16)
        return c

    lax.fori_loop(0, nb // 2, qtile, 0)


def _moba(proj, vt, batch, seq):
    nb = seq // MOBA_BLOCK
    assert nb % 8 == 0 and nb <= HEAD_DIM
    nh = MOBA_HEADS_PER_STEP
    width = nh * HEAD_DIM
    return pl.pallas_call(
        functools.partial(_moba_kernel, nb=nb),
        grid=(batch, ATTN_HEADS // nh),
        in_specs=[
            pl.BlockSpec((seq, width), lambda b, p: (b, COL_AQ // width + p)),
            pl.BlockSpec((seq, width), lambda b, p: (b, COL_AK // width + p)),
            pl.BlockSpec((None, nb, nh * VT_ROWS, MOBA_BLOCK), lambda b, p: (b, 0, p, 0)),
        ],
        out_specs=pl.BlockSpec((seq, width), lambda b, p: (b, p)),
        out_shape=jax.ShapeDtypeStruct((batch * seq, ATTN_WIDTH), BF16),
        scratch_shapes=[pltpu.VMEM((nh // 2, nb, LANES), F32), pltpu.VMEM((nh, seq, LANES), BF16),
                        pltpu.VMEM((nh, nb, 2 * MOBA_BLOCK), F32),
                        pltpu.VMEM((nh, LANES, 2 * MOBA_BLOCK), BF16)],
        compiler_params=_cparams(("arbitrary", "arbitrary")),
        name="moba",
    )(proj, proj, vt)


def _ret_log_gamma():
    return [math.log1p(-(2.0 ** (-5.0 - h))) for h in range(RET_HEADS)]


def _retention_tables():
    c = RET_TILE
    lg = np.array(_ret_log_gamma(), np.float64)
    n = np.arange(c, dtype=np.float64)
    diff = n[:, None] - n[None, :]
    decay = np.where(diff[None] >= 0, np.exp(np.maximum(diff, 0.0)[None] * lg[:, None, None]), 0.0)
    head_of_lane = np.arange(RET_QK_WIDTH) // HEAD_DIM
    xi = np.exp((n + 1.0)[:, None] * lg[head_of_lane][None, :])
    zeta = np.exp((c - 1.0 - n)[:, None] * lg[head_of_lane][None, :])
    chunk_decay = np.exp(c * lg[head_of_lane])[:, None]
    return (jnp.asarray(decay, F32), jnp.asarray(xi, F32), jnp.asarray(zeta, F32),
            jnp.asarray(np.broadcast_to(chunk_decay, (RET_QK_WIDTH, LANES)), F32))


def _retention_kernel(q_ref, k_ref, v_ref, g_ref, nw_ref, decay_ref, xi_ref, zeta_ref, cd_ref,
                      o_ref, state_scr):
    @pl.when(pl.program_id(1) == 0)
    def _():
        state_scr[...] = jnp.zeros_like(state_scr)

    lane = lax.broadcasted_iota(jnp.int32, (1, LANES), 1)
    srow = lax.broadcasted_iota(jnp.int32, (LANES, 1), 0)
    for pr in range(RET_HEADS // 2):
        cols = slice(pr * LANES, (pr + 1) * LANES)
        q = q_ref[:, cols]
        k = k_ref[:, cols]
        state = state_scr[cols, :]
        state_bf = state.astype(BF16)
        q_xi = (q.astype(F32) * xi_ref[:, cols]).astype(BF16)
        kz_t = (k.astype(F32) * zeta_ref[:, cols]).T.astype(BF16)
        upd = []
        for hh in range(2):
            h = 2 * pr + hh
            hmask = (lane // HEAD_DIM) == hh
            vcols = slice(h * RET_V_DIM, (h + 1) * RET_V_DIM)
            v = v_ref[:, vcols]
            qm = jnp.where(hmask, q, jnp.zeros_like(q))
            scores = lax.dot_general(qm, k, _NT, preferred_element_type=F32) * decay_ref[h]
            o = jnp.dot(scores.astype(BF16), v, preferred_element_type=F32)
            o = o + jnp.dot(jnp.where(hmask, q_xi, jnp.zeros_like(q_xi)), state_bf,
                            preferred_element_type=F32)
            upd.append(jnp.dot(kz_t, v, preferred_element_type=F32))
            ms = jnp.mean(o * o, axis=-1, keepdims=True)
            y = o * lax.rsqrt(ms + RMS_EPS) * nw_ref[:, vcols]
            g = g_ref[:, vcols].astype(F32)
            o_ref[:, vcols] = (g * jax.nn.sigmoid(g) * y).astype(BF16)
        state_scr[cols, :] = state * cd_ref[cols, :] + jnp.where(srow < HEAD_DIM, upd[0], upd[1])


def _retention(proj, ret_norm_w, tables, batch, seq):
    nc = seq // RET_TILE
    decay, xi, zeta, cd = tables
    row = lambda b, c: b * nc + c
    const2 = lambda b, c: (0, 0)
    return pl.pallas_call(
        _retention_kernel,
        grid=(batch, nc),
        in_specs=[
            pl.BlockSpec((RET_TILE, RET_QK_WIDTH), lambda b, c: (row(b, c), COL_RQ // RET_QK_WIDTH)),
            pl.BlockSpec((RET_TILE, RET_QK_WIDTH), lambda b, c: (row(b, c), COL_RK // RET_QK_WIDTH)),
            pl.BlockSpec((RET_TILE, RET_V_WIDTH), lambda b, c: (row(b, c), COL_RV // RET_V_WIDTH)),
            pl.BlockSpec((RET_TILE, RET_V_WIDTH), lambda b, c: (row(b, c), COL_RG // RET_V_WIDTH)),
            pl.BlockSpec((1, RET_V_WIDTH), const2),
            pl.BlockSpec((RET_HEADS, RET_TILE, RET_TILE), lambda b, c: (0, 0, 0)),
            pl.BlockSpec((RET_TILE, RET_QK_WIDTH), const2),
            pl.BlockSpec((RET_TILE, RET_QK_WIDTH), const2),
            pl.BlockSpec((RET_QK_WIDTH, LANES), const2),
        ],
        out_specs=pl.BlockSpec((RET_TILE, RET_V_WIDTH), lambda b, c: (row(b, c), 0)),
        out_shape=jax.ShapeDtypeStruct((batch * seq, RET_V_WIDTH), BF16),
        scratch_shapes=[pltpu.VMEM((RET_QK_WIDTH, RET_V_DIM), F32)],
        compiler_params=_cparams(("arbitrary", "arbitrary")),
        name="retention",
    )(proj, proj, proj, proj, ret_norm_w, decay, xi, zeta, cd)


CONV_HALO = 8


def _merge_kernel(x_ref, ya_ref, cb_ref, cc_ref, ch_ref, yr_ref, ga_ref, gc_ref, gr_ref, cw_ref,
                  cbias_ref, wa_ref, wc_ref, wr_ref, wo_ref, o_ref, u_scr, *, tiles_per_seq):
    i = pl.program_id(0)
    tm = x_ref.shape[0]

    @pl.when(i % tiles_per_seq == 0)
    def _():
        u_scr[0:CONV_HALO, :] = jnp.zeros((CONV_HALO, CONV_WIDTH), F32)

    u_scr[CONV_HALO:CONV_HALO + tm, :] = cc_ref[...].astype(F32) * ch_ref[...].astype(F32)
    conv = (cw_ref[2:3, :] * u_scr[CONV_HALO:CONV_HALO + tm, :]
            + cw_ref[1:2, :] * u_scr[CONV_HALO - 1:CONV_HALO - 1 + tm, :]
            + cw_ref[0:1, :] * u_scr[CONV_HALO - 2:CONV_HALO - 2 + tm, :]
            + cbias_ref[...])
    y_conv = (cb_ref[...].astype(F32) * conv).astype(BF16)
    u_scr[0:CONV_HALO, :] = u_scr[tm:tm + CONV_HALO, :]

    def gate(g_ref):
        return jax.nn.sigmoid(g_ref[...].astype(F32))

    merged = gate(ga_ref) * jnp.dot(ya_ref[...], wa_ref[...], preferred_element_type=F32)
    merged = merged + gate(gc_ref) * jnp.dot(y_conv, wc_ref[...], preferred_element_type=F32)
    merged = merged + gate(gr_ref) * jnp.dot(yr_ref[...], wr_ref[...], preferred_element_type=F32)
    o_ref[...] = x_ref[...] + jnp.dot(merged.astype(BF16), wo_ref[...], preferred_element_type=F32)


def _merge(x2, y_attn, proj, y_ret, conv_w, conv_b, wa, wc, wr, wo, seq):
    n = x2.shape[0]
    tm = MERGE_TM
    const = lambda i: (0, 0)
    wide = lambda c: pl.BlockSpec((tm, 512), lambda i: (i, c // 512))
    gate_spec = lambda b: pl.BlockSpec((tm, D_MODEL), lambda i: (i, COL_GATES // D_MODEL + b))
    return pl.pallas_call(
        functools.partial(_merge_kernel, tiles_per_seq=seq // tm),
        grid=(n // tm,),
        in_specs=[
            pl.BlockSpec((tm, D_MODEL), lambda i: (i, 0)),
            pl.BlockSpec((tm, ATTN_WIDTH), lambda i: (i, 0)),
            wide(COL_CB), wide(COL_CC), wide(COL_CH),
            pl.BlockSpec((tm, RET_V_WIDTH), lambda i: (i, 0)),
            gate_spec(0), gate_spec(1), gate_spec(2),
            pl.BlockSpec((CONV_KERNEL, CONV_WIDTH), const),
            pl.BlockSpec((1, CONV_WIDTH), const),
            pl.BlockSpec((ATTN_WIDTH, D_MODEL), const),
            pl.BlockSpec((CONV_WIDTH, D_MODEL), const),
            pl.BlockSpec((RET_V_WIDTH, D_MODEL), const),
            pl.BlockSpec((D_MODEL, D_MODEL), const),
        ],
        out_specs=pl.BlockSpec((tm, D_MODEL), lambda i: (i, 0)),
        out_shape=jax.ShapeDtypeStruct((n, D_MODEL), F32),
        scratch_shapes=[pltpu.VMEM((tm + CONV_HALO, CONV_WIDTH), F32)],
        compiler_params=_cparams(("arbitrary",)),
        name="merge",
    )(x2, y_attn, proj, proj, proj, y_ret, proj, proj, proj, conv_w, conv_b, wa, wc, wr, wo)


def _swiglu_step(h, wg_ref, wu_ref, wd_ref, acc_scr, c):
    g = jnp.dot(h, wg_ref[...].astype(BF16), preferred_element_type=F32)
    u = jnp.dot(h, wu_ref[...].astype(BF16), preferred_element_type=F32)
    a = (g * jax.nn.sigmoid(g) * u).astype(BF16)
    part = jnp.dot(a, wd_ref[...].astype(BF16), preferred_element_type=F32)

    @pl.when(c == 0)
    def _():
        acc_scr[...] = part

    @pl.when(c != 0)
    def _():
        acc_scr[...] += part


def _dense_ffn_kernel(x_ref, nw_ref, wg_ref, wu_ref, wd_ref, o_ref, h_scr, acc_scr):
    c = pl.program_id(1)

    @pl.when(c == 0)
    def _():
        x = x_ref[...]
        ms = jnp.mean(x * x, axis=-1, keepdims=True)
        h_scr[...] = (x * lax.rsqrt(ms + RMS_EPS) * nw_ref[...]).astype(BF16)

    _swiglu_step(h_scr[...], wg_ref, wu_ref, wd_ref, acc_scr, c)

    @pl.when(c == pl.num_programs(1) - 1)
    def _():
        o_ref[...] = x_ref[...] + acc_scr[...]


def _dense_ffn(x2, norm_w, wg, wu, wd, li):
    n = x2.shape[0]
    tm, tf = FFN_TM, FFN_TF
    return pl.pallas_call(
        _dense_ffn_kernel,
        grid=(n // tm, D_FF // tf),
        in_specs=[
            pl.BlockSpec((tm, D_MODEL), lambda i, c: (i, 0)),
            pl.BlockSpec((1, D_MODEL), lambda i, c: (0, 0)),
            pl.BlockSpec((None, D_MODEL, tf), lambda i, c: (li, 0, c)),
            pl.BlockSpec((None, D_MODEL, tf), lambda i, c: (li, 0, c)),
            pl.BlockSpec((None, tf, D_MODEL), lambda i, c: (li, c, 0)),
        ],
        out_specs=pl.BlockSpec((tm, D_MODEL), lambda i, c: (i, 0)),
        out_shape=jax.ShapeDtypeStruct((n, D_MODEL), F32),
        scratch_shapes=[pltpu.VMEM((tm, D_MODEL), BF16), pltpu.VMEM((tm, D_MODEL), F32)],
        compiler_params=_cparams(("arbitrary", "arbitrary")),
        name="dense_ffn",
    )(x2, norm_w, wg, wu, wd)


SLAB = D_MODEL // LANES


def _pack_rows(v, o_ref):
    rows = v.shape[0]
    for s in range(SLAB):
        o_ref[pl.ds(s, rows, stride=SLAB), :] = v[:, s * LANES:(s + 1) * LANES]


def _unpack_rows(x_ref, rows):
    return jnp.concatenate([x_ref[pl.ds(s, rows, stride=SLAB), :] for s in range(SLAB)], axis=1)


def _route_kernel(x_ref, nw_ref, wr_ref, h_ref, idx_ref, wgt_ref):
    x = x_ref[...]
    ms = jnp.mean(x * x, axis=-1, keepdims=True)
    h = x * lax.rsqrt(ms + RMS_EPS) * nw_ref[...]
    _pack_rows(h, h_ref)
    w = wr_ref[...]
    h_hi, w_hi = h.astype(BF16), w.astype(BF16)
    h_lo, w_lo = (h - h_hi.astype(F32)).astype(BF16), (w - w_hi.astype(F32)).astype(BF16)
    logits = (jnp.dot(h_hi, w_hi, preferred_element_type=F32)
              + jnp.dot(h_lo, w_hi, preferred_element_type=F32)
              + jnp.dot(h_hi, w_lo, preferred_element_type=F32))
    eid = lax.broadcasted_iota(jnp.int32, logits.shape, 1).astype(F32)
    logits = jnp.where(eid < N_EXPERTS, logits, -jnp.inf)
    m1 = jnp.max(logits, axis=-1, keepdims=True)
    i1 = jnp.min(jnp.where(logits == m1, eid, float(LANES)), axis=-1, keepdims=True)
    rest = jnp.where(eid == i1, -jnp.inf, logits)
    m2 = jnp.max(rest, axis=-1, keepdims=True)
    i2 = jnp.min(jnp.where(rest == m2, eid, float(LANES)), axis=-1, keepdims=True)
    e2 = jnp.exp(m2 - m1)
    denom = 1.0 + e2
    idx_ref[...] = jnp.where(eid == 0.0, i1, jnp.where(eid == 1.0, i2, 0.0)).astype(jnp.int32)
    wgt_ref[...] = jnp.where(eid == 0.0, 1.0 / denom, jnp.where(eid == 1.0, e2 / denom, 0.0))


def _route(x2, norm_w, w_router):
    n = x2.shape[0]
    tm = ROUTE_TM
    return pl.pallas_call(
        _route_kernel,
        grid=(n // tm,),
        in_specs=[
            pl.BlockSpec((tm, D_MODEL), lambda i: (i, 0)),
            pl.BlockSpec((1, D_MODEL), lambda i: (0, 0)),
            pl.BlockSpec((D_MODEL, LANES), lambda i: (0, 0)),
        ],
        out_specs=[
            pl.BlockSpec((tm * SLAB, LANES), lambda i: (i, 0)),
            pl.BlockSpec((tm, LANES), lambda i: (i, 0)),
            pl.BlockSpec((tm, LANES), lambda i: (i, 0)),
        ],
        out_shape=[
            jax.ShapeDtypeStruct((n * SLAB, LANES), F32),
            jax.ShapeDtypeStruct((n, LANES), jnp.int32),
            jax.ShapeDtypeStruct((n, LANES), F32),
        ],
        compiler_params=_cparams(("arbitrary",)),
        name="route",
    )(x2, norm_w, jnp.pad(w_router, ((0, 0), (0, LANES - N_EXPERTS))))


MOE_STEPS = D_FF // FFN_TF
MOE_ROWS_PER_STEP = -(-FFN_TM // MOE_STEPS)
MOE_DMA_ROWS = MOE_ROWS_PER_STEP * MOE_STEPS
MOE_BUF_ROWS = -(-MOE_DMA_ROWS // 8) * 8
MOE_SPARE_TILES = 2


def _moe_ffn_kernel(te_ref, nv_ref, gid0_ref, gidn_ref, sid_ref, h3_ref, wg_ref, wu_ref, wd_ref,
                    y_init_ref, y_ref, gbuf, obuf, h_scr, acc_scr, gsem, ssem):
    t = pl.program_id(0)
    c = pl.program_id(1)
    nv = nv_ref[0]
    slot = t % 2
    other = 1 - slot
    first = c == 0

    def slab(ref, start):
        return ref.at[pl.ds(pl.multiple_of(start, SLAB), SLAB)]

    def gather(ids_ref, row, s):
        return pltpu.make_async_copy(slab(h3_ref, ids_ref[0, 0, row]), slab(gbuf.at[s], row * SLAB),
                                     gsem.at[s])

    def scatter(row, s):
        return pltpu.make_async_copy(slab(obuf.at[s], row * SLAB), slab(y_ref, sid_ref[0, 0, row]),
                                     ssem.at[s])

    def tile_gather(s):
        return pltpu.make_async_copy(h3_ref.at[pl.ds(0, MOE_DMA_ROWS * SLAB)],
                                     gbuf.at[s, pl.ds(0, MOE_DMA_ROWS * SLAB)], gsem.at[s])

    def tile_scatter(s):
        return pltpu.make_async_copy(obuf.at[s, pl.ds(0, MOE_DMA_ROWS * SLAB)],
                                     y_ref.at[pl.ds(0, MOE_DMA_ROWS * SLAB)], ssem.at[s])

    @pl.when(first & (t == 0))
    def _():
        obuf[...] = jnp.zeros_like(obuf)

        def body(r, carry):
            gather(gid0_ref, r, 0).start()
            return carry

        lax.fori_loop(0, MOE_DMA_ROWS, body, 0, unroll=MOE_STEPS)

    @pl.when(first & (t <= nv))
    def _():
        tile_gather(slot).wait()

    @pl.when(first & (t >= 1) & (t <= nv + 1))
    def _():
        tile_scatter(slot).wait()

    @pl.when(first & (t < nv))
    def _():
        h_scr[...] = _unpack_rows(gbuf.at[slot], FFN_TM).astype(BF16)

    @pl.when(t < nv)
    def _():
        for u in range(MOE_ROWS_PER_STEP):
            row = c * MOE_ROWS_PER_STEP + u
            gather(gidn_ref, row, other).start()
            scatter(row, other).start()
        _swiglu_step(h_scr[...], wg_ref, wu_ref, wd_ref, acc_scr, c)

    @pl.when(t == nv)
    def _():
        def body(u, carry):
            scatter(c * MOE_ROWS_PER_STEP + u, other).start()
            return carry

        lax.fori_loop(0, MOE_ROWS_PER_STEP, body, 0, unroll=MOE_STEPS)

    @pl.when((t < nv) & (c == pl.num_programs(1) - 1))
    def _():
        _pack_rows(acc_scr[...], obuf.at[slot])


def _moe_ffn(tile_expert, n_valid, gid, sid, h3, wg, wu, wd, li, y_rows):
    nt = gid.shape[0]
    tm, tf = FFN_TM, FFN_TF
    ids_block = (1, 1, MOE_BUF_ROWS)
    smem = pltpu.SMEM
    wspec = lambda shape, imap: pl.BlockSpec((None, None) + shape, imap)
    grid_spec = pltpu.PrefetchScalarGridSpec(
        num_scalar_prefetch=2,
        grid=(nt, MOE_STEPS),
        in_specs=[
            pl.BlockSpec(ids_block, lambda t, c, te, nv: (0, 0, 0), memory_space=smem),
            pl.BlockSpec(ids_block, lambda t, c, te, nv: (jnp.minimum(t + 1, nt - 1), 0, 0),
                         memory_space=smem),
            pl.BlockSpec(ids_block, lambda t, c, te, nv: (t, 0, 0), memory_space=smem),
            pl.BlockSpec(memory_space=pl.ANY),
            wspec((D_MODEL, tf), lambda t, c, te, nv: (li, te[t], 0, c)),
            wspec((D_MODEL, tf), lambda t, c, te, nv: (li, te[t], 0, c)),
            wspec((tf, D_MODEL), lambda t, c, te, nv: (li, te[t], c, 0)),
            pl.BlockSpec(memory_space=pl.ANY),
        ],
        out_specs=pl.BlockSpec(memory_space=pl.ANY),
        scratch_shapes=[
            pltpu.VMEM((2, MOE_BUF_ROWS * SLAB, LANES), F32),
            pltpu.VMEM((2, MOE_BUF_ROWS * SLAB, LANES), F32),
            pltpu.VMEM((tm, D_MODEL), BF16),
            pltpu.VMEM((tm, D_MODEL), F32),
            pltpu.SemaphoreType.DMA((2,)),
            pltpu.SemaphoreType.DMA((2,)),
        ],
    )
    return pl.pallas_call(
        _moe_ffn_kernel,
        grid_spec=grid_spec,
        out_shape=jax.ShapeDtypeStruct((y_rows * SLAB, LANES), F32),
        input_output_aliases={9: 0},
        compiler_params=_cparams(("arbitrary", "arbitrary")),
        name="moe_ffn",
    )(tile_expert, n_valid, gid * SLAB, gid * SLAB, sid * SLAB, h3, wg, wu, wd,
      jnp.zeros((y_rows * SLAB, LANES), F32))


def _combine_kernel(x_ref, y0_ref, y1_ref, w_ref, nw_ref, o_ref, *, final_norm):
    w = w_ref[...]
    rows = x_ref.shape[0]
    x = x_ref[...] + w[:, 0:1] * _unpack_rows(y0_ref, rows) + w[:, 1:2] * _unpack_rows(y1_ref, rows)
    if final_norm:
        ms = jnp.mean(x * x, axis=-1, keepdims=True)
        x = x * lax.rsqrt(ms + RMS_EPS) * nw_ref[...]
    o_ref[...] = x


def _combine(x2, y_pairs, wgt, norm_w, final_norm):
    n = x2.shape[0]
    tm = MERGE_TM
    nt = n // tm
    return pl.pallas_call(
        functools.partial(_combine_kernel, final_norm=final_norm),
        grid=(nt,),
        in_specs=[
            pl.BlockSpec((tm, D_MODEL), lambda i: (i, 0)),
            pl.BlockSpec((tm * SLAB, LANES), lambda i: (i, 0)),
            pl.BlockSpec((tm * SLAB, LANES), lambda i: (i + nt, 0)),
            pl.BlockSpec((tm, LANES), lambda i: (i, 0)),
            pl.BlockSpec((1, D_MODEL), lambda i: (0, 0)),
        ],
        out_specs=pl.BlockSpec((tm, D_MODEL), lambda i: (i, 0)),
        out_shape=jax.ShapeDtypeStruct((n, D_MODEL), F32),
        compiler_params=_cparams(("arbitrary",)),
        name="combine",
    )(x2, y_pairs, y_pairs, wgt, norm_w)


def _moe_layout(idx):
    n = idx.shape[0]
    tm = FFN_TM
    flat = idx.T.reshape(-1)
    onehot = (flat[:, None] == jnp.arange(N_EXPERTS, dtype=jnp.int32)[None, :]).astype(jnp.int32)
    cum = jnp.cumsum(onehot, axis=0)
    counts = cum[-1]
    rank = jnp.sum(cum * onehot, axis=1) - 1
    padded = ((counts + tm - 1) // tm) * tm
    ends = jnp.cumsum(padded)
    starts = ends - padded
    dest = starts[flat] + rank
    pairs = TOP_K * n
    nt = pairs // tm + N_EXPERTS + MOE_SPARE_TILES
    p = nt * tm
    tile_start = jnp.arange(nt, dtype=jnp.int32) * tm
    tile_expert = jnp.minimum(jnp.sum((ends[None, :] <= tile_start[:, None]).astype(jnp.int32), axis=1),
                              N_EXPERTS - 1)
    n_valid = (ends[-1] // tm).astype(jnp.int32).reshape(1)
    real_before = jnp.cumsum(counts)[tile_expert]
    pos = jnp.arange(p, dtype=jnp.int32).reshape(nt, tm)
    pad_row = (pairs + pos - real_before[:, None]).reshape(p)
    row_pair = pad_row.at[dest].set(jnp.arange(pairs, dtype=jnp.int32))
    row_token = jnp.where(row_pair < pairs, row_pair % n, 0)
    tail = MOE_BUF_ROWS - tm
    gid = jnp.pad(row_token.reshape(nt, tm), ((0, 0), (0, tail))).reshape(nt, 1, MOE_BUF_ROWS)
    first_spare = p + jnp.arange(MOE_BUF_ROWS, dtype=jnp.int32)[None, :]
    tails = p + MOE_BUF_ROWS + jnp.arange(nt * tail, dtype=jnp.int32).reshape(nt, tail)
    sid = jnp.concatenate([row_pair.reshape(nt, tm), tails], axis=1)
    sid = jnp.concatenate([first_spare, sid], axis=0).reshape(nt + 1, 1, MOE_BUF_ROWS)
    y_rows = p + MOE_BUF_ROWS + nt * tail
    return gid, sid, tile_expert, n_valid, y_rows


def _moe(x2, norm_w, w_router, wg, wu, wd, li, final_norm_w, final_norm):
    h3, idx, wgt = _route(x2, norm_w, w_router)
    gid, sid, tile_expert, n_valid, y_rows = _moe_layout(idx[:, :TOP_K])
    y = _moe_ffn(tile_expert, n_valid, gid, sid, h3, wg, wu, wd, li, y_rows)
    return _combine(x2, y, wgt, final_norm_w, final_norm)


def _rope_tables(seq):
    inv_freq = 1.0 / (ROPE_THETA ** (jnp.arange(0, HEAD_DIM, 2, dtype=F32) / HEAD_DIM))
    ang = jnp.arange(seq, dtype=F32)[:, None] * inv_freq[None, :]
    cos, sin = jnp.cos(ang), jnp.sin(ang)
    cos_t = jnp.tile(cos, (1, LANES // (HEAD_DIM // 2)))
    sin_t = jnp.tile(jnp.concatenate([-sin, sin], axis=1), (1, LANES // HEAD_DIM))
    return cos_t, sin_t


def kernel(x, mix_norm_w, w_in, conv_w, conv_b, ret_norm_w, w_br_attn, w_br_conv, w_br_ret, w_out,
           ffn_norm_w, dense_w_gate, dense_w_up, dense_w_down, moe_router, moe_w_gate, moe_w_up,
           moe_w_down, final_norm_w):
    batch, seq, d = x.shape
    depth = w_in.shape[0]
    assert d == D_MODEL and seq % PROJ_TM == 0 and depth % 2 == 0
    n = batch * seq
    cos_t, sin_t = _rope_tables(seq)
    ret_tables = _retention_tables()
    x2 = x.reshape(n, d)
    for layer in range(depth):
        proj, vt = _inproj(x2, mix_norm_w[layer].reshape(1, d), w_in, layer, cos_t, sin_t, batch, seq)
        y_attn = _moba(proj, vt, batch, seq)
        y_ret = _retention(proj, ret_norm_w[layer].reshape(1, RET_V_WIDTH), ret_tables, batch, seq)
        x2 = _merge(x2, y_attn, proj, y_ret, conv_w[layer], conv_b[layer].reshape(1, CONV_WIDTH),
                    w_br_attn[layer].astype(BF16), w_br_conv[layer].astype(BF16),
                    w_br_ret[layer].astype(BF16), w_out[layer].astype(BF16), seq)
        i = layer // 2
        nw = ffn_norm_w[layer].reshape(1, d)
        if layer % 2 == 0:
            x2 = _dense_ffn(x2, nw, dense_w_gate, dense_w_up, dense_w_down, i)
        else:
            last = layer == depth - 1
            x2 = _moe(x2, nw, moe_router[i], moe_w_gate, moe_w_up, moe_w_down, i,
                      final_norm_w.reshape(1, d), last)
    return x2.reshape(batch, seq, d)
```

```python
import functools
import math

import jax
import jax.numpy as jnp
import numpy as np
from jax import lax
from jax.experimental import pallas as pl
from jax.experimental.pallas import tpu as pltpu

F32 = jnp.float32
BF16 = jnp.bfloat16

D_MODEL = 1024
HEAD_DIM = 64
ROPE_THETA = 10000.0
RMS_EPS = 1e-6

ATTN_HEADS = 8
ATTN_WIDTH = ATTN_HEADS * HEAD_DIM
MOBA_BLOCK = 256
MOBA_TOPK = 3
NEG_INF = -1e30
MOBA_HEADS_PER_STEP = 4
VT_ONES = 16
VT_ROWS = HEAD_DIM + VT_ONES
QK_SCALE_LOG2 = HEAD_DIM ** -0.5 * math.log2(math.e)

CONV_WIDTH = 512
CONV_KERNEL = 3

RET_HEADS = 4
RET_QK_WIDTH = RET_HEADS * HEAD_DIM
RET_V_DIM = 2 * HEAD_DIM
RET_V_WIDTH = RET_HEADS * RET_V_DIM
RET_TILE = 256

D_FF = 3584
N_EXPERTS = 8
TOP_K = 2

LANES = 128
IN_PROJ_WIDTH = 7680
COL_AQ, COL_AK, COL_AV = 0, 512, 1024
COL_CB, COL_CC, COL_CH = 1536, 2048, 2560
COL_RV, COL_RG = 3072, 3584
COL_GATES = 4096
COL_RQ, COL_RK = 7168, 7424

PROJ_TM = 2048
PROJ_TN = 512
MERGE_TM = 512
FFN_TM = 1024
FFN_TF = 512
ROUTE_TM = 1024
VMEM_LIMIT = 56 * 1024 * 1024


def _cparams(sem):
    return pltpu.CompilerParams(dimension_semantics=sem, vmem_limit_bytes=VMEM_LIMIT)


def _rope(acc, cos, sin_signed):
    lane = lax.broadcasted_iota(jnp.int32, (1, LANES), 1)
    first_half = (lane % HEAD_DIM) < (HEAD_DIM // 2)
    outs = []
    for g in range(acc.shape[1] // LANES):
        blk = acc[:, g * LANES:(g + 1) * LANES]
        packed = blk.astype(BF16)
        partner = jnp.where(first_half,
                            pltpu.roll(packed, LANES - HEAD_DIM // 2, 1),
                            pltpu.roll(packed, HEAD_DIM // 2, 1)).astype(F32)
        outs.append(blk * cos + partner * sin_signed)
    return jnp.concatenate(outs, axis=1)


def _inproj_kernel(x_ref, nw_ref, w_ref, cos_ref, sin_ref, proj_ref, vt_ref, h_scr):
    j = pl.program_id(1)

    @pl.when(j == 0)
    def _():
        x = x_ref[...]
        ms = jnp.mean(x * x, axis=-1, keepdims=True)
        h_scr[...] = (x * lax.rsqrt(ms + RMS_EPS) * nw_ref[...]).astype(BF16)

    acc = jnp.dot(h_scr[...], w_ref[...].astype(BF16), preferred_element_type=F32)
    jq, jk, jv, jr = COL_AQ // PROJ_TN, COL_AK // PROJ_TN, COL_AV // PROJ_TN, COL_RQ // PROJ_TN
    scale = HEAD_DIM ** -0.5

    @pl.when(j == jq)
    def _():
        proj_ref[...] = (_rope(acc, cos_ref[...], sin_ref[...]) * QK_SCALE_LOG2).astype(BF16)

    @pl.when(j == jk)
    def _():
        proj_ref[...] = _rope(acc, cos_ref[...], sin_ref[...]).astype(BF16)

    @pl.when(j == jv)
    def _():
        proj_ref[...] = acc.astype(BF16)
        ones = jnp.ones((VT_ONES, MOBA_BLOCK), F32)
        for c in range(PROJ_TM // MOBA_BLOCK):
            v_t = acc[c * MOBA_BLOCK:(c + 1) * MOBA_BLOCK, :].T
            rows = []
            for h in range(ATTN_HEADS):
                rows += [v_t[h * HEAD_DIM:(h + 1) * HEAD_DIM], ones]
            vt_ref[c] = jnp.concatenate(rows, axis=0).astype(BF16)

    @pl.when(j == jr)
    def _():
        r = _rope(acc, cos_ref[...], sin_ref[...])
        col = lax.broadcasted_iota(jnp.int32, (1, PROJ_TN), 1)
        r = r * jnp.where(col >= RET_QK_WIDTH, scale, 1.0)
        proj_ref[...] = r.astype(BF16)

    @pl.when((j != jq) & (j != jk) & (j != jv) & (j != jr))
    def _():
        proj_ref[...] = acc.astype(BF16)


def _inproj_src_block(j):
    first_moved = COL_RV // PROJ_TN
    last = IN_PROJ_WIDTH // PROJ_TN - 1
    return jnp.where(j < first_moved, j, jnp.where(j == last, first_moved, j + 1))


def _inproj(x2, norm_w, w_in, layer, cos_t, sin_t, batch, seq):
    n = x2.shape[0]
    nst = seq // PROJ_TM
    nblk = PROJ_TM // MOBA_BLOCK
    return pl.pallas_call(
        _inproj_kernel,
        grid=(n // PROJ_TM, IN_PROJ_WIDTH // PROJ_TN),
        in_specs=[
            pl.BlockSpec((PROJ_TM, D_MODEL), lambda i, j: (i, 0)),
            pl.BlockSpec((1, D_MODEL), lambda i, j: (0, 0)),
            pl.BlockSpec((None, D_MODEL, PROJ_TN), lambda i, j: (layer, 0, _inproj_src_block(j))),
            pl.BlockSpec((PROJ_TM, LANES), lambda i, j: (i % nst, 0)),
            pl.BlockSpec((PROJ_TM, LANES), lambda i, j: (i % nst, 0)),
        ],
        out_specs=[
            pl.BlockSpec((PROJ_TM, PROJ_TN), lambda i, j: (i, j)),
            pl.BlockSpec((None, nblk, ATTN_HEADS * VT_ROWS, MOBA_BLOCK),
                         lambda i, j: (i // nst, i % nst, 0, 0)),
        ],
        out_shape=[
            jax.ShapeDtypeStruct((n, IN_PROJ_WIDTH), BF16),
            jax.ShapeDtypeStruct((batch, seq // MOBA_BLOCK, ATTN_HEADS * VT_ROWS, MOBA_BLOCK), BF16),
        ],
        scratch_shapes=[pltpu.VMEM((PROJ_TM, D_MODEL), BF16)],
        compiler_params=_cparams(("arbitrary", "arbitrary")),
        name="inproj",
    )(x2, norm_w, w_in, cos_t, sin_t)


_NT = (((1,), (1,)), ((), ()))


def _moba_kernel(q_ref, k_ref, vt_ref, o_ref, km_scr, ka_scr, sel_scr, qt_scr, *, nb):
    nh = MOBA_HEADS_PER_STEP
    blk = MOBA_BLOCK
    tq = 2 * blk
    lane = lax.broadcasted_iota(jnp.int32, (1, LANES), 1)
    crow = lax.broadcasted_iota(jnp.int32, (LANES, 1), 0)

    def group(ref, rows, h):
        g = h // 2
        return ref[rows, g * LANES:(g + 1) * LANES]

    def prep(i, c):
        rows = pl.ds(pl.multiple_of(i * blk, blk), blk)
        for h in range(nh):
            kb, hh = group(k_ref, rows, h), h % 2
            if hh == 0:
                km_scr[h // 2, pl.ds(i, 1), :] = jnp.sum(kb.astype(F32), axis=0, keepdims=True) * (1.0 / blk)
            onehot = jnp.where(lane == HEAD_DIM * (1 - hh) + i, 1.0, 0.0).astype(BF16)
            ka_scr[h, rows, :] = jnp.where((lane // HEAD_DIM) == hh, kb, onehot)
        return c

    lax.fori_loop(0, nb, prep, 0)

    causal = (lax.broadcasted_iota(jnp.int32, (blk, blk), 0)
              <= lax.broadcasted_iota(jnp.int32, (blk, blk), 1))
    blk_id = lax.broadcasted_iota(jnp.int32, (nb, tq), 0)
    q_half = (lax.broadcasted_iota(jnp.int32, (nb, tq), 1) >= blk).astype(jnp.int32)

    def pv(t, h, p):
        rows = slice(h * VT_ROWS, (h + 1) * VT_ROWS)
        pb = p.astype(BF16)
        return (jnp.dot(vt_ref[2 * t, rows, :], pb[0:blk], preferred_element_type=F32)
                + jnp.dot(vt_ref[2 * t + 1, rows, :], pb[blk:tq], preferred_element_type=F32))

    def qtile(jt, c):
        row0 = pl.multiple_of(jt * tq, tq)
        init = []
        for h in range(nh):
            hh = h % 2
            q_t = group(q_ref, pl.ds(row0, tq), h).astype(F32).T
            k_diag = group(k_ref, pl.ds(row0, tq), h)
            q_m = jnp.where((crow // HEAD_DIM) == hh, q_t, 0.0)
            q_mb = q_m.astype(BF16)
            km = jnp.where((lane // HEAD_DIM) == hh, km_scr[h // 2], 0.0)
            km_hi = km.astype(BF16)
            km_lo = (km - km_hi.astype(F32)).astype(BF16)
            gate = (jnp.dot(km_hi, q_mb, preferred_element_type=F32)
                    + jnp.dot(km_lo, q_mb, preferred_element_type=F32))
            past = blk_id < 2 * jt + q_half
            gate = jnp.where(past, gate, -jnp.inf)
            rank = jnp.zeros((nb, tq), jnp.int32)
            for ip in range(nb):
                gi = gate[ip:ip + 1, :]
                beats = (gi > gate) | ((gi == gate) & (blk_id > ip))
                rank = rank + beats.astype(jnp.int32)
            sel = (rank < MOBA_TOPK) & past
            sel_scr[h] = sel.astype(F32)
            bias = jnp.where(sel, 0.0, NEG_INF)
            spare = HEAD_DIM * (1 - hh)
            pieces = [jnp.zeros((spare, tq), F32)] if spare else []
            pieces += [bias, jnp.zeros((LANES - spare - nb, tq), F32)]
            qt_scr[h] = (q_m + jnp.concatenate(pieces, axis=0)).astype(BF16)

            s_top = jnp.dot(k_diag[0:blk], q_mb, preferred_element_type=F32)
            s11 = jnp.where(causal, s_top[:, 0:blk], NEG_INF)
            s12 = jnp.where(sel_scr[h, pl.ds(2 * jt, 1), blk:tq] > 0.0, s_top[:, blk:tq], NEG_INF)
            s22 = jnp.where(causal, jnp.dot(k_diag[blk:tq], q_mb[:, blk:tq], preferred_element_type=F32),
                            NEG_INF)
            m_a = jnp.max(s11, axis=0, keepdims=True)
            m_b = jnp.maximum(jnp.max(s12, axis=0, keepdims=True), jnp.max(s22, axis=0, keepdims=True))
            rows = slice(h * VT_ROWS, (h + 1) * VT_ROWS)
            v_lo, v_hi = vt_ref[2 * jt, rows, :], vt_ref[2 * jt + 1, rows, :]
            acc_a = jnp.dot(v_lo, jnp.exp2(s11 - m_a).astype(BF16), preferred_element_type=F32)
            acc_b = (jnp.dot(v_lo, jnp.exp2(s12 - m_b).astype(BF16), preferred_element_type=F32)
                     + jnp.dot(v_hi, jnp.exp2(s22 - m_b).astype(BF16), preferred_element_type=F32))
            init += [jnp.concatenate([m_a, m_b], axis=1), jnp.concatenate([acc_a, acc_b], axis=1)]

        def kv_body(it, carry):
            rows = pl.ds(pl.multiple_of(it * tq, tq), tq)
            s = [jnp.dot(ka_scr[h, rows, :], qt_scr[h], preferred_element_type=F32) for h in range(nh)]
            m_new = [jnp.maximum(carry[2 * h], jnp.max(s[h], axis=0, keepdims=True)) for h in range(nh)]
            new = []
            for h in range(nh):
                alpha = jnp.exp2(carry[2 * h] - m_new[h])
                new += [m_new[h], alpha * carry[2 * h + 1] + pv(it, h, jnp.exp2(s[h] - m_new[h]))]
            return tuple(new)

        fin = lax.fori_loop(0, jt, kv_body, tuple(init))
        out_t = jnp.concatenate([fin[2 * h + 1][0:HEAD_DIM] / fin[2 * h + 1][HEAD_DIM:HEAD_DIM + 1]
                                 for h in range(nh)], axis=0)
        o_ref[pl.ds(row0, tq), :] = out_t.T.astype(BF16)
        return c

    lax.fori_loop(0, nb // 2, qtile, 0)


def _moba(proj, vt, batch, seq):
    nb = seq // MOBA_BLOCK
    assert nb % 8 == 0 and nb <= HEAD_DIM
    nh = MOBA_HEADS_PER_STEP
    width = nh * HEAD_DIM
    return pl.pallas_call(
        functools.partial(_moba_kernel, nb=nb),
        grid=(batch, ATTN_HEADS // nh),
        in_specs=[
            pl.BlockSpec((seq, width), lambda b, p: (b, COL_AQ // width + p)),
            pl.BlockSpec((seq, width), lambda b, p: (b, COL_AK // width + p)),
            pl.BlockSpec((None, nb, nh * VT_ROWS, MOBA_BLOCK), lambda b, p: (b, 0, p, 0)),
        ],
        out_specs=pl.BlockSpec((seq, width), lambda b, p: (b, p)),
        out_shape=jax.ShapeDtypeStruct((batch * seq, ATTN_WIDTH), BF16),
        scratch_shapes=[pltpu.VMEM((nh // 2, nb, LANES), F32), pltpu.VMEM((nh, seq, LANES), BF16),
                        pltpu.VMEM((nh, nb, 2 * MOBA_BLOCK), F32),
                        pltpu.VMEM((nh, LANES, 2 * MOBA_BLOCK), BF16)],
        compiler_params=_cparams(("arbitrary", "arbitrary")),
        name="moba",
    )(proj, proj, vt)


def _ret_log_gamma():
    return [math.log1p(-(2.0 ** (-5.0 - h))) for h in range(RET_HEADS)]


def _retention_tables():
    c = RET_TILE
    lg = np.array(_ret_log_gamma(), np.float64)
    n = np.arange(c, dtype=np.float64)
    diff = n[:, None] - n[None, :]
    decay = np.where(diff[None] >= 0, np.exp(np.maximum(diff, 0.0)[None] * lg[:, None, None]), 0.0)
    head_of_lane = np.arange(RET_QK_WIDTH) // HEAD_DIM
    xi = np.exp((n + 1.0)[:, None] * lg[head_of_lane][None, :])
    zeta = np.exp((c - 1.0 - n)[:, None] * lg[head_of_lane][None, :])
    chunk_decay = np.exp(c * lg[head_of_lane])[:, None]
    return (jnp.asarray(decay, F32), jnp.asarray(xi, F32), jnp.asarray(zeta, F32),
            jnp.asarray(np.broadcast_to(chunk_decay, (RET_QK_WIDTH, LANES)), F32))


def _retention_kernel(q_ref, k_ref, v_ref, g_ref, nw_ref, decay_ref, xi_ref, zeta_ref, cd_ref,
                      o_ref, state_scr):
    @pl.when(pl.program_id(1) == 0)
    def _():
        state_scr[...] = jnp.zeros_like(state_scr)

    lane = lax.broadcasted_iota(jnp.int32, (1, LANES), 1)
    srow = lax.broadcasted_iota(jnp.int32, (LANES, 1), 0)
    for pr in range(RET_HEADS // 2):
        cols = slice(pr * LANES, (pr + 1) * LANES)
        q = q_ref[:, cols]
        k = k_ref[:, cols]
        state = state_scr[cols, :]
        state_bf = state.astype(BF16)
        q_xi = (q.astype(F32) * xi_ref[:, cols]).astype(BF16)
        kz_t = (k.astype(F32) * zeta_ref[:, cols]).T.astype(BF16)
        upd = []
        for hh in range(2):
            h = 2 * pr + hh
            hmask = (lane // HEAD_DIM) == hh
            vcols = slice(h * RET_V_DIM, (h + 1) * RET_V_DIM)
            v = v_ref[:, vcols]
            qm = jnp.where(hmask, q, jnp.zeros_like(q))
            scores = lax.dot_general(qm, k, _NT, preferred_element_type=F32) * decay_ref[h]
            o = jnp.dot(scores.astype(BF16), v, preferred_element_type=F32)
            o = o + jnp.dot(jnp.where(hmask, q_xi, jnp.zeros_like(q_xi)), state_bf,
                            preferred_element_type=F32)
            upd.append(jnp.dot(kz_t, v, preferred_element_type=F32))
            ms = jnp.mean(o * o, axis=-1, keepdims=True)
            y = o * lax.rsqrt(ms + RMS_EPS) * nw_ref[:, vcols]
            g = g_ref[:, vcols].astype(F32)
            o_ref[:, vcols] = (g * jax.nn.sigmoid(g) * y).astype(BF16)
        state_scr[cols, :] = state * cd_ref[cols, :] + jnp.where(srow < HEAD_DIM, upd[0], upd[1])


def _retention(proj, ret_norm_w, tables, batch, seq):
    nc = seq // RET_TILE
    decay, xi, zeta, cd = tables
    row = lambda b, c: b * nc + c
    const2 = lambda b, c: (0, 0)
    return pl.pallas_call(
        _retention_kernel,
        grid=(batch, nc),
        in_specs=[
            pl.BlockSpec((RET_TILE, RET_QK_WIDTH), lambda b, c: (row(b, c), COL_RQ // RET_QK_WIDTH)),
            pl.BlockSpec((RET_TILE, RET_QK_WIDTH), lambda b, c: (row(b, c), COL_RK // RET_QK_WIDTH)),
            pl.BlockSpec((RET_TILE, RET_V_WIDTH), lambda b, c: (row(b, c), COL_RV // RET_V_WIDTH)),
            pl.BlockSpec((RET_TILE, RET_V_WIDTH), lambda b, c: (row(b, c), COL_RG // RET_V_WIDTH)),
            pl.BlockSpec((1, RET_V_WIDTH), const2),
            pl.BlockSpec((RET_HEADS, RET_TILE, RET_TILE), lambda b, c: (0, 0, 0)),
            pl.BlockSpec((RET_TILE, RET_QK_WIDTH), const2),
            pl.BlockSpec((RET_TILE, RET_QK_WIDTH), const2),
            pl.BlockSpec((RET_QK_WIDTH, LANES), const2),
        ],
        out_specs=pl.BlockSpec((RET_TILE, RET_V_WIDTH), lambda b, c: (row(b, c), 0)),
        out_shape=jax.ShapeDtypeStruct((batch * seq, RET_V_WIDTH), BF16),
        scratch_shapes=[pltpu.VMEM((RET_QK_WIDTH, RET_V_DIM), F32)],
        compiler_params=_cparams(("arbitrary", "arbitrary")),
        name="retention",
    )(proj, proj, proj, proj, ret_norm_w, decay, xi, zeta, cd)


CONV_HALO = 8


def _merge_kernel(x_ref, ya_ref, cb_ref, cc_ref, ch_ref, yr_ref, ga_ref, gc_ref, gr_ref, cw_ref,
                  cbias_ref, wa_ref, wc_ref, wr_ref, wo_ref, o_ref, u_scr, *, tiles_per_seq):
    i = pl.program_id(0)
    tm = x_ref.shape[0]

    @pl.when(i % tiles_per_seq == 0)
    def _():
        u_scr[0:CONV_HALO, :] = jnp.zeros((CONV_HALO, CONV_WIDTH), F32)

    u_scr[CONV_HALO:CONV_HALO + tm, :] = cc_ref[...].astype(F32) * ch_ref[...].astype(F32)
    conv = (cw_ref[2:3, :] * u_scr[CONV_HALO:CONV_HALO + tm, :]
            + cw_ref[1:2, :] * u_scr[CONV_HALO - 1:CONV_HALO - 1 + tm, :]
            + cw_ref[0:1, :] * u_scr[CONV_HALO - 2:CONV_HALO - 2 + tm, :]
            + cbias_ref[...])
    y_conv = (cb_ref[...].astype(F32) * conv).astype(BF16)
    u_scr[0:CONV_HALO, :] = u_scr[tm:tm + CONV_HALO, :]

    def gate(g_ref):
        return jax.nn.sigmoid(g_ref[...].astype(F32))

    merged = gate(ga_ref) * jnp.dot(ya_ref[...], wa_ref[...], preferred_element_type=F32)
    merged = merged + gate(gc_ref) * jnp.dot(y_conv, wc_ref[...], preferred_element_type=F32)
    merged = merged + gate(gr_ref) * jnp.dot(yr_ref[...], wr_ref[...], preferred_element_type=F32)
    o_ref[...] = x_ref[...] + jnp.dot(merged.astype(BF16), wo_ref[...], preferred_element_type=F32)


def _merge(x2, y_attn, proj, y_ret, conv_w, conv_b, wa, wc, wr, wo, seq):
    n = x2.shape[0]
    tm = MERGE_TM
    const = lambda i: (0, 0)
    wide = lambda c: pl.BlockSpec((tm, 512), lambda i: (i, c // 512))
    gate_spec = lambda b: pl.BlockSpec((tm, D_MODEL), lambda i: (i, COL_GATES // D_MODEL + b))
    return pl.pallas_call(
        functools.partial(_merge_kernel, tiles_per_seq=seq // tm),
        grid=(n // tm,),
        in_specs=[
            pl.BlockSpec((tm, D_MODEL), lambda i: (i, 0)),
            pl.BlockSpec((tm, ATTN_WIDTH), lambda i: (i, 0)),
            wide(COL_CB), wide(COL_CC), wide(COL_CH),
            pl.BlockSpec((tm, RET_V_WIDTH), lambda i: (i, 0)),
            gate_spec(0), gate_spec(1), gate_spec(2),
            pl.BlockSpec((CONV_KERNEL, CONV_WIDTH), const),
            pl.BlockSpec((1, CONV_WIDTH), const),
            pl.BlockSpec((ATTN_WIDTH, D_MODEL), const),
            pl.BlockSpec((CONV_WIDTH, D_MODEL), const),
            pl.BlockSpec((RET_V_WIDTH, D_MODEL), const),
            pl.BlockSpec((D_MODEL, D_MODEL), const),
        ],
        out_specs=pl.BlockSpec((tm, D_MODEL), lambda i: (i, 0)),
        out_shape=jax.ShapeDtypeStruct((n, D_MODEL), F32),
        scratch_shapes=[pltpu.VMEM((tm + CONV_HALO, CONV_WIDTH), F32)],
        compiler_params=_cparams(("arbitrary",)),
        name="merge",
    )(x2, y_attn, proj, proj, proj, y_ret, proj, proj, proj, conv_w, conv_b, wa, wc, wr, wo)


def _swiglu_step(h, wg_ref, wu_ref, wd_ref, acc_scr, c):
    g = jnp.dot(h, wg_ref[...].astype(BF16), preferred_element_type=F32)
    u = jnp.dot(h, wu_ref[...].astype(BF16), preferred_element_type=F32)
    a = (g * jax.nn.sigmoid(g) * u).astype(BF16)
    part = jnp.dot(a, wd_ref[...].astype(BF16), preferred_element_type=F32)

    @pl.when(c == 0)
    def _():
        acc_scr[...] = part

    @pl.when(c != 0)
    def _():
        acc_scr[...] += part


def _dense_ffn_kernel(x_ref, nw_ref, wg_ref, wu_ref, wd_ref, o_ref, h_scr, acc_scr):
    c = pl.program_id(1)

    @pl.when(c == 0)
    def _():
        x = x_ref[...]
        ms = jnp.mean(x * x, axis=-1, keepdims=True)
        h_scr[...] = (x * lax.rsqrt(ms + RMS_EPS) * nw_ref[...]).astype(BF16)

    _swiglu_step(h_scr[...], wg_ref, wu_ref, wd_ref, acc_scr, c)

    @pl.when(c == pl.num_programs(1) - 1)
    def _():
        o_ref[...] = x_ref[...] + acc_scr[...]


def _dense_ffn(x2, norm_w, wg, wu, wd, li):
    n = x2.shape[0]
    tm, tf = FFN_TM, FFN_TF
    return pl.pallas_call(
        _dense_ffn_kernel,
        grid=(n // tm, D_FF // tf),
        in_specs=[
            pl.BlockSpec((tm, D_MODEL), lambda i, c: (i, 0)),
            pl.BlockSpec((1, D_MODEL), lambda i, c: (0, 0)),
            pl.BlockSpec((None, D_MODEL, tf), lambda i, c: (li, 0, c)),
            pl.BlockSpec((None, D_MODEL, tf), lambda i, c: (li, 0, c)),
            pl.BlockSpec((None, tf, D_MODEL), lambda i, c: (li, c, 0)),
        ],
        out_specs=pl.BlockSpec((tm, D_MODEL), lambda i, c: (i, 0)),
        out_shape=jax.ShapeDtypeStruct((n, D_MODEL), F32),
        scratch_shapes=[pltpu.VMEM((tm, D_MODEL), BF16), pltpu.VMEM((tm, D_MODEL), F32)],
        compiler_params=_cparams(("arbitrary", "arbitrary")),
        name="dense_ffn",
    )(x2, norm_w, wg, wu, wd)


SLAB = D_MODEL // LANES


def _pack_rows(v, o_ref):
    rows = v.shape[0]
    for s in range(SLAB):
        o_ref[pl.ds(s, rows, stride=SLAB), :] = v[:, s * LANES:(s + 1) * LANES]


def _unpack_rows(x_ref, rows):
    return jnp.concatenate([x_ref[pl.ds(s, rows, stride=SLAB), :] for s in range(SLAB)], axis=1)


def _route_kernel(x_ref, nw_ref, wr_ref, h_ref, idx_ref, wgt_ref):
    x = x_ref[...]
    ms = jnp.mean(x * x, axis=-1, keepdims=True)
    h = x * lax.rsqrt(ms + RMS_EPS) * nw_ref[...]
    _pack_rows(h, h_ref)
    w = wr_ref[...]
    h_hi, w_hi = h.astype(BF16), w.astype(BF16)
    h_lo, w_lo = (h - h_hi.astype(F32)).astype(BF16), (w - w_hi.astype(F32)).astype(BF16)
    logits = (jnp.dot(h_hi, w_hi, preferred_element_type=F32)
              + jnp.dot(h_lo, w_hi, preferred_element_type=F32)
              + jnp.dot(h_hi, w_lo, preferred_element_type=F32))
    eid = lax.broadcasted_iota(jnp.int32, logits.shape, 1).astype(F32)
    logits = jnp.where(eid < N_EXPERTS, logits, -jnp.inf)
    m1 = jnp.max(logits, axis=-1, keepdims=True)
    i1 = jnp.min(jnp.where(logits == m1, eid, float(LANES)), axis=-1, keepdims=True)
    rest = jnp.where(eid == i1, -jnp.inf, logits)
    m2 = jnp.max(rest, axis=-1, keepdims=True)
    i2 = jnp.min(jnp.where(rest == m2, eid, float(LANES)), axis=-1, keepdims=True)
    e2 = jnp.exp(m2 - m1)
    denom = 1.0 + e2
    idx_ref[...] = jnp.where(eid == 0.0, i1, jnp.where(eid == 1.0, i2, 0.0)).astype(jnp.int32)
    wgt_ref[...] = jnp.where(eid == 0.0, 1.0 / denom, jnp.where(eid == 1.0, e2 / denom, 0.0))


def _route(x2, norm_w, w_router):
    n = x2.shape[0]
    tm = ROUTE_TM
    return pl.pallas_call(
        _route_kernel,
        grid=(n // tm,),
        in_specs=[
            pl.BlockSpec((tm, D_MODEL), lambda i: (i, 0)),
            pl.BlockSpec((1, D_MODEL), lambda i: (0, 0)),
            pl.BlockSpec((D_MODEL, LANES), lambda i: (0, 0)),
        ],
        out_specs=[
            pl.BlockSpec((tm * SLAB, LANES), lambda i: (i, 0)),
            pl.BlockSpec((tm, LANES), lambda i: (i, 0)),
            pl.BlockSpec((tm, LANES), lambda i: (i, 0)),
        ],
        out_shape=[
            jax.ShapeDtypeStruct((n * SLAB, LANES), F32),
            jax.ShapeDtypeStruct((n, LANES), jnp.int32),
            jax.ShapeDtypeStruct((n, LANES), F32),
        ],
        compiler_params=_cparams(("arbitrary",)),
        name="route",
    )(x2, norm_w, jnp.pad(w_router, ((0, 0), (0, LANES - N_EXPERTS))))


MOE_STEPS = D_FF // FFN_TF
MOE_ROWS_PER_STEP = -(-FFN_TM // MOE_STEPS)
MOE_DMA_ROWS = MOE_ROWS_PER_STEP * MOE_STEPS
MOE_BUF_ROWS = -(-MOE_DMA_ROWS // 8) * 8
MOE_SPARE_TILES = 2


def _moe_ffn_kernel(te_ref, nv_ref, gid0_ref, gidn_ref, sid_ref, h3_ref, wg_ref, wu_ref, wd_ref,
                    y_init_ref, y_ref, gbuf, obuf, h_scr, acc_scr, gsem, ssem):
    t = pl.program_id(0)
    c = pl.program_id(1)
    nv = nv_ref[0]
    slot = t % 2
    other = 1 - slot
    first = c == 0

    def slab(ref, start):
        return ref.at[pl.ds(pl.multiple_of(start, SLAB), SLAB)]

    def gather(ids_ref, row, s):
        return pltpu.make_async_copy(slab(h3_ref, ids_ref[0, 0, row]), slab(gbuf.at[s], row * SLAB),
                                     gsem.at[s])

    def scatter(row, s):
        return pltpu.make_async_copy(slab(obuf.at[s], row * SLAB), slab(y_ref, sid_ref[0, 0, row]),
                                     ssem.at[s])

    def tile_gather(s):
        return pltpu.make_async_copy(h3_ref.at[pl.ds(0, MOE_DMA_ROWS * SLAB)],
                                     gbuf.at[s, pl.ds(0, MOE_DMA_ROWS * SLAB)], gsem.at[s])

    def tile_scatter(s):
        return pltpu.make_async_copy(obuf.at[s, pl.ds(0, MOE_DMA_ROWS * SLAB)],
                                     y_ref.at[pl.ds(0, MOE_DMA_ROWS * SLAB)], ssem.at[s])

    @pl.when(first & (t == 0))
    def _():
        obuf[...] = jnp.zeros_like(obuf)

        def body(r, carry):
            gather(gid0_ref, r, 0).start()
            return carry

        lax.fori_loop(0, MOE_DMA_ROWS, body, 0, unroll=MOE_STEPS)

    @pl.when(first & (t <= nv))
    def _():
        tile_gather(slot).wait()

    @pl.when(first & (t >= 1) & (t <= nv + 1))
    def _():
        tile_scatter(slot).wait()

    @pl.when(first & (t < nv))
    def _():
        h_scr[...] = _unpack_rows(gbuf.at[slot], FFN_TM).astype(BF16)

    @pl.when(t < nv)
    def _():
        for u in range(MOE_ROWS_PER_STEP):
            row = c * MOE_ROWS_PER_STEP + u
            gather(gidn_ref, row, other).start(priority=1)
            scatter(row, other).start(priority=1)
        _swiglu_step(h_scr[...], wg_ref, wu_ref, wd_ref, acc_scr, c)

    @pl.when(t == nv)
    def _():
        def body(u, carry):
            scatter(c * MOE_ROWS_PER_STEP + u, other).start()
            return carry

        lax.fori_loop(0, MOE_ROWS_PER_STEP, body, 0, unroll=MOE_STEPS)

    @pl.when((t < nv) & (c == pl.num_programs(1) - 1))
    def _():
        _pack_rows(acc_scr[...], obuf.at[slot])


def _moe_ffn(tile_expert, n_valid, gid, sid, h3, wg, wu, wd, li, y_rows):
    nt = gid.shape[0]
    tm, tf = FFN_TM, FFN_TF
    ids_block = (1, 1, MOE_BUF_ROWS)
    smem = pltpu.SMEM
    wspec = lambda shape, imap: pl.BlockSpec((None, None) + shape, imap)
    grid_spec = pltpu.PrefetchScalarGridSpec(
        num_scalar_prefetch=2,
        grid=(nt, MOE_STEPS),
        in_specs=[
            pl.BlockSpec(ids_block, lambda t, c, te, nv: (0, 0, 0), memory_space=smem),
            pl.BlockSpec(ids_block, lambda t, c, te, nv: (jnp.minimum(t + 1, nt - 1), 0, 0),
                         memory_space=smem),
            pl.BlockSpec(ids_block, lambda t, c, te, nv: (t, 0, 0), memory_space=smem),
            pl.BlockSpec(memory_space=pl.ANY),
            wspec((D_MODEL, tf), lambda t, c, te, nv: (li, te[t], 0, c)),
            wspec((D_MODEL, tf), lambda t, c, te, nv: (li, te[t], 0, c)),
            wspec((tf, D_MODEL), lambda t, c, te, nv: (li, te[t], c, 0)),
            pl.BlockSpec(memory_space=pl.ANY),
        ],
        out_specs=pl.BlockSpec(memory_space=pl.ANY),
        scratch_shapes=[
            pltpu.VMEM((2, MOE_BUF_ROWS * SLAB, LANES), F32),
            pltpu.VMEM((2, MOE_BUF_ROWS * SLAB, LANES), F32),
            pltpu.VMEM((tm, D_MODEL), BF16),
            pltpu.VMEM((tm, D_MODEL), F32),
            pltpu.SemaphoreType.DMA((2,)),
            pltpu.SemaphoreType.DMA((2,)),
        ],
    )
    return pl.pallas_call(
        _moe_ffn_kernel,
        grid_spec=grid_spec,
        out_shape=jax.ShapeDtypeStruct((y_rows * SLAB, LANES), F32),
        input_output_aliases={9: 0},
        compiler_params=_cparams(("arbitrary", "arbitrary")),
        name="moe_ffn",
    )(tile_expert, n_valid, gid * SLAB, gid * SLAB, sid * SLAB, h3, wg, wu, wd,
      jnp.zeros((y_rows * SLAB, LANES), F32))


def _combine_kernel(x_ref, y0_ref, y1_ref, w_ref, nw_ref, o_ref, *, final_norm):
    w = w_ref[...]
    rows = x_ref.shape[0]
    x = x_ref[...] + w[:, 0:1] * _unpack_rows(y0_ref, rows) + w[:, 1:2] * _unpack_rows(y1_ref, rows)
    if final_norm:
        ms = jnp.mean(x * x, axis=-1, keepdims=True)
        x = x * lax.rsqrt(ms + RMS_EPS) * nw_ref[...]
    o_ref[...] = x


def _combine(x2, y_pairs, wgt, norm_w, final_norm):
    n = x2.shape[0]
    tm = MERGE_TM
    nt = n // tm
    return pl.pallas_call(
        functools.partial(_combine_kernel, final_norm=final_norm),
        grid=(nt,),
        in_specs=[
            pl.BlockSpec((tm, D_MODEL), lambda i: (i, 0)),
            pl.BlockSpec((tm * SLAB, LANES), lambda i: (i, 0)),
            pl.BlockSpec((tm * SLAB, LANES), lambda i: (i + nt, 0)),
            pl.BlockSpec((tm, LANES), lambda i: (i, 0)),
            pl.BlockSpec((1, D_MODEL), lambda i: (0, 0)),
        ],
        out_specs=pl.BlockSpec((tm, D_MODEL), lambda i: (i, 0)),
        out_shape=jax.ShapeDtypeStruct((n, D_MODEL), F32),
        compiler_params=_cparams(("arbitrary",)),
        name="combine",
    )(x2, y_pairs, y_pairs, wgt, norm_w)


def _moe_layout(idx):
    n = idx.shape[0]
    tm = FFN_TM
    flat = idx.T.reshape(-1)
    onehot = (flat[:, None] == jnp.arange(N_EXPERTS, dtype=jnp.int32)[None, :]).astype(jnp.int32)
    cum = jnp.cumsum(onehot, axis=0)
    counts = cum[-1]
    rank = jnp.sum(cum * onehot, axis=1) - 1
    padded = ((counts + tm - 1) // tm) * tm
    ends = jnp.cumsum(padded)
    starts = ends - padded
    dest = starts[flat] + rank
    pairs = TOP_K * n
    nt = pairs // tm + N_EXPERTS + MOE_SPARE_TILES
    p = nt * tm
    tile_start = jnp.arange(nt, dtype=jnp.int32) * tm
    tile_expert = jnp.minimum(jnp.sum((ends[None, :] <= tile_start[:, None]).astype(jnp.int32), axis=1),
                              N_EXPERTS - 1)
    n_valid = (ends[-1] // tm).astype(jnp.int32).reshape(1)
    real_before = jnp.cumsum(counts)[tile_expert]
    pos = jnp.arange(p, dtype=jnp.int32).reshape(nt, tm)
    pad_row = (pairs + pos - real_before[:, None]).reshape(p)
    row_pair = pad_row.at[dest].set(jnp.arange(pairs, dtype=jnp.int32))
    row_token = jnp.where(row_pair < pairs, row_pair % n, 0)
    tail = MOE_BUF_ROWS - tm
    gid = jnp.pad(row_token.reshape(nt, tm), ((0, 0), (0, tail))).reshape(nt, 1, MOE_BUF_ROWS)
    first_spare = p + jnp.arange(MOE_BUF_ROWS, dtype=jnp.int32)[None, :]
    tails = p + MOE_BUF_ROWS + jnp.arange(nt * tail, dtype=jnp.int32).reshape(nt, tail)
    sid = jnp.concatenate([row_pair.reshape(nt, tm), tails], axis=1)
    sid = jnp.concatenate([first_spare, sid], axis=0).reshape(nt + 1, 1, MOE_BUF_ROWS)
    y_rows = p + MOE_BUF_ROWS + nt * tail
    return gid, sid, tile_expert, n_valid, y_rows


def _moe(x2, norm_w, w_router, wg, wu, wd, li, final_norm_w, final_norm):
    h3, idx, wgt = _route(x2, norm_w, w_router)
    gid, sid, tile_expert, n_valid, y_rows = _moe_layout(idx[:, :TOP_K])
    y = _moe_ffn(tile_expert, n_valid, gid, sid, h3, wg, wu, wd, li, y_rows)
    return _combine(x2, y, wgt, final_norm_w, final_norm)


def _rope_tables(seq):
    inv_freq = 1.0 / (ROPE_THETA ** (jnp.arange(0, HEAD_DIM, 2, dtype=F32) / HEAD_DIM))
    ang = jnp.arange(seq, dtype=F32)[:, None] * inv_freq[None, :]
    cos, sin = jnp.cos(ang), jnp.sin(ang)
    cos_t = jnp.tile(cos, (1, LANES // (HEAD_DIM // 2)))
    sin_t = jnp.tile(jnp.concatenate([-sin, sin], axis=1), (1, LANES // HEAD_DIM))
    return cos_t, sin_t


def kernel(x, mix_norm_w, w_in, conv_w, conv_b, ret_norm_w, w_br_attn, w_br_conv, w_br_ret, w_out,
           ffn_norm_w, dense_w_gate, dense_w_up, dense_w_down, moe_router, moe_w_gate, moe_w_up,
           moe_w_down, final_norm_w):
    batch, seq, d = x.shape
    depth = w_in.shape[0]
    assert d == D_MODEL and seq % PROJ_TM == 0 and depth % 2 == 0
    n = batch * seq
    cos_t, sin_t = _rope_tables(seq)
    ret_tables = _retention_tables()
    x2 = x.reshape(n, d)
    for layer in range(depth):
        proj, vt = _inproj(x2, mix_norm_w[layer].reshape(1, d), w_in, layer, cos_t, sin_t, batch, seq)
        y_attn = _moba(proj, vt, batch, seq)
        y_ret = _retention(proj, ret_norm_w[layer].reshape(1, RET_V_WIDTH), ret_tables, batch, seq)
        x2 = _merge(x2, y_attn, proj, y_ret, conv_w[layer], conv_b[layer].reshape(1, CONV_WIDTH),
                    w_br_attn[layer].astype(BF16), w_br_conv[layer].astype(BF16),
                    w_br_ret[layer].astype(BF16), w_out[layer].astype(BF16), seq)
        i = layer // 2
        nw = ffn_norm_w[layer].reshape(1, d)
        if layer % 2 == 0:
            x2 = _dense_ffn(x2, nw, dense_w_gate, dense_w_up, dense_w_down, i)
        else:
            last = layer == depth - 1
            x2 = _moe(x2, nw, moe_router[i], moe_w_gate, moe_w_up, moe_w_down, i,
                      final_norm_w.reshape(1, d), last)
    return x2.reshape(batch, seq, d)
```

```python
import functools
import math

import jax
import jax.numpy as jnp
import numpy as np
from jax import lax
from jax.experimental import pallas as pl
from jax.experimental.pallas import tpu as pltpu

F32 = jnp.float32
BF16 = jnp.bfloat16

D_MODEL = 1024
HEAD_DIM = 64
ROPE_THETA = 10000.0
RMS_EPS = 1e-6

ATTN_HEADS = 8
ATTN_WIDTH = ATTN_HEADS * HEAD_DIM
MOBA_BLOCK = 256
MOBA_TOPK = 3
NEG_INF = -1e30
MOBA_HEADS_PER_STEP = 4
VT_ONES = 16
VT_ROWS = HEAD_DIM + VT_ONES
QK_SCALE_LOG2 = HEAD_DIM ** -0.5 * math.log2(math.e)

CONV_WIDTH = 512
CONV_KERNEL = 3

RET_HEADS = 4
RET_QK_WIDTH = RET_HEADS * HEAD_DIM
RET_V_DIM = 2 * HEAD_DIM
RET_V_WIDTH = RET_HEADS * RET_V_DIM
RET_TILE = 256

D_FF = 3584
N_EXPERTS = 8
TOP_K = 2

LANES = 128
IN_PROJ_WIDTH = 7680
COL_AQ, COL_AK, COL_AV = 0, 512, 1024
COL_CB, COL_CC, COL_CH = 1536, 2048, 2560
COL_RV, COL_RG = 3072, 3584
COL_GATES = 4096
COL_RQ, COL_RK = 7168, 7424

PROJ_TM = 2048
PROJ_TN = 512
MERGE_TM = 512
FFN_TM = 1024
FFN_TF = 512
ROUTE_TM = 1024
VMEM_LIMIT = 56 * 1024 * 1024


def _cparams(sem):
    return pltpu.CompilerParams(dimension_semantics=sem, vmem_limit_bytes=VMEM_LIMIT)


def _rope(acc, cos, sin_signed):
    lane = lax.broadcasted_iota(jnp.int32, (1, LANES), 1)
    first_half = (lane % HEAD_DIM) < (HEAD_DIM // 2)
    outs = []
    for g in range(acc.shape[1] // LANES):
        blk = acc[:, g * LANES:(g + 1) * LANES]
        packed = blk.astype(BF16)
        partner = jnp.where(first_half,
                            pltpu.roll(packed, LANES - HEAD_DIM // 2, 1),
                            pltpu.roll(packed, HEAD_DIM // 2, 1)).astype(F32)
        outs.append(blk * cos + partner * sin_signed)
    return jnp.concatenate(outs, axis=1)


def _inproj_kernel(x_ref, nw_ref, w_ref, cos_ref, sin_ref, proj_ref, vt_ref, h_scr):
    j = pl.program_id(1)

    @pl.when(j == 0)
    def _():
        x = x_ref[...]
        ms = jnp.mean(x * x, axis=-1, keepdims=True)
        h_scr[...] = (x * lax.rsqrt(ms + RMS_EPS) * nw_ref[...]).astype(BF16)

    acc = jnp.dot(h_scr[...], w_ref[...].astype(BF16), preferred_element_type=F32)
    jq, jk, jv, jr = COL_AQ // PROJ_TN, COL_AK // PROJ_TN, COL_AV // PROJ_TN, COL_RQ // PROJ_TN
    scale = HEAD_DIM ** -0.5

    @pl.when(j == jq)
    def _():
        proj_ref[...] = (_rope(acc, cos_ref[...], sin_ref[...]) * QK_SCALE_LOG2).astype(BF16)

    @pl.when(j == jk)
    def _():
        proj_ref[...] = _rope(acc, cos_ref[...], sin_ref[...]).astype(BF16)

    @pl.when(j == jv)
    def _():
        proj_ref[...] = acc.astype(BF16)
        ones = jnp.ones((VT_ONES, MOBA_BLOCK), F32)
        for c in range(PROJ_TM // MOBA_BLOCK):
            v_t = acc[c * MOBA_BLOCK:(c + 1) * MOBA_BLOCK, :].T
            rows = []
            for h in range(ATTN_HEADS):
                rows += [v_t[h * HEAD_DIM:(h + 1) * HEAD_DIM], ones]
            vt_ref[c] = jnp.concatenate(rows, axis=0).astype(BF16)

    @pl.when(j == jr)
    def _():
        r = _rope(acc, cos_ref[...], sin_ref[...])
        col = lax.broadcasted_iota(jnp.int32, (1, PROJ_TN), 1)
        r = r * jnp.where(col >= RET_QK_WIDTH, scale, 1.0)
        proj_ref[...] = r.astype(BF16)

    @pl.when((j != jq) & (j != jk) & (j != jv) & (j != jr))
    def _():
        proj_ref[...] = acc.astype(BF16)


def _inproj_src_block(j):
    first_moved = COL_RV // PROJ_TN
    last = IN_PROJ_WIDTH // PROJ_TN - 1
    return jnp.where(j < first_moved, j, jnp.where(j == last, first_moved, j + 1))


def _inproj(x2, norm_w, w_in, layer, cos_t, sin_t, batch, seq):
    n = x2.shape[0]
    nst = seq // PROJ_TM
    nblk = PROJ_TM // MOBA_BLOCK
    return pl.pallas_call(
        _inproj_kernel,
        grid=(n // PROJ_TM, IN_PROJ_WIDTH // PROJ_TN),
        in_specs=[
            pl.BlockSpec((PROJ_TM, D_MODEL), lambda i, j: (i, 0)),
            pl.BlockSpec((1, D_MODEL), lambda i, j: (0, 0)),
            pl.BlockSpec((None, D_MODEL, PROJ_TN), lambda i, j: (layer, 0, _inproj_src_block(j))),
            pl.BlockSpec((PROJ_TM, LANES), lambda i, j: (i % nst, 0)),
            pl.BlockSpec((PROJ_TM, LANES), lambda i, j: (i % nst, 0)),
        ],
        out_specs=[
            pl.BlockSpec((PROJ_TM, PROJ_TN), lambda i, j: (i, j)),
            pl.BlockSpec((None, nblk, ATTN_HEADS * VT_ROWS, MOBA_BLOCK),
                         lambda i, j: (i // nst, i % nst, 0, 0)),
        ],
        out_shape=[
            jax.ShapeDtypeStruct((n, IN_PROJ_WIDTH), BF16),
            jax.ShapeDtypeStruct((batch, seq // MOBA_BLOCK, ATTN_HEADS * VT_ROWS, MOBA_BLOCK), BF16),
        ],
        scratch_shapes=[pltpu.VMEM((PROJ_TM, D_MODEL), BF16)],
        compiler_params=_cparams(("arbitrary", "arbitrary")),
        name="inproj",
    )(x2, norm_w, w_in, cos_t, sin_t)


_NT = (((1,), (1,)), ((), ()))


def _moba_kernel(q_ref, k_ref, vt_ref, o_ref, km_scr, ka_scr, sel_scr, qt_scr, *, nb):
    nh = MOBA_HEADS_PER_STEP
    blk = MOBA_BLOCK
    tq = 2 * blk
    lane = lax.broadcasted_iota(jnp.int32, (1, LANES), 1)
    crow = lax.broadcasted_iota(jnp.int32, (LANES, 1), 0)

    def group(ref, rows, h):
        g = h // 2
        return ref[rows, g * LANES:(g + 1) * LANES]

    def prep(i, c):
        rows = pl.ds(pl.multiple_of(i * blk, blk), blk)
        for h in range(nh):
            kb, hh = group(k_ref, rows, h), h % 2
            if hh == 0:
                km_scr[h // 2, pl.ds(i, 1), :] = jnp.sum(kb.astype(F32), axis=0, keepdims=True) * (1.0 / blk)
            onehot = jnp.where(lane == HEAD_DIM * (1 - hh) + i, 1.0, 0.0).astype(BF16)
            ka_scr[h, rows, :] = jnp.where((lane // HEAD_DIM) == hh, kb, onehot)
        return c

    lax.fori_loop(0, nb, prep, 0)

    causal = (lax.broadcasted_iota(jnp.int32, (blk, blk), 0)
              <= lax.broadcasted_iota(jnp.int32, (blk, blk), 1))
    blk_id = lax.broadcasted_iota(jnp.int32, (nb, tq), 0)
    q_half = (lax.broadcasted_iota(jnp.int32, (nb, tq), 1) >= blk).astype(jnp.int32)

    def pv(t, h, p):
        rows = slice(h * VT_ROWS, (h + 1) * VT_ROWS)
        pb = p.astype(BF16)
        return (jnp.dot(vt_ref[2 * t, rows, :], pb[0:blk], preferred_element_type=F32)
                + jnp.dot(vt_ref[2 * t + 1, rows, :], pb[blk:tq], preferred_element_type=F32))

    def qtile(jt, c):
        row0 = pl.multiple_of(jt * tq, tq)
        init = []
        q_ts = [group(q_ref, pl.ds(row0, tq), h).astype(F32).T for h in range(0, nh, 2)]
        for h in range(nh):
            hh = h % 2
            q_t = q_ts[h // 2]
            k_diag = group(k_ref, pl.ds(row0, tq), h)
            q_m = jnp.where((crow // HEAD_DIM) == hh, q_t, 0.0)
            q_mb = q_m.astype(BF16)
            km = jnp.where((lane // HEAD_DIM) == hh, km_scr[h // 2], 0.0)
            km_hi = km.astype(BF16)
            km_lo = (km - km_hi.astype(F32)).astype(BF16)
            gate = (jnp.dot(km_hi, q_mb, preferred_element_type=F32)
                    + jnp.dot(km_lo, q_mb, preferred_element_type=F32))
            past = blk_id < 2 * jt + q_half
            gate = jnp.where(past, gate, -jnp.inf)
            rank = jnp.zeros((nb, tq), jnp.int32)
            for ip in range(nb):
                gi = gate[ip:ip + 1, :]
                beats = (gi > gate) | ((gi == gate) & (blk_id > ip))
                rank = rank + beats.astype(jnp.int32)
            sel = (rank < MOBA_TOPK) & past
            sel_scr[h] = sel.astype(F32)
            bias = jnp.where(sel, 0.0, NEG_INF)
            spare = HEAD_DIM * (1 - hh)
            pieces = [jnp.zeros((spare, tq), F32)] if spare else []
            pieces += [bias, jnp.zeros((LANES - spare - nb, tq), F32)]
            qt_scr[h] = (q_m + jnp.concatenate(pieces, axis=0)).astype(BF16)

            s_top = jnp.dot(k_diag[0:blk], q_mb, preferred_element_type=F32)
            s11 = jnp.where(causal, s_top[:, 0:blk], NEG_INF)
            s12 = jnp.where(sel_scr[h, pl.ds(2 * jt, 1), blk:tq] > 0.0, s_top[:, blk:tq], NEG_INF)
            s22 = jnp.where(causal, jnp.dot(k_diag[blk:tq], q_mb[:, blk:tq], preferred_element_type=F32),
                            NEG_INF)
            m_a = jnp.max(s11, axis=0, keepdims=True)
            m_b = jnp.maximum(jnp.max(s12, axis=0, keepdims=True), jnp.max(s22, axis=0, keepdims=True))
            rows = slice(h * VT_ROWS, (h + 1) * VT_ROWS)
            v_lo, v_hi = vt_ref[2 * jt, rows, :], vt_ref[2 * jt + 1, rows, :]
            acc_a = jnp.dot(v_lo, jnp.exp2(s11 - m_a).astype(BF16), preferred_element_type=F32)
            acc_b = (jnp.dot(v_lo, jnp.exp2(s12 - m_b).astype(BF16), preferred_element_type=F32)
                     + jnp.dot(v_hi, jnp.exp2(s22 - m_b).astype(BF16), preferred_element_type=F32))
            init += [jnp.concatenate([m_a, m_b], axis=1), jnp.concatenate([acc_a, acc_b], axis=1)]

        def kv_body(it, carry):
            rows = pl.ds(pl.multiple_of(it * tq, tq), tq)
            s = [jnp.dot(ka_scr[h, rows, :], qt_scr[h], preferred_element_type=F32) for h in range(nh)]
            m_new = [jnp.maximum(carry[2 * h], jnp.max(s[h], axis=0, keepdims=True)) for h in range(nh)]
            new = []
            for h in range(nh):
                alpha = jnp.exp2(carry[2 * h] - m_new[h])
                new += [m_new[h], alpha * carry[2 * h + 1] + pv(it, h, jnp.exp2(s[h] - m_new[h]))]
            return tuple(new)

        fin = lax.fori_loop(0, jt, kv_body, tuple(init))
        out_t = jnp.concatenate([fin[2 * h + 1][0:HEAD_DIM] / fin[2 * h + 1][HEAD_DIM:HEAD_DIM + 1]
                                 for h in range(nh)], axis=0)
        o_ref[pl.ds(row0, tq), :] = out_t.T.astype(BF16)
        return c

    lax.fori_loop(0, nb // 2, qtile, 0)


def _moba(proj, vt, batch, seq):
    nb = seq // MOBA_BLOCK
    assert nb % 8 == 0 and nb <= HEAD_DIM
    nh = MOBA_HEADS_PER_STEP
    width = nh * HEAD_DIM
    return pl.pallas_call(
        functools.partial(_moba_kernel, nb=nb),
        grid=(batch, ATTN_HEADS // nh),
        in_specs=[
            pl.BlockSpec((seq, width), lambda b, p: (b, COL_AQ // width + p)),
            pl.BlockSpec((seq, width), lambda b, p: (b, COL_AK // width + p)),
            pl.BlockSpec((None, nb, nh * VT_ROWS, MOBA_BLOCK), lambda b, p: (b, 0, p, 0)),
        ],
        out_specs=pl.BlockSpec((seq, width), lambda b, p: (b, p)),
        out_shape=jax.ShapeDtypeStruct((batch * seq, ATTN_WIDTH), BF16),
        scratch_shapes=[pltpu.VMEM((nh // 2, nb, LANES), F32), pltpu.VMEM((nh, seq, LANES), BF16),
                        pltpu.VMEM((nh, nb, 2 * MOBA_BLOCK), F32),
                        pltpu.VMEM((nh, LANES, 2 * MOBA_BLOCK), BF16)],
        compiler_params=_cparams(("arbitrary", "arbitrary")),
        name="moba",
    )(proj, proj, vt)


def _ret_log_gamma():
    return [math.log1p(-(2.0 ** (-5.0 - h))) for h in range(RET_HEADS)]


def _retention_tables():
    c = RET_TILE
    lg = np.array(_ret_log_gamma(), np.float64)
    n = np.arange(c, dtype=np.float64)
    diff = n[:, None] - n[None, :]
    decay = np.where(diff[None] >= 0, np.exp(np.maximum(diff, 0.0)[None] * lg[:, None, None]), 0.0)
    head_of_lane = np.arange(RET_QK_WIDTH) // HEAD_DIM
    xi = np.exp((n + 1.0)[:, None] * lg[head_of_lane][None, :])
    zeta = np.exp((c - 1.0 - n)[:, None] * lg[head_of_lane][None, :])
    chunk_decay = np.exp(c * lg[head_of_lane])[:, None]
    return (jnp.asarray(decay, F32), jnp.asarray(xi, F32), jnp.asarray(zeta, F32),
            jnp.asarray(np.broadcast_to(chunk_decay, (RET_QK_WIDTH, LANES)), F32))


def _retention_kernel(q_ref, k_ref, v_ref, g_ref, nw_ref, decay_ref, xi_ref, zeta_ref, cd_ref,
                      o_ref, state_scr):
    @pl.when(pl.program_id(1) == 0)
    def _():
        state_scr[...] = jnp.zeros_like(state_scr)

    lane = lax.broadcasted_iota(jnp.int32, (1, LANES), 1)
    srow = lax.broadcasted_iota(jnp.int32, (LANES, 1), 0)
    for pr in range(RET_HEADS // 2):
        cols = slice(pr * LANES, (pr + 1) * LANES)
        q = q_ref[:, cols]
        k = k_ref[:, cols]
        state = state_scr[cols, :]
        state_bf = state.astype(BF16)
        q_xi = (q.astype(F32) * xi_ref[:, cols]).astype(BF16)
        kz_t = (k.astype(F32) * zeta_ref[:, cols]).T.astype(BF16)
        upd = []
        for hh in range(2):
            h = 2 * pr + hh
            hmask = (lane // HEAD_DIM) == hh
            vcols = slice(h * RET_V_DIM, (h + 1) * RET_V_DIM)
            v = v_ref[:, vcols]
            qm = jnp.where(hmask, q, jnp.zeros_like(q))
            scores = lax.dot_general(qm, k, _NT, preferred_element_type=F32) * decay_ref[h]
            o = jnp.dot(scores.astype(BF16), v, preferred_element_type=F32)
            o = o + jnp.dot(jnp.where(hmask, q_xi, jnp.zeros_like(q_xi)), state_bf,
                            preferred_element_type=F32)
            upd.append(jnp.dot(kz_t, v, preferred_element_type=F32))
            ms = jnp.mean(o * o, axis=-1, keepdims=True)
            y = o * lax.rsqrt(ms + RMS_EPS) * nw_ref[:, vcols]
            g = g_ref[:, vcols].astype(F32)
            o_ref[:, vcols] = (g * jax.nn.sigmoid(g) * y).astype(BF16)
        state_scr[cols, :] = state * cd_ref[cols, :] + jnp.where(srow < HEAD_DIM, upd[0], upd[1])


def _retention(proj, ret_norm_w, tables, batch, seq):
    nc = seq // RET_TILE
    decay, xi, zeta, cd = tables
    row = lambda b, c: b * nc + c
    const2 = lambda b, c: (0, 0)
    return pl.pallas_call(
        _retention_kernel,
        grid=(batch, nc),
        in_specs=[
            pl.BlockSpec((RET_TILE, RET_QK_WIDTH), lambda b, c: (row(b, c), COL_RQ // RET_QK_WIDTH)),
            pl.BlockSpec((RET_TILE, RET_QK_WIDTH), lambda b, c: (row(b, c), COL_RK // RET_QK_WIDTH)),
            pl.BlockSpec((RET_TILE, RET_V_WIDTH), lambda b, c: (row(b, c), COL_RV // RET_V_WIDTH)),
            pl.BlockSpec((RET_TILE, RET_V_WIDTH), lambda b, c: (row(b, c), COL_RG // RET_V_WIDTH)),
            pl.BlockSpec((1, RET_V_WIDTH), const2),
            pl.BlockSpec((RET_HEADS, RET_TILE, RET_TILE), lambda b, c: (0, 0, 0)),
            pl.BlockSpec((RET_TILE, RET_QK_WIDTH), const2),
            pl.BlockSpec((RET_TILE, RET_QK_WIDTH), const2),
            pl.BlockSpec((RET_QK_WIDTH, LANES), const2),
        ],
        out_specs=pl.BlockSpec((RET_TILE, RET_V_WIDTH), lambda b, c: (row(b, c), 0)),
        out_shape=jax.ShapeDtypeStruct((batch * seq, RET_V_WIDTH), BF16),
        scratch_shapes=[pltpu.VMEM((RET_QK_WIDTH, RET_V_DIM), F32)],
        compiler_params=_cparams(("arbitrary", "arbitrary")),
        name="retention",
    )(proj, proj, proj, proj, ret_norm_w, decay, xi, zeta, cd)


CONV_HALO = 8


def _merge_kernel(x_ref, ya_ref, cb_ref, cc_ref, ch_ref, yr_ref, ga_ref, gc_ref, gr_ref, cw_ref,
                  cbias_ref, wa_ref, wc_ref, wr_ref, wo_ref, o_ref, u_scr, *, tiles_per_seq):
    i = pl.program_id(0)
    tm = x_ref.shape[0]

    @pl.when(i % tiles_per_seq == 0)
    def _():
        u_scr[0:CONV_HALO, :] = jnp.zeros((CONV_HALO, CONV_WIDTH), F32)

    u_scr[CONV_HALO:CONV_HALO + tm, :] = cc_ref[...].astype(F32) * ch_ref[...].astype(F32)
    conv = (cw_ref[2:3, :] * u_scr[CONV_HALO:CONV_HALO + tm, :]
            + cw_ref[1:2, :] * u_scr[CONV_HALO - 1:CONV_HALO - 1 + tm, :]
            + cw_ref[0:1, :] * u_scr[CONV_HALO - 2:CONV_HALO - 2 + tm, :]
            + cbias_ref[...])
    y_conv = (cb_ref[...].astype(F32) * conv).astype(BF16)
    u_scr[0:CONV_HALO, :] = u_scr[tm:tm + CONV_HALO, :]

    def gate(g_ref):
        return jax.nn.sigmoid(g_ref[...].astype(F32))

    merged = gate(ga_ref) * jnp.dot(ya_ref[...], wa_ref[...], preferred_element_type=F32)
    merged = merged + gate(gc_ref) * jnp.dot(y_conv, wc_ref[...], preferred_element_type=F32)
    merged = merged + gate(gr_ref) * jnp.dot(yr_ref[...], wr_ref[...], preferred_element_type=F32)
    o_ref[...] = x_ref[...] + jnp.dot(merged.astype(BF16), wo_ref[...], preferred_element_type=F32)


def _merge(x2, y_attn, proj, y_ret, conv_w, conv_b, wa, wc, wr, wo, seq):
    n = x2.shape[0]
    tm = MERGE_TM
    const = lambda i: (0, 0)
    wide = lambda c: pl.BlockSpec((tm, 512), lambda i: (i, c // 512))
    gate_spec = lambda b: pl.BlockSpec((tm, D_MODEL), lambda i: (i, COL_GATES // D_MODEL + b))
    return pl.pallas_call(
        functools.partial(_merge_kernel, tiles_per_seq=seq // tm),
        grid=(n // tm,),
        in_specs=[
            pl.BlockSpec((tm, D_MODEL), lambda i: (i, 0)),
            pl.BlockSpec((tm, ATTN_WIDTH), lambda i: (i, 0)),
            wide(COL_CB), wide(COL_CC), wide(COL_CH),
            pl.BlockSpec((tm, RET_V_WIDTH), lambda i: (i, 0)),
            gate_spec(0), gate_spec(1), gate_spec(2),
            pl.BlockSpec((CONV_KERNEL, CONV_WIDTH), const),
            pl.BlockSpec((1, CONV_WIDTH), const),
            pl.BlockSpec((ATTN_WIDTH, D_MODEL), const),
            pl.BlockSpec((CONV_WIDTH, D_MODEL), const),
            pl.BlockSpec((RET_V_WIDTH, D_MODEL), const),
            pl.BlockSpec((D_MODEL, D_MODEL), const),
        ],
        out_specs=pl.BlockSpec((tm, D_MODEL), lambda i: (i, 0)),
        out_shape=jax.ShapeDtypeStruct((n, D_MODEL), F32),
        scratch_shapes=[pltpu.VMEM((tm + CONV_HALO, CONV_WIDTH), F32)],
        compiler_params=_cparams(("arbitrary",)),
        name="merge",
    )(x2, y_attn, proj, proj, proj, y_ret, proj, proj, proj, conv_w, conv_b, wa, wc, wr, wo)


def _swiglu_step(h, wg_ref, wu_ref, wd_ref, acc_scr, c):
    g = jnp.dot(h, wg_ref[...].astype(BF16), preferred_element_type=F32)
    u = jnp.dot(h, wu_ref[...].astype(BF16), preferred_element_type=F32)
    a = (g * jax.nn.sigmoid(g) * u).astype(BF16)
    part = jnp.dot(a, wd_ref[...].astype(BF16), preferred_element_type=F32)

    @pl.when(c == 0)
    def _():
        acc_scr[...] = part

    @pl.when(c != 0)
    def _():
        acc_scr[...] += part


def _dense_ffn_kernel(x_ref, nw_ref, wg_ref, wu_ref, wd_ref, o_ref, h_scr, acc_scr):
    c = pl.program_id(1)

    @pl.when(c == 0)
    def _():
        x = x_ref[...]
        ms = jnp.mean(x * x, axis=-1, keepdims=True)
        h_scr[...] = (x * lax.rsqrt(ms + RMS_EPS) * nw_ref[...]).astype(BF16)

    _swiglu_step(h_scr[...], wg_ref, wu_ref, wd_ref, acc_scr, c)

    @pl.when(c == pl.num_programs(1) - 1)
    def _():
        o_ref[...] = x_ref[...] + acc_scr[...]


def _dense_ffn(x2, norm_w, wg, wu, wd, li):
    n = x2.shape[0]
    tm, tf = FFN_TM, FFN_TF
    return pl.pallas_call(
        _dense_ffn_kernel,
        grid=(n // tm, D_FF // tf),
        in_specs=[
            pl.BlockSpec((tm, D_MODEL), lambda i, c: (i, 0)),
            pl.BlockSpec((1, D_MODEL), lambda i, c: (0, 0)),
            pl.BlockSpec((None, D_MODEL, tf), lambda i, c: (li, 0, c)),
            pl.BlockSpec((None, D_MODEL, tf), lambda i, c: (li, 0, c)),
            pl.BlockSpec((None, tf, D_MODEL), lambda i, c: (li, c, 0)),
        ],
        out_specs=pl.BlockSpec((tm, D_MODEL), lambda i, c: (i, 0)),
        out_shape=jax.ShapeDtypeStruct((n, D_MODEL), F32),
        scratch_shapes=[pltpu.VMEM((tm, D_MODEL), BF16), pltpu.VMEM((tm, D_MODEL), F32)],
        compiler_params=_cparams(("arbitrary", "arbitrary")),
        name="dense_ffn",
    )(x2, norm_w, wg, wu, wd)


SLAB = D_MODEL // LANES


def _pack_rows(v, o_ref):
    rows = v.shape[0]
    for s in range(SLAB):
        o_ref[pl.ds(s, rows, stride=SLAB), :] = v[:, s * LANES:(s + 1) * LANES]


def _unpack_rows(x_ref, rows):
    return jnp.concatenate([x_ref[pl.ds(s, rows, stride=SLAB), :] for s in range(SLAB)], axis=1)


def _route_kernel(x_ref, nw_ref, wr_ref, h_ref, idx_ref, wgt_ref):
    x = x_ref[...]
    ms = jnp.mean(x * x, axis=-1, keepdims=True)
    h = x * lax.rsqrt(ms + RMS_EPS) * nw_ref[...]
    _pack_rows(h, h_ref)
    w = wr_ref[...]
    h_hi, w_hi = h.astype(BF16), w.astype(BF16)
    h_lo, w_lo = (h - h_hi.astype(F32)).astype(BF16), (w - w_hi.astype(F32)).astype(BF16)
    logits = (jnp.dot(h_hi, w_hi, preferred_element_type=F32)
              + jnp.dot(h_lo, w_hi, preferred_element_type=F32)
              + jnp.dot(h_hi, w_lo, preferred_element_type=F32))
    eid = lax.broadcasted_iota(jnp.int32, logits.shape, 1).astype(F32)
    logits = jnp.where(eid < N_EXPERTS, logits, -jnp.inf)
    m1 = jnp.max(logits, axis=-1, keepdims=True)
    i1 = jnp.min(jnp.where(logits == m1, eid, float(LANES)), axis=-1, keepdims=True)
    rest = jnp.where(eid == i1, -jnp.inf, logits)
    m2 = jnp.max(rest, axis=-1, keepdims=True)
    i2 = jnp.min(jnp.where(rest == m2, eid, float(LANES)), axis=-1, keepdims=True)
    e2 = jnp.exp(m2 - m1)
    denom = 1.0 + e2
    idx_ref[...] = jnp.where(eid == 0.0, i1, jnp.where(eid == 1.0, i2, 0.0)).astype(jnp.int32)
    wgt_ref[...] = jnp.where(eid == 0.0, 1.0 / denom, jnp.where(eid == 1.0, e2 / denom, 0.0))


def _route(x2, norm_w, w_router):
    n = x2.shape[0]
    tm = ROUTE_TM
    return pl.pallas_call(
        _route_kernel,
        grid=(n // tm,),
        in_specs=[
            pl.BlockSpec((tm, D_MODEL), lambda i: (i, 0)),
            pl.BlockSpec((1, D_MODEL), lambda i: (0, 0)),
            pl.BlockSpec((D_MODEL, LANES), lambda i: (0, 0)),
        ],
        out_specs=[
            pl.BlockSpec((tm * SLAB, LANES), lambda i: (i, 0)),
            pl.BlockSpec((tm, LANES), lambda i: (i, 0)),
            pl.BlockSpec((tm, LANES), lambda i: (i, 0)),
        ],
        out_shape=[
            jax.ShapeDtypeStruct((n * SLAB, LANES), F32),
            jax.ShapeDtypeStruct((n, LANES), jnp.int32),
            jax.ShapeDtypeStruct((n, LANES), F32),
        ],
        compiler_params=_cparams(("arbitrary",)),
        name="route",
    )(x2, norm_w, jnp.pad(w_router, ((0, 0), (0, LANES - N_EXPERTS))))


MOE_STEPS = D_FF // FFN_TF
MOE_ROWS_PER_STEP = -(-FFN_TM // MOE_STEPS)
MOE_DMA_ROWS = MOE_ROWS_PER_STEP * MOE_STEPS
MOE_BUF_ROWS = -(-MOE_DMA_ROWS // 8) * 8
MOE_SPARE_TILES = 2


def _moe_ffn_kernel(te_ref, nv_ref, gid0_ref, gidn_ref, sid_ref, h3_ref, wg_ref, wu_ref, wd_ref,
                    y_init_ref, y_ref, gbuf, obuf, h_scr, acc_scr, gsem, ssem):
    t = pl.program_id(0)
    c = pl.program_id(1)
    nv = nv_ref[0]
    slot = t % 2
    other = 1 - slot
    first = c == 0

    def slab(ref, start):
        return ref.at[pl.ds(pl.multiple_of(start, SLAB), SLAB)]

    def gather(ids_ref, row, s):
        return pltpu.make_async_copy(slab(h3_ref, ids_ref[0, 0, row]), slab(gbuf.at[s], row * SLAB),
                                     gsem.at[s])

    def scatter(row, s):
        return pltpu.make_async_copy(slab(obuf.at[s], row * SLAB), slab(y_ref, sid_ref[0, 0, row]),
                                     ssem.at[s])

    def tile_gather(s):
        return pltpu.make_async_copy(h3_ref.at[pl.ds(0, MOE_DMA_ROWS * SLAB)],
                                     gbuf.at[s, pl.ds(0, MOE_DMA_ROWS * SLAB)], gsem.at[s])

    def tile_scatter(s):
        return pltpu.make_async_copy(obuf.at[s, pl.ds(0, MOE_DMA_ROWS * SLAB)],
                                     y_ref.at[pl.ds(0, MOE_DMA_ROWS * SLAB)], ssem.at[s])

    @pl.when(first & (t == 0))
    def _():
        obuf[...] = jnp.zeros_like(obuf)

        def body(r, carry):
            gather(gid0_ref, r, 0).start()
            return carry

        lax.fori_loop(0, MOE_DMA_ROWS, body, 0, unroll=MOE_STEPS)

    @pl.when(first & (t <= nv))
    def _():
        tile_gather(slot).wait()

    @pl.when(first & (t >= 1) & (t <= nv + 1))
    def _():
        tile_scatter(slot).wait()

    @pl.when(first & (t < nv))
    def _():
        h_scr[...] = _unpack_rows(gbuf.at[slot], FFN_TM).astype(BF16)

    @pl.when(t < nv)
    def _():
        for u in range(MOE_ROWS_PER_STEP):
            row = c * MOE_ROWS_PER_STEP + u
            gather(gidn_ref, row, other).start(priority=1)
            scatter(row, other).start(priority=1)
        _swiglu_step(h_scr[...], wg_ref, wu_ref, wd_ref, acc_scr, c)

    @pl.when(t == nv)
    def _():
        def body(u, carry):
            scatter(c * MOE_ROWS_PER_STEP + u, other).start()
            return carry

        lax.fori_loop(0, MOE_ROWS_PER_STEP, body, 0, unroll=MOE_STEPS)

    @pl.when((t < nv) & (c == pl.num_programs(1) - 1))
    def _():
        _pack_rows(acc_scr[...], obuf.at[slot])


def _moe_ffn(tile_expert, n_valid, gid, sid, h3, wg, wu, wd, li, y_rows):
    nt = gid.shape[0]
    tm, tf = FFN_TM, FFN_TF
    ids_block = (1, 1, MOE_BUF_ROWS)
    smem = pltpu.SMEM
    wspec = lambda shape, imap: pl.BlockSpec((None, None) + shape, imap)
    expert = lambda t, te, nv: te[jnp.minimum(t, nv[0] - 1)]
    chunk = lambda t, c, nv: jnp.where(t < nv[0], c, MOE_STEPS - 1)
    grid_spec = pltpu.PrefetchScalarGridSpec(
        num_scalar_prefetch=2,
        grid=(nt, MOE_STEPS),
        in_specs=[
            pl.BlockSpec(ids_block, lambda t, c, te, nv: (0, 0, 0), memory_space=smem),
            pl.BlockSpec(ids_block, lambda t, c, te, nv: (jnp.minimum(t + 1, nt - 1), 0, 0),
                         memory_space=smem),
            pl.BlockSpec(ids_block, lambda t, c, te, nv: (t, 0, 0), memory_space=smem),
            pl.BlockSpec(memory_space=pl.ANY),
            wspec((D_MODEL, tf), lambda t, c, te, nv: (li, expert(t, te, nv), 0, chunk(t, c, nv))),
            wspec((D_MODEL, tf), lambda t, c, te, nv: (li, expert(t, te, nv), 0, chunk(t, c, nv))),
            wspec((tf, D_MODEL), lambda t, c, te, nv: (li, expert(t, te, nv), chunk(t, c, nv), 0)),
            pl.BlockSpec(memory_space=pl.ANY),
        ],
        out_specs=pl.BlockSpec(memory_space=pl.ANY),
        scratch_shapes=[
            pltpu.VMEM((2, MOE_BUF_ROWS * SLAB, LANES), F32),
            pltpu.VMEM((2, MOE_BUF_ROWS * SLAB, LANES), F32),
            pltpu.VMEM((tm, D_MODEL), BF16),
            pltpu.VMEM((tm, D_MODEL), F32),
            pltpu.SemaphoreType.DMA((2,)),
            pltpu.SemaphoreType.DMA((2,)),
        ],
    )
    return pl.pallas_call(
        _moe_ffn_kernel,
        grid_spec=grid_spec,
        out_shape=jax.ShapeDtypeStruct((y_rows * SLAB, LANES), F32),
        input_output_aliases={9: 0},
        compiler_params=_cparams(("arbitrary", "arbitrary")),
        name="moe_ffn",
    )(tile_expert, n_valid, gid * SLAB, gid * SLAB, sid * SLAB, h3, wg, wu, wd,
      jnp.zeros((y_rows * SLAB, LANES), F32))


def _combine_kernel(x_ref, y0_ref, y1_ref, w_ref, nw_ref, o_ref, *, final_norm):
    w = w_ref[...]
    rows = x_ref.shape[0]
    x = x_ref[...] + w[:, 0:1] * _unpack_rows(y0_ref, rows) + w[:, 1:2] * _unpack_rows(y1_ref, rows)
    if final_norm:
        ms = jnp.mean(x * x, axis=-1, keepdims=True)
        x = x * lax.rsqrt(ms + RMS_EPS) * nw_ref[...]
    o_ref[...] = x


def _combine(x2, y_pairs, wgt, norm_w, final_norm):
    n = x2.shape[0]
    tm = MERGE_TM
    nt = n // tm
    return pl.pallas_call(
        functools.partial(_combine_kernel, final_norm=final_norm),
        grid=(nt,),
        in_specs=[
            pl.BlockSpec((tm, D_MODEL), lambda i: (i, 0)),
            pl.BlockSpec((tm * SLAB, LANES), lambda i: (i, 0)),
            pl.BlockSpec((tm * SLAB, LANES), lambda i: (i + nt, 0)),
            pl.BlockSpec((tm, LANES), lambda i: (i, 0)),
            pl.BlockSpec((1, D_MODEL), lambda i: (0, 0)),
        ],
        out_specs=pl.BlockSpec((tm, D_MODEL), lambda i: (i, 0)),
        out_shape=jax.ShapeDtypeStruct((n, D_MODEL), F32),
        compiler_params=_cparams(("arbitrary",)),
        name="combine",
    )(x2, y_pairs, y_pairs, wgt, norm_w)


def _moe_layout(idx):
    n = idx.shape[0]
    tm = FFN_TM
    flat = idx.T.reshape(-1)
    onehot = (flat[:, None] == jnp.arange(N_EXPERTS, dtype=jnp.int32)[None, :]).astype(jnp.int32)
    cum = jnp.cumsum(onehot, axis=0)
    counts = cum[-1]
    rank = jnp.sum(cum * onehot, axis=1) - 1
    padded = ((counts + tm - 1) // tm) * tm
    ends = jnp.cumsum(padded)
    starts = ends - padded
    dest = starts[flat] + rank
    pairs = TOP_K * n
    nt = pairs // tm + N_EXPERTS + MOE_SPARE_TILES
    p = nt * tm
    tile_start = jnp.arange(nt, dtype=jnp.int32) * tm
    tile_expert = jnp.minimum(jnp.sum((ends[None, :] <= tile_start[:, None]).astype(jnp.int32), axis=1),
                              N_EXPERTS - 1)
    n_valid = (ends[-1] // tm).astype(jnp.int32).reshape(1)
    real_before = jnp.cumsum(counts)[tile_expert]
    pos = jnp.arange(p, dtype=jnp.int32).reshape(nt, tm)
    pad_row = (pairs + pos - real_before[:, None]).reshape(p)
    row_pair = pad_row.at[dest].set(jnp.arange(pairs, dtype=jnp.int32))
    row_token = jnp.where(row_pair < pairs, row_pair % n, 0)
    tail = MOE_BUF_ROWS - tm
    gid = jnp.pad(row_token.reshape(nt, tm), ((0, 0), (0, tail))).reshape(nt, 1, MOE_BUF_ROWS)
    first_spare = p + jnp.arange(MOE_BUF_ROWS, dtype=jnp.int32)[None, :]
    tails = p + MOE_BUF_ROWS + jnp.arange(nt * tail, dtype=jnp.int32).reshape(nt, tail)
    sid = jnp.concatenate([row_pair.reshape(nt, tm), tails], axis=1)
    sid = jnp.concatenate([first_spare, sid], axis=0).reshape(nt + 1, 1, MOE_BUF_ROWS)
    y_rows = p + MOE_BUF_ROWS + nt * tail
    return gid, sid, tile_expert, n_valid, y_rows


def _moe(x2, norm_w, w_router, wg, wu, wd, li, final_norm_w, final_norm):
    h3, idx, wgt = _route(x2, norm_w, w_router)
    gid, sid, tile_expert, n_valid, y_rows = _moe_layout(idx[:, :TOP_K])
    y = _moe_ffn(tile_expert, n_valid, gid, sid, h3, wg, wu, wd, li, y_rows)
    return _combine(x2, y, wgt, final_norm_w, final_norm)


def _rope_tables(seq):
    inv_freq = 1.0 / (ROPE_THETA ** (jnp.arange(0, HEAD_DIM, 2, dtype=F32) / HEAD_DIM))
    ang = jnp.arange(seq, dtype=F32)[:, None] * inv_freq[None, :]
    cos, sin = jnp.cos(ang), jnp.sin(ang)
    cos_t = jnp.tile(cos, (1, LANES // (HEAD_DIM // 2)))
    sin_t = jnp.tile(jnp.concatenate([-sin, sin], axis=1), (1, LANES // HEAD_DIM))
    return cos_t, sin_t


def kernel(x, mix_norm_w, w_in, conv_w, conv_b, ret_norm_w, w_br_attn, w_br_conv, w_br_ret, w_out,
           ffn_norm_w, dense_w_gate, dense_w_up, dense_w_down, moe_router, moe_w_gate, moe_w_up,
           moe_w_down, final_norm_w):
    batch, seq, d = x.shape
    depth = w_in.shape[0]
    assert d == D_MODEL and seq % PROJ_TM == 0 and depth % 2 == 0
    n = batch * seq
    cos_t, sin_t = _rope_tables(seq)
    ret_tables = _retention_tables()
    x2 = x.reshape(n, d)
    for layer in range(depth):
        proj, vt = _inproj(x2, mix_norm_w[layer].reshape(1, d), w_in, layer, cos_t, sin_t, batch, seq)
        y_attn = _moba(proj, vt, batch, seq)
        y_ret = _retention(proj, ret_norm_w[layer].reshape(1, RET_V_WIDTH), ret_tables, batch, seq)
        x2 = _merge(x2, y_attn, proj, y_ret, conv_w[layer], conv_b[layer].reshape(1, CONV_WIDTH),
                    w_br_attn[layer].astype(BF16), w_br_conv[layer].astype(BF16),
                    w_br_ret[layer].astype(BF16), w_out[layer].astype(BF16), seq)
        i = layer // 2
        nw = ffn_norm_w[layer].reshape(1, d)
        if layer % 2 == 0:
            x2 = _dense_ffn(x2, nw, dense_w_gate, dense_w_up, dense_w_down, i)
        else:
            last = layer == depth - 1
            x2 = _moe(x2, nw, moe_router[i], moe_w_gate, moe_w_up, moe_w_down, i,
                      final_norm_w.reshape(1, d), last)
    return x2.reshape(batch, seq, d)
```

```python
import functools
import math

import jax
import jax.numpy as jnp
import numpy as np
from jax import lax
from jax.experimental import pallas as pl
from jax.experimental.pallas import tpu as pltpu

F32 = jnp.float32
BF16 = jnp.bfloat16

D_MODEL = 1024
HEAD_DIM = 64
ROPE_THETA = 10000.0
RMS_EPS = 1e-6

ATTN_HEADS = 8
ATTN_WIDTH = ATTN_HEADS * HEAD_DIM
MOBA_BLOCK = 256
MOBA_TOPK = 3
NEG_INF = -1e30
MOBA_HEADS_PER_STEP = 4
VT_ONES = 16
VT_ROWS = HEAD_DIM + VT_ONES
QK_SCALE_LOG2 = HEAD_DIM ** -0.5 * math.log2(math.e)

CONV_WIDTH = 512
CONV_KERNEL = 3

RET_HEADS = 4
RET_QK_WIDTH = RET_HEADS * HEAD_DIM
RET_V_DIM = 2 * HEAD_DIM
RET_V_WIDTH = RET_HEADS * RET_V_DIM
RET_TILE = 256

D_FF = 3584
N_EXPERTS = 8
TOP_K = 2

LANES = 128
IN_PROJ_WIDTH = 7680
COL_AQ, COL_AK, COL_AV = 0, 512, 1024
COL_CB, COL_CC, COL_CH = 1536, 2048, 2560
COL_RV, COL_RG = 3072, 3584
COL_GATES = 4096
COL_RQ, COL_RK = 7168, 7424

PROJ_TM = 2048
PROJ_TN = 512
MERGE_TM = 512
FFN_TM = 1024
FFN_TF = 512
ROUTE_TM = 1024
VMEM_LIMIT = 56 * 1024 * 1024


def _cparams(sem):
    return pltpu.CompilerParams(dimension_semantics=sem, vmem_limit_bytes=VMEM_LIMIT)


def _rope(acc, cos, sin_signed):
    lane = lax.broadcasted_iota(jnp.int32, (1, LANES), 1)
    first_half = (lane % HEAD_DIM) < (HEAD_DIM // 2)
    outs = []
    for g in range(acc.shape[1] // LANES):
        blk = acc[:, g * LANES:(g + 1) * LANES]
        packed = blk.astype(BF16)
        partner = jnp.where(first_half,
                            pltpu.roll(packed, LANES - HEAD_DIM // 2, 1),
                            pltpu.roll(packed, HEAD_DIM // 2, 1)).astype(F32)
        outs.append(blk * cos + partner * sin_signed)
    return jnp.concatenate(outs, axis=1)


def _inproj_kernel(x_ref, nw_ref, w_ref, cos_ref, sin_ref, proj_ref, vt_ref, h_scr):
    j = pl.program_id(1)

    @pl.when(j == 0)
    def _():
        x = x_ref[...]
        ms = jnp.mean(x * x, axis=-1, keepdims=True)
        h_scr[...] = (x * lax.rsqrt(ms + RMS_EPS) * nw_ref[...]).astype(BF16)

    acc = jnp.dot(h_scr[...], w_ref[...].astype(BF16), preferred_element_type=F32)
    jq, jk, jv, jr = COL_AQ // PROJ_TN, COL_AK // PROJ_TN, COL_AV // PROJ_TN, COL_RQ // PROJ_TN
    scale = HEAD_DIM ** -0.5

    @pl.when(j == jq)
    def _():
        proj_ref[...] = (_rope(acc, cos_ref[...], sin_ref[...]) * QK_SCALE_LOG2).astype(BF16)

    @pl.when(j == jk)
    def _():
        proj_ref[...] = _rope(acc, cos_ref[...], sin_ref[...]).astype(BF16)

    @pl.when(j == jv)
    def _():
        proj_ref[...] = acc.astype(BF16)
        ones = jnp.ones((VT_ONES, MOBA_BLOCK), F32)
        for c in range(PROJ_TM // MOBA_BLOCK):
            v_t = acc[c * MOBA_BLOCK:(c + 1) * MOBA_BLOCK, :].T
            rows = []
            for h in range(ATTN_HEADS):
                rows += [v_t[h * HEAD_DIM:(h + 1) * HEAD_DIM], ones]
            vt_ref[c] = jnp.concatenate(rows, axis=0).astype(BF16)

    @pl.when(j == jr)
    def _():
        r = _rope(acc, cos_ref[...], sin_ref[...])
        col = lax.broadcasted_iota(jnp.int32, (1, PROJ_TN), 1)
        r = r * jnp.where(col >= RET_QK_WIDTH, scale, 1.0)
        proj_ref[...] = r.astype(BF16)

    @pl.when((j != jq) & (j != jk) & (j != jv) & (j != jr))
    def _():
        proj_ref[...] = acc.astype(BF16)


def _inproj_src_block(j):
    first_moved = COL_RV // PROJ_TN
    last = IN_PROJ_WIDTH // PROJ_TN - 1
    return jnp.where(j < first_moved, j, jnp.where(j == last, first_moved, j + 1))


def _inproj(x2, norm_w, w_in, layer, cos_t, sin_t, batch, seq):
    n = x2.shape[0]
    nst = seq // PROJ_TM
    nblk = PROJ_TM // MOBA_BLOCK
    return pl.pallas_call(
        _inproj_kernel,
        grid=(n // PROJ_TM, IN_PROJ_WIDTH // PROJ_TN),
        in_specs=[
            pl.BlockSpec((PROJ_TM, D_MODEL), lambda i, j: (i, 0)),
            pl.BlockSpec((1, D_MODEL), lambda i, j: (0, 0)),
            pl.BlockSpec((None, D_MODEL, PROJ_TN), lambda i, j: (layer, 0, _inproj_src_block(j))),
            pl.BlockSpec((PROJ_TM, LANES), lambda i, j: (i % nst, 0)),
            pl.BlockSpec((PROJ_TM, LANES), lambda i, j: (i % nst, 0)),
        ],
        out_specs=[
            pl.BlockSpec((PROJ_TM, PROJ_TN), lambda i, j: (i, j)),
            pl.BlockSpec((None, nblk, ATTN_HEADS * VT_ROWS, MOBA_BLOCK),
                         lambda i, j: (i // nst, i % nst, 0, 0)),
        ],
        out_shape=[
            jax.ShapeDtypeStruct((n, IN_PROJ_WIDTH), BF16),
            jax.ShapeDtypeStruct((batch, seq // MOBA_BLOCK, ATTN_HEADS * VT_ROWS, MOBA_BLOCK), BF16),
        ],
        scratch_shapes=[pltpu.VMEM((PROJ_TM, D_MODEL), BF16)],
        compiler_params=_cparams(("arbitrary", "arbitrary")),
        name="inproj",
    )(x2, norm_w, w_in, cos_t, sin_t)


_NT = (((1,), (1,)), ((), ()))


def _moba_kernel(q_ref, k_ref, vt_ref, o_ref, km_scr, ka_scr, sel_scr, qt_scr, *, nb):
    nh = MOBA_HEADS_PER_STEP
    blk = MOBA_BLOCK
    tq = 2 * blk
    lane = lax.broadcasted_iota(jnp.int32, (1, LANES), 1)
    crow = lax.broadcasted_iota(jnp.int32, (LANES, 1), 0)

    def group(ref, rows, h):
        g = h // 2
        return ref[rows, g * LANES:(g + 1) * LANES]

    def prep(i, c):
        rows = pl.ds(pl.multiple_of(i * blk, blk), blk)
        for h in range(nh):
            kb, hh = group(k_ref, rows, h), h % 2
            if hh == 0:
                km_scr[h // 2, pl.ds(i, 1), :] = jnp.sum(kb.astype(F32), axis=0, keepdims=True) * (1.0 / blk)
            onehot = jnp.where(lane == HEAD_DIM * (1 - hh) + i, 1.0, 0.0).astype(BF16)
            ka_scr[h, rows, :] = jnp.where((lane // HEAD_DIM) == hh, kb, onehot)
        return c

    lax.fori_loop(0, nb, prep, 0)

    causal = (lax.broadcasted_iota(jnp.int32, (blk, blk), 0)
              <= lax.broadcasted_iota(jnp.int32, (blk, blk), 1))
    blk_id = lax.broadcasted_iota(jnp.int32, (nb, tq), 0)
    q_half = (lax.broadcasted_iota(jnp.int32, (nb, tq), 1) >= blk).astype(jnp.int32)

    def pv(t, h, p):
        rows = slice(h * VT_ROWS, (h + 1) * VT_ROWS)
        pb = p.astype(BF16)
        return (jnp.dot(vt_ref[2 * t, rows, :], pb[0:blk], preferred_element_type=F32)
                + jnp.dot(vt_ref[2 * t + 1, rows, :], pb[blk:tq], preferred_element_type=F32))

    def qtile(jt, c):
        row0 = pl.multiple_of(jt * tq, tq)
        init = []
        q_ts = [group(q_ref, pl.ds(row0, tq), h).astype(F32).T for h in range(0, nh, 2)]
        for h in range(nh):
            hh = h % 2
            q_t = q_ts[h // 2]
            k_diag = group(k_ref, pl.ds(row0, tq), h)
            q_m = jnp.where((crow // HEAD_DIM) == hh, q_t, 0.0)
            q_mb = q_m.astype(BF16)
            km = jnp.where((lane // HEAD_DIM) == hh, km_scr[h // 2], 0.0)
            km_hi = km.astype(BF16)
            km_lo = (km - km_hi.astype(F32)).astype(BF16)
            gate = (jnp.dot(km_hi, q_mb, preferred_element_type=F32)
                    + jnp.dot(km_lo, q_mb, preferred_element_type=F32))
            past = blk_id < 2 * jt + q_half
            gate = jnp.where(past, gate, -jnp.inf)
            rank = jnp.zeros((nb, tq), jnp.int32)
            for ip in range(nb):
                gi = gate[ip:ip + 1, :]
                beats = (gi > gate) | ((gi == gate) & (blk_id > ip))
                rank = rank + beats.astype(jnp.int32)
            sel = (rank < MOBA_TOPK) & past
            sel_scr[h] = sel.astype(F32)
            bias = jnp.where(sel, 0.0, NEG_INF)
            spare = HEAD_DIM * (1 - hh)
            pieces = [jnp.zeros((spare, tq), F32)] if spare else []
            pieces += [bias, jnp.zeros((LANES - spare - nb, tq), F32)]
            qt_scr[h] = (q_m + jnp.concatenate(pieces, axis=0)).astype(BF16)

            s_top = jnp.dot(k_diag[0:blk], q_mb, preferred_element_type=F32)
            s11 = jnp.where(causal, s_top[:, 0:blk], NEG_INF)
            s12 = jnp.where(sel_scr[h, pl.ds(2 * jt, 1), blk:tq] > 0.0, s_top[:, blk:tq], NEG_INF)
            s22 = jnp.where(causal, jnp.dot(k_diag[blk:tq], q_mb[:, blk:tq], preferred_element_type=F32),
                            NEG_INF)
            m_a = jnp.max(s11, axis=0, keepdims=True)
            m_b = jnp.maximum(jnp.max(s12, axis=0, keepdims=True), jnp.max(s22, axis=0, keepdims=True))
            rows = slice(h * VT_ROWS, (h + 1) * VT_ROWS)
            v_lo, v_hi = vt_ref[2 * jt, rows, :], vt_ref[2 * jt + 1, rows, :]
            acc_a = jnp.dot(v_lo, jnp.exp2(s11 - m_a).astype(BF16), preferred_element_type=F32)
            acc_b = (jnp.dot(v_lo, jnp.exp2(s12 - m_b).astype(BF16), preferred_element_type=F32)
                     + jnp.dot(v_hi, jnp.exp2(s22 - m_b).astype(BF16), preferred_element_type=F32))
            init += [jnp.concatenate([m_a, m_b], axis=1), jnp.concatenate([acc_a, acc_b], axis=1)]

        def kv_body(it, carry):
            rows = pl.ds(pl.multiple_of(it * tq, tq), tq)
            s = [jnp.dot(ka_scr[h, rows, :], qt_scr[h], preferred_element_type=F32) for h in range(nh)]
            m_new = [jnp.maximum(carry[2 * h], jnp.max(s[h], axis=0, keepdims=True)) for h in range(nh)]
            new = []
            for h in range(nh):
                alpha = jnp.exp2(carry[2 * h] - m_new[h])
                new += [m_new[h], alpha * carry[2 * h + 1] + pv(it, h, jnp.exp2(s[h] - m_new[h]))]
            return tuple(new)

        fin = lax.fori_loop(0, jt, kv_body, tuple(init))
        out_t = jnp.concatenate([fin[2 * h + 1][0:HEAD_DIM] / fin[2 * h + 1][HEAD_DIM:HEAD_DIM + 1]
                                 for h in range(nh)], axis=0)
        o_ref[pl.ds(row0, tq), :] = out_t.T.astype(BF16)
        return c

    lax.fori_loop(0, nb // 2, qtile, 0)


def _moba(proj, vt, batch, seq):
    nb = seq // MOBA_BLOCK
    assert nb % 8 == 0 and nb <= HEAD_DIM
    nh = MOBA_HEADS_PER_STEP
    width = nh * HEAD_DIM
    return pl.pallas_call(
        functools.partial(_moba_kernel, nb=nb),
        grid=(batch, ATTN_HEADS // nh),
        in_specs=[
            pl.BlockSpec((seq, width), lambda b, p: (b, COL_AQ // width + p)),
            pl.BlockSpec((seq, width), lambda b, p: (b, COL_AK // width + p)),
            pl.BlockSpec((None, nb, nh * VT_ROWS, MOBA_BLOCK), lambda b, p: (b, 0, p, 0)),
        ],
        out_specs=pl.BlockSpec((seq, width), lambda b, p: (b, p)),
        out_shape=jax.ShapeDtypeStruct((batch * seq, ATTN_WIDTH), BF16),
        scratch_shapes=[pltpu.VMEM((nh // 2, nb, LANES), F32), pltpu.VMEM((nh, seq, LANES), BF16),
                        pltpu.VMEM((nh, nb, 2 * MOBA_BLOCK), F32),
                        pltpu.VMEM((nh, LANES, 2 * MOBA_BLOCK), BF16)],
        compiler_params=_cparams(("arbitrary", "arbitrary")),
        name="moba",
    )(proj, proj, vt)


def _ret_log_gamma():
    return [math.log1p(-(2.0 ** (-5.0 - h))) for h in range(RET_HEADS)]


def _retention_tables():
    c = RET_TILE
    lg = np.array(_ret_log_gamma(), np.float64)
    n = np.arange(c, dtype=np.float64)
    diff = n[:, None] - n[None, :]
    decay = np.where(diff[None] >= 0, np.exp(np.maximum(diff, 0.0)[None] * lg[:, None, None]), 0.0)
    head_of_lane = np.arange(RET_QK_WIDTH) // HEAD_DIM
    xi = np.exp((n + 1.0)[:, None] * lg[head_of_lane][None, :])
    zeta = np.exp((c - 1.0 - n)[:, None] * lg[head_of_lane][None, :])
    chunk_decay = np.exp(c * lg[head_of_lane])[:, None]
    return (jnp.asarray(decay, F32), jnp.asarray(xi, F32), jnp.asarray(zeta, F32),
            jnp.asarray(np.broadcast_to(chunk_decay, (RET_QK_WIDTH, LANES)), F32))


def _retention_kernel(q_ref, k_ref, v_ref, g_ref, nw_ref, decay_ref, xi_ref, zeta_ref, cd_ref,
                      o_ref, state_scr):
    @pl.when(pl.program_id(1) == 0)
    def _():
        state_scr[...] = jnp.zeros_like(state_scr)

    lane = lax.broadcasted_iota(jnp.int32, (1, LANES), 1)
    srow = lax.broadcasted_iota(jnp.int32, (LANES, 1), 0)
    for pr in range(RET_HEADS // 2):
        cols = slice(pr * LANES, (pr + 1) * LANES)
        q = q_ref[:, cols]
        k = k_ref[:, cols]
        state = state_scr[cols, :]
        state_bf = state.astype(BF16)
        q_xi = (q.astype(F32) * xi_ref[:, cols]).astype(BF16)
        kz_t = (k.astype(F32) * zeta_ref[:, cols]).T.astype(BF16)
        upd = []
        for hh in range(2):
            h = 2 * pr + hh
            hmask = (lane // HEAD_DIM) == hh
            vcols = slice(h * RET_V_DIM, (h + 1) * RET_V_DIM)
            v = v_ref[:, vcols]
            qm = jnp.where(hmask, q, jnp.zeros_like(q))
            scores = lax.dot_general(qm, k, _NT, preferred_element_type=F32) * decay_ref[h]
            o = jnp.dot(scores.astype(BF16), v, preferred_element_type=F32)
            o = o + jnp.dot(jnp.where(hmask, q_xi, jnp.zeros_like(q_xi)), state_bf,
                            preferred_element_type=F32)
            upd.append(jnp.dot(kz_t, v, preferred_element_type=F32))
            ms = jnp.mean(o * o, axis=-1, keepdims=True)
            y = o * lax.rsqrt(ms + RMS_EPS) * nw_ref[:, vcols]
            g = g_ref[:, vcols].astype(F32)
            o_ref[:, vcols] = (g * jax.nn.sigmoid(g) * y).astype(BF16)
        state_scr[cols, :] = state * cd_ref[cols, :] + jnp.where(srow < HEAD_DIM, upd[0], upd[1])


def _retention(proj, ret_norm_w, tables, batch, seq):
    nc = seq // RET_TILE
    decay, xi, zeta, cd = tables
    row = lambda b, c: b * nc + c
    const2 = lambda b, c: (0, 0)
    return pl.pallas_call(
        _retention_kernel,
        grid=(batch, nc),
        in_specs=[
            pl.BlockSpec((RET_TILE, RET_QK_WIDTH), lambda b, c: (row(b, c), COL_RQ // RET_QK_WIDTH)),
            pl.BlockSpec((RET_TILE, RET_QK_WIDTH), lambda b, c: (row(b, c), COL_RK // RET_QK_WIDTH)),
            pl.BlockSpec((RET_TILE, RET_V_WIDTH), lambda b, c: (row(b, c), COL_RV // RET_V_WIDTH)),
            pl.BlockSpec((RET_TILE, RET_V_WIDTH), lambda b, c: (row(b, c), COL_RG // RET_V_WIDTH)),
            pl.BlockSpec((1, RET_V_WIDTH), const2),
            pl.BlockSpec((RET_HEADS, RET_TILE, RET_TILE), lambda b, c: (0, 0, 0)),
            pl.BlockSpec((RET_TILE, RET_QK_WIDTH), const2),
            pl.BlockSpec((RET_TILE, RET_QK_WIDTH), const2),
            pl.BlockSpec((RET_QK_WIDTH, LANES), const2),
        ],
        out_specs=pl.BlockSpec((RET_TILE, RET_V_WIDTH), lambda b, c: (row(b, c), 0)),
        out_shape=jax.ShapeDtypeStruct((batch * seq, RET_V_WIDTH), BF16),
        scratch_shapes=[pltpu.VMEM((RET_QK_WIDTH, RET_V_DIM), F32)],
        compiler_params=_cparams(("arbitrary", "arbitrary")),
        name="retention",
    )(proj, proj, proj, proj, ret_norm_w, decay, xi, zeta, cd)


CONV_HALO = 8


def _merge_kernel(x_ref, ya_ref, cb_ref, cc_ref, ch_ref, yr_ref, ga_ref, gc_ref, gr_ref, cw_ref,
                  cbias_ref, wa_ref, wc_ref, wr_ref, wo_ref, o_ref, u_scr, *, tiles_per_seq):
    i = pl.program_id(0)
    tm = x_ref.shape[0]

    @pl.when(i % tiles_per_seq == 0)
    def _():
        u_scr[0:CONV_HALO, :] = jnp.zeros((CONV_HALO, CONV_WIDTH), F32)

    u_scr[CONV_HALO:CONV_HALO + tm, :] = cc_ref[...].astype(F32) * ch_ref[...].astype(F32)
    conv = (cw_ref[2:3, :] * u_scr[CONV_HALO:CONV_HALO + tm, :]
            + cw_ref[1:2, :] * u_scr[CONV_HALO - 1:CONV_HALO - 1 + tm, :]
            + cw_ref[0:1, :] * u_scr[CONV_HALO - 2:CONV_HALO - 2 + tm, :]
            + cbias_ref[...])
    y_conv = (cb_ref[...].astype(F32) * conv).astype(BF16)
    u_scr[0:CONV_HALO, :] = u_scr[tm:tm + CONV_HALO, :]

    def gate(g_ref):
        return jax.nn.sigmoid(g_ref[...].astype(F32))

    merged = gate(ga_ref) * jnp.dot(ya_ref[...], wa_ref[...], preferred_element_type=F32)
    merged = merged + gate(gc_ref) * jnp.dot(y_conv, wc_ref[...], preferred_element_type=F32)
    merged = merged + gate(gr_ref) * jnp.dot(yr_ref[...], wr_ref[...], preferred_element_type=F32)
    o_ref[...] = x_ref[...] + jnp.dot(merged.astype(BF16), wo_ref[...], preferred_element_type=F32)


def _merge(x2, y_attn, proj, y_ret, conv_w, conv_b, wa, wc, wr, wo, seq):
    n = x2.shape[0]
    tm = MERGE_TM
    const = lambda i: (0, 0)
    wide = lambda c: pl.BlockSpec((tm, CONV_WIDTH), lambda i: (i, c // CONV_WIDTH))
    gate_spec = lambda b: pl.BlockSpec((tm, D_MODEL), lambda i: (i, COL_GATES // D_MODEL + b))
    return pl.pallas_call(
        functools.partial(_merge_kernel, tiles_per_seq=seq // tm),
        grid=(n // tm,),
        in_specs=[
            pl.BlockSpec((tm, D_MODEL), lambda i: (i, 0)),
            pl.BlockSpec((tm, ATTN_WIDTH), lambda i: (i, 0)),
            wide(COL_CB), wide(COL_CC), wide(COL_CH),
            pl.BlockSpec((tm, RET_V_WIDTH), lambda i: (i, 0)),
            gate_spec(0), gate_spec(1), gate_spec(2),
            pl.BlockSpec((CONV_KERNEL, CONV_WIDTH), const),
            pl.BlockSpec((1, CONV_WIDTH), const),
            pl.BlockSpec((ATTN_WIDTH, D_MODEL), const),
            pl.BlockSpec((CONV_WIDTH, D_MODEL), const),
            pl.BlockSpec((RET_V_WIDTH, D_MODEL), const),
            pl.BlockSpec((D_MODEL, D_MODEL), const),
        ],
        out_specs=pl.BlockSpec((tm, D_MODEL), lambda i: (i, 0)),
        out_shape=jax.ShapeDtypeStruct((n, D_MODEL), F32),
        scratch_shapes=[pltpu.VMEM((tm + CONV_HALO, CONV_WIDTH), F32)],
        compiler_params=_cparams(("arbitrary",)),
        name="merge",
    )(x2, y_attn, proj, proj, proj, y_ret, proj, proj, proj, conv_w, conv_b, wa, wc, wr, wo)


def _swiglu_step(h, wg_ref, wu_ref, wd_ref, acc_scr, c):
    g = jnp.dot(h, wg_ref[...].astype(BF16), preferred_element_type=F32)
    u = jnp.dot(h, wu_ref[...].astype(BF16), preferred_element_type=F32)
    a = (g * jax.nn.sigmoid(g) * u).astype(BF16)
    part = jnp.dot(a, wd_ref[...].astype(BF16), preferred_element_type=F32)

    @pl.when(c == 0)
    def _():
        acc_scr[...] = part

    @pl.when(c != 0)
    def _():
        acc_scr[...] += part


def _dense_ffn_kernel(x_ref, nw_ref, wg_ref, wu_ref, wd_ref, o_ref, h_scr, acc_scr):
    c = pl.program_id(1)

    @pl.when(c == 0)
    def _():
        x = x_ref[...]
        ms = jnp.mean(x * x, axis=-1, keepdims=True)
        h_scr[...] = (x * lax.rsqrt(ms + RMS_EPS) * nw_ref[...]).astype(BF16)

    _swiglu_step(h_scr[...], wg_ref, wu_ref, wd_ref, acc_scr, c)

    @pl.when(c == pl.num_programs(1) - 1)
    def _():
        o_ref[...] = x_ref[...] + acc_scr[...]


def _dense_ffn(x2, norm_w, wg, wu, wd, li):
    n = x2.shape[0]
    tm, tf = FFN_TM, FFN_TF
    return pl.pallas_call(
        _dense_ffn_kernel,
        grid=(n // tm, D_FF // tf),
        in_specs=[
            pl.BlockSpec((tm, D_MODEL), lambda i, c: (i, 0)),
            pl.BlockSpec((1, D_MODEL), lambda i, c: (0, 0)),
            pl.BlockSpec((None, D_MODEL, tf), lambda i, c: (li, 0, c)),
            pl.BlockSpec((None, D_MODEL, tf), lambda i, c: (li, 0, c)),
            pl.BlockSpec((None, tf, D_MODEL), lambda i, c: (li, c, 0)),
        ],
        out_specs=pl.BlockSpec((tm, D_MODEL), lambda i, c: (i, 0)),
        out_shape=jax.ShapeDtypeStruct((n, D_MODEL), F32),
        scratch_shapes=[pltpu.VMEM((tm, D_MODEL), BF16), pltpu.VMEM((tm, D_MODEL), F32)],
        compiler_params=_cparams(("arbitrary", "arbitrary")),
        name="dense_ffn",
    )(x2, norm_w, wg, wu, wd)


SLAB = D_MODEL // LANES


def _pack_rows(v, o_ref):
    rows = v.shape[0]
    for s in range(SLAB):
        o_ref[pl.ds(s, rows, stride=SLAB), :] = v[:, s * LANES:(s + 1) * LANES]


def _unpack_rows(x_ref, rows):
    return jnp.concatenate([x_ref[pl.ds(s, rows, stride=SLAB), :] for s in range(SLAB)], axis=1)


def _route_kernel(x_ref, nw_ref, wr_ref, h_ref, idx_ref, wgt_ref):
    x = x_ref[...]
    ms = jnp.mean(x * x, axis=-1, keepdims=True)
    h = x * lax.rsqrt(ms + RMS_EPS) * nw_ref[...]
    _pack_rows(h, h_ref)
    w = wr_ref[...]
    h_hi, w_hi = h.astype(BF16), w.astype(BF16)
    h_lo, w_lo = (h - h_hi.astype(F32)).astype(BF16), (w - w_hi.astype(F32)).astype(BF16)
    logits = (jnp.dot(h_hi, w_hi, preferred_element_type=F32)
              + jnp.dot(h_lo, w_hi, preferred_element_type=F32)
              + jnp.dot(h_hi, w_lo, preferred_element_type=F32))
    eid = lax.broadcasted_iota(jnp.int32, logits.shape, 1).astype(F32)
    logits = jnp.where(eid < N_EXPERTS, logits, -jnp.inf)
    m1 = jnp.max(logits, axis=-1, keepdims=True)
    i1 = jnp.min(jnp.where(logits == m1, eid, float(LANES)), axis=-1, keepdims=True)
    rest = jnp.where(eid == i1, -jnp.inf, logits)
    m2 = jnp.max(rest, axis=-1, keepdims=True)
    i2 = jnp.min(jnp.where(rest == m2, eid, float(LANES)), axis=-1, keepdims=True)
    e2 = jnp.exp(m2 - m1)
    denom = 1.0 + e2
    idx_ref[...] = jnp.where(eid == 0.0, i1, jnp.where(eid == 1.0, i2, 0.0)).astype(jnp.int32)
    wgt_ref[...] = jnp.where(eid == 0.0, 1.0 / denom, jnp.where(eid == 1.0, e2 / denom, 0.0))


def _route(x2, norm_w, w_router):
    n = x2.shape[0]
    tm = ROUTE_TM
    return pl.pallas_call(
        _route_kernel,
        grid=(n // tm,),
        in_specs=[
            pl.BlockSpec((tm, D_MODEL), lambda i: (i, 0)),
            pl.BlockSpec((1, D_MODEL), lambda i: (0, 0)),
            pl.BlockSpec((D_MODEL, LANES), lambda i: (0, 0)),
        ],
        out_specs=[
            pl.BlockSpec((tm * SLAB, LANES), lambda i: (i, 0)),
            pl.BlockSpec((tm, LANES), lambda i: (i, 0)),
            pl.BlockSpec((tm, LANES), lambda i: (i, 0)),
        ],
        out_shape=[
            jax.ShapeDtypeStruct((n * SLAB, LANES), F32),
            jax.ShapeDtypeStruct((n, LANES), jnp.int32),
            jax.ShapeDtypeStruct((n, LANES), F32),
        ],
        compiler_params=_cparams(("arbitrary",)),
        name="route",
    )(x2, norm_w, jnp.pad(w_router, ((0, 0), (0, LANES - N_EXPERTS))))


MOE_STEPS = D_FF // FFN_TF
MOE_ROWS_PER_STEP = -(-FFN_TM // MOE_STEPS)
MOE_DMA_ROWS = MOE_ROWS_PER_STEP * MOE_STEPS
MOE_BUF_ROWS = -(-MOE_DMA_ROWS // 8) * 8
MOE_SPARE_TILES = 2


def _moe_ffn_kernel(te_ref, nv_ref, gid0_ref, gidn_ref, sid_ref, h3_ref, wg_ref, wu_ref, wd_ref,
                    y_init_ref, y_ref, gbuf, obuf, h_scr, acc_scr, gsem, ssem):
    t = pl.program_id(0)
    c = pl.program_id(1)
    nv = nv_ref[0]
    slot = t % 2
    other = 1 - slot
    first = c == 0

    def slab(ref, start):
        return ref.at[pl.ds(pl.multiple_of(start, SLAB), SLAB)]

    def gather(ids_ref, row, s):
        return pltpu.make_async_copy(slab(h3_ref, ids_ref[0, 0, row]), slab(gbuf.at[s], row * SLAB),
                                     gsem.at[s])

    def scatter(row, s):
        return pltpu.make_async_copy(slab(obuf.at[s], row * SLAB), slab(y_ref, sid_ref[0, 0, row]),
                                     ssem.at[s])

    def tile_gather(s):
        return pltpu.make_async_copy(h3_ref.at[pl.ds(0, MOE_DMA_ROWS * SLAB)],
                                     gbuf.at[s, pl.ds(0, MOE_DMA_ROWS * SLAB)], gsem.at[s])

    def tile_scatter(s):
        return pltpu.make_async_copy(obuf.at[s, pl.ds(0, MOE_DMA_ROWS * SLAB)],
                                     y_ref.at[pl.ds(0, MOE_DMA_ROWS * SLAB)], ssem.at[s])

    @pl.when(first & (t == 0))
    def _():
        obuf[...] = jnp.zeros_like(obuf)

        def body(r, carry):
            gather(gid0_ref, r, 0).start()
            return carry

        lax.fori_loop(0, MOE_DMA_ROWS, body, 0, unroll=MOE_STEPS)

    @pl.when(first & (t <= nv))
    def _():
        tile_gather(slot).wait()

    @pl.when(first & (t >= 1) & (t <= nv + 1))
    def _():
        tile_scatter(slot).wait()

    @pl.when(first & (t < nv))
    def _():
        h_scr[...] = _unpack_rows(gbuf.at[slot], FFN_TM).astype(BF16)

    @pl.when(t < nv)
    def _():
        for u in range(MOE_ROWS_PER_STEP):
            row = c * MOE_ROWS_PER_STEP + u
            gather(gidn_ref, row, other).start(priority=1)
            scatter(row, other).start(priority=1)
        _swiglu_step(h_scr[...], wg_ref, wu_ref, wd_ref, acc_scr, c)

    @pl.when(t == nv)
    def _():
        def body(u, carry):
            scatter(c * MOE_ROWS_PER_STEP + u, other).start()
            return carry

        lax.fori_loop(0, MOE_ROWS_PER_STEP, body, 0, unroll=MOE_STEPS)

    @pl.when((t < nv) & (c == pl.num_programs(1) - 1))
    def _():
        _pack_rows(acc_scr[...], obuf.at[slot])


def _moe_ffn(tile_expert, n_valid, gid, sid, h3, wg, wu, wd, li, y_rows):
    nt = gid.shape[0]
    tm, tf = FFN_TM, FFN_TF
    ids_block = (1, 1, MOE_BUF_ROWS)
    smem = pltpu.SMEM
    wspec = lambda shape, imap: pl.BlockSpec((None, None) + shape, imap)
    expert = lambda t, te, nv: te[jnp.minimum(t, nv[0] - 1)]
    chunk = lambda t, c, nv: jnp.where(t < nv[0], c, MOE_STEPS - 1)
    grid_spec = pltpu.PrefetchScalarGridSpec(
        num_scalar_prefetch=2,
        grid=(nt, MOE_STEPS),
        in_specs=[
            pl.BlockSpec(ids_block, lambda t, c, te, nv: (0, 0, 0), memory_space=smem),
            pl.BlockSpec(ids_block, lambda t, c, te, nv: (jnp.minimum(t + 1, nt - 1), 0, 0),
                         memory_space=smem),
            pl.BlockSpec(ids_block, lambda t, c, te, nv: (t, 0, 0), memory_space=smem),
            pl.BlockSpec(memory_space=pl.ANY),
            wspec((D_MODEL, tf), lambda t, c, te, nv: (li, expert(t, te, nv), 0, chunk(t, c, nv))),
            wspec((D_MODEL, tf), lambda t, c, te, nv: (li, expert(t, te, nv), 0, chunk(t, c, nv))),
            wspec((tf, D_MODEL), lambda t, c, te, nv: (li, expert(t, te, nv), chunk(t, c, nv), 0)),
            pl.BlockSpec(memory_space=pl.ANY),
        ],
        out_specs=pl.BlockSpec(memory_space=pl.ANY),
        scratch_shapes=[
            pltpu.VMEM((2, MOE_BUF_ROWS * SLAB, LANES), F32),
            pltpu.VMEM((2, MOE_BUF_ROWS * SLAB, LANES), F32),
            pltpu.VMEM((tm, D_MODEL), BF16),
            pltpu.VMEM((tm, D_MODEL), F32),
            pltpu.SemaphoreType.DMA((2,)),
            pltpu.SemaphoreType.DMA((2,)),
        ],
    )
    return pl.pallas_call(
        _moe_ffn_kernel,
        grid_spec=grid_spec,
        out_shape=jax.ShapeDtypeStruct((y_rows * SLAB, LANES), F32),
        input_output_aliases={9: 0},
        compiler_params=_cparams(("arbitrary", "arbitrary")),
        name="moe_ffn",
    )(tile_expert, n_valid, gid * SLAB, gid * SLAB, sid * SLAB, h3, wg, wu, wd,
      jnp.zeros((y_rows * SLAB, LANES), F32))


def _combine_kernel(x_ref, y0_ref, y1_ref, w_ref, nw_ref, o_ref, *, final_norm):
    w = w_ref[...]
    rows = x_ref.shape[0]
    x = x_ref[...] + w[:, 0:1] * _unpack_rows(y0_ref, rows) + w[:, 1:2] * _unpack_rows(y1_ref, rows)
    if final_norm:
        ms = jnp.mean(x * x, axis=-1, keepdims=True)
        x = x * lax.rsqrt(ms + RMS_EPS) * nw_ref[...]
    o_ref[...] = x


def _combine(x2, y_pairs, wgt, norm_w, final_norm):
    n = x2.shape[0]
    tm = MERGE_TM
    nt = n // tm
    return pl.pallas_call(
        functools.partial(_combine_kernel, final_norm=final_norm),
        grid=(nt,),
        in_specs=[
            pl.BlockSpec((tm, D_MODEL), lambda i: (i, 0)),
            pl.BlockSpec((tm * SLAB, LANES), lambda i: (i, 0)),
            pl.BlockSpec((tm * SLAB, LANES), lambda i: (i + nt, 0)),
            pl.BlockSpec((tm, LANES), lambda i: (i, 0)),
            pl.BlockSpec((1, D_MODEL), lambda i: (0, 0)),
        ],
        out_specs=pl.BlockSpec((tm, D_MODEL), lambda i: (i, 0)),
        out_shape=jax.ShapeDtypeStruct((n, D_MODEL), F32),
        compiler_params=_cparams(("arbitrary",)),
        name="combine",
    )(x2, y_pairs, y_pairs, wgt, norm_w)


def _moe_layout(idx):
    n = idx.shape[0]
    tm = FFN_TM
    flat = idx.T.reshape(-1)
    onehot = (flat[:, None] == jnp.arange(N_EXPERTS, dtype=jnp.int32)[None, :]).astype(jnp.int32)
    cum = jnp.cumsum(onehot, axis=0)
    counts = cum[-1]
    rank = jnp.sum(cum * onehot, axis=1) - 1
    padded = ((counts + tm - 1) // tm) * tm
    ends = jnp.cumsum(padded)
    starts = ends - padded
    dest = starts[flat] + rank
    pairs = TOP_K * n
    nt = pairs // tm + N_EXPERTS + MOE_SPARE_TILES
    p = nt * tm
    tile_start = jnp.arange(nt, dtype=jnp.int32) * tm
    tile_expert = jnp.minimum(jnp.sum((ends[None, :] <= tile_start[:, None]).astype(jnp.int32), axis=1),
                              N_EXPERTS - 1)
    n_valid = (ends[-1] // tm).astype(jnp.int32).reshape(1)
    real_before = jnp.cumsum(counts)[tile_expert]
    pos = jnp.arange(p, dtype=jnp.int32).reshape(nt, tm)
    pad_row = (pairs + pos - real_before[:, None]).reshape(p)
    row_pair = pad_row.at[dest].set(jnp.arange(pairs, dtype=jnp.int32), unique_indices=True,
                                    mode="promise_in_bounds")
    row_token = jnp.where(row_pair < pairs, row_pair % n, 0)
    tail = MOE_BUF_ROWS - tm
    gid = jnp.pad(row_token.reshape(nt, tm), ((0, 0), (0, tail))).reshape(nt, 1, MOE_BUF_ROWS)
    first_spare = p + jnp.arange(MOE_BUF_ROWS, dtype=jnp.int32)[None, :]
    tails = p + MOE_BUF_ROWS + jnp.arange(nt * tail, dtype=jnp.int32).reshape(nt, tail)
    sid = jnp.concatenate([row_pair.reshape(nt, tm), tails], axis=1)
    sid = jnp.concatenate([first_spare, sid], axis=0).reshape(nt + 1, 1, MOE_BUF_ROWS)
    y_rows = p + MOE_BUF_ROWS + nt * tail
    return gid, sid, tile_expert, n_valid, y_rows


def _moe(x2, norm_w, w_router, wg, wu, wd, li, final_norm_w, final_norm):
    h3, idx, wgt = _route(x2, norm_w, w_router)
    gid, sid, tile_expert, n_valid, y_rows = _moe_layout(idx[:, :TOP_K])
    y = _moe_ffn(tile_expert, n_valid, gid, sid, h3, wg, wu, wd, li, y_rows)
    return _combine(x2, y, wgt, final_norm_w, final_norm)


def _rope_tables(seq):
    inv_freq = 1.0 / (ROPE_THETA ** (jnp.arange(0, HEAD_DIM, 2, dtype=F32) / HEAD_DIM))
    ang = jnp.arange(seq, dtype=F32)[:, None] * inv_freq[None, :]
    cos, sin = jnp.cos(ang), jnp.sin(ang)
    cos_t = jnp.tile(cos, (1, LANES // (HEAD_DIM // 2)))
    sin_t = jnp.tile(jnp.concatenate([-sin, sin], axis=1), (1, LANES // HEAD_DIM))
    return cos_t, sin_t


def kernel(x, mix_norm_w, w_in, conv_w, conv_b, ret_norm_w, w_br_attn, w_br_conv, w_br_ret, w_out,
           ffn_norm_w, dense_w_gate, dense_w_up, dense_w_down, moe_router, moe_w_gate, moe_w_up,
           moe_w_down, final_norm_w):
    batch, seq, d = x.shape
    depth = w_in.shape[0]
    assert d == D_MODEL and seq % PROJ_TM == 0 and depth % 2 == 0
    n = batch * seq
    cos_t, sin_t = _rope_tables(seq)
    ret_tables = _retention_tables()
    x2 = x.reshape(n, d)
    for layer in range(depth):
        proj, vt = _inproj(x2, mix_norm_w[layer].reshape(1, d), w_in, layer, cos_t, sin_t, batch, seq)
        y_attn = _moba(proj, vt, batch, seq)
        y_ret = _retention(proj, ret_norm_w[layer].reshape(1, RET_V_WIDTH), ret_tables, batch, seq)
        x2 = _merge(x2, y_attn, proj, y_ret, conv_w[layer], conv_b[layer].reshape(1, CONV_WIDTH),
                    w_br_attn[layer].astype(BF16), w_br_conv[layer].astype(BF16),
                    w_br_ret[layer].astype(BF16), w_out[layer].astype(BF16), seq)
        i = layer // 2
        nw = ffn_norm_w[layer].reshape(1, d)
        if layer % 2 == 0:
            x2 = _dense_ffn(x2, nw, dense_w_gate, dense_w_up, dense_w_down, i)
        else:
            last = layer == depth - 1
            x2 = _moe(x2, nw, moe_router[i], moe_w_gate, moe_w_up, moe_w_down, i,
                      final_norm_w.reshape(1, d), last)
    return x2.reshape(batch, seq, d)
```

```python
import functools
import math

import jax
import jax.numpy as jnp
import numpy as np
from jax import lax
from jax.experimental import pallas as pl
from jax.experimental.pallas import tpu as pltpu

F32 = jnp.float32
BF16 = jnp.bfloat16

D_MODEL = 1024
HEAD_DIM = 64
ROPE_THETA = 10000.0
RMS_EPS = 1e-6

ATTN_HEADS = 8
ATTN_WIDTH = ATTN_HEADS * HEAD_DIM
MOBA_BLOCK = 256
MOBA_TOPK = 3
NEG_INF = -1e30
MOBA_HEADS_PER_STEP = 4
VT_ONES = 16
VT_ROWS = HEAD_DIM + VT_ONES
QK_SCALE_LOG2 = HEAD_DIM ** -0.5 * math.log2(math.e)

CONV_WIDTH = 512
CONV_KERNEL = 3

RET_HEADS = 4
RET_QK_WIDTH = RET_HEADS * HEAD_DIM
RET_V_DIM = 2 * HEAD_DIM
RET_V_WIDTH = RET_HEADS * RET_V_DIM
RET_TILE = 256

D_FF = 3584
N_EXPERTS = 8
TOP_K = 2

LANES = 128
IN_PROJ_WIDTH = 7680
COL_AQ, COL_AK, COL_AV = 0, 512, 1024
COL_CB, COL_CC, COL_CH = 1536, 2048, 2560
COL_RV, COL_RG = 3072, 3584
COL_GATES = 4096
COL_RQ, COL_RK = 7168, 7424

PROJ_TM = 2048
PROJ_TN = 512
MERGE_TM = 512
FFN_TM = 1024
FFN_TF = 512
ROUTE_TM = 1024
VMEM_LIMIT = 56 * 1024 * 1024


def _cparams(sem):
    return pltpu.CompilerParams(dimension_semantics=sem, vmem_limit_bytes=VMEM_LIMIT)


def _rope(acc, cos, sin_signed):
    lane = lax.broadcasted_iota(jnp.int32, (1, LANES), 1)
    first_half = (lane % HEAD_DIM) < (HEAD_DIM // 2)
    outs = []
    for g in range(acc.shape[1] // LANES):
        blk = acc[:, g * LANES:(g + 1) * LANES]
        packed = blk.astype(BF16)
        partner = jnp.where(first_half,
                            pltpu.roll(packed, LANES - HEAD_DIM // 2, 1),
                            pltpu.roll(packed, HEAD_DIM // 2, 1)).astype(F32)
        outs.append(blk * cos + partner * sin_signed)
    return jnp.concatenate(outs, axis=1)


def _inproj_kernel(x_ref, nw_ref, w_ref, cos_ref, sin_ref, proj_ref, vt_ref, h_scr):
    j = pl.program_id(1)

    @pl.when(j == 0)
    def _():
        x = x_ref[...]
        ms = jnp.mean(x * x, axis=-1, keepdims=True)
        h_scr[...] = (x * lax.rsqrt(ms + RMS_EPS) * nw_ref[...]).astype(BF16)

    acc = jnp.dot(h_scr[...], w_ref[...].astype(BF16), preferred_element_type=F32)
    jq, jk, jv, jr = COL_AQ // PROJ_TN, COL_AK // PROJ_TN, COL_AV // PROJ_TN, COL_RQ // PROJ_TN
    scale = HEAD_DIM ** -0.5

    @pl.when(j == jq)
    def _():
        proj_ref[...] = (_rope(acc, cos_ref[...], sin_ref[...]) * QK_SCALE_LOG2).astype(BF16)

    @pl.when(j == jk)
    def _():
        proj_ref[...] = _rope(acc, cos_ref[...], sin_ref[...]).astype(BF16)

    @pl.when(j == jv)
    def _():
        proj_ref[...] = acc.astype(BF16)
        ones = jnp.ones((VT_ONES, MOBA_BLOCK), F32)
        for c in range(PROJ_TM // MOBA_BLOCK):
            v_t = acc[c * MOBA_BLOCK:(c + 1) * MOBA_BLOCK, :].T
            rows = []
            for h in range(ATTN_HEADS):
                rows += [v_t[h * HEAD_DIM:(h + 1) * HEAD_DIM], ones]
            vt_ref[c] = jnp.concatenate(rows, axis=0).astype(BF16)

    @pl.when(j == jr)
    def _():
        r = _rope(acc, cos_ref[...], sin_ref[...])
        col = lax.broadcasted_iota(jnp.int32, (1, PROJ_TN), 1)
        r = r * jnp.where(col >= RET_QK_WIDTH, scale, 1.0)
        proj_ref[...] = r.astype(BF16)

    @pl.when((j != jq) & (j != jk) & (j != jv) & (j != jr))
    def _():
        proj_ref[...] = acc.astype(BF16)


def _inproj_src_block(j):
    first_moved = COL_RV // PROJ_TN
    last = IN_PROJ_WIDTH // PROJ_TN - 1
    return jnp.where(j < first_moved, j, jnp.where(j == last, first_moved, j + 1))


def _inproj(x2, norm_w, w_in, layer, cos_t, sin_t, batch, seq):
    n = x2.shape[0]
    nst = seq // PROJ_TM
    nblk = PROJ_TM // MOBA_BLOCK
    return pl.pallas_call(
        _inproj_kernel,
        grid=(n // PROJ_TM, IN_PROJ_WIDTH // PROJ_TN),
        in_specs=[
            pl.BlockSpec((PROJ_TM, D_MODEL), lambda i, j: (i, 0)),
            pl.BlockSpec((1, D_MODEL), lambda i, j: (0, 0)),
            pl.BlockSpec((None, D_MODEL, PROJ_TN), lambda i, j: (layer, 0, _inproj_src_block(j))),
            pl.BlockSpec((PROJ_TM, LANES), lambda i, j: (i % nst, 0)),
            pl.BlockSpec((PROJ_TM, LANES), lambda i, j: (i % nst, 0)),
        ],
        out_specs=[
            pl.BlockSpec((PROJ_TM, PROJ_TN), lambda i, j: (i, j)),
            pl.BlockSpec((None, nblk, ATTN_HEADS * VT_ROWS, MOBA_BLOCK),
                         lambda i, j: (i // nst, i % nst, 0, 0)),
        ],
        out_shape=[
            jax.ShapeDtypeStruct((n, IN_PROJ_WIDTH), BF16),
            jax.ShapeDtypeStruct((batch, seq // MOBA_BLOCK, ATTN_HEADS * VT_ROWS, MOBA_BLOCK), BF16),
        ],
        scratch_shapes=[pltpu.VMEM((PROJ_TM, D_MODEL), BF16)],
        compiler_params=_cparams(("arbitrary", "arbitrary")),
        name="inproj",
    )(x2, norm_w, w_in, cos_t, sin_t)


_NT = (((1,), (1,)), ((), ()))


def _moba_kernel(q_ref, k_ref, vt_ref, o_ref, km_scr, ka_scr, sel_scr, qt_scr, s_scr, m_scr, acc_scr,
                 *, nb):
    nh = MOBA_HEADS_PER_STEP
    blk = MOBA_BLOCK
    tq = 2 * blk
    lane = lax.broadcasted_iota(jnp.int32, (1, LANES), 1)
    crow = lax.broadcasted_iota(jnp.int32, (LANES, 1), 0)

    def group(ref, rows, h):
        g = h // 2
        return ref[rows, g * LANES:(g + 1) * LANES]

    def prep(i, c):
        rows = pl.ds(pl.multiple_of(i * blk, blk), blk)
        for h in range(nh):
            kb, hh = group(k_ref, rows, h), h % 2
            if hh == 0:
                km_scr[h // 2, pl.ds(i, 1), :] = jnp.sum(kb.astype(F32), axis=0, keepdims=True) * (1.0 / blk)
            onehot = jnp.where(lane == HEAD_DIM * (1 - hh) + i, 1.0, 0.0).astype(BF16)
            ka_scr[h, rows, :] = jnp.where((lane // HEAD_DIM) == hh, kb, onehot)
        return c

    lax.fori_loop(0, nb, prep, 0)

    causal = (lax.broadcasted_iota(jnp.int32, (blk, blk), 0)
              <= lax.broadcasted_iota(jnp.int32, (blk, blk), 1))
    blk_id = lax.broadcasted_iota(jnp.int32, (nb, tq), 0)
    q_half = (lax.broadcasted_iota(jnp.int32, (nb, tq), 1) >= blk).astype(jnp.int32)

    def pv(t, h, p):
        rows = slice(h * VT_ROWS, (h + 1) * VT_ROWS)
        pb = p.astype(BF16)
        return (jnp.dot(vt_ref[2 * t, rows, :], pb[0:blk], preferred_element_type=F32)
                + jnp.dot(vt_ref[2 * t + 1, rows, :], pb[blk:tq], preferred_element_type=F32))

    def qtile(jt, c):
        row0 = pl.multiple_of(jt * tq, tq)
        q_ts =[group(q_ref, pl.ds(row0, tq), h).astype(F32).T for h in range(0, nh, 2)]
        for h in range(nh):
            hh = h % 2
            q_t = q_ts[h // 2]
            k_diag = group(k_ref, pl.ds(row0, tq), h)
            q_m = jnp.where((crow // HEAD_DIM) == hh, q_t, 0.0)
            q_mb = q_m.astype(BF16)
            km = jnp.where((lane // HEAD_DIM) == hh, km_scr[h // 2], 0.0)
            km_hi = km.astype(BF16)
            km_lo = (km - km_hi.astype(F32)).astype(BF16)
            gate = (jnp.dot(km_hi, q_mb, preferred_element_type=F32)
                    + jnp.dot(km_lo, q_mb, preferred_element_type=F32))
            past = blk_id < 2 * jt + q_half
            gate = jnp.where(past, gate, -jnp.inf)
            rank = jnp.zeros((nb, tq), jnp.int32)
            for ip in range(nb):
                gi = gate[ip:ip + 1, :]
                beats = (gi > gate) | ((gi == gate) & (blk_id > ip))
                rank = rank + beats.astype(jnp.int32)
            sel = (rank < MOBA_TOPK) & past
            sel_scr[h] = sel.astype(F32)
            bias = jnp.where(sel, 0.0, NEG_INF)
            spare = HEAD_DIM * (1 - hh)
            pieces = [jnp.zeros((spare, tq), F32)] if spare else []
            pieces += [bias, jnp.zeros((LANES - spare - nb, tq), F32)]
            qt_scr[h] = (q_m + jnp.concatenate(pieces, axis=0)).astype(BF16)

            s_top = jnp.dot(k_diag[0:blk], q_mb, preferred_element_type=F32)
            s11 = jnp.where(causal, s_top[:, 0:blk], NEG_INF)
            s12 = jnp.where(sel_scr[h, pl.ds(2 * jt, 1), blk:tq] > 0.0, s_top[:, blk:tq], NEG_INF)
            s22 = jnp.where(causal, jnp.dot(k_diag[blk:tq], q_mb[:, blk:tq], preferred_element_type=F32),
                            NEG_INF)
            m_a = jnp.max(s11, axis=0, keepdims=True)
            m_b = jnp.maximum(jnp.max(s12, axis=0, keepdims=True), jnp.max(s22, axis=0, keepdims=True))
            rows = slice(h * VT_ROWS, (h + 1) * VT_ROWS)
            v_lo, v_hi = vt_ref[2 * jt, rows, :], vt_ref[2 * jt + 1, rows, :]
            acc_a = jnp.dot(v_lo, jnp.exp2(s11 - m_a).astype(BF16), preferred_element_type=F32)
            acc_b = (jnp.dot(v_lo, jnp.exp2(s12 - m_b).astype(BF16), preferred_element_type=F32)
                     + jnp.dot(v_hi, jnp.exp2(s22 - m_b).astype(BF16), preferred_element_type=F32))
            m_scr[2 * h] = jnp.concatenate([m_a, m_b], axis=1)
            m_scr[2 * h + 1] = m_scr[2 * h]
            acc_scr[h] = jnp.concatenate([acc_a, acc_b], axis=1)

        def park(t, h, slot):
            s = jnp.dot(ka_scr[h, pl.ds(pl.multiple_of(t * tq, tq), tq), :], qt_scr[h],
                        preferred_element_type=F32)
            s_scr[2 * h + slot] = s
            m_scr[2 * h + 1] = jnp.maximum(m_scr[2 * h + 1], jnp.max(s, axis=0, keepdims=True))

        def consume(t, h, slot):
            m_old, m_new = m_scr[2 * h], m_scr[2 * h + 1]
            acc_scr[h] = (jnp.exp2(m_old - m_new) * acc_scr[h]
                          + pv(t, h, jnp.exp2(s_scr[2 * h + slot] - m_new)))
            m_scr[2 * h] = m_new

        def step(t, slot):
            for h in range(nh):
                consume(t, h, slot)
                park(t + 1, h, 1 - slot)

        def run(trips, body):
            def wrapped(i, c):
                body(i)
                return c
            lax.fori_loop(0, trips, wrapped, 0)

        steps = jnp.maximum(jt - 1, 0)
        some = jnp.minimum(jt, 1)
        for h in range(nh):
            park(0, h, 0)
        run(steps // 2, lambda u: (step(2 * u, 0), step(2 * u + 1, 1)))
        run(steps % 2, lambda i: step(steps - 1, 0))
        run(some * (steps % 2), lambda i: [consume(jt - 1, h, 1) for h in range(nh)])
        run(some * (1 - steps % 2), lambda i: [consume(jt - 1, h, 0) for h in range(nh)])
        out_t = jnp.concatenate([acc_scr[h, 0:HEAD_DIM, :] / acc_scr[h, HEAD_DIM:HEAD_DIM + 1, :]
                                 for h in range(nh)], axis=0)
        o_ref[pl.ds(row0, tq), :] = out_t.T.astype(BF16)
        return c

    lax.fori_loop(0, nb // 2, qtile, 0)


def _moba(proj, vt, batch, seq):
    nb = seq // MOBA_BLOCK
    assert nb % 8 == 0 and nb <= HEAD_DIM
    nh = MOBA_HEADS_PER_STEP
    width = nh * HEAD_DIM
    return pl.pallas_call(
        functools.partial(_moba_kernel, nb=nb),
        grid=(batch, ATTN_HEADS // nh),
        in_specs=[
            pl.BlockSpec((seq, width), lambda b, p: (b, COL_AQ // width + p)),
            pl.BlockSpec((seq, width), lambda b, p: (b, COL_AK // width + p)),
            pl.BlockSpec((None, nb, nh * VT_ROWS, MOBA_BLOCK), lambda b, p: (b, 0, p, 0)),
        ],
        out_specs=pl.BlockSpec((seq, width), lambda b, p: (b, p)),
        out_shape=jax.ShapeDtypeStruct((batch * seq, ATTN_WIDTH), BF16),
        scratch_shapes=[pltpu.VMEM((nh // 2, nb, LANES), F32), pltpu.VMEM((nh, seq, LANES), BF16),
                        pltpu.VMEM((nh, nb, 2 * MOBA_BLOCK), F32),
                        pltpu.VMEM((nh, LANES, 2 * MOBA_BLOCK), BF16),
                        pltpu.VMEM((2 * nh, 2 * MOBA_BLOCK, 2 * MOBA_BLOCK), F32),
                        pltpu.VMEM((2 * nh, 1, 2 * MOBA_BLOCK), F32),
                        pltpu.VMEM((nh, VT_ROWS, 2 * MOBA_BLOCK), F32)],
        compiler_params=_cparams(("arbitrary", "arbitrary")),
        name="moba",
    )(proj, proj, vt)


def _ret_log_gamma():
    return [math.log1p(-(2.0 ** (-5.0 - h))) for h in range(RET_HEADS)]


def _retention_tables():
    c = RET_TILE
    lg = np.array(_ret_log_gamma(), np.float64)
    n = np.arange(c, dtype=np.float64)
    diff = n[:, None] - n[None, :]
    decay = np.where(diff[None] >= 0, np.exp(np.maximum(diff, 0.0)[None] * lg[:, None, None]), 0.0)
    head_of_lane = np.arange(RET_QK_WIDTH) // HEAD_DIM
    xi = np.exp((n + 1.0)[:, None] * lg[head_of_lane][None, :])
    zeta = np.exp((c - 1.0 - n)[:, None] * lg[head_of_lane][None, :])
    chunk_decay = np.exp(c * lg[head_of_lane])[:, None]
    return (jnp.asarray(decay, F32), jnp.asarray(xi, F32), jnp.asarray(zeta, F32),
            jnp.asarray(np.broadcast_to(chunk_decay, (RET_QK_WIDTH, LANES)), F32))


def _retention_kernel(q_ref, k_ref, v_ref, g_ref, nw_ref, decay_ref, xi_ref, zeta_ref, cd_ref,
                      o_ref, state_scr):
    @pl.when(pl.program_id(1) == 0)
    def _():
        state_scr[...] = jnp.zeros_like(state_scr)

    lane = lax.broadcasted_iota(jnp.int32, (1, LANES), 1)
    srow = lax.broadcasted_iota(jnp.int32, (LANES, 1), 0)
    for pr in range(RET_HEADS // 2):
        cols = slice(pr * LANES, (pr + 1) * LANES)
        q = q_ref[:, cols]
        k = k_ref[:, cols]
        state = state_scr[cols, :]
        state_bf = state.astype(BF16)
        q_xi = (q.astype(F32) * xi_ref[:, cols]).astype(BF16)
        kz_t = (k.astype(F32) * zeta_ref[:, cols]).T.astype(BF16)
        upd = []
        for hh in range(2):
            h = 2 * pr + hh
            hmask = (lane // HEAD_DIM) == hh
            vcols = slice(h * RET_V_DIM, (h + 1) * RET_V_DIM)
            v = v_ref[:, vcols]
            qm = jnp.where(hmask, q, jnp.zeros_like(q))
            scores = lax.dot_general(qm, k, _NT, preferred_element_type=F32) * decay_ref[h]
            o = jnp.dot(scores.astype(BF16), v, preferred_element_type=F32)
            o = o + jnp.dot(jnp.where(hmask, q_xi, jnp.zeros_like(q_xi)), state_bf,
                            preferred_element_type=F32)
            upd.append(jnp.dot(kz_t, v, preferred_element_type=F32))
            ms = jnp.mean(o * o, axis=-1, keepdims=True)
            y = o * lax.rsqrt(ms + RMS_EPS) * nw_ref[:, vcols]
            g = g_ref[:, vcols].astype(F32)
            o_ref[:, vcols] = (g * jax.nn.sigmoid(g) * y).astype(BF16)
        state_scr[cols, :] = state * cd_ref[cols, :] + jnp.where(srow < HEAD_DIM, upd[0], upd[1])


def _retention(proj, ret_norm_w, tables, batch, seq):
    nc = seq // RET_TILE
    decay, xi, zeta, cd = tables
    row = lambda b, c: b * nc + c
    const2 = lambda b, c: (0, 0)
    return pl.pallas_call(
        _retention_kernel,
        grid=(batch, nc),
        in_specs=[
            pl.BlockSpec((RET_TILE, RET_QK_WIDTH), lambda b, c: (row(b, c), COL_RQ // RET_QK_WIDTH)),
            pl.BlockSpec((RET_TILE, RET_QK_WIDTH), lambda b, c: (row(b, c), COL_RK // RET_QK_WIDTH)),
            pl.BlockSpec((RET_TILE, RET_V_WIDTH), lambda b, c: (row(b, c), COL_RV // RET_V_WIDTH)),
            pl.BlockSpec((RET_TILE, RET_V_WIDTH), lambda b, c: (row(b, c), COL_RG // RET_V_WIDTH)),
            pl.BlockSpec((1, RET_V_WIDTH), const2),
            pl.BlockSpec((RET_HEADS, RET_TILE, RET_TILE), lambda b, c: (0, 0, 0)),
            pl.BlockSpec((RET_TILE, RET_QK_WIDTH), const2),
            pl.BlockSpec((RET_TILE, RET_QK_WIDTH), const2),
            pl.BlockSpec((RET_QK_WIDTH, LANES), const2),
        ],
        out_specs=pl.BlockSpec((RET_TILE, RET_V_WIDTH), lambda b, c: (row(b, c), 0)),
        out_shape=jax.ShapeDtypeStruct((batch * seq, RET_V_WIDTH), BF16),
        scratch_shapes=[pltpu.VMEM((RET_QK_WIDTH, RET_V_DIM), F32)],
        compiler_params=_cparams(("arbitrary", "arbitrary")),
        name="retention",
    )(proj, proj, proj, proj, ret_norm_w, decay, xi, zeta, cd)


CONV_HALO = 8


def _merge_kernel(x_ref, ya_ref, cb_ref, cc_ref, ch_ref, yr_ref, ga_ref, gc_ref, gr_ref, cw_ref,
                  cbias_ref, wa_ref, wc_ref, wr_ref, wo_ref, o_ref, u_scr, *, tiles_per_seq):
    i = pl.program_id(0)
    tm = x_ref.shape[0]

    @pl.when(i % tiles_per_seq == 0)
    def _():
        u_scr[0:CONV_HALO, :] = jnp.zeros((CONV_HALO, CONV_WIDTH), F32)

    u_scr[CONV_HALO:CONV_HALO + tm, :] = cc_ref[...].astype(F32) * ch_ref[...].astype(F32)
    conv = (cw_ref[2:3, :] * u_scr[CONV_HALO:CONV_HALO + tm, :]
            + cw_ref[1:2, :] * u_scr[CONV_HALO - 1:CONV_HALO - 1 + tm, :]
            + cw_ref[0:1, :] * u_scr[CONV_HALO - 2:CONV_HALO - 2 + tm, :]
            + cbias_ref[...])
    y_conv = (cb_ref[...].astype(F32) * conv).astype(BF16)
    u_scr[0:CONV_HALO, :] = u_scr[tm:tm + CONV_HALO, :]

    def gate(g_ref):
        return jax.nn.sigmoid(g_ref[...].astype(F32))

    merged = gate(ga_ref) * jnp.dot(ya_ref[...], wa_ref[...], preferred_element_type=F32)
    merged = merged + gate(gc_ref) * jnp.dot(y_conv, wc_ref[...], preferred_element_type=F32)
    merged = merged + gate(gr_ref) * jnp.dot(yr_ref[...], wr_ref[...], preferred_element_type=F32)
    o_ref[...] = x_ref[...] + jnp.dot(merged.astype(BF16), wo_ref[...], preferred_element_type=F32)


def _merge(x2, y_attn, proj, y_ret, conv_w, conv_b, wa, wc, wr, wo, seq):
    n = x2.shape[0]
    tm = MERGE_TM
    const = lambda i: (0, 0)
    wide = lambda c: pl.BlockSpec((tm, CONV_WIDTH), lambda i: (i, c // CONV_WIDTH))
    gate_spec = lambda b: pl.BlockSpec((tm, D_MODEL), lambda i: (i, COL_GATES // D_MODEL + b))
    return pl.pallas_call(
        functools.partial(_merge_kernel, tiles_per_seq=seq // tm),
        grid=(n // tm,),
        in_specs=[
            pl.BlockSpec((tm, D_MODEL), lambda i: (i, 0)),
            pl.BlockSpec((tm, ATTN_WIDTH), lambda i: (i, 0)),
            wide(COL_CB), wide(COL_CC), wide(COL_CH),
            pl.BlockSpec((tm, RET_V_WIDTH), lambda i: (i, 0)),
            gate_spec(0), gate_spec(1), gate_spec(2),
            pl.BlockSpec((CONV_KERNEL, CONV_WIDTH), const),
            pl.BlockSpec((1, CONV_WIDTH), const),
            pl.BlockSpec((ATTN_WIDTH, D_MODEL), const),
            pl.BlockSpec((CONV_WIDTH, D_MODEL), const),
            pl.BlockSpec((RET_V_WIDTH, D_MODEL), const),
            pl.BlockSpec((D_MODEL, D_MODEL), const),
        ],
        out_specs=pl.BlockSpec((tm, D_MODEL), lambda i: (i, 0)),
        out_shape=jax.ShapeDtypeStruct((n, D_MODEL), F32),
        scratch_shapes=[pltpu.VMEM((tm + CONV_HALO, CONV_WIDTH), F32)],
        compiler_params=_cparams(("arbitrary",)),
        name="merge",
    )(x2, y_attn, proj, proj, proj, y_ret, proj, proj, proj, conv_w, conv_b, wa, wc, wr, wo)


def _swiglu_step(h, wg_ref, wu_ref, wd_ref, acc_scr, c):
    g = jnp.dot(h, wg_ref[...].astype(BF16), preferred_element_type=F32)
    u = jnp.dot(h, wu_ref[...].astype(BF16), preferred_element_type=F32)
    a = (g * jax.nn.sigmoid(g) * u).astype(BF16)
    part = jnp.dot(a, wd_ref[...].astype(BF16), preferred_element_type=F32)

    @pl.when(c == 0)
    def _():
        acc_scr[...] = part

    @pl.when(c != 0)
    def _():
        acc_scr[...] += part


def _dense_ffn_kernel(x_ref, nw_ref, wg_ref, wu_ref, wd_ref, o_ref, h_scr, acc_scr):
    c = pl.program_id(1)

    @pl.when(c == 0)
    def _():
        x = x_ref[...]
        ms = jnp.mean(x * x, axis=-1, keepdims=True)
        h_scr[...] = (x * lax.rsqrt(ms + RMS_EPS) * nw_ref[...]).astype(BF16)

    _swiglu_step(h_scr[...], wg_ref, wu_ref, wd_ref, acc_scr, c)

    @pl.when(c == pl.num_programs(1) - 1)
    def _():
        o_ref[...] = x_ref[...] + acc_scr[...]


def _dense_ffn(x2, norm_w, wg, wu, wd, li):
    n = x2.shape[0]
    tm, tf = FFN_TM, FFN_TF
    return pl.pallas_call(
        _dense_ffn_kernel,
        grid=(n // tm, D_FF // tf),
        in_specs=[
            pl.BlockSpec((tm, D_MODEL), lambda i, c: (i, 0)),
            pl.BlockSpec((1, D_MODEL), lambda i, c: (0, 0)),
            pl.BlockSpec((None, D_MODEL, tf), lambda i, c: (li, 0, c)),
            pl.BlockSpec((None, D_MODEL, tf), lambda i, c: (li, 0, c)),
            pl.BlockSpec((None, tf, D_MODEL), lambda i, c: (li, c, 0)),
        ],
        out_specs=pl.BlockSpec((tm, D_MODEL), lambda i, c: (i, 0)),
        out_shape=jax.ShapeDtypeStruct((n, D_MODEL), F32),
        scratch_shapes=[pltpu.VMEM((tm, D_MODEL), BF16), pltpu.VMEM((tm, D_MODEL), F32)],
        compiler_params=_cparams(("arbitrary", "arbitrary")),
        name="dense_ffn",
    )(x2, norm_w, wg, wu, wd)


SLAB = D_MODEL // LANES


def _pack_rows(v, o_ref):
    rows = v.shape[0]
    for s in range(SLAB):
        o_ref[pl.ds(s, rows, stride=SLAB), :] = v[:, s * LANES:(s + 1) * LANES]


def _unpack_rows(x_ref, rows):
    return jnp.concatenate([x_ref[pl.ds(s, rows, stride=SLAB), :] for s in range(SLAB)], axis=1)


def _route_kernel(x_ref, nw_ref, wr_ref, h_ref, idx_ref, wgt_ref):
    x = x_ref[...]
    ms = jnp.mean(x * x, axis=-1, keepdims=True)
    h = x * lax.rsqrt(ms + RMS_EPS) * nw_ref[...]
    _pack_rows(h, h_ref)
    w = wr_ref[...]
    h_hi, w_hi = h.astype(BF16), w.astype(BF16)
    h_lo, w_lo = (h - h_hi.astype(F32)).astype(BF16), (w - w_hi.astype(F32)).astype(BF16)
    logits = (jnp.dot(h_hi, w_hi, preferred_element_type=F32)
              + jnp.dot(h_lo, w_hi, preferred_element_type=F32)
              + jnp.dot(h_hi, w_lo, preferred_element_type=F32))
    eid = lax.broadcasted_iota(jnp.int32, logits.shape, 1).astype(F32)
    logits = jnp.where(eid < N_EXPERTS, logits, -jnp.inf)
    m1 = jnp.max(logits, axis=-1, keepdims=True)
    i1 = jnp.min(jnp.where(logits == m1, eid, float(LANES)), axis=-1, keepdims=True)
    rest = jnp.where(eid == i1, -jnp.inf, logits)
    m2 = jnp.max(rest, axis=-1, keepdims=True)
    i2 = jnp.min(jnp.where(rest == m2, eid, float(LANES)), axis=-1, keepdims=True)
    e2 = jnp.exp(m2 - m1)
    denom = 1.0 + e2
    idx_ref[...] = jnp.where(eid == 0.0, i1, jnp.where(eid == 1.0, i2, 0.0)).astype(jnp.int32)
    wgt_ref[...] = jnp.where(eid == 0.0, 1.0 / denom, jnp.where(eid == 1.0, e2 / denom, 0.0))


def _route(x2, norm_w, w_router):
    n = x2.shape[0]
    tm = ROUTE_TM
    return pl.pallas_call(
        _route_kernel,
        grid=(n // tm,),
        in_specs=[
            pl.BlockSpec((tm, D_MODEL), lambda i: (i, 0)),
            pl.BlockSpec((1, D_MODEL), lambda i: (0, 0)),
            pl.BlockSpec((D_MODEL, LANES), lambda i: (0, 0)),
        ],
        out_specs=[
            pl.BlockSpec((tm * SLAB, LANES), lambda i: (i, 0)),
            pl.BlockSpec((tm, LANES), lambda i: (i, 0)),
            pl.BlockSpec((tm, LANES), lambda i: (i, 0)),
        ],
        out_shape=[
            jax.ShapeDtypeStruct((n * SLAB, LANES), F32),
            jax.ShapeDtypeStruct((n, LANES), jnp.int32),
            jax.ShapeDtypeStruct((n, LANES), F32),
        ],
        compiler_params=_cparams(("arbitrary",)),
        name="route",
    )(x2, norm_w, jnp.pad(w_router, ((0, 0), (0, LANES - N_EXPERTS))))


MOE_STEPS = D_FF // FFN_TF
MOE_ROWS_PER_STEP = -(-FFN_TM // MOE_STEPS)
MOE_DMA_ROWS = MOE_ROWS_PER_STEP * MOE_STEPS
MOE_BUF_ROWS = -(-MOE_DMA_ROWS // 8) * 8
MOE_SPARE_TILES = 2


def _moe_ffn_kernel(te_ref, nv_ref, gid0_ref, gidn_ref, sid_ref, h3_ref, wg_ref, wu_ref, wd_ref,
                    y_init_ref, y_ref, gbuf, obuf, h_scr, acc_scr, gsem, ssem):
    t = pl.program_id(0)
    c = pl.program_id(1)
    nv = nv_ref[0]
    slot = t % 2
    other = 1 - slot
    first = c == 0

    def slab(ref, start):
        return ref.at[pl.ds(pl.multiple_of(start, SLAB), SLAB)]

    def gather(ids_ref, row, s):
        return pltpu.make_async_copy(slab(h3_ref, ids_ref[0, 0, row]), slab(gbuf.at[s], row * SLAB),
                                     gsem.at[s])

    def scatter(row, s):
        return pltpu.make_async_copy(slab(obuf.at[s], row * SLAB), slab(y_ref, sid_ref[0, 0, row]),
                                     ssem.at[s])

    def tile_gather(s):
        return pltpu.make_async_copy(h3_ref.at[pl.ds(0, MOE_DMA_ROWS * SLAB)],
                                     gbuf.at[s, pl.ds(0, MOE_DMA_ROWS * SLAB)], gsem.at[s])

    def tile_scatter(s):
        return pltpu.make_async_copy(obuf.at[s, pl.ds(0, MOE_DMA_ROWS * SLAB)],
                                     y_ref.at[pl.ds(0, MOE_DMA_ROWS * SLAB)], ssem.at[s])

    @pl.when(first & (t == 0))
    def _():
        obuf[...] = jnp.zeros_like(obuf)

        def body(r, carry):
            gather(gid0_ref, r, 0).start()
            return carry

        lax.fori_loop(0, MOE_DMA_ROWS, body, 0, unroll=MOE_STEPS)

    @pl.when(first & (t <= nv))
    def _():
        tile_gather(slot).wait()

    @pl.when(first & (t >= 1) & (t <= nv + 1))
    def _():
        tile_scatter(slot).wait()

    @pl.when(first & (t < nv))
    def _():
        h_scr[...] = _unpack_rows(gbuf.at[slot], FFN_TM).astype(BF16)

    @pl.when(t < nv)
    def _():
        for u in range(MOE_ROWS_PER_STEP):
            row = c * MOE_ROWS_PER_STEP + u
            gather(gidn_ref, row, other).start(priority=1)
            scatter(row, other).start(priority=1)
        _swiglu_step(h_scr[...], wg_ref, wu_ref, wd_ref, acc_scr, c)

    @pl.when(t == nv)
    def _():
        def body(u, carry):
            scatter(c * MOE_ROWS_PER_STEP + u, other).start()
            return carry

        lax.fori_loop(0, MOE_ROWS_PER_STEP, body, 0, unroll=MOE_STEPS)

    @pl.when((t < nv) & (c == pl.num_programs(1) - 1))
    def _():
        _pack_rows(acc_scr[...], obuf.at[slot])


def _moe_ffn(tile_expert, n_valid, gid, sid, h3, wg, wu, wd, li, y_rows):
    nt = gid.shape[0]
    tm, tf = FFN_TM, FFN_TF
    ids_block = (1, 1, MOE_BUF_ROWS)
    smem = pltpu.SMEM
    wspec = lambda shape, imap: pl.BlockSpec((None, None) + shape, imap)
    expert = lambda t, te, nv: te[jnp.minimum(t, nv[0] - 1)]
    chunk = lambda t, c, nv: jnp.where(t < nv[0], c, MOE_STEPS - 1)
    grid_spec = pltpu.PrefetchScalarGridSpec(
        num_scalar_prefetch=2,
        grid=(nt, MOE_STEPS),
        in_specs=[
            pl.BlockSpec(ids_block, lambda t, c, te, nv: (0, 0, 0), memory_space=smem),
            pl.BlockSpec(ids_block, lambda t, c, te, nv: (jnp.minimum(t + 1, nt - 1), 0, 0),
                         memory_space=smem),
            pl.BlockSpec(ids_block, lambda t, c, te, nv: (t, 0, 0), memory_space=smem),
            pl.BlockSpec(memory_space=pl.ANY),
            wspec((D_MODEL, tf), lambda t, c, te, nv: (li, expert(t, te, nv), 0, chunk(t, c, nv))),
            wspec((D_MODEL, tf), lambda t, c, te, nv: (li, expert(t, te, nv), 0, chunk(t, c, nv))),
            wspec((tf, D_MODEL), lambda t, c, te, nv: (li, expert(t, te, nv), chunk(t, c, nv), 0)),
            pl.BlockSpec(memory_space=pl.ANY),
        ],
        out_specs=pl.BlockSpec(memory_space=pl.ANY),
        scratch_shapes=[
            pltpu.VMEM((2, MOE_BUF_ROWS * SLAB, LANES), F32),
            pltpu.VMEM((2, MOE_BUF_ROWS * SLAB, LANES), F32),
            pltpu.VMEM((tm, D_MODEL), BF16),
            pltpu.VMEM((tm, D_MODEL), F32),
            pltpu.SemaphoreType.DMA((2,)),
            pltpu.SemaphoreType.DMA((2,)),
        ],
    )
    return pl.pallas_call(
        _moe_ffn_kernel,
        grid_spec=grid_spec,
        out_shape=jax.ShapeDtypeStruct((y_rows * SLAB, LANES), F32),
        input_output_aliases={9: 0},
        compiler_params=_cparams(("arbitrary", "arbitrary")),
        name="moe_ffn",
    )(tile_expert, n_valid, gid * SLAB, gid * SLAB, sid * SLAB, h3, wg, wu, wd,
      jnp.zeros((y_rows * SLAB, LANES), F32))


def _combine_kernel(x_ref, y0_ref, y1_ref, w_ref, nw_ref, o_ref, *, final_norm):
    w = w_ref[...]
    rows = x_ref.shape[0]
    x = x_ref[...] + w[:, 0:1] * _unpack_rows(y0_ref, rows) + w[:, 1:2] * _unpack_rows(y1_ref, rows)
    if final_norm:
        ms = jnp.mean(x * x, axis=-1, keepdims=True)
        x = x * lax.rsqrt(ms + RMS_EPS) * nw_ref[...]
    o_ref[...] = x


def _combine(x2, y_pairs, wgt, norm_w, final_norm):
    n = x2.shape[0]
    tm = MERGE_TM
    nt = n // tm
    return pl.pallas_call(
        functools.partial(_combine_kernel, final_norm=final_norm),
        grid=(nt,),
        in_specs=[
            pl.BlockSpec((tm, D_MODEL), lambda i: (i, 0)),
            pl.BlockSpec((tm * SLAB, LANES), lambda i: (i, 0)),
            pl.BlockSpec((tm * SLAB, LANES), lambda i: (i + nt, 0)),
            pl.BlockSpec((tm, LANES), lambda i: (i, 0)),
            pl.BlockSpec((1, D_MODEL), lambda i: (0, 0)),
        ],
        out_specs=pl.BlockSpec((tm, D_MODEL), lambda i: (i, 0)),
        out_shape=jax.ShapeDtypeStruct((n, D_MODEL), F32),
        compiler_params=_cparams(("arbitrary",)),
        name="combine",
    )(x2, y_pairs, y_pairs, wgt, norm_w)


def _moe_layout(idx):
    n = idx.shape[0]
    tm = FFN_TM
    flat = idx.T.reshape(-1)
    onehot = (flat[:, None] == jnp.arange(N_EXPERTS, dtype=jnp.int32)[None, :]).astype(jnp.int32)
    cum = jnp.cumsum(onehot, axis=0)
    counts = cum[-1]
    rank = jnp.sum(cum * onehot, axis=1) - 1
    padded = ((counts + tm - 1) // tm) * tm
    ends = jnp.cumsum(padded)
    starts = ends - padded
    dest = starts[flat] + rank
    pairs = TOP_K * n
    nt = pairs // tm + N_EXPERTS + MOE_SPARE_TILES
    p = nt * tm
    tile_start = jnp.arange(nt, dtype=jnp.int32) * tm
    tile_expert = jnp.minimum(jnp.sum((ends[None, :] <= tile_start[:, None]).astype(jnp.int32), axis=1),
                              N_EXPERTS - 1)
    n_valid = (ends[-1] // tm).astype(jnp.int32).reshape(1)
    real_before = jnp.cumsum(counts)[tile_expert]
    pos = jnp.arange(p, dtype=jnp.int32).reshape(nt, tm)
    pad_row = (pairs + pos - real_before[:, None]).reshape(p)
    row_pair = pad_row.at[dest].set(jnp.arange(pairs, dtype=jnp.int32), unique_indices=True,
                                    mode="promise_in_bounds")
    row_token = jnp.where(row_pair < pairs, row_pair % n, 0)
    tail = MOE_BUF_ROWS - tm
    gid = jnp.pad(row_token.reshape(nt, tm), ((0, 0), (0, tail))).reshape(nt, 1, MOE_BUF_ROWS)
    first_spare = p + jnp.arange(MOE_BUF_ROWS, dtype=jnp.int32)[None, :]
    tails = p + MOE_BUF_ROWS + jnp.arange(nt * tail, dtype=jnp.int32).reshape(nt, tail)
    sid = jnp.concatenate([row_pair.reshape(nt, tm), tails], axis=1)
    sid = jnp.concatenate([first_spare, sid], axis=0).reshape(nt + 1, 1, MOE_BUF_ROWS)
    y_rows = p + MOE_BUF_ROWS + nt * tail
    return gid, sid, tile_expert, n_valid, y_rows


def _moe(x2, norm_w, w_router, wg, wu, wd, li, final_norm_w, final_norm):
    h3, idx, wgt = _route(x2, norm_w, w_router)
    gid, sid, tile_expert, n_valid, y_rows = _moe_layout(idx[:, :TOP_K])
    y = _moe_ffn(tile_expert, n_valid, gid, sid, h3, wg, wu, wd, li, y_rows)
    return _combine(x2, y, wgt, final_norm_w, final_norm)


def _rope_tables(seq):
    inv_freq = 1.0 / (ROPE_THETA ** (jnp.arange(0, HEAD_DIM, 2, dtype=F32) / HEAD_DIM))
    ang = jnp.arange(seq, dtype=F32)[:, None] * inv_freq[None, :]
    cos, sin = jnp.cos(ang), jnp.sin(ang)
    cos_t = jnp.tile(cos, (1, LANES // (HEAD_DIM // 2)))
    sin_t = jnp.tile(jnp.concatenate([-sin, sin], axis=1), (1, LANES // HEAD_DIM))
    return cos_t, sin_t


def kernel(x, mix_norm_w, w_in, conv_w, conv_b, ret_norm_w, w_br_attn, w_br_conv, w_br_ret, w_out,
           ffn_norm_w, dense_w_gate, dense_w_up, dense_w_down, moe_router, moe_w_gate, moe_w_up,
           moe_w_down, final_norm_w):
    batch, seq, d = x.shape
    depth = w_in.shape[0]
    assert d == D_MODEL and seq % PROJ_TM == 0 and depth % 2 == 0
    n = batch * seq
    cos_t, sin_t = _rope_tables(seq)
    ret_tables = _retention_tables()
    x2 = x.reshape(n, d)
    for layer in range(depth):
        proj, vt = _inproj(x2, mix_norm_w[layer].reshape(1, d), w_in, layer, cos_t, sin_t, batch, seq)
        y_attn = _moba(proj, vt, batch, seq)
        y_ret = _retention(proj, ret_norm_w[layer].reshape(1, RET_V_WIDTH), ret_tables, batch, seq)
        x2 = _merge(x2, y_attn, proj, y_ret, conv_w[layer], conv_b[layer].reshape(1, CONV_WIDTH),
                    w_br_attn[layer].astype(BF16), w_br_conv[layer].astype(BF16),
                    w_br_ret[layer].astype(BF16), w_out[layer].astype(BF16), seq)
        i = layer // 2
        nw = ffn_norm_w[layer].reshape(1, d)
        if layer % 2 == 0:
            x2 = _dense_ffn(x2, nw, dense_w_gate, dense_w_up, dense_w_down, i)
        else:
            last = layer == depth - 1
            x2 = _moe(x2, nw, moe_router[i], moe_w_gate, moe_w_up, moe_w_down, i,
                      final_norm_w.reshape(1, d), last)
    return x2.reshape(batch, seq, d)
```

```python
import functools
import math

import jax
import jax.numpy as jnp
import numpy as np
from jax import lax
from jax.experimental import pallas as pl
from jax.experimental.pallas import tpu as pltpu

F32 = jnp.float32
BF16 = jnp.bfloat16

D_MODEL = 1024
HEAD_DIM = 64
ROPE_THETA = 10000.0
RMS_EPS = 1e-6

ATTN_HEADS = 8
ATTN_WIDTH = ATTN_HEADS * HEAD_DIM
MOBA_BLOCK = 256
MOBA_TOPK = 3
NEG_INF = -1e30
MOBA_HEADS_PER_STEP = 4
VT_ONES = 16
VT_ROWS = HEAD_DIM + VT_ONES
QK_SCALE_LOG2 = HEAD_DIM ** -0.5 * math.log2(math.e)

CONV_WIDTH = 512
CONV_KERNEL = 3

RET_HEADS = 4
RET_QK_WIDTH = RET_HEADS * HEAD_DIM
RET_V_DIM = 2 * HEAD_DIM
RET_V_WIDTH = RET_HEADS * RET_V_DIM
RET_TILE = 256

D_FF = 3584
N_EXPERTS = 8
TOP_K = 2

LANES = 128
IN_PROJ_WIDTH = 7680
COL_AQ, COL_AK, COL_AV = 0, 512, 1024
COL_CB, COL_CC, COL_CH = 1536, 2048, 2560
COL_RV, COL_RG = 3072, 3584
COL_GATES = 4096
COL_RQ, COL_RK = 7168, 7424

PROJ_TM = 2048
PROJ_TN = 512
MERGE_TM = 512
FFN_TM = 1024
FFN_TF = 512
ROUTE_TM = 1024
VMEM_LIMIT = 56 * 1024 * 1024


def _cparams(sem):
    return pltpu.CompilerParams(dimension_semantics=sem, vmem_limit_bytes=VMEM_LIMIT)


def _rope(acc, cos, sin_signed):
    lane = lax.broadcasted_iota(jnp.int32, (1, LANES), 1)
    first_half = (lane % HEAD_DIM) < (HEAD_DIM // 2)
    outs = []
    for g in range(acc.shape[1] // LANES):
        blk = acc[:, g * LANES:(g + 1) * LANES]
        packed = blk.astype(BF16)
        partner = jnp.where(first_half,
                            pltpu.roll(packed, LANES - HEAD_DIM // 2, 1),
                            pltpu.roll(packed, HEAD_DIM // 2, 1)).astype(F32)
        outs.append(blk * cos + partner * sin_signed)
    return jnp.concatenate(outs, axis=1)


def _inproj_kernel(x_ref, nw_ref, w_ref, cos_ref, sin_ref, proj_ref, vt_ref, h_scr):
    j = pl.program_id(1)

    @pl.when(j == 0)
    def _():
        x = x_ref[...]
        ms = jnp.mean(x * x, axis=-1, keepdims=True)
        h_scr[...] = (x * lax.rsqrt(ms + RMS_EPS) * nw_ref[...]).astype(BF16)

    acc = jnp.dot(h_scr[...], w_ref[...].astype(BF16), preferred_element_type=F32)
    jq, jk, jv, jr = COL_AQ // PROJ_TN, COL_AK // PROJ_TN, COL_AV // PROJ_TN, COL_RQ // PROJ_TN
    scale = HEAD_DIM ** -0.5

    @pl.when(j == jq)
    def _():
        proj_ref[...] = (_rope(acc, cos_ref[...], sin_ref[...]) * QK_SCALE_LOG2).astype(BF16)

    @pl.when(j == jk)
    def _():
        proj_ref[...] = _rope(acc, cos_ref[...], sin_ref[...]).astype(BF16)

    @pl.when(j == jv)
    def _():
        proj_ref[...] = acc.astype(BF16)
        ones = jnp.ones((VT_ONES, MOBA_BLOCK), F32)
        for c in range(PROJ_TM // MOBA_BLOCK):
            v_t = acc[c * MOBA_BLOCK:(c + 1) * MOBA_BLOCK, :].T
            rows = []
            for h in range(ATTN_HEADS):
                rows += [v_t[h * HEAD_DIM:(h + 1) * HEAD_DIM], ones]
            vt_ref[c] = jnp.concatenate(rows, axis=0).astype(BF16)

    @pl.when(j == jr)
    def _():
        r = _rope(acc, cos_ref[...], sin_ref[...])
        col = lax.broadcasted_iota(jnp.int32, (1, PROJ_TN), 1)
        r = r * jnp.where(col >= RET_QK_WIDTH, scale, 1.0)
        proj_ref[...] = r.astype(BF16)

    @pl.when((j != jq) & (j != jk) & (j != jv) & (j != jr))
    def _():
        proj_ref[...] = acc.astype(BF16)


def _inproj_src_block(j):
    first_moved = COL_RV // PROJ_TN
    last = IN_PROJ_WIDTH // PROJ_TN - 1
    return jnp.where(j < first_moved, j, jnp.where(j == last, first_moved, j + 1))


def _inproj(x2, norm_w, w_in, layer, cos_t, sin_t, batch, seq):
    n = x2.shape[0]
    nst = seq // PROJ_TM
    nblk = PROJ_TM // MOBA_BLOCK
    return pl.pallas_call(
        _inproj_kernel,
        grid=(n // PROJ_TM, IN_PROJ_WIDTH // PROJ_TN),
        in_specs=[
            pl.BlockSpec((PROJ_TM, D_MODEL), lambda i, j: (i, 0)),
            pl.BlockSpec((1, D_MODEL), lambda i, j: (0, 0)),
            pl.BlockSpec((None, D_MODEL, PROJ_TN), lambda i, j: (layer, 0, _inproj_src_block(j))),
            pl.BlockSpec((PROJ_TM, LANES), lambda i, j: (i % nst, 0)),
            pl.BlockSpec((PROJ_TM, LANES), lambda i, j: (i % nst, 0)),
        ],
        out_specs=[
            pl.BlockSpec((PROJ_TM, PROJ_TN), lambda i, j: (i, j)),
            pl.BlockSpec((None, nblk, ATTN_HEADS * VT_ROWS, MOBA_BLOCK),
                         lambda i, j: (i // nst, i % nst, 0, 0)),
        ],
        out_shape=[
            jax.ShapeDtypeStruct((n, IN_PROJ_WIDTH), BF16),
            jax.ShapeDtypeStruct((batch, seq // MOBA_BLOCK, ATTN_HEADS * VT_ROWS, MOBA_BLOCK), BF16),
        ],
        scratch_shapes=[pltpu.VMEM((PROJ_TM, D_MODEL), BF16)],
        compiler_params=_cparams(("arbitrary", "arbitrary")),
        name="inproj",
    )(x2, norm_w, w_in, cos_t, sin_t)


_NT = (((1,), (1,)), ((), ()))


def _moba_kernel(q_ref, k_ref, vt_ref, o_ref, km_scr, ka_scr, sel_scr, qt_scr, s_scr, m_scr, acc_scr,
                 *, nb):
    nh = MOBA_HEADS_PER_STEP
    blk = MOBA_BLOCK
    tq = 2 * blk
    lane = lax.broadcasted_iota(jnp.int32, (1, LANES), 1)
    crow = lax.broadcasted_iota(jnp.int32, (LANES, 1), 0)

    def group(ref, rows, h):
        g = h // 2
        return ref[rows, g * LANES:(g + 1) * LANES]

    def prep(i, c):
        rows = pl.ds(pl.multiple_of(i * blk, blk), blk)
        for h in range(nh):
            kb, hh = group(k_ref, rows, h), h % 2
            if hh == 0:
                km_scr[h // 2, pl.ds(i, 1), :] = jnp.sum(kb.astype(F32), axis=0, keepdims=True) * (1.0 / blk)
            onehot = jnp.where(lane == HEAD_DIM * (1 - hh) + i, 1.0, 0.0).astype(BF16)
            ka_scr[h, rows, :] = jnp.where((lane // HEAD_DIM) == hh, kb, onehot)
        return c

    lax.fori_loop(0, nb, prep, 0)

    causal = (lax.broadcasted_iota(jnp.int32, (blk, blk), 0)
              <= lax.broadcasted_iota(jnp.int32, (blk, blk), 1))
    blk_id = lax.broadcasted_iota(jnp.int32, (nb, tq), 0)
    q_half = (lax.broadcasted_iota(jnp.int32, (nb, tq), 1) >= blk).astype(jnp.int32)

    def pv(t, h, p):
        rows = slice(h * VT_ROWS, (h + 1) * VT_ROWS)
        pb = p.astype(BF16)
        return (jnp.dot(vt_ref[2 * t, rows, :], pb[0:blk], preferred_element_type=F32)
                + jnp.dot(vt_ref[2 * t + 1, rows, :], pb[blk:tq], preferred_element_type=F32))

    def qtile(jt, c):
        row0 = pl.multiple_of(jt * tq, tq)
        q_ts =[group(q_ref, pl.ds(row0, tq), h).astype(F32).T for h in range(0, nh, 2)]
        for h in range(nh):
            hh = h % 2
            q_t = q_ts[h // 2]
            k_diag = group(k_ref, pl.ds(row0, tq), h)
            q_m = jnp.where((crow // HEAD_DIM) == hh, q_t, 0.0)
            q_mb = q_m.astype(BF16)
            km = jnp.where((lane // HEAD_DIM) == hh, km_scr[h // 2], 0.0)
            km_hi = km.astype(BF16)
            km_lo = (km - km_hi.astype(F32)).astype(BF16)
            gate = (jnp.dot(km_hi, q_mb, preferred_element_type=F32)
                    + jnp.dot(km_lo, q_mb, preferred_element_type=F32))
            past = blk_id < 2 * jt + q_half
            gate = jnp.where(past, gate, -jnp.inf)
            rank = jnp.zeros((nb, tq), jnp.int32)
            for ip in range(nb):
                gi = gate[ip:ip + 1, :]
                beats = (gi > gate) | ((gi == gate) & (blk_id > ip))
                rank = rank + beats.astype(jnp.int32)
            sel = (rank < MOBA_TOPK) & past
            sel_scr[h] = sel.astype(F32)
            bias = jnp.where(sel, 0.0, NEG_INF)
            spare = HEAD_DIM * (1 - hh)
            pieces = [jnp.zeros((spare, tq), F32)] if spare else []
            pieces += [bias, jnp.zeros((LANES - spare - nb, tq), F32)]
            qt_scr[h] = (q_m + jnp.concatenate(pieces, axis=0)).astype(BF16)

            s_top = jnp.dot(k_diag[0:blk], q_mb, preferred_element_type=F32)
            s11 = jnp.where(causal, s_top[:, 0:blk], NEG_INF)
            s12 = jnp.where(sel_scr[h, pl.ds(2 * jt, 1), blk:tq] > 0.0, s_top[:, blk:tq], NEG_INF)
            s22 = jnp.where(causal, jnp.dot(k_diag[blk:tq], q_mb[:, blk:tq], preferred_element_type=F32),
                            NEG_INF)
            m_a = jnp.max(s11, axis=0, keepdims=True)
            m_b = jnp.maximum(jnp.max(s12, axis=0, keepdims=True), jnp.max(s22, axis=0, keepdims=True))
            s_scr[2 * h, 0:blk, 0:blk] = s11
            s_scr[2 * h, 0:blk, blk:tq] = s12
            s_scr[2 * h, blk:tq, 0:blk] = jnp.full((blk, blk), NEG_INF, F32)
            s_scr[2 * h, blk:tq, blk:tq] = s22
            m_scr[2 * h] = jnp.concatenate([m_a, m_b], axis=1)
            m_scr[2 * h + 1] = m_scr[2 * h]
            acc_scr[h] = jnp.zeros((VT_ROWS, tq), F32)

        def park(t, h, slot):
            s = jnp.dot(ka_scr[h, pl.ds(pl.multiple_of(t * tq, tq), tq), :], qt_scr[h],
                        preferred_element_type=F32)
            s_scr[2 * h + slot] = s
            m_scr[2 * h + 1] = jnp.maximum(m_scr[2 * h + 1], jnp.max(s, axis=0, keepdims=True))

        def consume(k, h, slot):
            m_old, m_new = m_scr[2 * h], m_scr[2 * h + 1]
            acc_scr[h] = (jnp.exp2(m_old - m_new) * acc_scr[h]
                          + pv(jnp.where(k == 0, jt, k - 1), h, jnp.exp2(s_scr[2 * h + slot] - m_new)))
            m_scr[2 * h] = m_new

        def step(k, slot):
            for h in range(nh):
                consume(k, h, slot)
                park(k, h, 1 - slot)

        def run(trips, body):
            def wrapped(i, c):
                body(i)
                return c
            lax.fori_loop(0, trips, wrapped, 0)

        run(jt // 2, lambda u: (step(2 * u, 0), step(2 * u + 1, 1)))
        run(jt % 2, lambda i: step(jt - 1, 0))
        run(jt % 2, lambda i: [consume(jt, h, 1) for h in range(nh)])
        run(1 - jt % 2, lambda i: [consume(jt, h, 0) for h in range(nh)])
        out_t = jnp.concatenate([acc_scr[h, 0:HEAD_DIM, :] / acc_scr[h, HEAD_DIM:HEAD_DIM + 1, :]
                                 for h in range(nh)], axis=0)
        o_ref[pl.ds(row0, tq), :] = out_t.T.astype(BF16)
        return c

    lax.fori_loop(0, nb // 2, qtile, 0)


def _moba(proj, vt, batch, seq):
    nb = seq // MOBA_BLOCK
    assert nb % 8 == 0 and nb <= HEAD_DIM
    nh = MOBA_HEADS_PER_STEP
    width = nh * HEAD_DIM
    return pl.pallas_call(
        functools.partial(_moba_kernel, nb=nb),
        grid=(batch, ATTN_HEADS // nh),
        in_specs=[
            pl.BlockSpec((seq, width), lambda b, p: (b, COL_AQ // width + p)),
            pl.BlockSpec((seq, width), lambda b, p: (b, COL_AK // width + p)),
            pl.BlockSpec((None, nb, nh * VT_ROWS, MOBA_BLOCK), lambda b, p: (b, 0, p, 0)),
        ],
        out_specs=pl.BlockSpec((seq, width), lambda b, p: (b, p)),
        out_shape=jax.ShapeDtypeStruct((batch * seq, ATTN_WIDTH), BF16),
        scratch_shapes=[pltpu.VMEM((nh // 2, nb, LANES), F32), pltpu.VMEM((nh, seq, LANES), BF16),
                        pltpu.VMEM((nh, nb, 2 * MOBA_BLOCK), F32),
                        pltpu.VMEM((nh, LANES, 2 * MOBA_BLOCK), BF16),
                        pltpu.VMEM((2 * nh, 2 * MOBA_BLOCK, 2 * MOBA_BLOCK), F32),
                        pltpu.VMEM((2 * nh, 1, 2 * MOBA_BLOCK), F32),
                        pltpu.VMEM((nh, VT_ROWS, 2 * MOBA_BLOCK), F32)],
        compiler_params=_cparams(("arbitrary", "arbitrary")),
        name="moba",
    )(proj, proj, vt)


def _ret_log_gamma():
    return [math.log1p(-(2.0 ** (-5.0 - h))) for h in range(RET_HEADS)]


def _retention_tables():
    c = RET_TILE
    lg = np.array(_ret_log_gamma(), np.float64)
    n = np.arange(c, dtype=np.float64)
    diff = n[:, None] - n[None, :]
    decay = np.where(diff[None] >= 0, np.exp(np.maximum(diff, 0.0)[None] * lg[:, None, None]), 0.0)
    head_of_lane = np.arange(RET_QK_WIDTH) // HEAD_DIM
    xi = np.exp((n + 1.0)[:, None] * lg[head_of_lane][None, :])
    zeta = np.exp((c - 1.0 - n)[:, None] * lg[head_of_lane][None, :])
    chunk_decay = np.exp(c * lg[head_of_lane])[:, None]
    return (jnp.asarray(decay, F32), jnp.asarray(xi, F32), jnp.asarray(zeta, F32),
            jnp.asarray(np.broadcast_to(chunk_decay, (RET_QK_WIDTH, LANES)), F32))


def _retention_kernel(q_ref, k_ref, v_ref, g_ref, nw_ref, decay_ref, xi_ref, zeta_ref, cd_ref,
                      o_ref, state_scr):
    @pl.when(pl.program_id(1) == 0)
    def _():
        state_scr[...] = jnp.zeros_like(state_scr)

    lane = lax.broadcasted_iota(jnp.int32, (1, LANES), 1)
    srow = lax.broadcasted_iota(jnp.int32, (LANES, 1), 0)
    for pr in range(RET_HEADS // 2):
        cols = slice(pr * LANES, (pr + 1) * LANES)
        q = q_ref[:, cols]
        k = k_ref[:, cols]
        state = state_scr[cols, :]
        state_bf = state.astype(BF16)
        q_xi = (q.astype(F32) * xi_ref[:, cols]).astype(BF16)
        kz_t = (k.astype(F32) * zeta_ref[:, cols]).T.astype(BF16)
        upd = []
        for hh in range(2):
            h = 2 * pr + hh
            hmask = (lane // HEAD_DIM) == hh
            vcols = slice(h * RET_V_DIM, (h + 1) * RET_V_DIM)
            v = v_ref[:, vcols]
            qm = jnp.where(hmask, q, jnp.zeros_like(q))
            scores = lax.dot_general(qm, k, _NT, preferred_element_type=F32) * decay_ref[h]
            o = jnp.dot(scores.astype(BF16), v, preferred_element_type=F32)
            o = o + jnp.dot(jnp.where(hmask, q_xi, jnp.zeros_like(q_xi)), state_bf,
                            preferred_element_type=F32)
            upd.append(jnp.dot(kz_t, v, preferred_element_type=F32))
            ms = jnp.mean(o * o, axis=-1, keepdims=True)
            y = o * lax.rsqrt(ms + RMS_EPS) * nw_ref[:, vcols]
            g = g_ref[:, vcols].astype(F32)
            o_ref[:, vcols] = (g * jax.nn.sigmoid(g) * y).astype(BF16)
        state_scr[cols, :] = state * cd_ref[cols, :] + jnp.where(srow < HEAD_DIM, upd[0], upd[1])


def _retention(proj, ret_norm_w, tables, batch, seq):
    nc = seq // RET_TILE
    decay, xi, zeta, cd = tables
    row = lambda b, c: b * nc + c
    const2 = lambda b, c: (0, 0)
    return pl.pallas_call(
        _retention_kernel,
        grid=(batch, nc),
        in_specs=[
            pl.BlockSpec((RET_TILE, RET_QK_WIDTH), lambda b, c: (row(b, c), COL_RQ // RET_QK_WIDTH)),
            pl.BlockSpec((RET_TILE, RET_QK_WIDTH), lambda b, c: (row(b, c), COL_RK // RET_QK_WIDTH)),
            pl.BlockSpec((RET_TILE, RET_V_WIDTH), lambda b, c: (row(b, c), COL_RV // RET_V_WIDTH)),
            pl.BlockSpec((RET_TILE, RET_V_WIDTH), lambda b, c: (row(b, c), COL_RG // RET_V_WIDTH)),
            pl.BlockSpec((1, RET_V_WIDTH), const2),
            pl.BlockSpec((RET_HEADS, RET_TILE, RET_TILE), lambda b, c: (0, 0, 0)),
            pl.BlockSpec((RET_TILE, RET_QK_WIDTH), const2),
            pl.BlockSpec((RET_TILE, RET_QK_WIDTH), const2),
            pl.BlockSpec((RET_QK_WIDTH, LANES), const2),
        ],
        out_specs=pl.BlockSpec((RET_TILE, RET_V_WIDTH), lambda b, c: (row(b, c), 0)),
        out_shape=jax.ShapeDtypeStruct((batch * seq, RET_V_WIDTH), BF16),
        scratch_shapes=[pltpu.VMEM((RET_QK_WIDTH, RET_V_DIM), F32)],
        compiler_params=_cparams(("arbitrary", "arbitrary")),
        name="retention",
    )(proj, proj, proj, proj, ret_norm_w, decay, xi, zeta, cd)


CONV_HALO = 8


def _merge_kernel(x_ref, ya_ref, cb_ref, cc_ref, ch_ref, yr_ref, ga_ref, gc_ref, gr_ref, cw_ref,
                  cbias_ref, wa_ref, wc_ref, wr_ref, wo_ref, o_ref, u_scr, *, tiles_per_seq):
    i = pl.program_id(0)
    tm = x_ref.shape[0]

    @pl.when(i % tiles_per_seq == 0)
    def _():
        u_scr[0:CONV_HALO, :] = jnp.zeros((CONV_HALO, CONV_WIDTH), F32)

    u_scr[CONV_HALO:CONV_HALO + tm, :] = cc_ref[...].astype(F32) * ch_ref[...].astype(F32)
    conv = (cw_ref[2:3, :] * u_scr[CONV_HALO:CONV_HALO + tm, :]
            + cw_ref[1:2, :] * u_scr[CONV_HALO - 1:CONV_HALO - 1 + tm, :]
            + cw_ref[0:1, :] * u_scr[CONV_HALO - 2:CONV_HALO - 2 + tm, :]
            + cbias_ref[...])
    y_conv = (cb_ref[...].astype(F32) * conv).astype(BF16)
    u_scr[0:CONV_HALO, :] = u_scr[tm:tm + CONV_HALO, :]

    def gate(g_ref):
        return jax.nn.sigmoid(g_ref[...].astype(F32))

    merged = gate(ga_ref) * jnp.dot(ya_ref[...], wa_ref[...], preferred_element_type=F32)
    merged = merged + gate(gc_ref) * jnp.dot(y_conv, wc_ref[...], preferred_element_type=F32)
    merged = merged + gate(gr_ref) * jnp.dot(yr_ref[...], wr_ref[...], preferred_element_type=F32)
    o_ref[...] = x_ref[...] + jnp.dot(merged.astype(BF16), wo_ref[...], preferred_element_type=F32)


def _merge(x2, y_attn, proj, y_ret, conv_w, conv_b, wa, wc, wr, wo, seq):
    n = x2.shape[0]
    tm = MERGE_TM
    const = lambda i: (0, 0)
    wide = lambda c: pl.BlockSpec((tm, CONV_WIDTH), lambda i: (i, c // CONV_WIDTH))
    gate_spec = lambda b: pl.BlockSpec((tm, D_MODEL), lambda i: (i, COL_GATES // D_MODEL + b))
    return pl.pallas_call(
        functools.partial(_merge_kernel, tiles_per_seq=seq // tm),
        grid=(n // tm,),
        in_specs=[
            pl.BlockSpec((tm, D_MODEL), lambda i: (i, 0)),
            pl.BlockSpec((tm, ATTN_WIDTH), lambda i: (i, 0)),
            wide(COL_CB), wide(COL_CC), wide(COL_CH),
            pl.BlockSpec((tm, RET_V_WIDTH), lambda i: (i, 0)),
            gate_spec(0), gate_spec(1), gate_spec(2),
            pl.BlockSpec((CONV_KERNEL, CONV_WIDTH), const),
            pl.BlockSpec((1, CONV_WIDTH), const),
            pl.BlockSpec((ATTN_WIDTH, D_MODEL), const),
            pl.BlockSpec((CONV_WIDTH, D_MODEL), const),
            pl.BlockSpec((RET_V_WIDTH, D_MODEL), const),
            pl.BlockSpec((D_MODEL, D_MODEL), const),
        ],
        out_specs=pl.BlockSpec((tm, D_MODEL), lambda i: (i, 0)),
        out_shape=jax.ShapeDtypeStruct((n, D_MODEL), F32),
        scratch_shapes=[pltpu.VMEM((tm + CONV_HALO, CONV_WIDTH), F32)],
        compiler_params=_cparams(("arbitrary",)),
        name="merge",
    )(x2, y_attn, proj, proj, proj, y_ret, proj, proj, proj, conv_w, conv_b, wa, wc, wr, wo)


def _swiglu_step(h, wg_ref, wu_ref, wd_ref, acc_scr, c):
    g = jnp.dot(h, wg_ref[...].astype(BF16), preferred_element_type=F32)
    u = jnp.dot(h, wu_ref[...].astype(BF16), preferred_element_type=F32)
    a = (g * jax.nn.sigmoid(g) * u).astype(BF16)
    part = jnp.dot(a, wd_ref[...].astype(BF16), preferred_element_type=F32)

    @pl.when(c == 0)
    def _():
        acc_scr[...] = part

    @pl.when(c != 0)
    def _():
        acc_scr[...] += part


def _dense_ffn_kernel(x_ref, nw_ref, wg_ref, wu_ref, wd_ref, o_ref, h_scr, acc_scr):
    c = pl.program_id(1)

    @pl.when(c == 0)
    def _():
        x = x_ref[...]
        ms = jnp.mean(x * x, axis=-1, keepdims=True)
        h_scr[...] = (x * lax.rsqrt(ms + RMS_EPS) * nw_ref[...]).astype(BF16)

    _swiglu_step(h_scr[...], wg_ref, wu_ref, wd_ref, acc_scr, c)

    @pl.when(c == pl.num_programs(1) - 1)
    def _():
        o_ref[...] = x_ref[...] + acc_scr[...]


def _dense_ffn(x2, norm_w, wg, wu, wd, li):
    n = x2.shape[0]
    tm, tf = FFN_TM, FFN_TF
    return pl.pallas_call(
        _dense_ffn_kernel,
        grid=(n // tm, D_FF // tf),
        in_specs=[
            pl.BlockSpec((tm, D_MODEL), lambda i, c: (i, 0)),
            pl.BlockSpec((1, D_MODEL), lambda i, c: (0, 0)),
            pl.BlockSpec((None, D_MODEL, tf), lambda i, c: (li, 0, c)),
            pl.BlockSpec((None, D_MODEL, tf), lambda i, c: (li, 0, c)),
            pl.BlockSpec((None, tf, D_MODEL), lambda i, c: (li, c, 0)),
        ],
        out_specs=pl.BlockSpec((tm, D_MODEL), lambda i, c: (i, 0)),
        out_shape=jax.ShapeDtypeStruct((n, D_MODEL), F32),
        scratch_shapes=[pltpu.VMEM((tm, D_MODEL), BF16), pltpu.VMEM((tm, D_MODEL), F32)],
        compiler_params=_cparams(("arbitrary", "arbitrary")),
        name="dense_ffn",
    )(x2, norm_w, wg, wu, wd)


SLAB = D_MODEL // LANES


def _pack_rows(v, o_ref):
    rows = v.shape[0]
    for s in range(SLAB):
        o_ref[pl.ds(s, rows, stride=SLAB), :] = v[:, s * LANES:(s + 1) * LANES]


def _unpack_rows(x_ref, rows):
    return jnp.concatenate([x_ref[pl.ds(s, rows, stride=SLAB), :] for s in range(SLAB)], axis=1)


def _route_kernel(x_ref, nw_ref, wr_ref, h_ref, idx_ref, wgt_ref):
    x = x_ref[...]
    ms = jnp.mean(x * x, axis=-1, keepdims=True)
    h = x * lax.rsqrt(ms + RMS_EPS) * nw_ref[...]
    _pack_rows(h, h_ref)
    w = wr_ref[...]
    h_hi, w_hi = h.astype(BF16), w.astype(BF16)
    h_lo, w_lo = (h - h_hi.astype(F32)).astype(BF16), (w - w_hi.astype(F32)).astype(BF16)
    logits = (jnp.dot(h_hi, w_hi, preferred_element_type=F32)
              + jnp.dot(h_lo, w_hi, preferred_element_type=F32)
              + jnp.dot(h_hi, w_lo, preferred_element_type=F32))
    eid = lax.broadcasted_iota(jnp.int32, logits.shape, 1).astype(F32)
    logits = jnp.where(eid < N_EXPERTS, logits, -jnp.inf)
    m1 = jnp.max(logits, axis=-1, keepdims=True)
    i1 = jnp.min(jnp.where(logits == m1, eid, float(LANES)), axis=-1, keepdims=True)
    rest = jnp.where(eid == i1, -jnp.inf, logits)
    m2 = jnp.max(rest, axis=-1, keepdims=True)
    i2 = jnp.min(jnp.where(rest == m2, eid, float(LANES)), axis=-1, keepdims=True)
    e2 = jnp.exp(m2 - m1)
    denom = 1.0 + e2
    idx_ref[...] = jnp.where(eid == 0.0, i1, jnp.where(eid == 1.0, i2, 0.0)).astype(jnp.int32)
    wgt_ref[...] = jnp.where(eid == 0.0, 1.0 / denom, jnp.where(eid == 1.0, e2 / denom, 0.0))


def _route(x2, norm_w, w_router):
    n = x2.shape[0]
    tm = ROUTE_TM
    return pl.pallas_call(
        _route_kernel,
        grid=(n // tm,),
        in_specs=[
            pl.BlockSpec((tm, D_MODEL), lambda i: (i, 0)),
            pl.BlockSpec((1, D_MODEL), lambda i: (0, 0)),
            pl.BlockSpec((D_MODEL, LANES), lambda i: (0, 0)),
        ],
        out_specs=[
            pl.BlockSpec((tm * SLAB, LANES), lambda i: (i, 0)),
            pl.BlockSpec((tm, LANES), lambda i: (i, 0)),
            pl.BlockSpec((tm, LANES), lambda i: (i, 0)),
        ],
        out_shape=[
            jax.ShapeDtypeStruct((n * SLAB, LANES), F32),
            jax.ShapeDtypeStruct((n, LANES), jnp.int32),
            jax.ShapeDtypeStruct((n, LANES), F32),
        ],
        compiler_params=_cparams(("arbitrary",)),
        name="route",
    )(x2, norm_w, jnp.pad(w_router, ((0, 0), (0, LANES - N_EXPERTS))))


MOE_STEPS = D_FF // FFN_TF
MOE_ROWS_PER_STEP = -(-FFN_TM // MOE_STEPS)
MOE_DMA_ROWS = MOE_ROWS_PER_STEP * MOE_STEPS
MOE_BUF_ROWS = -(-MOE_DMA_ROWS // 8) * 8
MOE_SPARE_TILES = 2


def _moe_ffn_kernel(te_ref, nv_ref, gid0_ref, gidn_ref, sid_ref, h3_ref, wg_ref, wu_ref, wd_ref,
                    y_init_ref, y_ref, gbuf, obuf, h_scr, acc_scr, gsem, ssem):
    t = pl.program_id(0)
    c = pl.program_id(1)
    nv = nv_ref[0]
    slot = t % 2
    other = 1 - slot
    first = c == 0

    def slab(ref, start):
        return ref.at[pl.ds(pl.multiple_of(start, SLAB), SLAB)]

    def gather(ids_ref, row, s):
        return pltpu.make_async_copy(slab(h3_ref, ids_ref[0, 0, row]), slab(gbuf.at[s], row * SLAB),
                                     gsem.at[s])

    def scatter(row, s):
        return pltpu.make_async_copy(slab(obuf.at[s], row * SLAB), slab(y_ref, sid_ref[0, 0, row]),
                                     ssem.at[s])

    def tile_gather(s):
        return pltpu.make_async_copy(h3_ref.at[pl.ds(0, MOE_DMA_ROWS * SLAB)],
                                     gbuf.at[s, pl.ds(0, MOE_DMA_ROWS * SLAB)], gsem.at[s])

    def tile_scatter(s):
        return pltpu.make_async_copy(obuf.at[s, pl.ds(0, MOE_DMA_ROWS * SLAB)],
                                     y_ref.at[pl.ds(0, MOE_DMA_ROWS * SLAB)], ssem.at[s])

    @pl.when(first & (t == 0))
    def _():
        obuf[...] = jnp.zeros_like(obuf)

        def body(r, carry):
            gather(gid0_ref, r, 0).start()
            return carry

        lax.fori_loop(0, MOE_DMA_ROWS, body, 0, unroll=MOE_STEPS)

    @pl.when(first & (t <= nv))
    def _():
        tile_gather(slot).wait()

    @pl.when(first & (t >= 1) & (t <= nv + 1))
    def _():
        tile_scatter(slot).wait()

    @pl.when(first & (t < nv))
    def _():
        h_scr[...] = _unpack_rows(gbuf.at[slot], FFN_TM).astype(BF16)

    @pl.when(t < nv)
    def _():
        for u in range(MOE_ROWS_PER_STEP):
            row = c * MOE_ROWS_PER_STEP + u
            gather(gidn_ref, row, other).start(priority=1)
            scatter(row, other).start(priority=1)
        _swiglu_step(h_scr[...], wg_ref, wu_ref, wd_ref, acc_scr, c)

    @pl.when(t == nv)
    def _():
        def body(u, carry):
            scatter(c * MOE_ROWS_PER_STEP + u, other).start()
            return carry

        lax.fori_loop(0, MOE_ROWS_PER_STEP, body, 0, unroll=MOE_STEPS)

    @pl.when((t < nv) & (c == pl.num_programs(1) - 1))
    def _():
        _pack_rows(acc_scr[...], obuf.at[slot])


def _moe_ffn(tile_expert, n_valid, gid, sid, h3, wg, wu, wd, li, y_rows):
    nt = gid.shape[0]
    tm, tf = FFN_TM, FFN_TF
    ids_block = (1, 1, MOE_BUF_ROWS)
    smem = pltpu.SMEM
    wspec = lambda shape, imap: pl.BlockSpec((None, None) + shape, imap)
    expert = lambda t, te, nv: te[jnp.minimum(t, nv[0] - 1)]
    chunk = lambda t, c, nv: jnp.where(t < nv[0], c, MOE_STEPS - 1)
    grid_spec = pltpu.PrefetchScalarGridSpec(
        num_scalar_prefetch=2,
        grid=(nt, MOE_STEPS),
        in_specs=[
            pl.BlockSpec(ids_block, lambda t, c, te, nv: (0, 0, 0), memory_space=smem),
            pl.BlockSpec(ids_block, lambda t, c, te, nv: (jnp.minimum(t + 1, nt - 1), 0, 0),
                         memory_space=smem),
            pl.BlockSpec(ids_block, lambda t, c, te, nv: (t, 0, 0), memory_space=smem),
            pl.BlockSpec(memory_space=pl.ANY),
            wspec((D_MODEL, tf), lambda t, c, te, nv: (li, expert(t, te, nv), 0, chunk(t, c, nv))),
            wspec((D_MODEL, tf), lambda t, c, te, nv: (li, expert(t, te, nv), 0, chunk(t, c, nv))),
            wspec((tf, D_MODEL), lambda t, c, te, nv: (li, expert(t, te, nv), chunk(t, c, nv), 0)),
            pl.BlockSpec(memory_space=pl.ANY),
        ],
        out_specs=pl.BlockSpec(memory_space=pl.ANY),
        scratch_shapes=[
            pltpu.VMEM((2, MOE_BUF_ROWS * SLAB, LANES), F32),
            pltpu.VMEM((2, MOE_BUF_ROWS * SLAB, LANES), F32),
            pltpu.VMEM((tm, D_MODEL), BF16),
            pltpu.VMEM((tm, D_MODEL), F32),
            pltpu.SemaphoreType.DMA((2,)),
            pltpu.SemaphoreType.DMA((2,)),
        ],
    )
    return pl.pallas_call(
        _moe_ffn_kernel,
        grid_spec=grid_spec,
        out_shape=jax.ShapeDtypeStruct((y_rows * SLAB, LANES), F32),
        input_output_aliases={9: 0},
        compiler_params=_cparams(("arbitrary", "arbitrary")),
        name="moe_ffn",
    )(tile_expert, n_valid, gid * SLAB, gid * SLAB, sid * SLAB, h3, wg, wu, wd,
      jnp.zeros((y_rows * SLAB, LANES), F32))


def _combine_kernel(x_ref, y0_ref, y1_ref, w_ref, nw_ref, o_ref, *, final_norm):
    w = w_ref[...]
    rows = x_ref.shape[0]
    x = x_ref[...] + w[:, 0:1] * _unpack_rows(y0_ref, rows) + w[:, 1:2] * _unpack_rows(y1_ref, rows)
    if final_norm:
        ms = jnp.mean(x * x, axis=-1, keepdims=True)
        x = x * lax.rsqrt(ms + RMS_EPS) * nw_ref[...]
    o_ref[...] = x


def _combine(x2, y_pairs, wgt, norm_w, final_norm):
    n = x2.shape[0]
    tm = MERGE_TM
    nt = n // tm
    return pl.pallas_call(
        functools.partial(_combine_kernel, final_norm=final_norm),
        grid=(nt,),
        in_specs=[
            pl.BlockSpec((tm, D_MODEL), lambda i: (i, 0)),
            pl.BlockSpec((tm * SLAB, LANES), lambda i: (i, 0)),
            pl.BlockSpec((tm * SLAB, LANES), lambda i: (i + nt, 0)),
            pl.BlockSpec((tm, LANES), lambda i: (i, 0)),
            pl.BlockSpec((1, D_MODEL), lambda i: (0, 0)),
        ],
        out_specs=pl.BlockSpec((tm, D_MODEL), lambda i: (i, 0)),
        out_shape=jax.ShapeDtypeStruct((n, D_MODEL), F32),
        compiler_params=_cparams(("arbitrary",)),
        name="combine",
    )(x2, y_pairs, y_pairs, wgt, norm_w)


def _moe_layout(idx):
    n = idx.shape[0]
    tm = FFN_TM
    flat = idx.T.reshape(-1)
    onehot = (flat[:, None] == jnp.arange(N_EXPERTS, dtype=jnp.int32)[None, :]).astype(jnp.int32)
    cum = jnp.cumsum(onehot, axis=0)
    counts = cum[-1]
    rank = jnp.sum(cum * onehot, axis=1) - 1
    padded = ((counts + tm - 1) // tm) * tm
    ends = jnp.cumsum(padded)
    starts = ends - padded
    dest = starts[flat] + rank
    pairs = TOP_K * n
    nt = pairs // tm + N_EXPERTS + MOE_SPARE_TILES
    p = nt * tm
    tile_start = jnp.arange(nt, dtype=jnp.int32) * tm
    tile_expert = jnp.minimum(jnp.sum((ends[None, :] <= tile_start[:, None]).astype(jnp.int32), axis=1),
                              N_EXPERTS - 1)
    n_valid = (ends[-1] // tm).astype(jnp.int32).reshape(1)
    real_before = jnp.cumsum(counts)[tile_expert]
    pos = jnp.arange(p, dtype=jnp.int32).reshape(nt, tm)
    pad_row = (pairs + pos - real_before[:, None]).reshape(p)
    row_pair = pad_row.at[dest].set(jnp.arange(pairs, dtype=jnp.int32), unique_indices=True,
                                    mode="promise_in_bounds")
    row_token = jnp.where(row_pair < pairs, row_pair % n, 0)
    tail = MOE_BUF_ROWS - tm
    gid = jnp.pad(row_token.reshape(nt, tm), ((0, 0), (0, tail))).reshape(nt, 1, MOE_BUF_ROWS)
    first_spare = p + jnp.arange(MOE_BUF_ROWS, dtype=jnp.int32)[None, :]
    tails = p + MOE_BUF_ROWS + jnp.arange(nt * tail, dtype=jnp.int32).reshape(nt, tail)
    sid = jnp.concatenate([row_pair.reshape(nt, tm), tails], axis=1)
    sid = jnp.concatenate([first_spare, sid], axis=0).reshape(nt + 1, 1, MOE_BUF_ROWS)
    y_rows = p + MOE_BUF_ROWS + nt * tail
    return gid, sid, tile_expert, n_valid, y_rows


def _moe(x2, norm_w, w_router, wg, wu, wd, li, final_norm_w, final_norm):
    h3, idx, wgt = _route(x2, norm_w, w_router)
    gid, sid, tile_expert, n_valid, y_rows = _moe_layout(idx[:, :TOP_K])
    y = _moe_ffn(tile_expert, n_valid, gid, sid, h3, wg, wu, wd, li, y_rows)
    return _combine(x2, y, wgt, final_norm_w, final_norm)


def _rope_tables(seq):
    inv_freq = 1.0 / (ROPE_THETA ** (jnp.arange(0, HEAD_DIM, 2, dtype=F32) / HEAD_DIM))
    ang = jnp.arange(seq, dtype=F32)[:, None] * inv_freq[None, :]
    cos, sin = jnp.cos(ang), jnp.sin(ang)
    cos_t = jnp.tile(cos, (1, LANES // (HEAD_DIM // 2)))
    sin_t = jnp.tile(jnp.concatenate([-sin, sin], axis=1), (1, LANES // HEAD_DIM))
    return cos_t, sin_t


def kernel(x, mix_norm_w, w_in, conv_w, conv_b, ret_norm_w, w_br_attn, w_br_conv, w_br_ret, w_out,
           ffn_norm_w, dense_w_gate, dense_w_up, dense_w_down, moe_router, moe_w_gate, moe_w_up,
           moe_w_down, final_norm_w):
    batch, seq, d = x.shape
    depth = w_in.shape[0]
    assert d == D_MODEL and seq % PROJ_TM == 0 and depth % 2 == 0
    n = batch * seq
    cos_t, sin_t = _rope_tables(seq)
    ret_tables = _retention_tables()
    x2 = x.reshape(n, d)
    for layer in range(depth):
        proj, vt = _inproj(x2, mix_norm_w[layer].reshape(1, d), w_in, layer, cos_t, sin_t, batch, seq)
        y_attn = _moba(proj, vt, batch, seq)
        y_ret = _retention(proj, ret_norm_w[layer].reshape(1, RET_V_WIDTH), ret_tables, batch, seq)
        x2 = _merge(x2, y_attn, proj, y_ret, conv_w[layer], conv_b[layer].reshape(1, CONV_WIDTH),
                    w_br_attn[layer].astype(BF16), w_br_conv[layer].astype(BF16),
                    w_br_ret[layer].astype(BF16), w_out[layer].astype(BF16), seq)
        i = layer // 2
        nw = ffn_norm_w[layer].reshape(1, d)
        if layer % 2 == 0:
            x2 = _dense_ffn(x2, nw, dense_w_gate, dense_w_up, dense_w_down, i)
        else:
            last = layer == depth - 1
            x2 = _moe(x2, nw, moe_router[i], moe_w_gate, moe_w_up, moe_w_down, i,
                      final_norm_w.reshape(1, d), last)
    return x2.reshape(batch, seq, d)
```

```python
import functools
import math

import jax
import jax.numpy as jnp
import numpy as np
from jax import lax
from jax.experimental import pallas as pl
from jax.experimental.pallas import tpu as pltpu

F32 = jnp.float32
BF16 = jnp.bfloat16

D_MODEL = 1024
HEAD_DIM = 64
ROPE_THETA = 10000.0
RMS_EPS = 1e-6

ATTN_HEADS = 8
ATTN_WIDTH = ATTN_HEADS * HEAD_DIM
MOBA_BLOCK = 256
MOBA_TOPK = 3
NEG_INF = -1e30
MOBA_HEADS_PER_STEP = 4
VT_ONES = 16
VT_ROWS = HEAD_DIM + VT_ONES
QK_SCALE_LOG2 = HEAD_DIM ** -0.5 * math.log2(math.e)

CONV_WIDTH = 512
CONV_KERNEL = 3

RET_HEADS = 4
RET_QK_WIDTH = RET_HEADS * HEAD_DIM
RET_V_DIM = 2 * HEAD_DIM
RET_V_WIDTH = RET_HEADS * RET_V_DIM
RET_TILE = 256

D_FF = 3584
N_EXPERTS = 8
TOP_K = 2

LANES = 128
IN_PROJ_WIDTH = 7680
COL_AQ, COL_AK, COL_AV = 0, 512, 1024
COL_CB, COL_CC, COL_CH = 1536, 2048, 2560
COL_RV, COL_RG = 3072, 3584
COL_GATES = 4096
COL_RQ, COL_RK = 7168, 7424

PROJ_TM = 2048
PROJ_TN = 512
MERGE_TM = 512
FFN_TM = 1024
FFN_TF = 512
ROUTE_TM = 1024
VMEM_LIMIT = 56 * 1024 * 1024


def _cparams(sem):
    return pltpu.CompilerParams(dimension_semantics=sem, vmem_limit_bytes=VMEM_LIMIT)


def _rope(acc, cos, sin_signed):
    lane = lax.broadcasted_iota(jnp.int32, (1, LANES), 1)
    first_half = (lane % HEAD_DIM) < (HEAD_DIM // 2)
    outs = []
    for g in range(acc.shape[1] // LANES):
        blk = acc[:, g * LANES:(g + 1) * LANES]
        packed = blk.astype(BF16)
        partner = jnp.where(first_half,
                            pltpu.roll(packed, LANES - HEAD_DIM // 2, 1),
                            pltpu.roll(packed, HEAD_DIM // 2, 1)).astype(F32)
        outs.append(blk * cos + partner * sin_signed)
    return jnp.concatenate(outs, axis=1)


def _inproj_kernel(x_ref, nw_ref, w_ref, cos_ref, sin_ref, proj_ref, vt_ref, h_scr):
    j = pl.program_id(1)

    @pl.when(j == 0)
    def _():
        x = x_ref[...]
        ms = jnp.mean(x * x, axis=-1, keepdims=True)
        h_scr[...] = (x * lax.rsqrt(ms + RMS_EPS) * nw_ref[...]).astype(BF16)

    acc = jnp.dot(h_scr[...], w_ref[...].astype(BF16), preferred_element_type=F32)
    jq, jk, jv, jr = COL_AQ // PROJ_TN, COL_AK // PROJ_TN, COL_AV // PROJ_TN, COL_RQ // PROJ_TN
    scale = HEAD_DIM ** -0.5

    @pl.when(j == jq)
    def _():
        proj_ref[...] = (_rope(acc, cos_ref[...], sin_ref[...]) * QK_SCALE_LOG2).astype(BF16)

    @pl.when(j == jk)
    def _():
        proj_ref[...] = _rope(acc, cos_ref[...], sin_ref[...]).astype(BF16)

    @pl.when(j == jv)
    def _():
        proj_ref[...] = acc.astype(BF16)
        ones = jnp.ones((VT_ONES, MOBA_BLOCK), F32)
        for c in range(PROJ_TM // MOBA_BLOCK):
            v_t = acc[c * MOBA_BLOCK:(c + 1) * MOBA_BLOCK, :].T
            rows = []
            for h in range(ATTN_HEADS):
                rows += [v_t[h * HEAD_DIM:(h + 1) * HEAD_DIM], ones]
            vt_ref[c] = jnp.concatenate(rows, axis=0).astype(BF16)

    @pl.when(j == jr)
    def _():
        r = _rope(acc, cos_ref[...], sin_ref[...])
        col = lax.broadcasted_iota(jnp.int32, (1, PROJ_TN), 1)
        r = r * jnp.where(col >= RET_QK_WIDTH, scale, 1.0)
        proj_ref[...] = r.astype(BF16)

    @pl.when((j != jq) & (j != jk) & (j != jv) & (j != jr))
    def _():
        proj_ref[...] = acc.astype(BF16)


def _inproj_src_block(j):
    first_moved = COL_RV // PROJ_TN
    last = IN_PROJ_WIDTH // PROJ_TN - 1
    return jnp.where(j < first_moved, j, jnp.where(j == last, first_moved, j + 1))


def _inproj(x2, norm_w, w_in, layer, cos_t, sin_t, batch, seq):
    n = x2.shape[0]
    nst = seq // PROJ_TM
    nblk = PROJ_TM // MOBA_BLOCK
    return pl.pallas_call(
        _inproj_kernel,
        grid=(n // PROJ_TM, IN_PROJ_WIDTH // PROJ_TN),
        in_specs=[
            pl.BlockSpec((PROJ_TM, D_MODEL), lambda i, j: (i, 0)),
            pl.BlockSpec((1, D_MODEL), lambda i, j: (0, 0)),
            pl.BlockSpec((None, D_MODEL, PROJ_TN), lambda i, j: (layer, 0, _inproj_src_block(j))),
            pl.BlockSpec((PROJ_TM, LANES), lambda i, j: (i % nst, 0)),
            pl.BlockSpec((PROJ_TM, LANES), lambda i, j: (i % nst, 0)),
        ],
        out_specs=[
            pl.BlockSpec((PROJ_TM, PROJ_TN), lambda i, j: (i, j)),
            pl.BlockSpec((None, nblk, ATTN_HEADS * VT_ROWS, MOBA_BLOCK),
                         lambda i, j: (i // nst, i % nst, 0, 0)),
        ],
        out_shape=[
            jax.ShapeDtypeStruct((n, IN_PROJ_WIDTH), BF16),
            jax.ShapeDtypeStruct((batch, seq // MOBA_BLOCK, ATTN_HEADS * VT_ROWS, MOBA_BLOCK), BF16),
        ],
        scratch_shapes=[pltpu.VMEM((PROJ_TM, D_MODEL), BF16)],
        compiler_params=_cparams(("arbitrary", "arbitrary")),
        name="inproj",
    )(x2, norm_w, w_in, cos_t, sin_t)


_NT = (((1,), (1,)), ((), ()))


def _moba_kernel(q_ref, k_ref, vt_ref, o_ref, km_scr, ka_scr, sel_scr, qt_scr, s_scr, m_scr, acc_scr,
                 *, nb):
    nh = MOBA_HEADS_PER_STEP
    blk = MOBA_BLOCK
    tq = 2 * blk
    lane = lax.broadcasted_iota(jnp.int32, (1, LANES), 1)
    crow = lax.broadcasted_iota(jnp.int32, (LANES, 1), 0)

    def group(ref, rows, h):
        g = h // 2
        return ref[rows, g * LANES:(g + 1) * LANES]

    def prep(i, c):
        rows = pl.ds(pl.multiple_of(i * blk, blk), blk)
        for h in range(nh):
            kb, hh = group(k_ref, rows, h), h % 2
            if hh == 0:
                km_scr[h // 2, pl.ds(i, 1), :] = jnp.sum(kb.astype(F32), axis=0, keepdims=True) * (1.0 / blk)
            onehot = jnp.where(lane == HEAD_DIM * (1 - hh) + i, 1.0, 0.0).astype(BF16)
            ka_scr[h, rows, :] = jnp.where((lane // HEAD_DIM) == hh, kb, onehot)
        return c

    lax.fori_loop(0, nb, prep, 0)

    causal = (lax.broadcasted_iota(jnp.int32, (blk, blk), 0)
              <= lax.broadcasted_iota(jnp.int32, (blk, blk), 1))
    blk_id = lax.broadcasted_iota(jnp.int32, (nb, tq), 0)
    q_half = (lax.broadcasted_iota(jnp.int32, (nb, tq), 1) >= blk).astype(jnp.int32)

    def pv(t, h, p):
        rows = slice(h * VT_ROWS, (h + 1) * VT_ROWS)
        pb = p.astype(BF16)
        return (jnp.dot(vt_ref[2 * t, rows, :], pb[0:blk], preferred_element_type=F32)
                + jnp.dot(vt_ref[2 * t + 1, rows, :], pb[blk:tq], preferred_element_type=F32))

    def qtile(jt, c):
        row0 = pl.multiple_of(jt * tq, tq)
        q_ts =[group(q_ref, pl.ds(row0, tq), h).astype(F32).T for h in range(0, nh, 2)]
        for h in range(nh):
            hh = h % 2
            q_t = q_ts[h // 2]
            k_diag = group(k_ref, pl.ds(row0, tq), h)
            q_m = jnp.where((crow // HEAD_DIM) == hh, q_t, 0.0)
            q_mb = q_m.astype(BF16)
            km = jnp.where((lane // HEAD_DIM) == hh, km_scr[h // 2], 0.0)
            km_hi = km.astype(BF16)
            km_lo = (km - km_hi.astype(F32)).astype(BF16)
            gate = (jnp.dot(km_hi, q_mb, preferred_element_type=F32)
                    + jnp.dot(km_lo, q_mb, preferred_element_type=F32))
            past = blk_id < 2 * jt + q_half
            gate = jnp.where(past, gate, -jnp.inf)
            rank = jnp.zeros((nb, tq), jnp.int32)
            for ip in range(nb):
                gi = gate[ip:ip + 1, :]
                beats = (gi > gate) | ((gi == gate) & (blk_id > ip))
                rank = rank + beats.astype(jnp.int32)
            sel = (rank < MOBA_TOPK) & past
            sel_scr[h] = sel.astype(F32)
            bias = jnp.where(sel, 0.0, NEG_INF)
            spare = HEAD_DIM * (1 - hh)
            pieces = [jnp.zeros((spare, tq), F32)] if spare else []
            pieces += [bias, jnp.zeros((LANES - spare - nb, tq), F32)]
            qt_scr[h] = (q_m + jnp.concatenate(pieces, axis=0)).astype(BF16)

            s_top = jnp.dot(k_diag[0:blk], q_mb, preferred_element_type=F32)
            s11 = jnp.where(causal, s_top[:, 0:blk], NEG_INF)
            s12 = jnp.where(sel_scr[h, pl.ds(2 * jt, 1), blk:tq] > 0.0, s_top[:, blk:tq], NEG_INF)
            s22 = jnp.where(causal, jnp.dot(k_diag[blk:tq], q_mb[:, blk:tq], preferred_element_type=F32),
                            NEG_INF)
            m_a = jnp.max(s11, axis=0, keepdims=True)
            m_b = jnp.maximum(jnp.max(s12, axis=0, keepdims=True), jnp.max(s22, axis=0, keepdims=True))
            s_scr[2 * h, 0:blk, 0:blk] = s11
            s_scr[2 * h, 0:blk, blk:tq] = s12
            s_scr[2 * h, blk:tq, 0:blk] = jnp.full((blk, blk), NEG_INF, F32)
            s_scr[2 * h, blk:tq, blk:tq] = s22
            m_scr[2 * h] = jnp.concatenate([m_a, m_b], axis=1)
            m_scr[2 * h + 1] = m_scr[2 * h]
            acc_scr[h] = jnp.zeros((VT_ROWS, tq), F32)

        def park(t, h, slot):
            s = jnp.dot(ka_scr[h, pl.ds(pl.multiple_of(t * tq, tq), tq), :], qt_scr[h],
                        preferred_element_type=F32)
            s_scr[2 * h + slot] = s
            m_scr[2 * h + 1] = jnp.maximum(m_scr[2 * h + 1], jnp.max(s, axis=0, keepdims=True))

        def consume(k, h, slot):
            m_old, m_new = m_scr[2 * h], m_scr[2 * h + 1]
            acc_scr[h] = (jnp.exp2(m_old - m_new) * acc_scr[h]
                          + pv(jnp.where(k == 0, jt, k - 1), h, jnp.exp2(s_scr[2 * h + slot] - m_new)))
            m_scr[2 * h] = m_new

        def step(k, slot):
            for h in range(nh):
                consume(k, h, slot)
                park(k, h, 1 - slot)

        def run(trips, body):
            def wrapped(i, c):
                body(i)
                return c
            lax.fori_loop(0, trips, wrapped, 0)

        run(jt // 2, lambda u: (step(2 * u, 0), step(2 * u + 1, 1)))
        run(jt % 2, lambda i: step(jt - 1, 0))
        run(jt % 2, lambda i: [consume(jt, h, 1) for h in range(nh)])
        run(1 - jt % 2, lambda i: [consume(jt, h, 0) for h in range(nh)])
        out_t = jnp.concatenate([acc_scr[h, 0:HEAD_DIM, :] / acc_scr[h, HEAD_DIM:HEAD_DIM + 1, :]
                                 for h in range(nh)], axis=0)
        o_ref[pl.ds(row0, tq), :] = out_t.T.astype(BF16)
        return c

    lax.fori_loop(0, nb // 2, qtile, 0)


def _moba(proj, vt, batch, seq):
    nb = seq // MOBA_BLOCK
    assert nb % 8 == 0 and nb <= HEAD_DIM
    nh = MOBA_HEADS_PER_STEP
    width = nh * HEAD_DIM
    return pl.pallas_call(
        functools.partial(_moba_kernel, nb=nb),
        grid=(batch, ATTN_HEADS // nh),
        in_specs=[
            pl.BlockSpec((seq, width), lambda b, p: (b, COL_AQ // width + p)),
            pl.BlockSpec((seq, width), lambda b, p: (b, COL_AK // width + p)),
            pl.BlockSpec((None, nb, nh * VT_ROWS, MOBA_BLOCK), lambda b, p: (b, 0, p, 0)),
        ],
        out_specs=pl.BlockSpec((seq, width), lambda b, p: (b, p)),
        out_shape=jax.ShapeDtypeStruct((batch * seq, ATTN_WIDTH), BF16),
        scratch_shapes=[pltpu.VMEM((nh // 2, nb, LANES), F32), pltpu.VMEM((nh, seq, LANES), BF16),
                        pltpu.VMEM((nh, nb, 2 * MOBA_BLOCK), F32),
                        pltpu.VMEM((nh, LANES, 2 * MOBA_BLOCK), BF16),
                        pltpu.VMEM((2 * nh, 2 * MOBA_BLOCK, 2 * MOBA_BLOCK), F32),
                        pltpu.VMEM((2 * nh, 1, 2 * MOBA_BLOCK), F32),
                        pltpu.VMEM((nh, VT_ROWS, 2 * MOBA_BLOCK), F32)],
        compiler_params=_cparams(("arbitrary", "arbitrary")),
        name="moba",
    )(proj, proj, vt)


def _ret_log_gamma():
    return [math.log1p(-(2.0 ** (-5.0 - h))) for h in range(RET_HEADS)]


def _retention_tables():
    c = RET_TILE
    lg = np.array(_ret_log_gamma(), np.float64)
    n = np.arange(c, dtype=np.float64)
    diff = n[:, None] - n[None, :]
    decay = np.where(diff[None] >= 0, np.exp(np.maximum(diff, 0.0)[None] * lg[:, None, None]), 0.0)
    head_of_lane = np.arange(RET_QK_WIDTH) // HEAD_DIM
    xi = np.exp((n + 1.0)[:, None] * lg[head_of_lane][None, :])
    zeta = np.exp((c - 1.0 - n)[:, None] * lg[head_of_lane][None, :])
    chunk_decay = np.exp(c * lg[head_of_lane])[:, None]
    return (jnp.asarray(decay, F32), jnp.asarray(xi, F32), jnp.asarray(zeta, F32),
            jnp.asarray(np.broadcast_to(chunk_decay, (RET_QK_WIDTH, LANES)), F32))


def _retention_kernel(q_ref, k_ref, v_ref, g_ref, nw_ref, decay_ref, xi_ref, zeta_ref, cd_ref,
                      o_ref, state_scr):
    @pl.when(pl.program_id(1) == 0)
    def _():
        state_scr[...] = jnp.zeros_like(state_scr)

    lane = lax.broadcasted_iota(jnp.int32, (1, LANES), 1)
    srow = lax.broadcasted_iota(jnp.int32, (LANES, 1), 0)
    for pr in range(RET_HEADS // 2):
        cols = slice(pr * LANES, (pr + 1) * LANES)
        q = q_ref[:, cols]
        k = k_ref[:, cols]
        state = state_scr[cols, :]
        state_bf = state.astype(BF16)
        q_xi = (q.astype(F32) * xi_ref[:, cols]).astype(BF16)
        kz_t = (k.astype(F32) * zeta_ref[:, cols]).T.astype(BF16)
        upd = []
        for hh in range(2):
            h = 2 * pr + hh
            hmask = (lane // HEAD_DIM) == hh
            vcols = slice(h * RET_V_DIM, (h + 1) * RET_V_DIM)
            v = v_ref[:, vcols]
            qm = jnp.where(hmask, q, jnp.zeros_like(q))
            scores = lax.dot_general(qm, k, _NT, preferred_element_type=F32) * decay_ref[h]
            o = jnp.dot(scores.astype(BF16), v, preferred_element_type=F32)
            o = o + jnp.dot(jnp.where(hmask, q_xi, jnp.zeros_like(q_xi)), state_bf,
                            preferred_element_type=F32)
            upd.append(jnp.dot(kz_t, v, preferred_element_type=F32))
            ms = jnp.mean(o * o, axis=-1, keepdims=True)
            y = o * lax.rsqrt(ms + RMS_EPS) * nw_ref[:, vcols]
            g = g_ref[:, vcols].astype(F32)
            o_ref[:, vcols] = (g * jax.nn.sigmoid(g) * y).astype(BF16)
        state_scr[cols, :] = state * cd_ref[cols, :] + jnp.where(srow < HEAD_DIM, upd[0], upd[1])


def _retention(proj, ret_norm_w, tables, batch, seq):
    nc = seq // RET_TILE
    decay, xi, zeta, cd = tables
    row = lambda b, c: b * nc + c
    const2 = lambda b, c: (0, 0)
    return pl.pallas_call(
        _retention_kernel,
        grid=(batch, nc),
        in_specs=[
            pl.BlockSpec((RET_TILE, RET_QK_WIDTH), lambda b, c: (row(b, c), COL_RQ // RET_QK_WIDTH)),
            pl.BlockSpec((RET_TILE, RET_QK_WIDTH), lambda b, c: (row(b, c), COL_RK // RET_QK_WIDTH)),
            pl.BlockSpec((RET_TILE, RET_V_WIDTH), lambda b, c: (row(b, c), COL_RV // RET_V_WIDTH)),
            pl.BlockSpec((RET_TILE, RET_V_WIDTH), lambda b, c: (row(b, c), COL_RG // RET_V_WIDTH)),
            pl.BlockSpec((1, RET_V_WIDTH), const2),
            pl.BlockSpec((RET_HEADS, RET_TILE, RET_TILE), lambda b, c: (0, 0, 0)),
            pl.BlockSpec((RET_TILE, RET_QK_WIDTH), const2),
            pl.BlockSpec((RET_TILE, RET_QK_WIDTH), const2),
            pl.BlockSpec((RET_QK_WIDTH, LANES), const2),
        ],
        out_specs=pl.BlockSpec((RET_TILE, RET_V_WIDTH), lambda b, c: (row(b, c), 0)),
        out_shape=jax.ShapeDtypeStruct((batch * seq, RET_V_WIDTH), BF16),
        scratch_shapes=[pltpu.VMEM((RET_QK_WIDTH, RET_V_DIM), F32)],
        compiler_params=_cparams(("arbitrary", "arbitrary")),
        name="retention",
    )(proj, proj, proj, proj, ret_norm_w, decay, xi, zeta, cd)


CONV_HALO = 8


def _merge_kernel(x_ref, ya_ref, cb_ref, cc_ref, ch_ref, yr_ref, ga_ref, gc_ref, gr_ref, cw_ref,
                  cbias_ref, wa_ref, wc_ref, wr_ref, wo_ref, o_ref, u_scr, *, tiles_per_seq):
    i = pl.program_id(0)
    tm = x_ref.shape[0]

    @pl.when(i % tiles_per_seq == 0)
    def _():
        u_scr[0:CONV_HALO, :] = jnp.zeros((CONV_HALO, CONV_WIDTH), F32)

    u_scr[CONV_HALO:CONV_HALO + tm, :] = cc_ref[...].astype(F32) * ch_ref[...].astype(F32)
    conv = (cw_ref[2:3, :] * u_scr[CONV_HALO:CONV_HALO + tm, :]
            + cw_ref[1:2, :] * u_scr[CONV_HALO - 1:CONV_HALO - 1 + tm, :]
            + cw_ref[0:1, :] * u_scr[CONV_HALO - 2:CONV_HALO - 2 + tm, :]
            + cbias_ref[...])
    y_conv = (cb_ref[...].astype(F32) * conv).astype(BF16)
    u_scr[0:CONV_HALO, :] = u_scr[tm:tm + CONV_HALO, :]

    def gate(g_ref):
        return jax.nn.sigmoid(g_ref[...].astype(F32))

    merged = gate(ga_ref) * jnp.dot(ya_ref[...], wa_ref[...], preferred_element_type=F32)
    merged = merged + gate(gc_ref) * jnp.dot(y_conv, wc_ref[...], preferred_element_type=F32)
    merged = merged + gate(gr_ref) * jnp.dot(yr_ref[...], wr_ref[...], preferred_element_type=F32)
    o_ref[...] = x_ref[...] + jnp.dot(merged.astype(BF16), wo_ref[...], preferred_element_type=F32)


def _merge(x2, y_attn, proj, y_ret, conv_w, conv_b, wa, wc, wr, wo, seq):
    n = x2.shape[0]
    tm = MERGE_TM
    const = lambda i: (0, 0)
    wide = lambda c: pl.BlockSpec((tm, CONV_WIDTH), lambda i: (i, c // CONV_WIDTH))
    gate_spec = lambda b: pl.BlockSpec((tm, D_MODEL), lambda i: (i, COL_GATES // D_MODEL + b))
    return pl.pallas_call(
        functools.partial(_merge_kernel, tiles_per_seq=seq // tm),
        grid=(n // tm,),
        in_specs=[
            pl.BlockSpec((tm, D_MODEL), lambda i: (i, 0)),
            pl.BlockSpec((tm, ATTN_WIDTH), lambda i: (i, 0)),
            wide(COL_CB), wide(COL_CC), wide(COL_CH),
            pl.BlockSpec((tm, RET_V_WIDTH), lambda i: (i, 0)),
            gate_spec(0), gate_spec(1), gate_spec(2),
            pl.BlockSpec((CONV_KERNEL, CONV_WIDTH), const),
            pl.BlockSpec((1, CONV_WIDTH), const),
            pl.BlockSpec((ATTN_WIDTH, D_MODEL), const),
            pl.BlockSpec((CONV_WIDTH, D_MODEL), const),
            pl.BlockSpec((RET_V_WIDTH, D_MODEL), const),
            pl.BlockSpec((D_MODEL, D_MODEL), const),
        ],
        out_specs=pl.BlockSpec((tm, D_MODEL), lambda i: (i, 0)),
        out_shape=jax.ShapeDtypeStruct((n, D_MODEL), F32),
        scratch_shapes=[pltpu.VMEM((tm + CONV_HALO, CONV_WIDTH), F32)],
        compiler_params=_cparams(("arbitrary",)),
        name="merge",
    )(x2, y_attn, proj, proj, proj, y_ret, proj, proj, proj, conv_w, conv_b, wa, wc, wr, wo)


FFN_CHUNKS = D_FF // FFN_TF
FFN_STEPS = FFN_CHUNKS + 1


def _swiglu_pipelined(c, h_scr, wg_ref, wu_ref, wd_ref, gu_scr, acc_scr, enable=True, beside=None):
    beside = beside or (lambda: None)
    def fold(slot):
        g, u = gu_scr[2 * slot], gu_scr[2 * slot + 1]
        a = (g * jax.nn.sigmoid(g) * u).astype(BF16)
        acc_scr[...] += jnp.dot(a, wd_ref[...].astype(BF16), preferred_element_type=F32)

    def park(slot):
        h = h_scr[...]
        gu_scr[2 * slot] = jnp.dot(h, wg_ref[...].astype(BF16), preferred_element_type=F32)
        gu_scr[2 * slot + 1] = jnp.dot(h, wu_ref[...].astype(BF16), preferred_element_type=F32)

    @pl.when(enable & (c == 0))
    def _():
        beside()
        park(0)

    for parity in (0, 1):
        @pl.when(enable & (c >= 1) & (c < FFN_CHUNKS) & (c % 2 == parity))
        def _():
            beside()
            fold(1 - parity)
            park(parity)

    @pl.when(enable & (c == FFN_CHUNKS))
    def _():
        beside()
        fold((FFN_CHUNKS - 1) % 2)


def _dense_ffn_kernel(x_ref, nw_ref, wg_ref, wu_ref, wd_ref, o_ref, h_scr, gu_scr, acc_scr):
    c = pl.program_id(1)

    @pl.when(c == 0)
    def _():
        x = x_ref[...]
        ms = jnp.mean(x * x, axis=-1, keepdims=True)
        h_scr[...] = (x * lax.rsqrt(ms + RMS_EPS) * nw_ref[...]).astype(BF16)
        acc_scr[...] = jnp.zeros_like(acc_scr)

    _swiglu_pipelined(c, h_scr, wg_ref, wu_ref, wd_ref, gu_scr, acc_scr)

    @pl.when(c == FFN_CHUNKS)
    def _():
        o_ref[...] = x_ref[...] + acc_scr[...]


def _dense_ffn(x2, norm_w, wg, wu, wd, li):
    n = x2.shape[0]
    tm, tf = FFN_TM, FFN_TF
    return pl.pallas_call(
        _dense_ffn_kernel,
        grid=(n // tm, FFN_STEPS),
        in_specs=[
            pl.BlockSpec((tm, D_MODEL), lambda i, c: (i, 0)),
            pl.BlockSpec((1, D_MODEL), lambda i, c: (0, 0)),
            pl.BlockSpec((None, D_MODEL, tf), lambda i, c: (li, 0, jnp.minimum(c, FFN_CHUNKS - 1))),
            pl.BlockSpec((None, D_MODEL, tf), lambda i, c: (li, 0, jnp.minimum(c, FFN_CHUNKS - 1))),
            pl.BlockSpec((None, tf, D_MODEL), lambda i, c: (li, jnp.maximum(c - 1, 0), 0)),
        ],
        out_specs=pl.BlockSpec((tm, D_MODEL), lambda i, c: (i, 0)),
        out_shape=jax.ShapeDtypeStruct((n, D_MODEL), F32),
        scratch_shapes=[pltpu.VMEM((tm, D_MODEL), BF16), pltpu.VMEM((4, tm, tf), F32),
                        pltpu.VMEM((tm, D_MODEL), F32)],
        compiler_params=_cparams(("arbitrary", "arbitrary")),
        name="dense_ffn",
    )(x2, norm_w, wg, wu, wd)


SLAB = D_MODEL // LANES


def _pack_rows(v, o_ref):
    rows = v.shape[0]
    for s in range(SLAB):
        o_ref[pl.ds(s, rows, stride=SLAB), :] = v[:, s * LANES:(s + 1) * LANES]


def _unpack_rows(x_ref, rows):
    return jnp.concatenate([x_ref[pl.ds(s, rows, stride=SLAB), :] for s in range(SLAB)], axis=1)


def _route_kernel(x_ref, nw_ref, wr_ref, h_ref, idx_ref, wgt_ref):
    x = x_ref[...]
    ms = jnp.mean(x * x, axis=-1, keepdims=True)
    h = x * lax.rsqrt(ms + RMS_EPS) * nw_ref[...]
    _pack_rows(h, h_ref)
    w = wr_ref[...]
    h_hi, w_hi = h.astype(BF16), w.astype(BF16)
    h_lo, w_lo = (h - h_hi.astype(F32)).astype(BF16), (w - w_hi.astype(F32)).astype(BF16)
    logits = (jnp.dot(h_hi, w_hi, preferred_element_type=F32)
              + jnp.dot(h_lo, w_hi, preferred_element_type=F32)
              + jnp.dot(h_hi, w_lo, preferred_element_type=F32))
    eid = lax.broadcasted_iota(jnp.int32, logits.shape, 1).astype(F32)
    logits = jnp.where(eid < N_EXPERTS, logits, -jnp.inf)
    m1 = jnp.max(logits, axis=-1, keepdims=True)
    i1 = jnp.min(jnp.where(logits == m1, eid, float(LANES)), axis=-1, keepdims=True)
    rest = jnp.where(eid == i1, -jnp.inf, logits)
    m2 = jnp.max(rest, axis=-1, keepdims=True)
    i2 = jnp.min(jnp.where(rest == m2, eid, float(LANES)), axis=-1, keepdims=True)
    e2 = jnp.exp(m2 - m1)
    denom = 1.0 + e2
    idx_ref[...] = jnp.where(eid == 0.0, i1, jnp.where(eid == 1.0, i2, 0.0)).astype(jnp.int32)
    wgt_ref[...] = jnp.where(eid == 0.0, 1.0 / denom, jnp.where(eid == 1.0, e2 / denom, 0.0))


def _route(x2, norm_w, w_router):
    n = x2.shape[0]
    tm = ROUTE_TM
    return pl.pallas_call(
        _route_kernel,
        grid=(n // tm,),
        in_specs=[
            pl.BlockSpec((tm, D_MODEL), lambda i: (i, 0)),
            pl.BlockSpec((1, D_MODEL), lambda i: (0, 0)),
            pl.BlockSpec((D_MODEL, LANES), lambda i: (0, 0)),
        ],
        out_specs=[
            pl.BlockSpec((tm * SLAB, LANES), lambda i: (i, 0)),
            pl.BlockSpec((tm, LANES), lambda i: (i, 0)),
            pl.BlockSpec((tm, LANES), lambda i: (i, 0)),
        ],
        out_shape=[
            jax.ShapeDtypeStruct((n * SLAB, LANES), F32),
            jax.ShapeDtypeStruct((n, LANES), jnp.int32),
            jax.ShapeDtypeStruct((n, LANES), F32),
        ],
        compiler_params=_cparams(("arbitrary",)),
        name="route",
    )(x2, norm_w, jnp.pad(w_router, ((0, 0), (0, LANES - N_EXPERTS))))


MOE_STEPS = FFN_STEPS
MOE_ROWS_PER_STEP = -(-FFN_TM // MOE_STEPS)
MOE_DMA_ROWS = MOE_ROWS_PER_STEP * MOE_STEPS
MOE_BUF_ROWS = -(-MOE_DMA_ROWS // 8) * 8
MOE_SPARE_TILES = 2


def _moe_ffn_kernel(te_ref, nv_ref, gid0_ref, gidn_ref, sid_ref, h3_ref, wg_ref, wu_ref, wd_ref,
                    y_init_ref, y_ref, gbuf, obuf, h_scr, gu_scr, acc_scr, gsem, ssem):
    t = pl.program_id(0)
    c = pl.program_id(1)
    nv = nv_ref[0]
    slot = t % 2
    other = 1 - slot
    first = c == 0

    def slab(ref, start):
        return ref.at[pl.ds(pl.multiple_of(start, SLAB), SLAB)]

    def gather(ids_ref, row, s):
        return pltpu.make_async_copy(slab(h3_ref, ids_ref[0, 0, row]), slab(gbuf.at[s], row * SLAB),
                                     gsem.at[s])

    def scatter(row, s):
        return pltpu.make_async_copy(slab(obuf.at[s], row * SLAB), slab(y_ref, sid_ref[0, 0, row]),
                                     ssem.at[s])

    def tile_gather(s):
        return pltpu.make_async_copy(h3_ref.at[pl.ds(0, MOE_DMA_ROWS * SLAB)],
                                     gbuf.at[s, pl.ds(0, MOE_DMA_ROWS * SLAB)], gsem.at[s])

    def tile_scatter(s):
        return pltpu.make_async_copy(obuf.at[s, pl.ds(0, MOE_DMA_ROWS * SLAB)],
                                     y_ref.at[pl.ds(0, MOE_DMA_ROWS * SLAB)], ssem.at[s])

    @pl.when(first & (t == 0))
    def _():
        obuf[...] = jnp.zeros_like(obuf)

        def body(r, carry):
            gather(gid0_ref, r, 0).start()
            return carry

        lax.fori_loop(0, MOE_DMA_ROWS, body, 0, unroll=MOE_STEPS)

    @pl.when(first & (t <= nv))
    def _():
        tile_gather(slot).wait()

    @pl.when(first & (t >= 1) & (t <= nv + 1))
    def _():
        tile_scatter(slot).wait()

    @pl.when(first & (t < nv))
    def _():
        h_scr[...] = _unpack_rows(gbuf.at[slot], FFN_TM).astype(BF16)
        acc_scr[...] = jnp.zeros_like(acc_scr)

    def row_dmas():
        for u in range(MOE_ROWS_PER_STEP):
            row = c * MOE_ROWS_PER_STEP + u
            gather(gidn_ref, row, other).start(priority=1)
            scatter(row, other).start(priority=1)

    _swiglu_pipelined(c, h_scr, wg_ref, wu_ref, wd_ref, gu_scr, acc_scr, enable=t < nv, beside=row_dmas)

    @pl.when(t == nv)
    def _():
        def body(u, carry):
            scatter(c * MOE_ROWS_PER_STEP + u, other).start()
            return carry

        lax.fori_loop(0, MOE_ROWS_PER_STEP, body, 0, unroll=MOE_STEPS)

    @pl.when((t < nv) & (c == pl.num_programs(1) - 1))
    def _():
        _pack_rows(acc_scr[...], obuf.at[slot])


def _moe_ffn(tile_expert, n_valid, gid, sid, h3, wg, wu, wd, li, y_rows):
    nt = gid.shape[0]
    tm, tf = FFN_TM, FFN_TF
    ids_block = (1, 1, MOE_BUF_ROWS)
    smem = pltpu.SMEM
    wspec = lambda shape, imap: pl.BlockSpec((None, None) + shape, imap)
    expert = lambda t, te, nv: te[jnp.minimum(t, nv[0] - 1)]
    last = FFN_CHUNKS - 1
    up_chunk = lambda t, c, nv: jnp.where(t < nv[0], jnp.minimum(c, last), last)
    down_chunk = lambda t, c, nv: jnp.where(t < nv[0], jnp.maximum(c - 1, 0), last)
    grid_spec = pltpu.PrefetchScalarGridSpec(
        num_scalar_prefetch=2,
        grid=(nt, MOE_STEPS),
        in_specs=[
            pl.BlockSpec(ids_block, lambda t, c, te, nv: (0, 0, 0), memory_space=smem),
            pl.BlockSpec(ids_block, lambda t, c, te, nv: (jnp.minimum(t + 1, nt - 1), 0, 0),
                         memory_space=smem),
            pl.BlockSpec(ids_block, lambda t, c, te, nv: (t, 0, 0), memory_space=smem),
            pl.BlockSpec(memory_space=pl.ANY),
            wspec((D_MODEL, tf), lambda t, c, te, nv: (li, expert(t, te, nv), 0, up_chunk(t, c, nv))),
            wspec((D_MODEL, tf), lambda t, c, te, nv: (li, expert(t, te, nv), 0, up_chunk(t, c, nv))),
            wspec((tf, D_MODEL), lambda t, c, te, nv: (li, expert(t, te, nv), down_chunk(t, c, nv), 0)),
            pl.BlockSpec(memory_space=pl.ANY),
        ],
        out_specs=pl.BlockSpec(memory_space=pl.ANY),
        scratch_shapes=[
            pltpu.VMEM((2, MOE_BUF_ROWS * SLAB, LANES), F32),
            pltpu.VMEM((2, MOE_BUF_ROWS * SLAB, LANES), F32),
            pltpu.VMEM((tm, D_MODEL), BF16),
            pltpu.VMEM((4, tm, tf), F32),
            pltpu.VMEM((tm, D_MODEL), F32),
            pltpu.SemaphoreType.DMA((2,)),
            pltpu.SemaphoreType.DMA((2,)),
        ],
    )
    return pl.pallas_call(
        _moe_ffn_kernel,
        grid_spec=grid_spec,
        out_shape=jax.ShapeDtypeStruct((y_rows * SLAB, LANES), F32),
        input_output_aliases={9: 0},
        compiler_params=_cparams(("arbitrary", "arbitrary")),
        name="moe_ffn",
    )(tile_expert, n_valid, gid * SLAB, gid * SLAB, sid * SLAB, h3, wg, wu, wd,
      jnp.zeros((y_rows * SLAB, LANES), F32))


def _combine_kernel(x_ref, y0_ref, y1_ref, w_ref, nw_ref, o_ref, *, final_norm):
    w = w_ref[...]
    rows = x_ref.shape[0]
    x = x_ref[...] + w[:, 0:1] * _unpack_rows(y0_ref, rows) + w[:, 1:2] * _unpack_rows(y1_ref, rows)
    if final_norm:
        ms = jnp.mean(x * x, axis=-1, keepdims=True)
        x = x * lax.rsqrt(ms + RMS_EPS) * nw_ref[...]
    o_ref[...] = x


def _combine(x2, y_pairs, wgt, norm_w, final_norm):
    n = x2.shape[0]
    tm = MERGE_TM
    nt = n // tm
    return pl.pallas_call(
        functools.partial(_combine_kernel, final_norm=final_norm),
        grid=(nt,),
        in_specs=[
            pl.BlockSpec((tm, D_MODEL), lambda i: (i, 0)),
            pl.BlockSpec((tm * SLAB, LANES), lambda i: (i, 0)),
            pl.BlockSpec((tm * SLAB, LANES), lambda i: (i + nt, 0)),
            pl.BlockSpec((tm, LANES), lambda i: (i, 0)),
            pl.BlockSpec((1, D_MODEL), lambda i: (0, 0)),
        ],
        out_specs=pl.BlockSpec((tm, D_MODEL), lambda i: (i, 0)),
        out_shape=jax.ShapeDtypeStruct((n, D_MODEL), F32),
        compiler_params=_cparams(("arbitrary",)),
        name="combine",
    )(x2, y_pairs, y_pairs, wgt, norm_w)


def _moe_layout(idx):
    n = idx.shape[0]
    tm = FFN_TM
    flat = idx.T.reshape(-1)
    onehot = (flat[:, None] == jnp.arange(N_EXPERTS, dtype=jnp.int32)[None, :]).astype(jnp.int32)
    cum = jnp.cumsum(onehot, axis=0)
    counts = cum[-1]
    rank = jnp.sum(cum * onehot, axis=1) - 1
    padded = ((counts + tm - 1) // tm) * tm
    ends = jnp.cumsum(padded)
    starts = ends - padded
    dest = starts[flat] + rank
    pairs = TOP_K * n
    nt = pairs // tm + N_EXPERTS + MOE_SPARE_TILES
    p = nt * tm
    tile_start = jnp.arange(nt, dtype=jnp.int32) * tm
    tile_expert = jnp.minimum(jnp.sum((ends[None, :] <= tile_start[:, None]).astype(jnp.int32), axis=1),
                              N_EXPERTS - 1)
    n_valid = (ends[-1] // tm).astype(jnp.int32).reshape(1)
    real_before = jnp.cumsum(counts)[tile_expert]
    pos = jnp.arange(p, dtype=jnp.int32).reshape(nt, tm)
    pad_row = (pairs + pos - real_before[:, None]).reshape(p)
    row_pair = pad_row.at[dest].set(jnp.arange(pairs, dtype=jnp.int32), unique_indices=True,
                                    mode="promise_in_bounds")
    row_token = jnp.where(row_pair < pairs, row_pair % n, 0)
    tail = MOE_BUF_ROWS - tm
    gid = jnp.pad(row_token.reshape(nt, tm), ((0, 0), (0, tail))).reshape(nt, 1, MOE_BUF_ROWS)
    first_spare = p + jnp.arange(MOE_BUF_ROWS, dtype=jnp.int32)[None, :]
    tails = p + MOE_BUF_ROWS + jnp.arange(nt * tail, dtype=jnp.int32).reshape(nt, tail)
    sid = jnp.concatenate([row_pair.reshape(nt, tm), tails], axis=1)
    sid = jnp.concatenate([first_spare, sid], axis=0).reshape(nt + 1, 1, MOE_BUF_ROWS)
    y_rows = p + MOE_BUF_ROWS + nt * tail
    return gid, sid, tile_expert, n_valid, y_rows


def _moe(x2, norm_w, w_router, wg, wu, wd, li, final_norm_w, final_norm):
    h3, idx, wgt = _route(x2, norm_w, w_router)
    gid, sid, tile_expert, n_valid, y_rows = _moe_layout(idx[:, :TOP_K])
    y = _moe_ffn(tile_expert, n_valid, gid, sid, h3, wg, wu, wd, li, y_rows)
    return _combine(x2, y, wgt, final_norm_w, final_norm)


def _rope_tables(seq):
    inv_freq = 1.0 / (ROPE_THETA ** (jnp.arange(0, HEAD_DIM, 2, dtype=F32) / HEAD_DIM))
    ang = jnp.arange(seq, dtype=F32)[:, None] * inv_freq[None, :]
    cos, sin = jnp.cos(ang), jnp.sin(ang)
    cos_t = jnp.tile(cos, (1, LANES // (HEAD_DIM // 2)))
    sin_t = jnp.tile(jnp.concatenate([-sin, sin], axis=1), (1, LANES // HEAD_DIM))
    return cos_t, sin_t


def kernel(x, mix_norm_w, w_in, conv_w, conv_b, ret_norm_w, w_br_attn, w_br_conv, w_br_ret, w_out,
           ffn_norm_w, dense_w_gate, dense_w_up, dense_w_down, moe_router, moe_w_gate, moe_w_up,
           moe_w_down, final_norm_w):
    batch, seq, d = x.shape
    depth = w_in.shape[0]
    assert d == D_MODEL and seq % PROJ_TM == 0 and depth % 2 == 0
    n = batch * seq
    cos_t, sin_t = _rope_tables(seq)
    ret_tables = _retention_tables()
    x2 = x.reshape(n, d)
    for layer in range(depth):
        proj, vt = _inproj(x2, mix_norm_w[layer].reshape(1, d), w_in, layer, cos_t, sin_t, batch, seq)
        y_attn = _moba(proj, vt, batch, seq)
        y_ret = _retention(proj, ret_norm_w[layer].reshape(1, RET_V_WIDTH), ret_tables, batch, seq)
        x2 = _merge(x2, y_attn, proj, y_ret, conv_w[layer], conv_b[layer].reshape(1, CONV_WIDTH),
                    w_br_attn[layer].astype(BF16), w_br_conv[layer].astype(BF16),
                    w_br_ret[layer].astype(BF16), w_out[layer].astype(BF16), seq)
        i = layer // 2
        nw = ffn_norm_w[layer].reshape(1, d)
        if layer % 2 == 0:
            x2 = _dense_ffn(x2, nw, dense_w_gate, dense_w_up, dense_w_down, i)
        else:
            last = layer == depth - 1
            x2 = _moe(x2, nw, moe_router[i], moe_w_gate, moe_w_up, moe_w_down, i,
                      final_norm_w.reshape(1, d), last)
    return x2.reshape(batch, seq, d)
```

```python
import functools
import math

import jax
import jax.numpy as jnp
import numpy as np
from jax import lax
from jax.experimental import pallas as pl
from jax.experimental.pallas import tpu as pltpu

F32 = jnp.float32
BF16 = jnp.bfloat16

D_MODEL = 1024
HEAD_DIM = 64
ROPE_THETA = 10000.0
RMS_EPS = 1e-6

ATTN_HEADS = 8
ATTN_WIDTH = ATTN_HEADS * HEAD_DIM
MOBA_BLOCK = 256
MOBA_TOPK = 3
NEG_INF = -1e30
MOBA_HEADS_PER_STEP = 4
VT_ONES = 16
VT_ROWS = HEAD_DIM + VT_ONES
QK_SCALE_LOG2 = HEAD_DIM ** -0.5 * math.log2(math.e)

CONV_WIDTH = 512
CONV_KERNEL = 3

RET_HEADS = 4
RET_QK_WIDTH = RET_HEADS * HEAD_DIM
RET_V_DIM = 2 * HEAD_DIM
RET_V_WIDTH = RET_HEADS * RET_V_DIM
RET_TILE = 256

D_FF = 3584
N_EXPERTS = 8
TOP_K = 2

LANES = 128
IN_PROJ_WIDTH = 7680
COL_AQ, COL_AK, COL_AV = 0, 512, 1024
COL_CB, COL_CC, COL_CH = 1536, 2048, 2560
COL_RV, COL_RG = 3072, 3584
COL_GATES = 4096
COL_RQ, COL_RK = 7168, 7424

PROJ_TM = 2048
PROJ_TN = 512
MERGE_TM = 512
FFN_TM = 1024
FFN_TF = 512
ROUTE_TM = 1024
VMEM_LIMIT = 56 * 1024 * 1024


def _cparams(sem):
    return pltpu.CompilerParams(dimension_semantics=sem, vmem_limit_bytes=VMEM_LIMIT)


def _rope(acc, cos, sin_signed):
    lane = lax.broadcasted_iota(jnp.int32, (1, LANES), 1)
    first_half = (lane % HEAD_DIM) < (HEAD_DIM // 2)
    outs = []
    for g in range(acc.shape[1] // LANES):
        blk = acc[:, g * LANES:(g + 1) * LANES]
        packed = blk.astype(BF16)
        partner = jnp.where(first_half,
                            pltpu.roll(packed, LANES - HEAD_DIM // 2, 1),
                            pltpu.roll(packed, HEAD_DIM // 2, 1)).astype(F32)
        outs.append(blk * cos + partner * sin_signed)
    return jnp.concatenate(outs, axis=1)


def _inproj_kernel(x_ref, nw_ref, w_ref, cos_ref, sin_ref, proj_ref, vt_ref, h_scr):
    j = pl.program_id(1)

    @pl.when(j == 0)
    def _():
        x = x_ref[...]
        ms = jnp.mean(x * x, axis=-1, keepdims=True)
        h_scr[...] = (x * lax.rsqrt(ms + RMS_EPS) * nw_ref[...]).astype(BF16)

    acc = jnp.dot(h_scr[...], w_ref[...].astype(BF16), preferred_element_type=F32)
    jq, jk, jv, jr = COL_AQ // PROJ_TN, COL_AK // PROJ_TN, COL_AV // PROJ_TN, COL_RQ // PROJ_TN
    scale = HEAD_DIM ** -0.5

    @pl.when(j == jq)
    def _():
        proj_ref[...] = (_rope(acc, cos_ref[...], sin_ref[...]) * QK_SCALE_LOG2).astype(BF16)

    @pl.when(j == jk)
    def _():
        proj_ref[...] = _rope(acc, cos_ref[...], sin_ref[...]).astype(BF16)

    @pl.when(j == jv)
    def _():
        proj_ref[...] = acc.astype(BF16)
        ones = jnp.ones((VT_ONES, MOBA_BLOCK), F32)
        for c in range(PROJ_TM // MOBA_BLOCK):
            v_t = acc[c * MOBA_BLOCK:(c + 1) * MOBA_BLOCK, :].T
            rows = []
            for h in range(ATTN_HEADS):
                rows += [v_t[h * HEAD_DIM:(h + 1) * HEAD_DIM], ones]
            vt_ref[c] = jnp.concatenate(rows, axis=0).astype(BF16)

    @pl.when(j == jr)
    def _():
        r = _rope(acc, cos_ref[...], sin_ref[...])
        col = lax.broadcasted_iota(jnp.int32, (1, PROJ_TN), 1)
        r = r * jnp.where(col >= RET_QK_WIDTH, scale, 1.0)
        proj_ref[...] = r.astype(BF16)

    @pl.when((j != jq) & (j != jk) & (j != jv) & (j != jr))
    def _():
        proj_ref[...] = acc.astype(BF16)


def _inproj_src_block(j):
    first_moved = COL_RV // PROJ_TN
    last = IN_PROJ_WIDTH // PROJ_TN - 1
    return jnp.where(j < first_moved, j, jnp.where(j == last, first_moved, j + 1))


def _inproj(x2, norm_w, w_in, layer, cos_t, sin_t, batch, seq):
    n = x2.shape[0]
    nst = seq // PROJ_TM
    nblk = PROJ_TM // MOBA_BLOCK
    return pl.pallas_call(
        _inproj_kernel,
        grid=(n // PROJ_TM, IN_PROJ_WIDTH // PROJ_TN),
        in_specs=[
            pl.BlockSpec((PROJ_TM, D_MODEL), lambda i, j: (i, 0)),
            pl.BlockSpec((1, D_MODEL), lambda i, j: (0, 0)),
            pl.BlockSpec((None, D_MODEL, PROJ_TN), lambda i, j: (layer, 0, _inproj_src_block(j))),
            pl.BlockSpec((PROJ_TM, LANES), lambda i, j: (i % nst, 0)),
            pl.BlockSpec((PROJ_TM, LANES), lambda i, j: (i % nst, 0)),
        ],
        out_specs=[
            pl.BlockSpec((PROJ_TM, PROJ_TN), lambda i, j: (i, j)),
            pl.BlockSpec((None, nblk, ATTN_HEADS * VT_ROWS, MOBA_BLOCK),
                         lambda i, j: (i // nst, i % nst, 0, 0)),
        ],
        out_shape=[
            jax.ShapeDtypeStruct((n, IN_PROJ_WIDTH), BF16),
            jax.ShapeDtypeStruct((batch, seq // MOBA_BLOCK, ATTN_HEADS * VT_ROWS, MOBA_BLOCK), BF16),
        ],
        scratch_shapes=[pltpu.VMEM((PROJ_TM, D_MODEL), BF16)],
        compiler_params=_cparams(("arbitrary", "arbitrary")),
        name="inproj",
    )(x2, norm_w, w_in, cos_t, sin_t)


_NT = (((1,), (1,)), ((), ()))


def _moba_kernel(q_ref, k_ref, vt_ref, o_ref, km_scr, ka_scr, sel_scr, qt_scr, s_scr, m_scr, acc_scr,
                 *, nb):
    nh = MOBA_HEADS_PER_STEP
    blk = MOBA_BLOCK
    tq = 2 * blk
    lane = lax.broadcasted_iota(jnp.int32, (1, LANES), 1)
    crow = lax.broadcasted_iota(jnp.int32, (LANES, 1), 0)

    def group(ref, rows, h):
        g = h // 2
        return ref[rows, g * LANES:(g + 1) * LANES]

    def prep(i, c):
        rows = pl.ds(pl.multiple_of(i * blk, blk), blk)
        for h in range(nh):
            kb, hh = group(k_ref, rows, h), h % 2
            if hh == 0:
                km_scr[h // 2, pl.ds(i, 1), :] = jnp.sum(kb.astype(F32), axis=0, keepdims=True) * (1.0 / blk)
            onehot = jnp.where(lane == HEAD_DIM * (1 - hh) + i, 1.0, 0.0).astype(BF16)
            ka_scr[h, rows, :] = jnp.where((lane // HEAD_DIM) == hh, kb, onehot)
        return c

    lax.fori_loop(0, nb, prep, 0)

    causal = (lax.broadcasted_iota(jnp.int32, (blk, blk), 0)
              <= lax.broadcasted_iota(jnp.int32, (blk, blk), 1))
    blk_id = lax.broadcasted_iota(jnp.int32, (nb, tq), 0)
    q_half = (lax.broadcasted_iota(jnp.int32, (nb, tq), 1) >= blk).astype(jnp.int32)

    def pv(t, h, p):
        rows = slice(h * VT_ROWS, (h + 1) * VT_ROWS)
        pb = p.astype(BF16)
        return (jnp.dot(vt_ref[2 * t, rows, :], pb[0:blk], preferred_element_type=F32)
                + jnp.dot(vt_ref[2 * t + 1, rows, :], pb[blk:tq], preferred_element_type=F32))

    def qtile(jt, c):
        row0 = pl.multiple_of(jt * tq, tq)
        q_ts =[group(q_ref, pl.ds(row0, tq), h).astype(F32).T for h in range(0, nh, 2)]
        for h in range(nh):
            hh = h % 2
            q_t = q_ts[h // 2]
            k_diag = group(k_ref, pl.ds(row0, tq), h)
            q_m = jnp.where((crow // HEAD_DIM) == hh, q_t, 0.0)
            q_mb = q_m.astype(BF16)
            km = jnp.where((lane // HEAD_DIM) == hh, km_scr[h // 2], 0.0)
            km_hi = km.astype(BF16)
            km_lo = (km - km_hi.astype(F32)).astype(BF16)
            gate = (jnp.dot(km_hi, q_mb, preferred_element_type=F32)
                    + jnp.dot(km_lo, q_mb, preferred_element_type=F32))
            past = blk_id < 2 * jt + q_half
            gate = jnp.where(past, gate, -jnp.inf)
            rank = jnp.zeros((nb, tq), jnp.int32)
            for ip in range(nb):
                gi = gate[ip:ip + 1, :]
                beats = (gi > gate) | ((gi == gate) & (blk_id > ip))
                rank = rank + beats.astype(jnp.int32)
            sel = (rank < MOBA_TOPK) & past
            sel_scr[h] = sel.astype(F32)
            bias = jnp.where(sel, 0.0, NEG_INF)
            spare = HEAD_DIM * (1 - hh)
            pieces = [jnp.zeros((spare, tq), F32)] if spare else []
            pieces += [bias, jnp.zeros((LANES - spare - nb, tq), F32)]
            qt_scr[h] = (q_m + jnp.concatenate(pieces, axis=0)).astype(BF16)

            s_top = jnp.dot(k_diag[0:blk], q_mb, preferred_element_type=F32)
            s11 = jnp.where(causal, s_top[:, 0:blk], NEG_INF)
            s12 = jnp.where(sel_scr[h, pl.ds(2 * jt, 1), blk:tq] > 0.0, s_top[:, blk:tq], NEG_INF)
            s22 = jnp.where(causal, jnp.dot(k_diag[blk:tq], q_mb[:, blk:tq], preferred_element_type=F32),
                            NEG_INF)
            m_a = jnp.max(s11, axis=0, keepdims=True)
            m_b = jnp.maximum(jnp.max(s12, axis=0, keepdims=True), jnp.max(s22, axis=0, keepdims=True))
            s_scr[2 * h, 0:blk, 0:blk] = s11
            s_scr[2 * h, 0:blk, blk:tq] = s12
            s_scr[2 * h, blk:tq, 0:blk] = jnp.full((blk, blk), NEG_INF, F32)
            s_scr[2 * h, blk:tq, blk:tq] = s22
            m_scr[2 * h] = jnp.concatenate([m_a, m_b], axis=1)
            m_scr[2 * h + 1] = m_scr[2 * h]
            acc_scr[h] = jnp.zeros((VT_ROWS, tq), F32)

        def park(t, h, slot):
            s = jnp.dot(ka_scr[h, pl.ds(pl.multiple_of(t * tq, tq), tq), :], qt_scr[h],
                        preferred_element_type=F32)
            s_scr[2 * h + slot] = s
            m_scr[2 * h + 1] = jnp.maximum(m_scr[2 * h + 1], jnp.max(s, axis=0, keepdims=True))

        def consume(k, h, slot):
            m_old, m_new = m_scr[2 * h], m_scr[2 * h + 1]
            acc_scr[h] = (jnp.exp2(m_old - m_new) * acc_scr[h]
                          + pv(jnp.where(k == 0, jt, k - 1), h, jnp.exp2(s_scr[2 * h + slot] - m_new)))
            m_scr[2 * h] = m_new

        def step(k, slot):
            for h in range(nh):
                consume(k, h, slot)
                park(k, h, 1 - slot)

        def run(trips, body):
            def wrapped(i, c):
                body(i)
                return c
            lax.fori_loop(0, trips, wrapped, 0)

        run(jt // 2, lambda u: (step(2 * u, 0), step(2 * u + 1, 1)))
        run(jt % 2, lambda i: step(jt - 1, 0))
        run(jt % 2, lambda i: [consume(jt, h, 1) for h in range(nh)])
        run(1 - jt % 2, lambda i: [consume(jt, h, 0) for h in range(nh)])
        out_t = jnp.concatenate([acc_scr[h, 0:HEAD_DIM, :] / acc_scr[h, HEAD_DIM:HEAD_DIM + 1, :]
                                 for h in range(nh)], axis=0)
        o_ref[pl.ds(row0, tq), :] = out_t.T.astype(BF16)
        return c

    lax.fori_loop(0, nb // 2, qtile, 0)


def _moba(proj, vt, batch, seq):
    nb = seq // MOBA_BLOCK
    assert nb % 8 == 0 and nb <= HEAD_DIM
    nh = MOBA_HEADS_PER_STEP
    width = nh * HEAD_DIM
    return pl.pallas_call(
        functools.partial(_moba_kernel, nb=nb),
        grid=(batch, ATTN_HEADS // nh),
        in_specs=[
            pl.BlockSpec((seq, width), lambda b, p: (b, COL_AQ // width + p)),
            pl.BlockSpec((seq, width), lambda b, p: (b, COL_AK // width + p)),
            pl.BlockSpec((None, nb, nh * VT_ROWS, MOBA_BLOCK), lambda b, p: (b, 0, p, 0)),
        ],
        out_specs=pl.BlockSpec((seq, width), lambda b, p: (b, p)),
        out_shape=jax.ShapeDtypeStruct((batch * seq, ATTN_WIDTH), BF16),
        scratch_shapes=[pltpu.VMEM((nh // 2, nb, LANES), F32), pltpu.VMEM((nh, seq, LANES), BF16),
                        pltpu.VMEM((nh, nb, 2 * MOBA_BLOCK), F32),
                        pltpu.VMEM((nh, LANES, 2 * MOBA_BLOCK), BF16),
                        pltpu.VMEM((2 * nh, 2 * MOBA_BLOCK, 2 * MOBA_BLOCK), F32),
                        pltpu.VMEM((2 * nh, 1, 2 * MOBA_BLOCK), F32),
                        pltpu.VMEM((nh, VT_ROWS, 2 * MOBA_BLOCK), F32)],
        compiler_params=_cparams(("arbitrary", "arbitrary")),
        name="moba",
    )(proj, proj, vt)


def _ret_log_gamma():
    return [math.log1p(-(2.0 ** (-5.0 - h))) for h in range(RET_HEADS)]


def _retention_tables():
    c = RET_TILE
    lg = np.array(_ret_log_gamma(), np.float64)
    n = np.arange(c, dtype=np.float64)
    diff = n[:, None] - n[None, :]
    decay = np.where(diff[None] >= 0, np.exp(np.maximum(diff, 0.0)[None] * lg[:, None, None]), 0.0)
    head_of_lane = np.arange(RET_QK_WIDTH) // HEAD_DIM
    xi = np.exp((n + 1.0)[:, None] * lg[head_of_lane][None, :])
    zeta = np.exp((c - 1.0 - n)[:, None] * lg[head_of_lane][None, :])
    chunk_decay = np.exp(c * lg[head_of_lane])[:, None]
    return (jnp.asarray(decay, F32), jnp.asarray(xi, F32), jnp.asarray(zeta, F32),
            jnp.asarray(np.broadcast_to(chunk_decay, (RET_QK_WIDTH, LANES)), F32))


def _retention_kernel(q_ref, k_ref, v_ref, g_ref, nw_ref, decay_ref, xi_ref, zeta_ref, cd_ref,
                      o_ref, state_scr):
    @pl.when(pl.program_id(1) == 0)
    def _():
        state_scr[...] = jnp.zeros_like(state_scr)

    lane = lax.broadcasted_iota(jnp.int32, (1, LANES), 1)
    srow = lax.broadcasted_iota(jnp.int32, (LANES, 1), 0)
    for pr in range(RET_HEADS // 2):
        cols = slice(pr * LANES, (pr + 1) * LANES)
        q = q_ref[:, cols]
        k = k_ref[:, cols]
        state = state_scr[cols, :]
        state_bf = state.astype(BF16)
        q_xi = (q.astype(F32) * xi_ref[:, cols]).astype(BF16)
        kz_t = (k.astype(F32) * zeta_ref[:, cols]).T.astype(BF16)
        upd = []
        for hh in range(2):
            h = 2 * pr + hh
            hmask = (lane // HEAD_DIM) == hh
            vcols = slice(h * RET_V_DIM, (h + 1) * RET_V_DIM)
            v = v_ref[:, vcols]
            qm = jnp.where(hmask, q, jnp.zeros_like(q))
            scores = lax.dot_general(qm, k, _NT, preferred_element_type=F32) * decay_ref[h]
            o = jnp.dot(scores.astype(BF16), v, preferred_element_type=F32)
            o = o + jnp.dot(jnp.where(hmask, q_xi, jnp.zeros_like(q_xi)), state_bf,
                            preferred_element_type=F32)
            upd.append(jnp.dot(kz_t, v, preferred_element_type=F32))
            ms = jnp.mean(o * o, axis=-1, keepdims=True)
            y = o * lax.rsqrt(ms + RMS_EPS) * nw_ref[:, vcols]
            g = g_ref[:, vcols].astype(F32)
            o_ref[:, vcols] = (g * jax.nn.sigmoid(g) * y).astype(BF16)
        state_scr[cols, :] = state * cd_ref[cols, :] + jnp.where(srow < HEAD_DIM, upd[0], upd[1])


def _retention(proj, ret_norm_w, tables, batch, seq):
    nc = seq // RET_TILE
    decay, xi, zeta, cd = tables
    row = lambda b, c: b * nc + c
    const2 = lambda b, c: (0, 0)
    return pl.pallas_call(
        _retention_kernel,
        grid=(batch, nc),
        in_specs=[
            pl.BlockSpec((RET_TILE, RET_QK_WIDTH), lambda b, c: (row(b, c), COL_RQ // RET_QK_WIDTH)),
            pl.BlockSpec((RET_TILE, RET_QK_WIDTH), lambda b, c: (row(b, c), COL_RK // RET_QK_WIDTH)),
            pl.BlockSpec((RET_TILE, RET_V_WIDTH), lambda b, c: (row(b, c), COL_RV // RET_V_WIDTH)),
            pl.BlockSpec((RET_TILE, RET_V_WIDTH), lambda b, c: (row(b, c), COL_RG // RET_V_WIDTH)),
            pl.BlockSpec((1, RET_V_WIDTH), const2),
            pl.BlockSpec((RET_HEADS, RET_TILE, RET_TILE), lambda b, c: (0, 0, 0)),
            pl.BlockSpec((RET_TILE, RET_QK_WIDTH), const2),
            pl.BlockSpec((RET_TILE, RET_QK_WIDTH), const2),
            pl.BlockSpec((RET_QK_WIDTH, LANES), const2),
        ],
        out_specs=pl.BlockSpec((RET_TILE, RET_V_WIDTH), lambda b, c: (row(b, c), 0)),
        out_shape=jax.ShapeDtypeStruct((batch * seq, RET_V_WIDTH), BF16),
        scratch_shapes=[pltpu.VMEM((RET_QK_WIDTH, RET_V_DIM), F32)],
        compiler_params=_cparams(("arbitrary", "arbitrary")),
        name="retention",
    )(proj, proj, proj, proj, ret_norm_w, decay, xi, zeta, cd)


CONV_HALO = 8


def _merge_kernel(x_ref, ya_ref, cb_ref, cc_ref, ch_ref, yr_ref, ga_ref, gc_ref, gr_ref, cw_ref,
                  cbias_ref, wa_ref, wc_ref, wr_ref, wo_ref, o_ref, u_scr, *, tiles_per_seq):
    i = pl.program_id(0)
    tm = x_ref.shape[0]

    @pl.when(i % tiles_per_seq == 0)
    def _():
        u_scr[0:CONV_HALO, :] = jnp.zeros((CONV_HALO, CONV_WIDTH), F32)

    u_scr[CONV_HALO:CONV_HALO + tm, :] = cc_ref[...].astype(F32) * ch_ref[...].astype(F32)
    conv = (cw_ref[2:3, :] * u_scr[CONV_HALO:CONV_HALO + tm, :]
            + cw_ref[1:2, :] * u_scr[CONV_HALO - 1:CONV_HALO - 1 + tm, :]
            + cw_ref[0:1, :] * u_scr[CONV_HALO - 2:CONV_HALO - 2 + tm, :]
            + cbias_ref[...])
    y_conv = (cb_ref[...].astype(F32) * conv).astype(BF16)
    u_scr[0:CONV_HALO, :] = u_scr[tm:tm + CONV_HALO, :]

    def gate(g_ref):
        return jax.nn.sigmoid(g_ref[...].astype(F32))

    merged = gate(ga_ref) * jnp.dot(ya_ref[...], wa_ref[...], preferred_element_type=F32)
    merged = merged + gate(gc_ref) * jnp.dot(y_conv, wc_ref[...], preferred_element_type=F32)
    merged = merged + gate(gr_ref) * jnp.dot(yr_ref[...], wr_ref[...], preferred_element_type=F32)
    o_ref[...] = x_ref[...] + jnp.dot(merged.astype(BF16), wo_ref[...], preferred_element_type=F32)


def _merge(x2, y_attn, proj, y_ret, conv_w, conv_b, wa, wc, wr, wo, seq):
    n = x2.shape[0]
    tm = MERGE_TM
    const = lambda i: (0, 0)
    wide = lambda c: pl.BlockSpec((tm, CONV_WIDTH), lambda i: (i, c // CONV_WIDTH))
    gate_spec = lambda b: pl.BlockSpec((tm, D_MODEL), lambda i: (i, COL_GATES // D_MODEL + b))
    return pl.pallas_call(
        functools.partial(_merge_kernel, tiles_per_seq=seq // tm),
        grid=(n // tm,),
        in_specs=[
            pl.BlockSpec((tm, D_MODEL), lambda i: (i, 0)),
            pl.BlockSpec((tm, ATTN_WIDTH), lambda i: (i, 0)),
            wide(COL_CB), wide(COL_CC), wide(COL_CH),
            pl.BlockSpec((tm, RET_V_WIDTH), lambda i: (i, 0)),
            gate_spec(0), gate_spec(1), gate_spec(2),
            pl.BlockSpec((CONV_KERNEL, CONV_WIDTH), const),
            pl.BlockSpec((1, CONV_WIDTH), const),
            pl.BlockSpec((ATTN_WIDTH, D_MODEL), const),
            pl.BlockSpec((CONV_WIDTH, D_MODEL), const),
            pl.BlockSpec((RET_V_WIDTH, D_MODEL), const),
            pl.BlockSpec((D_MODEL, D_MODEL), const),
        ],
        out_specs=pl.BlockSpec((tm, D_MODEL), lambda i: (i, 0)),
        out_shape=jax.ShapeDtypeStruct((n, D_MODEL), F32),
        scratch_shapes=[pltpu.VMEM((tm + CONV_HALO, CONV_WIDTH), F32)],
        compiler_params=_cparams(("arbitrary",)),
        name="merge",
    )(x2, y_attn, proj, proj, proj, y_ret, proj, proj, proj, conv_w, conv_b, wa, wc, wr, wo)


FFN_CHUNKS = D_FF // FFN_TF
FFN_STEPS = FFN_CHUNKS + 1


def _swiglu_pipelined(c, h_scr, wg_ref, wu_ref, wd_ref, gu_scr, acc_scr, enable=True, beside=None):
    beside = beside or (lambda: None)
    def fold(slot):
        g, u = gu_scr[2 * slot], gu_scr[2 * slot + 1]
        a = (g * jax.nn.sigmoid(g) * u).astype(BF16)
        acc_scr[...] += jnp.dot(a, wd_ref[...].astype(BF16), preferred_element_type=F32)

    def park(slot):
        h = h_scr[...]
        gu_scr[2 * slot] = jnp.dot(h, wg_ref[...].astype(BF16), preferred_element_type=F32)
        gu_scr[2 * slot + 1] = jnp.dot(h, wu_ref[...].astype(BF16), preferred_element_type=F32)

    @pl.when(enable & (c == 0))
    def _():
        beside()
        park(0)

    for parity in (0, 1):
        @pl.when(enable & (c >= 1) & (c < FFN_CHUNKS) & (c % 2 == parity))
        def _():
            beside()
            fold(1 - parity)
            park(parity)

    @pl.when(enable & (c == FFN_CHUNKS))
    def _():
        beside()
        fold((FFN_CHUNKS - 1) % 2)


def _dense_ffn_kernel(x_ref, nw_ref, wg_ref, wu_ref, wd_ref, o_ref, h_scr, acc_scr):
    c = pl.program_id(1)

    @pl.when(c == 0)
    def _():
        x = x_ref[...]
        ms = jnp.mean(x * x, axis=-1, keepdims=True)
        h_scr[...] = (x * lax.rsqrt(ms + RMS_EPS) * nw_ref[...]).astype(BF16)
        acc_scr[...] = jnp.zeros_like(acc_scr)

    h = h_scr[...]
    g = jnp.dot(h, wg_ref[...].astype(BF16), preferred_element_type=F32)
    u = jnp.dot(h, wu_ref[...].astype(BF16), preferred_element_type=F32)
    a = (g * jax.nn.sigmoid(g) * u).astype(BF16)
    acc_scr[...] += jnp.dot(a, wd_ref[...].astype(BF16), preferred_element_type=F32)

    @pl.when(c == FFN_CHUNKS - 1)
    def _():
        o_ref[...] = x_ref[...] + acc_scr[...]


def _dense_ffn(x2, norm_w, wg, wu, wd, li):
    n = x2.shape[0]
    tm, tf = FFN_TM, FFN_TF
    return pl.pallas_call(
        _dense_ffn_kernel,
        grid=(n // tm, FFN_CHUNKS),
        in_specs=[
            pl.BlockSpec((tm, D_MODEL), lambda i, c: (i, 0)),
            pl.BlockSpec((1, D_MODEL), lambda i, c: (0, 0)),
            pl.BlockSpec((None, D_MODEL, tf), lambda i, c: (li, 0, c)),
            pl.BlockSpec((None, D_MODEL, tf), lambda i, c: (li, 0, c)),
            pl.BlockSpec((None, tf, D_MODEL), lambda i, c: (li, c, 0)),
        ],
        out_specs=pl.BlockSpec((tm, D_MODEL), lambda i, c: (i, 0)),
        out_shape=jax.ShapeDtypeStruct((n, D_MODEL), F32),
        scratch_shapes=[pltpu.VMEM((tm, D_MODEL), BF16), pltpu.VMEM((tm, D_MODEL), F32)],
        compiler_params=_cparams(("arbitrary", "arbitrary")),
        name="dense_ffn",
    )(x2, norm_w, wg, wu, wd)


SLAB = D_MODEL // LANES


def _pack_rows(v, o_ref):
    rows = v.shape[0]
    for s in range(SLAB):
        o_ref[pl.ds(s, rows, stride=SLAB), :] = v[:, s * LANES:(s + 1) * LANES]


def _unpack_rows(x_ref, rows):
    return jnp.concatenate([x_ref[pl.ds(s, rows, stride=SLAB), :] for s in range(SLAB)], axis=1)


def _route_kernel(x_ref, nw_ref, wr_ref, h_ref, idx_ref, wgt_ref):
    x = x_ref[...]
    ms = jnp.mean(x * x, axis=-1, keepdims=True)
    h = x * lax.rsqrt(ms + RMS_EPS) * nw_ref[...]
    _pack_rows(h, h_ref)
    w = wr_ref[...]
    h_hi, w_hi = h.astype(BF16), w.astype(BF16)
    h_lo, w_lo = (h - h_hi.astype(F32)).astype(BF16), (w - w_hi.astype(F32)).astype(BF16)
    logits = (jnp.dot(h_hi, w_hi, preferred_element_type=F32)
              + jnp.dot(h_lo, w_hi, preferred_element_type=F32)
              + jnp.dot(h_hi, w_lo, preferred_element_type=F32))
    eid = lax.broadcasted_iota(jnp.int32, logits.shape, 1).astype(F32)
    logits = jnp.where(eid < N_EXPERTS, logits, -jnp.inf)
    m1 = jnp.max(logits, axis=-1, keepdims=True)
    i1 = jnp.min(jnp.where(logits == m1, eid, float(LANES)), axis=-1, keepdims=True)
    rest = jnp.where(eid == i1, -jnp.inf, logits)
    m2 = jnp.max(rest, axis=-1, keepdims=True)
    i2 = jnp.min(jnp.where(rest == m2, eid, float(LANES)), axis=-1, keepdims=True)
    e2 = jnp.exp(m2 - m1)
    denom = 1.0 + e2
    idx_ref[...] = jnp.where(eid == 0.0, i1, jnp.where(eid == 1.0, i2, 0.0)).astype(jnp.int32)
    wgt_ref[...] = jnp.where(eid == 0.0, 1.0 / denom, jnp.where(eid == 1.0, e2 / denom, 0.0))


def _route(x2, norm_w, w_router):
    n = x2.shape[0]
    tm = ROUTE_TM
    return pl.pallas_call(
        _route_kernel,
        grid=(n // tm,),
        in_specs=[
            pl.BlockSpec((tm, D_MODEL), lambda i: (i, 0)),
            pl.BlockSpec((1, D_MODEL), lambda i: (0, 0)),
            pl.BlockSpec((D_MODEL, LANES), lambda i: (0, 0)),
        ],
        out_specs=[
            pl.BlockSpec((tm * SLAB, LANES), lambda i: (i, 0)),
            pl.BlockSpec((tm, LANES), lambda i: (i, 0)),
            pl.BlockSpec((tm, LANES), lambda i: (i, 0)),
        ],
        out_shape=[
            jax.ShapeDtypeStruct((n * SLAB, LANES), F32),
            jax.ShapeDtypeStruct((n, LANES), jnp.int32),
            jax.ShapeDtypeStruct((n, LANES), F32),
        ],
        compiler_params=_cparams(("arbitrary",)),
        name="route",
    )(x2, norm_w, jnp.pad(w_router, ((0, 0), (0, LANES - N_EXPERTS))))


MOE_STEPS = FFN_STEPS
MOE_ROWS_PER_STEP = -(-FFN_TM // MOE_STEPS)
MOE_DMA_ROWS = MOE_ROWS_PER_STEP * MOE_STEPS
MOE_BUF_ROWS = -(-MOE_DMA_ROWS // 8) * 8
MOE_SPARE_TILES = 2


def _moe_ffn_kernel(te_ref, nv_ref, gid0_ref, gidn_ref, sid_ref, h3_ref, wg_ref, wu_ref, wd_ref,
                    y_init_ref, y_ref, gbuf, obuf, h_scr, gu_scr, acc_scr, gsem, ssem):
    t = pl.program_id(0)
    c = pl.program_id(1)
    nv = nv_ref[0]
    slot = t % 2
    other = 1 - slot
    first = c == 0

    def slab(ref, start):
        return ref.at[pl.ds(pl.multiple_of(start, SLAB), SLAB)]

    def gather(ids_ref, row, s):
        return pltpu.make_async_copy(slab(h3_ref, ids_ref[0, 0, row]), slab(gbuf.at[s], row * SLAB),
                                     gsem.at[s])

    def scatter(row, s):
        return pltpu.make_async_copy(slab(obuf.at[s], row * SLAB), slab(y_ref, sid_ref[0, 0, row]),
                                     ssem.at[s])

    def tile_gather(s):
        return pltpu.make_async_copy(h3_ref.at[pl.ds(0, MOE_DMA_ROWS * SLAB)],
                                     gbuf.at[s, pl.ds(0, MOE_DMA_ROWS * SLAB)], gsem.at[s])

    def tile_scatter(s):
        return pltpu.make_async_copy(obuf.at[s, pl.ds(0, MOE_DMA_ROWS * SLAB)],
                                     y_ref.at[pl.ds(0, MOE_DMA_ROWS * SLAB)], ssem.at[s])

    @pl.when(first & (t == 0))
    def _():
        obuf[...] = jnp.zeros_like(obuf)

        def body(r, carry):
            gather(gid0_ref, r, 0).start()
            return carry

        lax.fori_loop(0, MOE_DMA_ROWS, body, 0, unroll=MOE_STEPS)

    @pl.when(first & (t <= nv))
    def _():
        tile_gather(slot).wait()

    @pl.when(first & (t >= 1) & (t <= nv + 1))
    def _():
        tile_scatter(slot).wait()

    @pl.when(first & (t < nv))
    def _():
        h_scr[...] = _unpack_rows(gbuf.at[slot], FFN_TM).astype(BF16)
        acc_scr[...] = jnp.zeros_like(acc_scr)

    def row_dmas():
        for u in range(MOE_ROWS_PER_STEP):
            row = c * MOE_ROWS_PER_STEP + u
            gather(gidn_ref, row, other).start(priority=1)
            scatter(row, other).start(priority=1)

    _swiglu_pipelined(c, h_scr, wg_ref, wu_ref, wd_ref, gu_scr, acc_scr, enable=t < nv, beside=row_dmas)

    @pl.when(t == nv)
    def _():
        def body(u, carry):
            scatter(c * MOE_ROWS_PER_STEP + u, other).start()
            return carry

        lax.fori_loop(0, MOE_ROWS_PER_STEP, body, 0, unroll=MOE_STEPS)

    @pl.when((t < nv) & (c == pl.num_programs(1) - 1))
    def _():
        _pack_rows(acc_scr[...], obuf.at[slot])


def _moe_ffn(tile_expert, n_valid, gid, sid, h3, wg, wu, wd, li, y_rows):
    nt = gid.shape[0]
    tm, tf = FFN_TM, FFN_TF
    ids_block = (1, 1, MOE_BUF_ROWS)
    smem = pltpu.SMEM
    wspec = lambda shape, imap: pl.BlockSpec((None, None) + shape, imap)
    expert = lambda t, te, nv: te[jnp.minimum(t, nv[0] - 1)]
    last = FFN_CHUNKS - 1
    up_chunk = lambda t, c, nv: jnp.where(t < nv[0], jnp.minimum(c, last), last)
    down_chunk = lambda t, c, nv: jnp.where(t < nv[0], jnp.maximum(c - 1, 0), last)
    grid_spec = pltpu.PrefetchScalarGridSpec(
        num_scalar_prefetch=2,
        grid=(nt, MOE_STEPS),
        in_specs=[
            pl.BlockSpec(ids_block, lambda t, c, te, nv: (0, 0, 0), memory_space=smem),
            pl.BlockSpec(ids_block, lambda t, c, te, nv: (jnp.minimum(t + 1, nt - 1), 0, 0),
                         memory_space=smem),
            pl.BlockSpec(ids_block, lambda t, c, te, nv: (t, 0, 0), memory_space=smem),
            pl.BlockSpec(memory_space=pl.ANY),
            wspec((D_MODEL, tf), lambda t, c, te, nv: (li, expert(t, te, nv), 0, up_chunk(t, c, nv))),
            wspec((D_MODEL, tf), lambda t, c, te, nv: (li, expert(t, te, nv), 0, up_chunk(t, c, nv))),
            wspec((tf, D_MODEL), lambda t, c, te, nv: (li, expert(t, te, nv), down_chunk(t, c, nv), 0)),
            pl.BlockSpec(memory_space=pl.ANY),
        ],
        out_specs=pl.BlockSpec(memory_space=pl.ANY),
        scratch_shapes=[
            pltpu.VMEM((2, MOE_BUF_ROWS * SLAB, LANES), F32),
            pltpu.VMEM((2, MOE_BUF_ROWS * SLAB, LANES), F32),
            pltpu.VMEM((tm, D_MODEL), BF16),
            pltpu.VMEM((4, tm, tf), F32),
            pltpu.VMEM((tm, D_MODEL), F32),
            pltpu.SemaphoreType.DMA((2,)),
            pltpu.SemaphoreType.DMA((2,)),
        ],
    )
    return pl.pallas_call(
        _moe_ffn_kernel,
        grid_spec=grid_spec,
        out_shape=jax.ShapeDtypeStruct((y_rows * SLAB, LANES), F32),
        input_output_aliases={9: 0},
        compiler_params=_cparams(("arbitrary", "arbitrary")),
        name="moe_ffn",
    )(tile_expert, n_valid, gid * SLAB, gid * SLAB, sid * SLAB, h3, wg, wu, wd,
      jnp.zeros((y_rows * SLAB, LANES), F32))


def _combine_kernel(x_ref, y0_ref, y1_ref, w_ref, nw_ref, o_ref, *, final_norm):
    w = w_ref[...]
    rows = x_ref.shape[0]
    x = x_ref[...] + w[:, 0:1] * _unpack_rows(y0_ref, rows) + w[:, 1:2] * _unpack_rows(y1_ref, rows)
    if final_norm:
        ms = jnp.mean(x * x, axis=-1, keepdims=True)
        x = x * lax.rsqrt(ms + RMS_EPS) * nw_ref[...]
    o_ref[...] = x


def _combine(x2, y_pairs, wgt, norm_w, final_norm):
    n = x2.shape[0]
    tm = MERGE_TM
    nt = n // tm
    return pl.pallas_call(
        functools.partial(_combine_kernel, final_norm=final_norm),
        grid=(nt,),
        in_specs=[
            pl.BlockSpec((tm, D_MODEL), lambda i: (i, 0)),
            pl.BlockSpec((tm * SLAB, LANES), lambda i: (i, 0)),
            pl.BlockSpec((tm * SLAB, LANES), lambda i: (i + nt, 0)),
            pl.BlockSpec((tm, LANES), lambda i: (i, 0)),
            pl.BlockSpec((1, D_MODEL), lambda i: (0, 0)),
        ],
        out_specs=pl.BlockSpec((tm, D_MODEL), lambda i: (i, 0)),
        out_shape=jax.ShapeDtypeStruct((n, D_MODEL), F32),
        compiler_params=_cparams(("arbitrary",)),
        name="combine",
    )(x2, y_pairs, y_pairs, wgt, norm_w)


def _moe_layout(idx):
    n = idx.shape[0]
    tm = FFN_TM
    flat = idx.T.reshape(-1)
    onehot = (flat[:, None] == jnp.arange(N_EXPERTS, dtype=jnp.int32)[None, :]).astype(jnp.int32)
    cum = jnp.cumsum(onehot, axis=0)
    counts = cum[-1]
    rank = jnp.sum(cum * onehot, axis=1) - 1
    padded = ((counts + tm - 1) // tm) * tm
    ends = jnp.cumsum(padded)
    starts = ends - padded
    dest = starts[flat] + rank
    pairs = TOP_K * n
    nt = pairs // tm + N_EXPERTS + MOE_SPARE_TILES
    p = nt * tm
    tile_start = jnp.arange(nt, dtype=jnp.int32) * tm
    tile_expert = jnp.minimum(jnp.sum((ends[None, :] <= tile_start[:, None]).astype(jnp.int32), axis=1),
                              N_EXPERTS - 1)
    n_valid = (ends[-1] // tm).astype(jnp.int32).reshape(1)
    real_before = jnp.cumsum(counts)[tile_expert]
    pos = jnp.arange(p, dtype=jnp.int32).reshape(nt, tm)
    pad_row = (pairs + pos - real_before[:, None]).reshape(p)
    row_pair = pad_row.at[dest].set(jnp.arange(pairs, dtype=jnp.int32), unique_indices=True,
                                    mode="promise_in_bounds")
    row_token = jnp.where(row_pair < pairs, row_pair % n, 0)
    tail = MOE_BUF_ROWS - tm
    gid = jnp.pad(row_token.reshape(nt, tm), ((0, 0), (0, tail))).reshape(nt, 1, MOE_BUF_ROWS)
    first_spare = p + jnp.arange(MOE_BUF_ROWS, dtype=jnp.int32)[None, :]
    tails = p + MOE_BUF_ROWS + jnp.arange(nt * tail, dtype=jnp.int32).reshape(nt, tail)
    sid = jnp.concatenate([row_pair.reshape(nt, tm), tails], axis=1)
    sid = jnp.concatenate([first_spare, sid], axis=0).reshape(nt + 1, 1, MOE_BUF_ROWS)
    y_rows = p + MOE_BUF_ROWS + nt * tail
    return gid, sid, tile_expert, n_valid, y_rows


def _moe(x2, norm_w, w_router, wg, wu, wd, li, final_norm_w, final_norm):
    h3, idx, wgt = _route(x2, norm_w, w_router)
    gid, sid, tile_expert, n_valid, y_rows = _moe_layout(idx[:, :TOP_K])
    y = _moe_ffn(tile_expert, n_valid, gid, sid, h3, wg, wu, wd, li, y_rows)
    return _combine(x2, y, wgt, final_norm_w, final_norm)


def _rope_tables(seq):
    inv_freq = 1.0 / (ROPE_THETA ** (jnp.arange(0, HEAD_DIM, 2, dtype=F32) / HEAD_DIM))
    ang = jnp.arange(seq, dtype=F32)[:, None] * inv_freq[None, :]
    cos, sin = jnp.cos(ang), jnp.sin(ang)
    cos_t = jnp.tile(cos, (1, LANES // (HEAD_DIM // 2)))
    sin_t = jnp.tile(jnp.concatenate([-sin, sin], axis=1), (1, LANES // HEAD_DIM))
    return cos_t, sin_t


def kernel(x, mix_norm_w, w_in, conv_w, conv_b, ret_norm_w, w_br_attn, w_br_conv, w_br_ret, w_out,
           ffn_norm_w, dense_w_gate, dense_w_up, dense_w_down, moe_router, moe_w_gate, moe_w_up,
           moe_w_down, final_norm_w):
    batch, seq, d = x.shape
    depth = w_in.shape[0]
    assert d == D_MODEL and seq % PROJ_TM == 0 and depth % 2 == 0
    n = batch * seq
    cos_t, sin_t = _rope_tables(seq)
    ret_tables = _retention_tables()
    x2 = x.reshape(n, d)
    for layer in range(depth):
        proj, vt = _inproj(x2, mix_norm_w[layer].reshape(1, d), w_in, layer, cos_t, sin_t, batch, seq)
        y_attn = _moba(proj, vt, batch, seq)
        y_ret = _retention(proj, ret_norm_w[layer].reshape(1, RET_V_WIDTH), ret_tables, batch, seq)
        x2 = _merge(x2, y_attn, proj, y_ret, conv_w[layer], conv_b[layer].reshape(1, CONV_WIDTH),
                    w_br_attn[layer].astype(BF16), w_br_conv[layer].astype(BF16),
                    w_br_ret[layer].astype(BF16), w_out[layer].astype(BF16), seq)
        i = layer // 2
        nw = ffn_norm_w[layer].reshape(1, d)
        if layer % 2 == 0:
            x2 = _dense_ffn(x2, nw, dense_w_gate, dense_w_up, dense_w_down, i)
        else:
            last = layer == depth - 1
            x2 = _moe(x2, nw, moe_router[i], moe_w_gate, moe_w_up, moe_w_down, i,
                      final_norm_w.reshape(1, d), last)
    return x2.reshape(batch, seq, d)
```

```python
import functools
import math

import jax
import jax.numpy as jnp
import numpy as np
from jax import lax
from jax.experimental import pallas as pl
from jax.experimental.pallas import tpu as pltpu

F32 = jnp.float32
BF16 = jnp.bfloat16

D_MODEL = 1024
HEAD_DIM = 64
ROPE_THETA = 10000.0
RMS_EPS = 1e-6

ATTN_HEADS = 8
ATTN_WIDTH = ATTN_HEADS * HEAD_DIM
MOBA_BLOCK = 256
MOBA_TOPK = 3
NEG_INF = -1e30
MOBA_HEADS_PER_STEP = 4
VT_ONES = 16
VT_ROWS = HEAD_DIM + VT_ONES
QK_SCALE_LOG2 = HEAD_DIM ** -0.5 * math.log2(math.e)

CONV_WIDTH = 512
CONV_KERNEL = 3

RET_HEADS = 4
RET_QK_WIDTH = RET_HEADS * HEAD_DIM
RET_V_DIM = 2 * HEAD_DIM
RET_V_WIDTH = RET_HEADS * RET_V_DIM
RET_TILE = 256

D_FF = 3584
N_EXPERTS = 8
TOP_K = 2

LANES = 128
IN_PROJ_WIDTH = 7680
COL_AQ, COL_AK, COL_AV = 0, 512, 1024
COL_CB, COL_CC, COL_CH = 1536, 2048, 2560
COL_RV, COL_RG = 3072, 3584
COL_GATES = 4096
COL_RQ, COL_RK = 7168, 7424

PROJ_TM = 2048
PROJ_TN = 512
MERGE_TM = 512
FFN_TM = 1024
FFN_TF = 512
ROUTE_TM = 1024
VMEM_LIMIT = 56 * 1024 * 1024


def _cparams(sem):
    return pltpu.CompilerParams(dimension_semantics=sem, vmem_limit_bytes=VMEM_LIMIT)


def _rope(acc, cos, sin_signed):
    lane = lax.broadcasted_iota(jnp.int32, (1, LANES), 1)
    first_half = (lane % HEAD_DIM) < (HEAD_DIM // 2)
    outs = []
    for g in range(acc.shape[1] // LANES):
        blk = acc[:, g * LANES:(g + 1) * LANES]
        packed = blk.astype(BF16)
        partner = jnp.where(first_half,
                            pltpu.roll(packed, LANES - HEAD_DIM // 2, 1),
                            pltpu.roll(packed, HEAD_DIM // 2, 1)).astype(F32)
        outs.append(blk * cos + partner * sin_signed)
    return jnp.concatenate(outs, axis=1)


def _inproj_kernel(x_ref, nw_ref, w_ref, cos_ref, sin_ref, proj_ref, vt_ref, h_scr):
    j = pl.program_id(1)

    @pl.when(j == 0)
    def _():
        x = x_ref[...]
        ms = jnp.mean(x * x, axis=-1, keepdims=True)
        h_scr[...] = (x * lax.rsqrt(ms + RMS_EPS) * nw_ref[...]).astype(BF16)

    acc = jnp.dot(h_scr[...], w_ref[...].astype(BF16), preferred_element_type=F32)
    jq, jk, jv, jr = COL_AQ // PROJ_TN, COL_AK // PROJ_TN, COL_AV // PROJ_TN, COL_RQ // PROJ_TN
    scale = HEAD_DIM ** -0.5

    @pl.when(j == jq)
    def _():
        proj_ref[...] = (_rope(acc, cos_ref[...], sin_ref[...]) * QK_SCALE_LOG2).astype(BF16)

    @pl.when(j == jk)
    def _():
        proj_ref[...] = _rope(acc, cos_ref[...], sin_ref[...]).astype(BF16)

    @pl.when(j == jv)
    def _():
        proj_ref[...] = acc.astype(BF16)
        ones = jnp.ones((VT_ONES, MOBA_BLOCK), F32)
        for c in range(PROJ_TM // MOBA_BLOCK):
            v_t = acc[c * MOBA_BLOCK:(c + 1) * MOBA_BLOCK, :].T
            rows = []
            for h in range(ATTN_HEADS):
                rows += [v_t[h * HEAD_DIM:(h + 1) * HEAD_DIM], ones]
            vt_ref[c] = jnp.concatenate(rows, axis=0).astype(BF16)

    @pl.when(j == jr)
    def _():
        r = _rope(acc, cos_ref[...], sin_ref[...])
        col = lax.broadcasted_iota(jnp.int32, (1, PROJ_TN), 1)
        r = r * jnp.where(col >= RET_QK_WIDTH, scale, 1.0)
        proj_ref[...] = r.astype(BF16)

    @pl.when((j != jq) & (j != jk) & (j != jv) & (j != jr))
    def _():
        proj_ref[...] = acc.astype(BF16)


def _inproj_src_block(j):
    first_moved = COL_RV // PROJ_TN
    last = IN_PROJ_WIDTH // PROJ_TN - 1
    return jnp.where(j < first_moved, j, jnp.where(j == last, first_moved, j + 1))


def _inproj(x2, norm_w, w_in, layer, cos_t, sin_t, batch, seq):
    n = x2.shape[0]
    nst = seq // PROJ_TM
    nblk = PROJ_TM // MOBA_BLOCK
    return pl.pallas_call(
        _inproj_kernel,
        grid=(n // PROJ_TM, IN_PROJ_WIDTH // PROJ_TN),
        in_specs=[
            pl.BlockSpec((PROJ_TM, D_MODEL), lambda i, j: (i, 0)),
            pl.BlockSpec((1, D_MODEL), lambda i, j: (0, 0)),
            pl.BlockSpec((None, D_MODEL, PROJ_TN), lambda i, j: (layer, 0, _inproj_src_block(j))),
            pl.BlockSpec((PROJ_TM, LANES), lambda i, j: (i % nst, 0)),
            pl.BlockSpec((PROJ_TM, LANES), lambda i, j: (i % nst, 0)),
        ],
        out_specs=[
            pl.BlockSpec((PROJ_TM, PROJ_TN), lambda i, j: (i, j)),
            pl.BlockSpec((None, nblk, ATTN_HEADS * VT_ROWS, MOBA_BLOCK),
                         lambda i, j: (i // nst, i % nst, 0, 0)),
        ],
        out_shape=[
            jax.ShapeDtypeStruct((n, IN_PROJ_WIDTH), BF16),
            jax.ShapeDtypeStruct((batch, seq // MOBA_BLOCK, ATTN_HEADS * VT_ROWS, MOBA_BLOCK), BF16),
        ],
        scratch_shapes=[pltpu.VMEM((PROJ_TM, D_MODEL), BF16)],
        compiler_params=_cparams(("arbitrary", "arbitrary")),
        name="inproj",
    )(x2, norm_w, w_in, cos_t, sin_t)


_NT = (((1,), (1,)), ((), ()))


def _moba_kernel(q_ref, k_ref, vt_ref, o_ref, km_scr, ka_scr, sel_scr, qt_scr, s_scr, m_scr, acc_scr,
                 *, nb):
    nh = MOBA_HEADS_PER_STEP
    blk = MOBA_BLOCK
    tq = 2 * blk
    lane = lax.broadcasted_iota(jnp.int32, (1, LANES), 1)
    crow = lax.broadcasted_iota(jnp.int32, (LANES, 1), 0)

    def group(ref, rows, h):
        g = h // 2
        return ref[rows, g * LANES:(g + 1) * LANES]

    def prep(i, c):
        rows = pl.ds(pl.multiple_of(i * blk, blk), blk)
        for h in range(nh):
            kb, hh = group(k_ref, rows, h), h % 2
            if hh == 0:
                km_scr[h // 2, pl.ds(i, 1), :] = jnp.sum(kb.astype(F32), axis=0, keepdims=True) * (1.0 / blk)
            onehot = jnp.where(lane == HEAD_DIM * (1 - hh) + i, 1.0, 0.0).astype(BF16)
            ka_scr[h, rows, :] = jnp.where((lane // HEAD_DIM) == hh, kb, onehot)
        return c

    lax.fori_loop(0, nb, prep, 0)

    causal = (lax.broadcasted_iota(jnp.int32, (blk, blk), 0)
              <= lax.broadcasted_iota(jnp.int32, (blk, blk), 1))
    blk_id = lax.broadcasted_iota(jnp.int32, (nb, tq), 0)
    q_half = (lax.broadcasted_iota(jnp.int32, (nb, tq), 1) >= blk).astype(jnp.int32)

    def pv(t, h, p):
        rows = slice(h * VT_ROWS, (h + 1) * VT_ROWS)
        pb = p.astype(BF16)
        return (jnp.dot(vt_ref[2 * t, rows, :], pb[0:blk], preferred_element_type=F32)
                + jnp.dot(vt_ref[2 * t + 1, rows, :], pb[blk:tq], preferred_element_type=F32))

    def qtile(jt, c):
        row0 = pl.multiple_of(jt * tq, tq)
        q_ts =[group(q_ref, pl.ds(row0, tq), h).astype(F32).T for h in range(0, nh, 2)]
        for h in range(nh):
            hh = h % 2
            q_t = q_ts[h // 2]
            k_diag = group(k_ref, pl.ds(row0, tq), h)
            q_m = jnp.where((crow // HEAD_DIM) == hh, q_t, 0.0)
            q_mb = q_m.astype(BF16)
            km = jnp.where((lane // HEAD_DIM) == hh, km_scr[h // 2], 0.0)
            km_hi = km.astype(BF16)
            km_lo = (km - km_hi.astype(F32)).astype(BF16)
            gate = (jnp.dot(km_hi, q_mb, preferred_element_type=F32)
                    + jnp.dot(km_lo, q_mb, preferred_element_type=F32))
            past = blk_id < 2 * jt + q_half
            gate = jnp.where(past, gate, -jnp.inf)
            rank = jnp.zeros((nb, tq), jnp.int32)
            for ip in range(nb):
                gi = gate[ip:ip + 1, :]
                beats = (gi > gate) | ((gi == gate) & (blk_id > ip))
                rank = rank + beats.astype(jnp.int32)
            sel = (rank < MOBA_TOPK) & past
            sel_scr[h] = sel.astype(F32)
            bias = jnp.where(sel, 0.0, NEG_INF)
            spare = HEAD_DIM * (1 - hh)
            pieces = [jnp.zeros((spare, tq), F32)] if spare else []
            pieces += [bias, jnp.zeros((LANES - spare - nb, tq), F32)]
            qt_scr[h] = (q_m + jnp.concatenate(pieces, axis=0)).astype(BF16)

            s_top = jnp.dot(k_diag[0:blk], q_mb, preferred_element_type=F32)
            s11 = jnp.where(causal, s_top[:, 0:blk], NEG_INF)
            s12 = jnp.where(sel_scr[h, pl.ds(2 * jt, 1), blk:tq] > 0.0, s_top[:, blk:tq], NEG_INF)
            s22 = jnp.where(causal, jnp.dot(k_diag[blk:tq], q_mb[:, blk:tq], preferred_element_type=F32),
                            NEG_INF)
            m_a = jnp.max(s11, axis=0, keepdims=True)
            m_b = jnp.maximum(jnp.max(s12, axis=0, keepdims=True), jnp.max(s22, axis=0, keepdims=True))
            s_scr[2 * h, 0:blk, 0:blk] = s11
            s_scr[2 * h, 0:blk, blk:tq] = s12
            s_scr[2 * h, blk:tq, 0:blk] = jnp.full((blk, blk), NEG_INF, F32)
            s_scr[2 * h, blk:tq, blk:tq] = s22
            m_scr[2 * h] = jnp.concatenate([m_a, m_b], axis=1)
            m_scr[2 * h + 1] = m_scr[2 * h]
            acc_scr[h] = jnp.zeros((VT_ROWS, tq), F32)

        def park(t, h, slot):
            s = jnp.dot(ka_scr[h, pl.ds(pl.multiple_of(t * tq, tq), tq), :], qt_scr[h],
                        preferred_element_type=F32)
            s_scr[2 * h + slot] = s
            m_scr[2 * h + 1] = jnp.maximum(m_scr[2 * h + 1], jnp.max(s, axis=0, keepdims=True))

        def consume(k, h, slot):
            m_old, m_new = m_scr[2 * h], m_scr[2 * h + 1]
            acc_scr[h] = (jnp.exp2(m_old - m_new) * acc_scr[h]
                          + pv(jnp.where(k == 0, jt, k - 1), h, jnp.exp2(s_scr[2 * h + slot] - m_new)))
            m_scr[2 * h] = m_new

        def step(k, slot):
            for h in range(nh):
                consume(k, h, slot)
                park(k, h, 1 - slot)

        def run(trips, body):
            def wrapped(i, c):
                body(i)
                return c
            lax.fori_loop(0, trips, wrapped, 0)

        run(jt // 2, lambda u: (step(2 * u, 0), step(2 * u + 1, 1)))
        run(jt % 2, lambda i: step(jt - 1, 0))
        run(jt % 2, lambda i: [consume(jt, h, 1) for h in range(nh)])
        run(1 - jt % 2, lambda i: [consume(jt, h, 0) for h in range(nh)])
        out_t = jnp.concatenate([acc_scr[h, 0:HEAD_DIM, :] / acc_scr[h, HEAD_DIM:HEAD_DIM + 1, :]
                                 for h in range(nh)], axis=0)
        o_ref[pl.ds(row0, tq), :] = out_t.T.astype(BF16)
        return c

    lax.fori_loop(0, nb // 2, qtile, 0)


def _moba(proj, vt, batch, seq):
    nb = seq // MOBA_BLOCK
    assert nb % 8 == 0 and nb <= HEAD_DIM
    nh = MOBA_HEADS_PER_STEP
    width = nh * HEAD_DIM
    return pl.pallas_call(
        functools.partial(_moba_kernel, nb=nb),
        grid=(batch, ATTN_HEADS // nh),
        in_specs=[
            pl.BlockSpec((seq, width), lambda b, p: (b, COL_AQ // width + p)),
            pl.BlockSpec((seq, width), lambda b, p: (b, COL_AK // width + p)),
            pl.BlockSpec((None, nb, nh * VT_ROWS, MOBA_BLOCK), lambda b, p: (b, 0, p, 0)),
        ],
        out_specs=pl.BlockSpec((seq, width), lambda b, p: (b, p)),
        out_shape=jax.ShapeDtypeStruct((batch * seq, ATTN_WIDTH), BF16),
        scratch_shapes=[pltpu.VMEM((nh // 2, nb, LANES), F32), pltpu.VMEM((nh, seq, LANES), BF16),
                        pltpu.VMEM((nh, nb, 2 * MOBA_BLOCK), F32),
                        pltpu.VMEM((nh, LANES, 2 * MOBA_BLOCK), BF16),
                        pltpu.VMEM((2 * nh, 2 * MOBA_BLOCK, 2 * MOBA_BLOCK), F32),
                        pltpu.VMEM((2 * nh, 1, 2 * MOBA_BLOCK), F32),
                        pltpu.VMEM((nh, VT_ROWS, 2 * MOBA_BLOCK), F32)],
        compiler_params=_cparams(("arbitrary", "arbitrary")),
        name="moba",
    )(proj, proj, vt)


def _ret_log_gamma():
    return [math.log1p(-(2.0 ** (-5.0 - h))) for h in range(RET_HEADS)]


def _retention_tables():
    c = RET_TILE
    lg = np.array(_ret_log_gamma(), np.float64)
    n = np.arange(c, dtype=np.float64)
    diff = n[:, None] - n[None, :]
    decay = np.where(diff[None] >= 0, np.exp(np.maximum(diff, 0.0)[None] * lg[:, None, None]), 0.0)
    head_of_lane = np.arange(RET_QK_WIDTH) // HEAD_DIM
    xi = np.exp((n + 1.0)[:, None] * lg[head_of_lane][None, :])
    zeta = np.exp((c - 1.0 - n)[:, None] * lg[head_of_lane][None, :])
    chunk_decay = np.exp(c * lg[head_of_lane])[:, None]
    return (jnp.asarray(decay, F32), jnp.asarray(xi, F32), jnp.asarray(zeta, F32),
            jnp.asarray(np.broadcast_to(chunk_decay, (RET_QK_WIDTH, LANES)), F32))


def _retention_kernel(q_ref, k_ref, v_ref, g_ref, nw_ref, decay_ref, xi_ref, zeta_ref, cd_ref,
                      o_ref, state_scr):
    @pl.when(pl.program_id(1) == 0)
    def _():
        state_scr[...] = jnp.zeros_like(state_scr)

    lane = lax.broadcasted_iota(jnp.int32, (1, LANES), 1)
    srow = lax.broadcasted_iota(jnp.int32, (LANES, 1), 0)
    for pr in range(RET_HEADS // 2):
        cols = slice(pr * LANES, (pr + 1) * LANES)
        q = q_ref[:, cols]
        k = k_ref[:, cols]
        state = state_scr[cols, :]
        state_bf = state.astype(BF16)
        q_xi = (q.astype(F32) * xi_ref[:, cols]).astype(BF16)
        kz_t = (k.astype(F32) * zeta_ref[:, cols]).T.astype(BF16)
        upd = []
        for hh in range(2):
            h = 2 * pr + hh
            hmask = (lane // HEAD_DIM) == hh
            vcols = slice(h * RET_V_DIM, (h + 1) * RET_V_DIM)
            v = v_ref[:, vcols]
            qm = jnp.where(hmask, q, jnp.zeros_like(q))
            scores = lax.dot_general(qm, k, _NT, preferred_element_type=F32) * decay_ref[h]
            o = jnp.dot(scores.astype(BF16), v, preferred_element_type=F32)
            o = o + jnp.dot(jnp.where(hmask, q_xi, jnp.zeros_like(q_xi)), state_bf,
                            preferred_element_type=F32)
            upd.append(jnp.dot(kz_t, v, preferred_element_type=F32))
            ms = jnp.mean(o * o, axis=-1, keepdims=True)
            y = o * lax.rsqrt(ms + RMS_EPS) * nw_ref[:, vcols]
            g = g_ref[:, vcols].astype(F32)
            o_ref[:, vcols] = (g * jax.nn.sigmoid(g) * y).astype(BF16)
        state_scr[cols, :] = state * cd_ref[cols, :] + jnp.where(srow < HEAD_DIM, upd[0], upd[1])


def _retention(proj, ret_norm_w, tables, batch, seq):
    nc = seq // RET_TILE
    decay, xi, zeta, cd = tables
    row = lambda b, c: b * nc + c
    const2 = lambda b, c: (0, 0)
    return pl.pallas_call(
        _retention_kernel,
        grid=(batch, nc),
        in_specs=[
            pl.BlockSpec((RET_TILE, RET_QK_WIDTH), lambda b, c: (row(b, c), COL_RQ // RET_QK_WIDTH)),
            pl.BlockSpec((RET_TILE, RET_QK_WIDTH), lambda b, c: (row(b, c), COL_RK // RET_QK_WIDTH)),
            pl.BlockSpec((RET_TILE, RET_V_WIDTH), lambda b, c: (row(b, c), COL_RV // RET_V_WIDTH)),
            pl.BlockSpec((RET_TILE, RET_V_WIDTH), lambda b, c: (row(b, c), COL_RG // RET_V_WIDTH)),
            pl.BlockSpec((1, RET_V_WIDTH), const2),
            pl.BlockSpec((RET_HEADS, RET_TILE, RET_TILE), lambda b, c: (0, 0, 0)),
            pl.BlockSpec((RET_TILE, RET_QK_WIDTH), const2),
            pl.BlockSpec((RET_TILE, RET_QK_WIDTH), const2),
            pl.BlockSpec((RET_QK_WIDTH, LANES), const2),
        ],
        out_specs=pl.BlockSpec((RET_TILE, RET_V_WIDTH), lambda b, c: (row(b, c), 0)),
        out_shape=jax.ShapeDtypeStruct((batch * seq, RET_V_WIDTH), BF16),
        scratch_shapes=[pltpu.VMEM((RET_QK_WIDTH, RET_V_DIM), F32)],
        compiler_params=_cparams(("arbitrary", "arbitrary")),
        name="retention",
    )(proj, proj, proj, proj, ret_norm_w, decay, xi, zeta, cd)


CONV_HALO = 8


def _merge_kernel(x_ref, ya_ref, cb_ref, cc_ref, ch_ref, yr_ref, ga_ref, gc_ref, gr_ref, cw_ref,
                  cbias_ref, wa_ref, wc_ref, wr_ref, wo_ref, o_ref, u_scr, *, tiles_per_seq):
    i = pl.program_id(0)
    tm = x_ref.shape[0]

    @pl.when(i % tiles_per_seq == 0)
    def _():
        u_scr[0:CONV_HALO, :] = jnp.zeros((CONV_HALO, CONV_WIDTH), F32)

    u_scr[CONV_HALO:CONV_HALO + tm, :] = cc_ref[...].astype(F32) * ch_ref[...].astype(F32)
    conv = (cw_ref[2:3, :] * u_scr[CONV_HALO:CONV_HALO + tm, :]
            + cw_ref[1:2, :] * u_scr[CONV_HALO - 1:CONV_HALO - 1 + tm, :]
            + cw_ref[0:1, :] * u_scr[CONV_HALO - 2:CONV_HALO - 2 + tm, :]
            + cbias_ref[...])
    y_conv = (cb_ref[...].astype(F32) * conv).astype(BF16)
    u_scr[0:CONV_HALO, :] = u_scr[tm:tm + CONV_HALO, :]

    def gate(g_ref):
        return jax.nn.sigmoid(g_ref[...].astype(F32))

    merged = gate(ga_ref) * jnp.dot(ya_ref[...], wa_ref[...], preferred_element_type=F32)
    merged = merged + gate(gc_ref) * jnp.dot(y_conv, wc_ref[...], preferred_element_type=F32)
    merged = merged + gate(gr_ref) * jnp.dot(yr_ref[...], wr_ref[...], preferred_element_type=F32)
    o_ref[...] = x_ref[...] + jnp.dot(merged.astype(BF16), wo_ref[...], preferred_element_type=F32)


def _merge(x2, y_attn, proj, y_ret, conv_w, conv_b, wa, wc, wr, wo, seq):
    n = x2.shape[0]
    tm = MERGE_TM
    const = lambda i: (0, 0)
    wide = lambda c: pl.BlockSpec((tm, CONV_WIDTH), lambda i: (i, c // CONV_WIDTH))
    gate_spec = lambda b: pl.BlockSpec((tm, D_MODEL), lambda i: (i, COL_GATES // D_MODEL + b))
    return pl.pallas_call(
        functools.partial(_merge_kernel, tiles_per_seq=seq // tm),
        grid=(n // tm,),
        in_specs=[
            pl.BlockSpec((tm, D_MODEL), lambda i: (i, 0)),
            pl.BlockSpec((tm, ATTN_WIDTH), lambda i: (i, 0)),
            wide(COL_CB), wide(COL_CC), wide(COL_CH),
            pl.BlockSpec((tm, RET_V_WIDTH), lambda i: (i, 0)),
            gate_spec(0), gate_spec(1), gate_spec(2),
            pl.BlockSpec((CONV_KERNEL, CONV_WIDTH), const),
            pl.BlockSpec((1, CONV_WIDTH), const),
            pl.BlockSpec((ATTN_WIDTH, D_MODEL), const),
            pl.BlockSpec((CONV_WIDTH, D_MODEL), const),
            pl.BlockSpec((RET_V_WIDTH, D_MODEL), const),
            pl.BlockSpec((D_MODEL, D_MODEL), const),
        ],
        out_specs=pl.BlockSpec((tm, D_MODEL), lambda i: (i, 0)),
        out_shape=jax.ShapeDtypeStruct((n, D_MODEL), F32),
        scratch_shapes=[pltpu.VMEM((tm + CONV_HALO, CONV_WIDTH), F32)],
        compiler_params=_cparams(("arbitrary",)),
        name="merge",
    )(x2, y_attn, proj, proj, proj, y_ret, proj, proj, proj, conv_w, conv_b, wa, wc, wr, wo)


FFN_CHUNKS = D_FF // FFN_TF
FFN_STEPS = FFN_CHUNKS + 1


def _swiglu_pipelined(c, h_scr, wg_ref, wu_ref, wd_ref, gu_scr, acc_scr, enable=True, beside=None):
    beside = beside or (lambda: None)
    def fold(slot):
        g, u = gu_scr[2 * slot], gu_scr[2 * slot + 1]
        a = (g * jax.nn.sigmoid(g) * u).astype(BF16)
        acc_scr[...] += jnp.dot(a, wd_ref[...].astype(BF16), preferred_element_type=F32)

    def park(slot):
        h = h_scr[...]
        gu_scr[2 * slot] = jnp.dot(h, wg_ref[...].astype(BF16), preferred_element_type=F32)
        gu_scr[2 * slot + 1] = jnp.dot(h, wu_ref[...].astype(BF16), preferred_element_type=F32)

    @pl.when(enable & (c == 0))
    def _():
        beside()
        park(0)

    for parity in (0, 1):
        @pl.when(enable & (c >= 1) & (c < FFN_CHUNKS) & (c % 2 == parity))
        def _():
            beside()
            fold(1 - parity)
            park(parity)

    @pl.when(enable & (c == FFN_CHUNKS))
    def _():
        beside()
        fold((FFN_CHUNKS - 1) % 2)


def _dense_ffn_kernel(x_ref, nw_ref, wg_ref, wu_ref, wd_ref, o_ref, h_scr, acc_scr):
    c = pl.program_id(1)

    @pl.when(c == 0)
    def _():
        x = x_ref[...]
        ms = jnp.mean(x * x, axis=-1, keepdims=True)
        h_scr[...] = (x * lax.rsqrt(ms + RMS_EPS) * nw_ref[...]).astype(BF16)
        acc_scr[...] = jnp.zeros_like(acc_scr)

    h = h_scr[...]
    g = jnp.dot(h, wg_ref[...].astype(BF16), preferred_element_type=F32)
    u = jnp.dot(h, wu_ref[...].astype(BF16), preferred_element_type=F32)
    a = (g * jax.nn.sigmoid(g) * u).astype(BF16)
    acc_scr[...] += jnp.dot(a, wd_ref[...].astype(BF16), preferred_element_type=F32)

    @pl.when(c == FFN_CHUNKS - 1)
    def _():
        o_ref[...] = x_ref[...] + acc_scr[...]


def _dense_ffn(x2, norm_w, wg, wu, wd, li):
    n = x2.shape[0]
    tm, tf = FFN_TM, FFN_TF
    return pl.pallas_call(
        _dense_ffn_kernel,
        grid=(n // tm, FFN_CHUNKS),
        in_specs=[
            pl.BlockSpec((tm, D_MODEL), lambda i, c: (i, 0)),
            pl.BlockSpec((1, D_MODEL), lambda i, c: (0, 0)),
            pl.BlockSpec((None, D_MODEL, tf), lambda i, c: (li, 0, c)),
            pl.BlockSpec((None, D_MODEL, tf), lambda i, c: (li, 0, c)),
            pl.BlockSpec((None, tf, D_MODEL), lambda i, c: (li, c, 0)),
        ],
        out_specs=pl.BlockSpec((tm, D_MODEL), lambda i, c: (i, 0)),
        out_shape=jax.ShapeDtypeStruct((n, D_MODEL), F32),
        scratch_shapes=[pltpu.VMEM((tm, D_MODEL), BF16), pltpu.VMEM((tm, D_MODEL), F32)],
        compiler_params=_cparams(("arbitrary", "arbitrary")),
        name="dense_ffn",
    )(x2, norm_w, wg, wu, wd)


SLAB = D_MODEL // LANES


def _pack_rows(v, o_ref):
    rows = v.shape[0]
    for s in range(SLAB):
        o_ref[pl.ds(s, rows, stride=SLAB), :] = v[:, s * LANES:(s + 1) * LANES]


def _unpack_rows(x_ref, rows):
    return jnp.concatenate([x_ref[pl.ds(s, rows, stride=SLAB), :] for s in range(SLAB)], axis=1)


def _route_kernel(x_ref, nw_ref, wr_ref, h_ref, idx_ref, wgt_ref):
    x = x_ref[...]
    ms = jnp.mean(x * x, axis=-1, keepdims=True)
    h = x * lax.rsqrt(ms + RMS_EPS) * nw_ref[...]
    _pack_rows(h, h_ref)
    w = wr_ref[...]
    h_hi, w_hi = h.astype(BF16), w.astype(BF16)
    h_lo, w_lo = (h - h_hi.astype(F32)).astype(BF16), (w - w_hi.astype(F32)).astype(BF16)
    logits = (jnp.dot(h_hi, w_hi, preferred_element_type=F32)
              + jnp.dot(h_lo, w_hi, preferred_element_type=F32)
              + jnp.dot(h_hi, w_lo, preferred_element_type=F32))
    eid = lax.broadcasted_iota(jnp.int32, logits.shape, 1).astype(F32)
    logits = jnp.where(eid < N_EXPERTS, logits, -jnp.inf)
    m1 = jnp.max(logits, axis=-1, keepdims=True)
    i1 = jnp.min(jnp.where(logits == m1, eid, float(LANES)), axis=-1, keepdims=True)
    rest = jnp.where(eid == i1, -jnp.inf, logits)
    m2 = jnp.max(rest, axis=-1, keepdims=True)
    i2 = jnp.min(jnp.where(rest == m2, eid, float(LANES)), axis=-1, keepdims=True)
    e2 = jnp.exp(m2 - m1)
    denom = 1.0 + e2
    idx_ref[...] = jnp.where(eid == 0.0, i1, jnp.where(eid == 1.0, i2, 0.0)).astype(jnp.int32)
    wgt_ref[...] = jnp.where(eid == 0.0, 1.0 / denom, jnp.where(eid == 1.0, e2 / denom, 0.0))


def _route(x2, norm_w, w_router):
    n = x2.shape[0]
    tm = ROUTE_TM
    return pl.pallas_call(
        _route_kernel,
        grid=(n // tm,),
        in_specs=[
            pl.BlockSpec((tm, D_MODEL), lambda i: (i, 0)),
            pl.BlockSpec((1, D_MODEL), lambda i: (0, 0)),
            pl.BlockSpec((D_MODEL, LANES), lambda i: (0, 0)),
        ],
        out_specs=[
            pl.BlockSpec((tm * SLAB, LANES), lambda i: (i, 0)),
            pl.BlockSpec((tm, LANES), lambda i: (i, 0)),
            pl.BlockSpec((tm, LANES), lambda i: (i, 0)),
        ],
        out_shape=[
            jax.ShapeDtypeStruct((n * SLAB, LANES), F32),
            jax.ShapeDtypeStruct((n, LANES), jnp.int32),
            jax.ShapeDtypeStruct((n, LANES), F32),
        ],
        compiler_params=_cparams(("arbitrary",)),
        name="route",
    )(x2, norm_w, jnp.pad(w_router, ((0, 0), (0, LANES - N_EXPERTS))))


MOE_STEPS = FFN_STEPS
MOE_ROWS_PER_STEP = -(-FFN_TM // MOE_STEPS)
MOE_DMA_ROWS = MOE_ROWS_PER_STEP * MOE_STEPS
MOE_BUF_ROWS = -(-MOE_DMA_ROWS // 8) * 8
MOE_SPARE_TILES = 2


def _moe_ffn_kernel(te_ref, nv_ref, gid0_ref, gidn_ref, sid_ref, h3_ref, wg_ref, wu_ref, wd_ref,
                    y_init_ref, y_ref, gbuf, obuf, h_scr, gu_scr, acc_scr, gsem, ssem):
    t = pl.program_id(0)
    c = pl.program_id(1)
    nv = nv_ref[0]
    slot = t % 2
    other = 1 - slot
    first = c == 0

    def slab(ref, start):
        return ref.at[pl.ds(pl.multiple_of(start, SLAB), SLAB)]

    def gather(ids_ref, row, s):
        return pltpu.make_async_copy(slab(h3_ref, ids_ref[0, 0, row]), slab(gbuf.at[s], row * SLAB),
                                     gsem.at[s])

    def scatter(row, s):
        return pltpu.make_async_copy(slab(obuf.at[s], row * SLAB), slab(y_ref, sid_ref[0, 0, row]),
                                     ssem.at[s])

    def tile_gather(s):
        return pltpu.make_async_copy(h3_ref.at[pl.ds(0, MOE_DMA_ROWS * SLAB)],
                                     gbuf.at[s, pl.ds(0, MOE_DMA_ROWS * SLAB)], gsem.at[s])

    def tile_scatter(s):
        return pltpu.make_async_copy(obuf.at[s, pl.ds(0, MOE_DMA_ROWS * SLAB)],
                                     y_ref.at[pl.ds(0, MOE_DMA_ROWS * SLAB)], ssem.at[s])

    @pl.when(first & (t == 0))
    def _():
        obuf[...] = jnp.zeros_like(obuf)

        def body(r, carry):
            gather(gid0_ref, r, 0).start()
            return carry

        lax.fori_loop(0, MOE_DMA_ROWS, body, 0, unroll=MOE_STEPS)

    @pl.when(first & (t <= nv))
    def _():
        tile_gather(slot).wait()

    @pl.when(first & (t >= 1) & (t <= nv + 1))
    def _():
        tile_scatter(slot).wait()

    @pl.when(first & (t < nv))
    def _():
        h_scr[...] = _unpack_rows(gbuf.at[slot], FFN_TM).astype(BF16)
        acc_scr[...] = jnp.zeros_like(acc_scr)

    def row_dmas():
        for u in range(MOE_ROWS_PER_STEP):
            row = c * MOE_ROWS_PER_STEP + u
            gather(gidn_ref, row, other).start(priority=1)
            scatter(row, other).start(priority=1)

    _swiglu_pipelined(c, h_scr, wg_ref, wu_ref, wd_ref, gu_scr, acc_scr, enable=t < nv, beside=row_dmas)

    @pl.when(t == nv)
    def _():
        def body(u, carry):
            scatter(c * MOE_ROWS_PER_STEP + u, other).start()
            return carry

        lax.fori_loop(0, MOE_ROWS_PER_STEP, body, 0, unroll=MOE_STEPS)

    @pl.when((t < nv) & (c == pl.num_programs(1) - 1))
    def _():
        _pack_rows(acc_scr[...], obuf.at[slot])


def _moe_ffn(tile_expert, n_valid, gid, sid, h3, wg, wu, wd, li, y_rows):
    nt = gid.shape[0]
    tm, tf = FFN_TM, FFN_TF
    ids_block = (1, 1, MOE_BUF_ROWS)
    smem = pltpu.SMEM
    wspec = lambda shape, imap: pl.BlockSpec((None, None) + shape, imap)
    expert = lambda t, te, nv: te[jnp.minimum(t, nv[0] - 1)]
    last = FFN_CHUNKS - 1
    up_chunk = lambda t, c, nv: jnp.where(t < nv[0], jnp.minimum(c, last), last)
    down_chunk = lambda t, c, nv: jnp.where(t < nv[0], jnp.maximum(c - 1, 0), last)
    grid_spec = pltpu.PrefetchScalarGridSpec(
        num_scalar_prefetch=2,
        grid=(nt, MOE_STEPS),
        in_specs=[
            pl.BlockSpec(ids_block, lambda t, c, te, nv: (0, 0, 0), memory_space=smem),
            pl.BlockSpec(ids_block, lambda t, c, te, nv: (jnp.minimum(t + 1, nt - 1), 0, 0),
                         memory_space=smem),
            pl.BlockSpec(ids_block, lambda t, c, te, nv: (t, 0, 0), memory_space=smem),
            pl.BlockSpec(memory_space=pl.ANY),
            wspec((D_MODEL, tf), lambda t, c, te, nv: (li, expert(t, te, nv), 0, up_chunk(t, c, nv))),
            wspec((D_MODEL, tf), lambda t, c, te, nv: (li, expert(t, te, nv), 0, up_chunk(t, c, nv))),
            wspec((tf, D_MODEL), lambda t, c, te, nv: (li, expert(t, te, nv), down_chunk(t, c, nv), 0)),
            pl.BlockSpec(memory_space=pl.ANY),
        ],
        out_specs=pl.BlockSpec(memory_space=pl.ANY),
        scratch_shapes=[
            pltpu.VMEM((2, MOE_BUF_ROWS * SLAB, LANES), F32),
            pltpu.VMEM((2, MOE_BUF_ROWS * SLAB, LANES), F32),
            pltpu.VMEM((tm, D_MODEL), BF16),
            pltpu.VMEM((4, tm, tf), F32),
            pltpu.VMEM((tm, D_MODEL), F32),
            pltpu.SemaphoreType.DMA((2,)),
            pltpu.SemaphoreType.DMA((2,)),
        ],
    )
    return pl.pallas_call(
        _moe_ffn_kernel,
        grid_spec=grid_spec,
        out_shape=jax.ShapeDtypeStruct((y_rows * SLAB, LANES), F32),
        input_output_aliases={9: 0},
        compiler_params=_cparams(("arbitrary", "arbitrary")),
        name="moe_ffn",
    )(tile_expert, n_valid, gid * SLAB, gid * SLAB, sid * SLAB, h3, wg, wu, wd,
      jnp.zeros((y_rows * SLAB, LANES), F32))


def _combine_kernel(x_ref, y0_ref, y1_ref, w_ref, nw_ref, o_ref, *, final_norm):
    w = w_ref[...]
    rows = x_ref.shape[0]
    x = x_ref[...] + w[:, 0:1] * _unpack_rows(y0_ref, rows) + w[:, 1:2] * _unpack_rows(y1_ref, rows)
    if final_norm:
        ms = jnp.mean(x * x, axis=-1, keepdims=True)
        x = x * lax.rsqrt(ms + RMS_EPS) * nw_ref[...]
    o_ref[...] = x


def _combine(x2, y_pairs, wgt, norm_w, final_norm):
    n = x2.shape[0]
    tm = MERGE_TM
    nt = n // tm
    return pl.pallas_call(
        functools.partial(_combine_kernel, final_norm=final_norm),
        grid=(nt,),
        in_specs=[
            pl.BlockSpec((tm, D_MODEL), lambda i: (i, 0)),
            pl.BlockSpec((tm * SLAB, LANES), lambda i: (i, 0)),
            pl.BlockSpec((tm * SLAB, LANES), lambda i: (i + nt, 0)),
            pl.BlockSpec((tm, LANES), lambda i: (i, 0)),
            pl.BlockSpec((1, D_MODEL), lambda i: (0, 0)),
        ],
        out_specs=pl.BlockSpec((tm, D_MODEL), lambda i: (i, 0)),
        out_shape=jax.ShapeDtypeStruct((n, D_MODEL), F32),
        compiler_params=_cparams(("arbitrary",)),
        name="combine",
    )(x2, y_pairs, y_pairs, wgt, norm_w)


def _moe_layout(idx):
    n = idx.shape[0]
    tm = FFN_TM
    flat = idx.T.reshape(-1)
    onehot = (flat[:, None] == jnp.arange(N_EXPERTS, dtype=jnp.int32)[None, :]).astype(jnp.int32)
    cum = jnp.cumsum(onehot, axis=0)
    counts = cum[-1]
    rank = jnp.sum(cum * onehot, axis=1) - 1
    padded = ((counts + tm - 1) // tm) * tm
    ends = jnp.cumsum(padded)
    starts = ends - padded
    dest = starts[flat] + rank
    pairs = TOP_K * n
    nt = pairs // tm + N_EXPERTS + MOE_SPARE_TILES
    p = nt * tm
    tile_start = jnp.arange(nt, dtype=jnp.int32) * tm
    tile_expert = jnp.minimum(jnp.sum((ends[None, :] <= tile_start[:, None]).astype(jnp.int32), axis=1),
                              N_EXPERTS - 1)
    n_valid = (ends[-1] // tm).astype(jnp.int32).reshape(1)
    real_before = jnp.cumsum(counts)[tile_expert]
    pos = jnp.arange(p, dtype=jnp.int32).reshape(nt, tm)
    pad_row = (pairs + pos - real_before[:, None]).reshape(p)
    row_pair = pad_row.at[dest].set(jnp.arange(pairs, dtype=jnp.int32), unique_indices=True,
                                    mode="promise_in_bounds")
    row_token = jnp.where(row_pair < pairs, row_pair % n, 0)
    assert MOE_BUF_ROWS == tm
    gid = row_token.reshape(nt, 1, tm)
    first_spare = p + jnp.arange(tm, dtype=jnp.int32)[None, :]
    sid = jnp.concatenate([first_spare, row_pair.reshape(nt, tm)], axis=0).reshape(nt + 1, 1, tm)
    y_rows = p + tm
    return gid, sid, tile_expert, n_valid, y_rows


def _moe(x2, norm_w, w_router, wg, wu, wd, li, final_norm_w, final_norm):
    h3, idx, wgt = _route(x2, norm_w, w_router)
    gid, sid, tile_expert, n_valid, y_rows = _moe_layout(idx[:, :TOP_K])
    y = _moe_ffn(tile_expert, n_valid, gid, sid, h3, wg, wu, wd, li, y_rows)
    return _combine(x2, y, wgt, final_norm_w, final_norm)


def _rope_tables(seq):
    inv_freq = 1.0 / (ROPE_THETA ** (jnp.arange(0, HEAD_DIM, 2, dtype=F32) / HEAD_DIM))
    ang = jnp.arange(seq, dtype=F32)[:, None] * inv_freq[None, :]
    cos, sin = jnp.cos(ang), jnp.sin(ang)
    cos_t = jnp.tile(cos, (1, LANES // (HEAD_DIM // 2)))
    sin_t = jnp.tile(jnp.concatenate([-sin, sin], axis=1), (1, LANES // HEAD_DIM))
    return cos_t, sin_t


def kernel(x, mix_norm_w, w_in, conv_w, conv_b, ret_norm_w, w_br_attn, w_br_conv, w_br_ret, w_out,
           ffn_norm_w, dense_w_gate, dense_w_up, dense_w_down, moe_router, moe_w_gate, moe_w_up,
           moe_w_down, final_norm_w):
    batch, seq, d = x.shape
    depth = w_in.shape[0]
    assert d == D_MODEL and seq % PROJ_TM == 0 and depth % 2 == 0
    n = batch * seq
    cos_t, sin_t = _rope_tables(seq)
    ret_tables = _retention_tables()
    x2 = x.reshape(n, d)
    for layer in range(depth):
        proj, vt = _inproj(x2, mix_norm_w[layer].reshape(1, d), w_in, layer, cos_t, sin_t, batch, seq)
        y_attn = _moba(proj, vt, batch, seq)
        y_ret = _retention(proj, ret_norm_w[layer].reshape(1, RET_V_WIDTH), ret_tables, batch, seq)
        x2 = _merge(x2, y_attn, proj, y_ret, conv_w[layer], conv_b[layer].reshape(1, CONV_WIDTH),
                    w_br_attn[layer].astype(BF16), w_br_conv[layer].astype(BF16),
                    w_br_ret[layer].astype(BF16), w_out[layer].astype(BF16), seq)
        i = layer // 2
        nw = ffn_norm_w[layer].reshape(1, d)
        if layer % 2 == 0:
            x2 = _dense_ffn(x2, nw, dense_w_gate, dense_w_up, dense_w_down, i)
        else:
            last = layer == depth - 1
            x2 = _moe(x2, nw, moe_router[i], moe_w_gate, moe_w_up, moe_w_down, i,
                      final_norm_w.reshape(1, d), last)
    return x2.reshape(batch, seq, d)
```

```python
import functools
import math

import jax
import jax.numpy as jnp
import numpy as np
from jax import lax
from jax.experimental import pallas as pl
from jax.experimental.pallas import tpu as pltpu

F32 = jnp.float32
BF16 = jnp.bfloat16

D_MODEL = 1024
HEAD_DIM = 64
ROPE_THETA = 10000.0
RMS_EPS = 1e-6

ATTN_HEADS = 8
ATTN_WIDTH = ATTN_HEADS * HEAD_DIM
MOBA_BLOCK = 256
MOBA_TOPK = 3
NEG_INF = -1e30
MOBA_HEADS_PER_STEP = 4
VT_ONES = 16
VT_ROWS = HEAD_DIM + VT_ONES
QK_SCALE_LOG2 = HEAD_DIM ** -0.5 * math.log2(math.e)

CONV_WIDTH = 512
CONV_KERNEL = 3

RET_HEADS = 4
RET_QK_WIDTH = RET_HEADS * HEAD_DIM
RET_V_DIM = 2 * HEAD_DIM
RET_V_WIDTH = RET_HEADS * RET_V_DIM
RET_TILE = 256

D_FF = 3584
N_EXPERTS = 8
TOP_K = 2

LANES = 128
IN_PROJ_WIDTH = 7680
COL_AQ, COL_AK, COL_AV = 0, 512, 1024
COL_CB, COL_CC, COL_CH = 1536, 2048, 2560
COL_RV, COL_RG = 3072, 3584
COL_GATES = 4096
COL_RQ, COL_RK = 7168, 7424

PROJ_TM = 2048
PROJ_TN = 512
MERGE_TM = 512
FFN_TM = 1024
FFN_TF = 512
ROUTE_TM = 1024
VMEM_LIMIT = 56 * 1024 * 1024


def _cparams(sem):
    return pltpu.CompilerParams(dimension_semantics=sem, vmem_limit_bytes=VMEM_LIMIT)


def _rope(acc, cos, sin_signed):
    lane = lax.broadcasted_iota(jnp.int32, (1, LANES), 1)
    first_half = (lane % HEAD_DIM) < (HEAD_DIM // 2)
    outs = []
    for g in range(acc.shape[1] // LANES):
        blk = acc[:, g * LANES:(g + 1) * LANES]
        packed = blk.astype(BF16)
        partner = jnp.where(first_half,
                            pltpu.roll(packed, LANES - HEAD_DIM // 2, 1),
                            pltpu.roll(packed, HEAD_DIM // 2, 1)).astype(F32)
        outs.append(blk * cos + partner * sin_signed)
    return jnp.concatenate(outs, axis=1)


def _inproj_kernel(x_ref, nw_ref, w_ref, cos_ref, sin_ref, proj_ref, vt_ref, h_scr):
    j = pl.program_id(1)

    @pl.when(j == 0)
    def _():
        x = x_ref[...]
        ms = jnp.mean(x * x, axis=-1, keepdims=True)
        h_scr[...] = (x * lax.rsqrt(ms + RMS_EPS) * nw_ref[...]).astype(BF16)

    acc = jnp.dot(h_scr[...], w_ref[...].astype(BF16), preferred_element_type=F32)
    jq, jk, jv, jr = COL_AQ // PROJ_TN, COL_AK // PROJ_TN, COL_AV // PROJ_TN, COL_RQ // PROJ_TN
    scale = HEAD_DIM ** -0.5
    proj_ref[...] = acc.astype(BF16)

    @pl.when(j == jq)
    def _():
        proj_ref[...] = (_rope(acc, cos_ref[...], sin_ref[...]) * QK_SCALE_LOG2).astype(BF16)

    @pl.when(j == jk)
    def _():
        proj_ref[...] = _rope(acc, cos_ref[...], sin_ref[...]).astype(BF16)

    @pl.when(j == jv)
    def _():
        ones = jnp.ones((VT_ONES, MOBA_BLOCK), F32)
        for c in range(PROJ_TM // MOBA_BLOCK):
            v_t = acc[c * MOBA_BLOCK:(c + 1) * MOBA_BLOCK, :].T
            rows = []
            for h in range(ATTN_HEADS):
                rows += [v_t[h * HEAD_DIM:(h + 1) * HEAD_DIM], ones]
            vt_ref[c] = jnp.concatenate(rows, axis=0).astype(BF16)

    @pl.when(j == jr)
    def _():
        r = _rope(acc, cos_ref[...], sin_ref[...])
        col = lax.broadcasted_iota(jnp.int32, (1, PROJ_TN), 1)
        r = r * jnp.where(col >= RET_QK_WIDTH, scale, 1.0)
        proj_ref[...] = r.astype(BF16)


def _inproj_src_block(j):
    first_moved = COL_RV // PROJ_TN
    last = IN_PROJ_WIDTH // PROJ_TN - 1
    return jnp.where(j < first_moved, j, jnp.where(j == last, first_moved, j + 1))


def _inproj(x2, norm_w, w_in, layer, cos_t, sin_t, batch, seq):
    n = x2.shape[0]
    nst = seq // PROJ_TM
    nblk = PROJ_TM // MOBA_BLOCK
    return pl.pallas_call(
        _inproj_kernel,
        grid=(n // PROJ_TM, IN_PROJ_WIDTH // PROJ_TN),
        in_specs=[
            pl.BlockSpec((PROJ_TM, D_MODEL), lambda i, j: (i, 0)),
            pl.BlockSpec((1, D_MODEL), lambda i, j: (0, 0)),
            pl.BlockSpec((None, D_MODEL, PROJ_TN), lambda i, j: (layer, 0, _inproj_src_block(j))),
            pl.BlockSpec((PROJ_TM, LANES), lambda i, j: (i % nst, 0)),
            pl.BlockSpec((PROJ_TM, LANES), lambda i, j: (i % nst, 0)),
        ],
        out_specs=[
            pl.BlockSpec((PROJ_TM, PROJ_TN), lambda i, j: (i, j)),
            pl.BlockSpec((None, nblk, ATTN_HEADS * VT_ROWS, MOBA_BLOCK),
                         lambda i, j: (i // nst, i % nst, 0, 0)),
        ],
        out_shape=[
            jax.ShapeDtypeStruct((n, IN_PROJ_WIDTH), BF16),
            jax.ShapeDtypeStruct((batch, seq // MOBA_BLOCK, ATTN_HEADS * VT_ROWS, MOBA_BLOCK), BF16),
        ],
        scratch_shapes=[pltpu.VMEM((PROJ_TM, D_MODEL), BF16)],
        compiler_params=_cparams(("arbitrary", "arbitrary")),
        name="inproj",
    )(x2, norm_w, w_in, cos_t, sin_t)


_NT = (((1,), (1,)), ((), ()))


def _moba_kernel(q_ref, k_ref, vt_ref, o_ref, km_scr, ka_scr, sel_scr, qt_scr, s_scr, m_scr, acc_scr,
                 *, nb):
    nh = MOBA_HEADS_PER_STEP
    blk = MOBA_BLOCK
    tq = 2 * blk
    lane = lax.broadcasted_iota(jnp.int32, (1, LANES), 1)
    crow = lax.broadcasted_iota(jnp.int32, (LANES, 1), 0)

    def group(ref, rows, h):
        g = h // 2
        return ref[rows, g * LANES:(g + 1) * LANES]

    def prep(i, c):
        rows = pl.ds(pl.multiple_of(i * blk, blk), blk)
        for h in range(nh):
            kb, hh = group(k_ref, rows, h), h % 2
            if hh == 0:
                km_scr[h // 2, pl.ds(i, 1), :] = jnp.sum(kb.astype(F32), axis=0, keepdims=True) * (1.0 / blk)
            onehot = jnp.where(lane == HEAD_DIM * (1 - hh) + i, 1.0, 0.0).astype(BF16)
            ka_scr[h, rows, :] = jnp.where((lane // HEAD_DIM) == hh, kb, onehot)
        return c

    lax.fori_loop(0, nb, prep, 0)

    causal = (lax.broadcasted_iota(jnp.int32, (blk, blk), 0)
              <= lax.broadcasted_iota(jnp.int32, (blk, blk), 1))
    blk_id = lax.broadcasted_iota(jnp.int32, (nb, tq), 0)
    q_half = (lax.broadcasted_iota(jnp.int32, (nb, tq), 1) >= blk).astype(jnp.int32)

    def pv(t, h, p):
        rows = slice(h * VT_ROWS, (h + 1) * VT_ROWS)
        pb = p.astype(BF16)
        return (jnp.dot(vt_ref[2 * t, rows, :], pb[0:blk], preferred_element_type=F32)
                + jnp.dot(vt_ref[2 * t + 1, rows, :], pb[blk:tq], preferred_element_type=F32))

    def qtile(jt, c):
        row0 = pl.multiple_of(jt * tq, tq)
        q_ts =[group(q_ref, pl.ds(row0, tq), h).astype(F32).T for h in range(0, nh, 2)]
        for h in range(nh):
            hh = h % 2
            q_t = q_ts[h // 2]
            k_diag = group(k_ref, pl.ds(row0, tq), h)
            q_m = jnp.where((crow // HEAD_DIM) == hh, q_t, 0.0)
            q_mb = q_m.astype(BF16)
            km = jnp.where((lane // HEAD_DIM) == hh, km_scr[h // 2], 0.0)
            km_hi = km.astype(BF16)
            km_lo = (km - km_hi.astype(F32)).astype(BF16)
            gate = (jnp.dot(km_hi, q_mb, preferred_element_type=F32)
                    + jnp.dot(km_lo, q_mb, preferred_element_type=F32))
            past = blk_id < 2 * jt + q_half
            gate = jnp.where(past, gate, -jnp.inf)
            rank = jnp.zeros((nb, tq), jnp.int32)
            for ip in range(nb):
                gi = gate[ip:ip + 1, :]
                beats = (gi > gate) | ((gi == gate) & (blk_id > ip))
                rank = rank + beats.astype(jnp.int32)
            sel = (rank < MOBA_TOPK) & past
            sel_scr[h] = sel.astype(F32)
            bias = jnp.where(sel, 0.0, NEG_INF)
            spare = HEAD_DIM * (1 - hh)
            pieces = [jnp.zeros((spare, tq), F32)] if spare else []
            pieces += [bias, jnp.zeros((LANES - spare - nb, tq), F32)]
            qt_scr[h] = (q_m + jnp.concatenate(pieces, axis=0)).astype(BF16)

            s_top = jnp.dot(k_diag[0:blk], q_mb, preferred_element_type=F32)
            s11 = jnp.where(causal, s_top[:, 0:blk], NEG_INF)
            s12 = jnp.where(sel_scr[h, pl.ds(2 * jt, 1), blk:tq] > 0.0, s_top[:, blk:tq], NEG_INF)
            s22 = jnp.where(causal, jnp.dot(k_diag[blk:tq], q_mb[:, blk:tq], preferred_element_type=F32),
                            NEG_INF)
            m_a = jnp.max(s11, axis=0, keepdims=True)
            m_b = jnp.maximum(jnp.max(s12, axis=0, keepdims=True), jnp.max(s22, axis=0, keepdims=True))
            s_scr[2 * h, 0:blk, 0:blk] = s11
            s_scr[2 * h, 0:blk, blk:tq] = s12
            s_scr[2 * h, blk:tq, 0:blk] = jnp.full((blk, blk), NEG_INF, F32)
            s_scr[2 * h, blk:tq, blk:tq] = s22
            m_scr[2 * h] = jnp.concatenate([m_a, m_b], axis=1)
            m_scr[2 * h + 1] = m_scr[2 * h]
            acc_scr[h] = jnp.zeros((VT_ROWS, tq), F32)

        def park(t, h, slot):
            s = jnp.dot(ka_scr[h, pl.ds(pl.multiple_of(t * tq, tq), tq), :], qt_scr[h],
                        preferred_element_type=F32)
            s_scr[2 * h + slot] = s
            m_scr[2 * h + 1] = jnp.maximum(m_scr[2 * h + 1], jnp.max(s, axis=0, keepdims=True))

        def consume(k, h, slot):
            m_old, m_new = m_scr[2 * h], m_scr[2 * h + 1]
            acc_scr[h] = (jnp.exp2(m_old - m_new) * acc_scr[h]
                          + pv(jnp.where(k == 0, jt, k - 1), h, jnp.exp2(s_scr[2 * h + slot] - m_new)))
            m_scr[2 * h] = m_new

        def step(k, slot):
            for h in range(nh):
                consume(k, h, slot)
                park(k, h, 1 - slot)

        def run(trips, body):
            def wrapped(i, c):
                body(i)
                return c
            lax.fori_loop(0, trips, wrapped, 0)

        run(jt // 2, lambda u: (step(2 * u, 0), step(2 * u + 1, 1)))
        run(jt % 2, lambda i: step(jt - 1, 0))
        run(jt % 2, lambda i: [consume(jt, h, 1) for h in range(nh)])
        run(1 - jt % 2, lambda i: [consume(jt, h, 0) for h in range(nh)])
        out_t = jnp.concatenate([acc_scr[h, 0:HEAD_DIM, :] / acc_scr[h, HEAD_DIM:HEAD_DIM + 1, :]
                                 for h in range(nh)], axis=0)
        o_ref[pl.ds(row0, tq), :] = out_t.T.astype(BF16)
        return c

    lax.fori_loop(0, nb // 2, qtile, 0)


def _moba(proj, vt, batch, seq):
    nb = seq // MOBA_BLOCK
    assert nb % 8 == 0 and nb <= HEAD_DIM
    nh = MOBA_HEADS_PER_STEP
    width = nh * HEAD_DIM
    return pl.pallas_call(
        functools.partial(_moba_kernel, nb=nb),
        grid=(batch, ATTN_HEADS // nh),
        in_specs=[
            pl.BlockSpec((seq, width), lambda b, p: (b, COL_AQ // width + p)),
            pl.BlockSpec((seq, width), lambda b, p: (b, COL_AK // width + p)),
            pl.BlockSpec((None, nb, nh * VT_ROWS, MOBA_BLOCK), lambda b, p: (b, 0, p, 0)),
        ],
        out_specs=pl.BlockSpec((seq, width), lambda b, p: (b, p)),
        out_shape=jax.ShapeDtypeStruct((batch * seq, ATTN_WIDTH), BF16),
        scratch_shapes=[pltpu.VMEM((nh // 2, nb, LANES), F32), pltpu.VMEM((nh, seq, LANES), BF16),
                        pltpu.VMEM((nh, nb, 2 * MOBA_BLOCK), F32),
                        pltpu.VMEM((nh, LANES, 2 * MOBA_BLOCK), BF16),
                        pltpu.VMEM((2 * nh, 2 * MOBA_BLOCK, 2 * MOBA_BLOCK), F32),
                        pltpu.VMEM((2 * nh, 1, 2 * MOBA_BLOCK), F32),
                        pltpu.VMEM((nh, VT_ROWS, 2 * MOBA_BLOCK), F32)],
        compiler_params=_cparams(("arbitrary", "arbitrary")),
        name="moba",
    )(proj, proj, vt)


def _ret_log_gamma():
    return [math.log1p(-(2.0 ** (-5.0 - h))) for h in range(RET_HEADS)]


def _retention_tables():
    c = RET_TILE
    lg = np.array(_ret_log_gamma(), np.float64)
    n = np.arange(c, dtype=np.float64)
    diff = n[:, None] - n[None, :]
    decay = np.where(diff[None] >= 0, np.exp(np.maximum(diff, 0.0)[None] * lg[:, None, None]), 0.0)
    head_of_lane = np.arange(RET_QK_WIDTH) // HEAD_DIM
    xi = np.exp((n + 1.0)[:, None] * lg[head_of_lane][None, :])
    zeta = np.exp((c - 1.0 - n)[:, None] * lg[head_of_lane][None, :])
    chunk_decay = np.exp(c * lg[head_of_lane])[:, None]
    return (jnp.asarray(decay, F32), jnp.asarray(xi, F32), jnp.asarray(zeta, F32),
            jnp.asarray(np.broadcast_to(chunk_decay, (RET_QK_WIDTH, LANES)), F32))


def _retention_kernel(q_ref, k_ref, v_ref, g_ref, nw_ref, decay_ref, xi_ref, zeta_ref, cd_ref,
                      o_ref, state_scr):
    @pl.when(pl.program_id(1) == 0)
    def _():
        state_scr[...] = jnp.zeros_like(state_scr)

    lane = lax.broadcasted_iota(jnp.int32, (1, LANES), 1)
    srow = lax.broadcasted_iota(jnp.int32, (LANES, 1), 0)
    for pr in range(RET_HEADS // 2):
        cols = slice(pr * LANES, (pr + 1) * LANES)
        q = q_ref[:, cols]
        k = k_ref[:, cols]
        state = state_scr[cols, :]
        state_bf = state.astype(BF16)
        q_xi = (q.astype(F32) * xi_ref[:, cols]).astype(BF16)
        kz_t = (k.astype(F32) * zeta_ref[:, cols]).T.astype(BF16)
        upd = []
        for hh in range(2):
            h = 2 * pr + hh
            hmask = (lane // HEAD_DIM) == hh
            vcols = slice(h * RET_V_DIM, (h + 1) * RET_V_DIM)
            v = v_ref[:, vcols]
            qm = jnp.where(hmask, q, jnp.zeros_like(q))
            scores = lax.dot_general(qm, k, _NT, preferred_element_type=F32) * decay_ref[h]
            o = jnp.dot(scores.astype(BF16), v, preferred_element_type=F32)
            o = o + jnp.dot(jnp.where(hmask, q_xi, jnp.zeros_like(q_xi)), state_bf,
                            preferred_element_type=F32)
            upd.append(jnp.dot(kz_t, v, preferred_element_type=F32))
            ms = jnp.mean(o * o, axis=-1, keepdims=True)
            y = o * lax.rsqrt(ms + RMS_EPS) * nw_ref[:, vcols]
            g = g_ref[:, vcols].astype(F32)
            o_ref[:, vcols] = (g * jax.nn.sigmoid(g) * y).astype(BF16)
        state_scr[cols, :] = state * cd_ref[cols, :] + jnp.where(srow < HEAD_DIM, upd[0], upd[1])


def _retention(proj, ret_norm_w, tables, batch, seq):
    nc = seq // RET_TILE
    decay, xi, zeta, cd = tables
    row = lambda b, c: b * nc + c
    const2 = lambda b, c: (0, 0)
    return pl.pallas_call(
        _retention_kernel,
        grid=(batch, nc),
        in_specs=[
            pl.BlockSpec((RET_TILE, RET_QK_WIDTH), lambda b, c: (row(b, c), COL_RQ // RET_QK_WIDTH)),
            pl.BlockSpec((RET_TILE, RET_QK_WIDTH), lambda b, c: (row(b, c), COL_RK // RET_QK_WIDTH)),
            pl.BlockSpec((RET_TILE, RET_V_WIDTH), lambda b, c: (row(b, c), COL_RV // RET_V_WIDTH)),
            pl.BlockSpec((RET_TILE, RET_V_WIDTH), lambda b, c: (row(b, c), COL_RG // RET_V_WIDTH)),
            pl.BlockSpec((1, RET_V_WIDTH), const2),
            pl.BlockSpec((RET_HEADS, RET_TILE, RET_TILE), lambda b, c: (0, 0, 0)),
            pl.BlockSpec((RET_TILE, RET_QK_WIDTH), const2),
            pl.BlockSpec((RET_TILE, RET_QK_WIDTH), const2),
            pl.BlockSpec((RET_QK_WIDTH, LANES), const2),
        ],
        out_specs=pl.BlockSpec((RET_TILE, RET_V_WIDTH), lambda b, c: (row(b, c), 0)),
        out_shape=jax.ShapeDtypeStruct((batch * seq, RET_V_WIDTH), BF16),
        scratch_shapes=[pltpu.VMEM((RET_QK_WIDTH, RET_V_DIM), F32)],
        compiler_params=_cparams(("arbitrary", "arbitrary")),
        name="retention",
    )(proj, proj, proj, proj, ret_norm_w, decay, xi, zeta, cd)


CONV_HALO = 8


def _merge_kernel(x_ref, ya_ref, cb_ref, cc_ref, ch_ref, yr_ref, ga_ref, gc_ref, gr_ref, cw_ref,
                  cbias_ref, wa_ref, wc_ref, wr_ref, wo_ref, o_ref, u_scr, *, tiles_per_seq):
    i = pl.program_id(0)
    tm = x_ref.shape[0]

    @pl.when(i % tiles_per_seq == 0)
    def _():
        u_scr[0:CONV_HALO, :] = jnp.zeros((CONV_HALO, CONV_WIDTH), F32)

    u_scr[CONV_HALO:CONV_HALO + tm, :] = cc_ref[...].astype(F32) * ch_ref[...].astype(F32)
    conv = (cw_ref[2:3, :] * u_scr[CONV_HALO:CONV_HALO + tm, :]
            + cw_ref[1:2, :] * u_scr[CONV_HALO - 1:CONV_HALO - 1 + tm, :]
            + cw_ref[0:1, :] * u_scr[CONV_HALO - 2:CONV_HALO - 2 + tm, :]
            + cbias_ref[...])
    y_conv = (cb_ref[...].astype(F32) * conv).astype(BF16)
    u_scr[0:CONV_HALO, :] = u_scr[tm:tm + CONV_HALO, :]

    def gate(g_ref):
        return jax.nn.sigmoid(g_ref[...].astype(F32))

    merged = gate(ga_ref) * jnp.dot(ya_ref[...], wa_ref[...], preferred_element_type=F32)
    merged = merged + gate(gc_ref) * jnp.dot(y_conv, wc_ref[...], preferred_element_type=F32)
    merged = merged + gate(gr_ref) * jnp.dot(yr_ref[...], wr_ref[...], preferred_element_type=F32)
    o_ref[...] = x_ref[...] + jnp.dot(merged.astype(BF16), wo_ref[...], preferred_element_type=F32)


def _merge(x2, y_attn, proj, y_ret, conv_w, conv_b, wa, wc, wr, wo, seq):
    n = x2.shape[0]
    tm = MERGE_TM
    const = lambda i: (0, 0)
    wide = lambda c: pl.BlockSpec((tm, CONV_WIDTH), lambda i: (i, c // CONV_WIDTH))
    gate_spec = lambda b: pl.BlockSpec((tm, D_MODEL), lambda i: (i, COL_GATES // D_MODEL + b))
    return pl.pallas_call(
        functools.partial(_merge_kernel, tiles_per_seq=seq // tm),
        grid=(n // tm,),
        in_specs=[
            pl.BlockSpec((tm, D_MODEL), lambda i: (i, 0)),
            pl.BlockSpec((tm, ATTN_WIDTH), lambda i: (i, 0)),
            wide(COL_CB), wide(COL_CC), wide(COL_CH),
            pl.BlockSpec((tm, RET_V_WIDTH), lambda i: (i, 0)),
            gate_spec(0), gate_spec(1), gate_spec(2),
            pl.BlockSpec((CONV_KERNEL, CONV_WIDTH), const),
            pl.BlockSpec((1, CONV_WIDTH), const),
            pl.BlockSpec((ATTN_WIDTH, D_MODEL), const),
            pl.BlockSpec((CONV_WIDTH, D_MODEL), const),
            pl.BlockSpec((RET_V_WIDTH, D_MODEL), const),
            pl.BlockSpec((D_MODEL, D_MODEL), const),
        ],
        out_specs=pl.BlockSpec((tm, D_MODEL), lambda i: (i, 0)),
        out_shape=jax.ShapeDtypeStruct((n, D_MODEL), F32),
        scratch_shapes=[pltpu.VMEM((tm + CONV_HALO, CONV_WIDTH), F32)],
        compiler_params=_cparams(("arbitrary",)),
        name="merge",
    )(x2, y_attn, proj, proj, proj, y_ret, proj, proj, proj, conv_w, conv_b, wa, wc, wr, wo)


FFN_CHUNKS = D_FF // FFN_TF
FFN_STEPS = FFN_CHUNKS + 1


def _swiglu_pipelined(c, h_scr, wg_ref, wu_ref, wd_ref, gu_scr, acc_scr, enable=True, beside=None):
    beside = beside or (lambda: None)
    def fold(slot):
        g, u = gu_scr[2 * slot], gu_scr[2 * slot + 1]
        a = (g * jax.nn.sigmoid(g) * u).astype(BF16)
        acc_scr[...] += jnp.dot(a, wd_ref[...].astype(BF16), preferred_element_type=F32)

    def park(slot):
        h = h_scr[...]
        gu_scr[2 * slot] = jnp.dot(h, wg_ref[...].astype(BF16), preferred_element_type=F32)
        gu_scr[2 * slot + 1] = jnp.dot(h, wu_ref[...].astype(BF16), preferred_element_type=F32)

    @pl.when(enable & (c == 0))
    def _():
        beside()
        park(0)

    for parity in (0, 1):
        @pl.when(enable & (c >= 1) & (c < FFN_CHUNKS) & (c % 2 == parity))
        def _():
            beside()
            fold(1 - parity)
            park(parity)

    @pl.when(enable & (c == FFN_CHUNKS))
    def _():
        beside()
        fold((FFN_CHUNKS - 1) % 2)


def _dense_ffn_kernel(x_ref, nw_ref, wg_ref, wu_ref, wd_ref, o_ref, h_scr, acc_scr):
    c = pl.program_id(1)

    @pl.when(c == 0)
    def _():
        x = x_ref[...]
        ms = jnp.mean(x * x, axis=-1, keepdims=True)
        h_scr[...] = (x * lax.rsqrt(ms + RMS_EPS) * nw_ref[...]).astype(BF16)
        acc_scr[...] = jnp.zeros_like(acc_scr)

    h = h_scr[...]
    g = jnp.dot(h, wg_ref[...].astype(BF16), preferred_element_type=F32)
    u = jnp.dot(h, wu_ref[...].astype(BF16), preferred_element_type=F32)
    a = (g * jax.nn.sigmoid(g) * u).astype(BF16)
    acc_scr[...] += jnp.dot(a, wd_ref[...].astype(BF16), preferred_element_type=F32)

    @pl.when(c == FFN_CHUNKS - 1)
    def _():
        o_ref[...] = x_ref[...] + acc_scr[...]


def _dense_ffn(x2, norm_w, wg, wu, wd, li):
    n = x2.shape[0]
    tm, tf = FFN_TM, FFN_TF
    return pl.pallas_call(
        _dense_ffn_kernel,
        grid=(n // tm, FFN_CHUNKS),
        in_specs=[
            pl.BlockSpec((tm, D_MODEL), lambda i, c: (i, 0)),
            pl.BlockSpec((1, D_MODEL), lambda i, c: (0, 0)),
            pl.BlockSpec((None, D_MODEL, tf), lambda i, c: (li, 0, c)),
            pl.BlockSpec((None, D_MODEL, tf), lambda i, c: (li, 0, c)),
            pl.BlockSpec((None, tf, D_MODEL), lambda i, c: (li, c, 0)),
        ],
        out_specs=pl.BlockSpec((tm, D_MODEL), lambda i, c: (i, 0)),
        out_shape=jax.ShapeDtypeStruct((n, D_MODEL), F32),
        scratch_shapes=[pltpu.VMEM((tm, D_MODEL), BF16), pltpu.VMEM((tm, D_MODEL), F32)],
        compiler_params=_cparams(("arbitrary", "arbitrary")),
        name="dense_ffn",
    )(x2, norm_w, wg, wu, wd)


SLAB = D_MODEL // LANES


def _pack_rows(v, o_ref):
    rows = v.shape[0]
    for s in range(SLAB):
        o_ref[pl.ds(s, rows, stride=SLAB), :] = v[:, s * LANES:(s + 1) * LANES]


def _unpack_rows(x_ref, rows):
    return jnp.concatenate([x_ref[pl.ds(s, rows, stride=SLAB), :] for s in range(SLAB)], axis=1)


def _route_kernel(x_ref, nw_ref, wr_ref, h_ref, idx_ref, wgt_ref):
    x = x_ref[...]
    ms = jnp.mean(x * x, axis=-1, keepdims=True)
    h = x * lax.rsqrt(ms + RMS_EPS) * nw_ref[...]
    _pack_rows(h, h_ref)
    w = wr_ref[...]
    h_hi, w_hi = h.astype(BF16), w.astype(BF16)
    h_lo, w_lo = (h - h_hi.astype(F32)).astype(BF16), (w - w_hi.astype(F32)).astype(BF16)
    logits = (jnp.dot(h_hi, w_hi, preferred_element_type=F32)
              + jnp.dot(h_lo, w_hi, preferred_element_type=F32)
              + jnp.dot(h_hi, w_lo, preferred_element_type=F32))
    eid = lax.broadcasted_iota(jnp.int32, logits.shape, 1).astype(F32)
    logits = jnp.where(eid < N_EXPERTS, logits, -jnp.inf)
    m1 = jnp.max(logits, axis=-1, keepdims=True)
    i1 = jnp.min(jnp.where(logits == m1, eid, float(LANES)), axis=-1, keepdims=True)
    rest = jnp.where(eid == i1, -jnp.inf, logits)
    m2 = jnp.max(rest, axis=-1, keepdims=True)
    i2 = jnp.min(jnp.where(rest == m2, eid, float(LANES)), axis=-1, keepdims=True)
    e2 = jnp.exp(m2 - m1)
    denom = 1.0 + e2
    idx_ref[...] = jnp.where(eid == 0.0, i1, jnp.where(eid == 1.0, i2, 0.0)).astype(jnp.int32)
    wgt_ref[...] = jnp.where(eid == 0.0, 1.0 / denom, jnp.where(eid == 1.0, e2 / denom, 0.0))


def _route(x2, norm_w, w_router):
    n = x2.shape[0]
    tm = ROUTE_TM
    return pl.pallas_call(
        _route_kernel,
        grid=(n // tm,),
        in_specs=[
            pl.BlockSpec((tm, D_MODEL), lambda i: (i, 0)),
            pl.BlockSpec((1, D_MODEL), lambda i: (0, 0)),
            pl.BlockSpec((D_MODEL, LANES), lambda i: (0, 0)),
        ],
        out_specs=[
            pl.BlockSpec((tm * SLAB, LANES), lambda i: (i, 0)),
            pl.BlockSpec((tm, LANES), lambda i: (i, 0)),
            pl.BlockSpec((tm, LANES), lambda i: (i, 0)),
        ],
        out_shape=[
            jax.ShapeDtypeStruct((n * SLAB, LANES), F32),
            jax.ShapeDtypeStruct((n, LANES), jnp.int32),
            jax.ShapeDtypeStruct((n, LANES), F32),
        ],
        compiler_params=_cparams(("arbitrary",)),
        name="route",
    )(x2, norm_w, jnp.pad(w_router, ((0, 0), (0, LANES - N_EXPERTS))))


MOE_STEPS = FFN_STEPS
MOE_ROWS_PER_STEP = -(-FFN_TM // MOE_STEPS)
MOE_DMA_ROWS = MOE_ROWS_PER_STEP * MOE_STEPS
MOE_BUF_ROWS = -(-MOE_DMA_ROWS // 8) * 8
MOE_SPARE_TILES = 2


def _moe_ffn_kernel(te_ref, nv_ref, gid0_ref, gidn_ref, sid_ref, h3_ref, wg_ref, wu_ref, wd_ref,
                    y_init_ref, y_ref, gbuf, obuf, h_scr, gu_scr, acc_scr, gsem, ssem):
    t = pl.program_id(0)
    c = pl.program_id(1)
    nv = nv_ref[0]
    slot = t % 2
    other = 1 - slot
    first = c == 0

    def slab(ref, start):
        return ref.at[pl.ds(pl.multiple_of(start, SLAB), SLAB)]

    def gather(ids_ref, row, s):
        return pltpu.make_async_copy(slab(h3_ref, ids_ref[0, 0, row]), slab(gbuf.at[s], row * SLAB),
                                     gsem.at[s])

    def scatter(row, s):
        return pltpu.make_async_copy(slab(obuf.at[s], row * SLAB), slab(y_ref, sid_ref[0, 0, row]),
                                     ssem.at[s])

    def tile_gather(s):
        return pltpu.make_async_copy(h3_ref.at[pl.ds(0, MOE_DMA_ROWS * SLAB)],
                                     gbuf.at[s, pl.ds(0, MOE_DMA_ROWS * SLAB)], gsem.at[s])

    def tile_scatter(s):
        return pltpu.make_async_copy(obuf.at[s, pl.ds(0, MOE_DMA_ROWS * SLAB)],
                                     y_ref.at[pl.ds(0, MOE_DMA_ROWS * SLAB)], ssem.at[s])

    @pl.when(first & (t == 0))
    def _():
        obuf[...] = jnp.zeros_like(obuf)

        def body(r, carry):
            gather(gid0_ref, r, 0).start()
            return carry

        lax.fori_loop(0, MOE_DMA_ROWS, body, 0, unroll=MOE_STEPS)

    @pl.when(first & (t <= nv))
    def _():
        tile_gather(slot).wait()

    @pl.when(first & (t >= 1) & (t <= nv + 1))
    def _():
        tile_scatter(slot).wait()

    @pl.when(first & (t < nv))
    def _():
        h_scr[...] = _unpack_rows(gbuf.at[slot], FFN_TM).astype(BF16)
        acc_scr[...] = jnp.zeros_like(acc_scr)

    def row_dmas():
        for u in range(MOE_ROWS_PER_STEP):
            row = c * MOE_ROWS_PER_STEP + u
            gather(gidn_ref, row, other).start(priority=1)
            scatter(row, other).start(priority=1)

    _swiglu_pipelined(c, h_scr, wg_ref, wu_ref, wd_ref, gu_scr, acc_scr, enable=t < nv, beside=row_dmas)

    @pl.when(t == nv)
    def _():
        def body(u, carry):
            scatter(c * MOE_ROWS_PER_STEP + u, other).start()
            return carry

        lax.fori_loop(0, MOE_ROWS_PER_STEP, body, 0, unroll=MOE_STEPS)

    @pl.when((t < nv) & (c == pl.num_programs(1) - 1))
    def _():
        _pack_rows(acc_scr[...], obuf.at[slot])


def _moe_ffn(tile_expert, n_valid, gid, sid, h3, wg, wu, wd, li, y_rows):
    nt = gid.shape[0]
    tm, tf = FFN_TM, FFN_TF
    ids_block = (1, 1, MOE_BUF_ROWS)
    smem = pltpu.SMEM
    wspec = lambda shape, imap: pl.BlockSpec((None, None) + shape, imap)
    expert = lambda t, te, nv: te[jnp.minimum(t, nv[0] - 1)]
    last = FFN_CHUNKS - 1
    up_chunk = lambda t, c, nv: jnp.where(t < nv[0], jnp.minimum(c, last), last)
    down_chunk = lambda t, c, nv: jnp.where(t < nv[0], jnp.maximum(c - 1, 0), last)
    grid_spec = pltpu.PrefetchScalarGridSpec(
        num_scalar_prefetch=2,
        grid=(nt, MOE_STEPS),
        in_specs=[
            pl.BlockSpec(ids_block, lambda t, c, te, nv: (0, 0, 0), memory_space=smem),
            pl.BlockSpec(ids_block, lambda t, c, te, nv: (jnp.minimum(t + 1, nt - 1), 0, 0),
                         memory_space=smem),
            pl.BlockSpec(ids_block, lambda t, c, te, nv: (t, 0, 0), memory_space=smem),
            pl.BlockSpec(memory_space=pl.ANY),
            wspec((D_MODEL, tf), lambda t, c, te, nv: (li, expert(t, te, nv), 0, up_chunk(t, c, nv))),
            wspec((D_MODEL, tf), lambda t, c, te, nv: (li, expert(t, te, nv), 0, up_chunk(t, c, nv))),
            wspec((tf, D_MODEL), lambda t, c, te, nv: (li, expert(t, te, nv), down_chunk(t, c, nv), 0)),
            pl.BlockSpec(memory_space=pl.ANY),
        ],
        out_specs=pl.BlockSpec(memory_space=pl.ANY),
        scratch_shapes=[
            pltpu.VMEM((2, MOE_BUF_ROWS * SLAB, LANES), F32),
            pltpu.VMEM((2, MOE_BUF_ROWS * SLAB, LANES), F32),
            pltpu.VMEM((tm, D_MODEL), BF16),
            pltpu.VMEM((4, tm, tf), F32),
            pltpu.VMEM((tm, D_MODEL), F32),
            pltpu.SemaphoreType.DMA((2,)),
            pltpu.SemaphoreType.DMA((2,)),
        ],
    )
    return pl.pallas_call(
        _moe_ffn_kernel,
        grid_spec=grid_spec,
        out_shape=jax.ShapeDtypeStruct((y_rows * SLAB, LANES), F32),
        input_output_aliases={9: 0},
        compiler_params=_cparams(("arbitrary", "arbitrary")),
        name="moe_ffn",
    )(tile_expert, n_valid, gid * SLAB, gid * SLAB, sid * SLAB, h3, wg, wu, wd,
      jnp.zeros((y_rows * SLAB, LANES), F32))


def _combine_kernel(x_ref, y0_ref, y1_ref, w_ref, nw_ref, o_ref, *, final_norm):
    w = w_ref[...]
    rows = x_ref.shape[0]
    x = x_ref[...] + w[:, 0:1] * _unpack_rows(y0_ref, rows) + w[:, 1:2] * _unpack_rows(y1_ref, rows)
    if final_norm:
        ms = jnp.mean(x * x, axis=-1, keepdims=True)
        x = x * lax.rsqrt(ms + RMS_EPS) * nw_ref[...]
    o_ref[...] = x


def _combine(x2, y_pairs, wgt, norm_w, final_norm):
    n = x2.shape[0]
    tm = MERGE_TM
    nt = n // tm
    return pl.pallas_call(
        functools.partial(_combine_kernel, final_norm=final_norm),
        grid=(nt,),
        in_specs=[
            pl.BlockSpec((tm, D_MODEL), lambda i: (i, 0)),
            pl.BlockSpec((tm * SLAB, LANES), lambda i: (i, 0)),
            pl.BlockSpec((tm * SLAB, LANES), lambda i: (i + nt, 0)),
            pl.BlockSpec((tm, LANES), lambda i: (i, 0)),
            pl.BlockSpec((1, D_MODEL), lambda i: (0, 0)),
        ],
        out_specs=pl.BlockSpec((tm, D_MODEL), lambda i: (i, 0)),
        out_shape=jax.ShapeDtypeStruct((n, D_MODEL), F32),
        compiler_params=_cparams(("arbitrary",)),
        name="combine",
    )(x2, y_pairs, y_pairs, wgt, norm_w)


def _moe_layout(idx):
    n = idx.shape[0]
    tm = FFN_TM
    flat = idx.T.reshape(-1)
    onehot = (flat[:, None] == jnp.arange(N_EXPERTS, dtype=jnp.int32)[None, :]).astype(jnp.int32)
    cum = jnp.cumsum(onehot, axis=0)
    counts = cum[-1]
    rank = jnp.sum(cum * onehot, axis=1) - 1
    padded = ((counts + tm - 1) // tm) * tm
    ends = jnp.cumsum(padded)
    starts = ends - padded
    dest = starts[flat] + rank
    pairs = TOP_K * n
    nt = pairs // tm + N_EXPERTS + MOE_SPARE_TILES
    p = nt * tm
    tile_start = jnp.arange(nt, dtype=jnp.int32) * tm
    tile_expert = jnp.minimum(jnp.sum((ends[None, :] <= tile_start[:, None]).astype(jnp.int32), axis=1),
                              N_EXPERTS - 1)
    n_valid = (ends[-1] // tm).astype(jnp.int32).reshape(1)
    real_before = jnp.cumsum(counts)[tile_expert]
    pos = jnp.arange(p, dtype=jnp.int32).reshape(nt, tm)
    pad_row = (pairs + pos - real_before[:, None]).reshape(p)
    row_pair = pad_row.at[dest].set(jnp.arange(pairs, dtype=jnp.int32), unique_indices=True,
                                    mode="promise_in_bounds")
    row_token = jnp.where(row_pair < pairs, row_pair % n, 0)
    assert MOE_BUF_ROWS == tm
    gid = row_token.reshape(nt, 1, tm)
    first_spare = p + jnp.arange(tm, dtype=jnp.int32)[None, :]
    sid = jnp.concatenate([first_spare, row_pair.reshape(nt, tm)], axis=0).reshape(nt + 1, 1, tm)
    y_rows = p + tm
    return gid, sid, tile_expert, n_valid, y_rows


def _moe(x2, norm_w, w_router, wg, wu, wd, li, final_norm_w, final_norm):
    h3, idx, wgt = _route(x2, norm_w, w_router)
    gid, sid, tile_expert, n_valid, y_rows = _moe_layout(idx[:, :TOP_K])
    y = _moe_ffn(tile_expert, n_valid, gid, sid, h3, wg, wu, wd, li, y_rows)
    return _combine(x2, y, wgt, final_norm_w, final_norm)


def _rope_tables(seq):
    inv_freq = 1.0 / (ROPE_THETA ** (jnp.arange(0, HEAD_DIM, 2, dtype=F32) / HEAD_DIM))
    ang = jnp.arange(seq, dtype=F32)[:, None] * inv_freq[None, :]
    cos, sin = jnp.cos(ang), jnp.sin(ang)
    cos_t = jnp.tile(cos, (1, LANES // (HEAD_DIM // 2)))
    sin_t = jnp.tile(jnp.concatenate([-sin, sin], axis=1), (1, LANES // HEAD_DIM))
    return cos_t, sin_t


def kernel(x, mix_norm_w, w_in, conv_w, conv_b, ret_norm_w, w_br_attn, w_br_conv, w_br_ret, w_out,
           ffn_norm_w, dense_w_gate, dense_w_up, dense_w_down, moe_router, moe_w_gate, moe_w_up,
           moe_w_down, final_norm_w):
    batch, seq, d = x.shape
    depth = w_in.shape[0]
    assert d == D_MODEL and seq % PROJ_TM == 0 and depth % 2 == 0
    n = batch * seq
    cos_t, sin_t = _rope_tables(seq)
    ret_tables = _retention_tables()
    x2 = x.reshape(n, d)
    for layer in range(depth):
        proj, vt = _inproj(x2, mix_norm_w[layer].reshape(1, d), w_in, layer, cos_t, sin_t, batch, seq)
        y_attn = _moba(proj, vt, batch, seq)
        y_ret = _retention(proj, ret_norm_w[layer].reshape(1, RET_V_WIDTH), ret_tables, batch, seq)
        x2 = _merge(x2, y_attn, proj, y_ret, conv_w[layer], conv_b[layer].reshape(1, CONV_WIDTH),
                    w_br_attn[layer].astype(BF16), w_br_conv[layer].astype(BF16),
                    w_br_ret[layer].astype(BF16), w_out[layer].astype(BF16), seq)
        i = layer // 2
        nw = ffn_norm_w[layer].reshape(1, d)
        if layer % 2 == 0:
            x2 = _dense_ffn(x2, nw, dense_w_gate, dense_w_up, dense_w_down, i)
        else:
            last = layer == depth - 1
            x2 = _moe(x2, nw, moe_router[i], moe_w_gate, moe_w_up, moe_w_down, i,
                      final_norm_w.reshape(1, d), last)
    return x2.reshape(batch, seq, d)
```
